```python
import jax, jax.numpy as jnp
from jax import lax
import numpy as np

D_MODEL = 2048
BATCH = 4
SEQ = 2048
DEPTH = 1

N_META = 16
CHUNK = 128
MIX_DIM = D_MODEL
N_RET_HEADS = 8
RET_HEAD_DIM = 128
RET_DIM = N_RET_HEADS * RET_HEAD_DIM
N_FOX_HEADS = 8
FOX_HEAD_DIM = 128
FOX_DIM = N_FOX_HEADS * FOX_HEAD_DIM
IN_DIM = 4 * RET_DIM + 3 * FOX_DIM + N_FOX_HEADS
D_FF = 5632
CONV_WIDTH = 3
ROPE_BASE = 10000.0
NORM_EPS = 1e-6

kernel_name = "hymba_retention_fox_convffn_block"


def _rmsnorm(x, gain):
    x32 = x.astype(jnp.float32)
    y = x32 * lax.rsqrt(jnp.mean(x32 * x32, axis=-1, keepdims=True) + NORM_EPS)
    return (y * gain.astype(jnp.float32)).astype(x.dtype)


def _heads(t, n_heads):
    b, l, _ = t.shape
    return t.reshape(b, l, n_heads, -1).transpose(0, 2, 1, 3)


def _rotary(t, pos):
    d = t.shape[-1]
    inv_freq = 1.0 / (ROPE_BASE ** (jnp.arange(0, d, 2, dtype=jnp.float32) / d))
    ang = pos[:, None] * inv_freq[None, :]
    cos, sin = jnp.cos(ang), jnp.sin(ang)
    t1, t2 = t[..., : d // 2], t[..., d // 2:]
    return jnp.concatenate([t1 * cos - t2 * sin, t1 * sin + t2 * cos], axis=-1)


def _decay_matrix(log_g, n):
    idx = jnp.arange(n, dtype=jnp.float32)
    diff = idx[:, None] - idx[None, :]
    return jnp.where(diff >= 0, jnp.exp(jnp.maximum(diff, 0.0)[None] * log_g[:, None, None]), 0.0)


def _ret_intra(q, k, v, dmat):
    s = jnp.einsum('bhid,bhjd->bhij', q, k) * dmat[None]
    return jnp.einsum('bhij,bhjv->bhiv', s, v)


def _retention(q, k, v, log_g):
    b, h, l, dv = v.shape
    m = N_META
    out_m = _ret_intra(q[:, :, :m], k[:, :, :m], v[:, :, :m], _decay_matrix(log_g, m))
    zeta_m = jnp.exp((m - 1 - jnp.arange(m, dtype=jnp.float32))[None, :] * log_g[:, None])
    state0 = jnp.einsum('bhjd,bhjv,hj->bhdv', k[:, :, :m], v[:, :, :m], zeta_m)
    n_chunks = (l - m) // CHUNK

    def to_chunks(t):
        return t[:, :, m:].reshape(b, h, n_chunks, CHUNK, t.shape[-1]).transpose(2, 0, 1, 3, 4)

    pos_c = jnp.arange(CHUNK, dtype=jnp.float32)
    d_c = _decay_matrix(log_g, CHUNK)
    xi = jnp.exp((pos_c + 1.0)[None, :] * log_g[:, None])
    zeta = jnp.exp((CHUNK - 1.0 - pos_c)[None, :] * log_g[:, None])
    g_chunk = jnp.exp(CHUNK * log_g)[None, :, None, None]

    def step(state, qkv):
        qc, kc, vc = qkv
        o = _ret_intra(qc, kc, vc, d_c) + jnp.einsum('bhid,bhdv,hi->bhiv', qc, state, xi)
        state = g_chunk * state + jnp.einsum('bhjd,bhjv,hj->bhdv', kc, vc, zeta)
        return state, o

    _, o = lax.scan(step, state0, (to_chunks(q), to_chunks(k), to_chunks(v)))
    o = o.transpose(1, 2, 0, 3, 4).reshape(b, h, n_chunks * CHUNK, dv)
    return jnp.concatenate([out_m, o], axis=2)


def _forgetting_attention(q, k, v, log_f):
    l = q.shape[2]
    scale = q.shape[-1] ** -0.5
    cum = jnp.cumsum(log_f, axis=-1)
    bounds = [0, N_META] + [N_META + CHUNK * (i + 1) for i in range((l - N_META) // CHUNK)]
    outs = []
    for s, e in zip(bounds[:-1], bounds[1:]):
        logits = jnp.einsum('bhqd,bhkd->bhqk', q[:, :, s:e], k[:, :, :e]).astype(jnp.float32) * scale
        logits = logits + cum[:, :, s:e, None] - cum[:, :, None, :e]
        causal = jnp.arange(s, e)[:, None] >= jnp.arange(e)[None, :]
        p = jax.nn.softmax(jnp.where(causal[None, None], logits, -jnp.inf), axis=-1)
        outs.append(jnp.einsum('bhqk,bhkd->bhqd', p.astype(v.dtype), v[:, :, :e]))
    return jnp.concatenate(outs, axis=2)


def _head_groupnorm(o, gain):
    mu = jnp.mean(o, axis=-1, keepdims=True)
    var = jnp.mean(jnp.square(o - mu), axis=-1, keepdims=True)
    y = (o - mu) * lax.rsqrt(var + NORM_EPS)
    b, h, l, d = o.shape
    return y.transpose(0, 2, 1, 3).reshape(b, l, h * d) * gain.astype(jnp.float32)


def _causal_dwconv(u, w, bias):
    kw = w.shape[0]
    l = u.shape[1]
    up = jnp.pad(u, ((0, 0), (kw - 1, 0), (0, 0)))
    y = bias
    for i in range(kw):
        y = y + w[i] * up[:, i:i + l]
    return y


def setup_inputs(seed: int = 0) -> dict:
    key = jax.random.key(seed)
    ks = jax.random.split(key, 16)
    f32 = jnp.float32
    x = jax.random.normal(ks[0], (BATCH, SEQ, D_MODEL), f32)
    meta_tokens = jax.random.normal(ks[1], (N_META, D_MODEL), f32)
    norm1_gain = 1.0 + 0.01 * jax.random.normal(ks[2], (DEPTH, D_MODEL), f32)
    w_in = jax.random.normal(ks[3], (DEPTH, D_MODEL, IN_DIM), f32) * D_MODEL ** -0.5
    b_forget = (jnp.linspace(1.0, 5.0, N_FOX_HEADS, dtype=f32)[None, :]
                + 0.1 * jax.random.normal(ks[4], (DEPTH, N_FOX_HEADS), f32))
    ret_norm_gain = 1.0 + 0.01 * jax.random.normal(ks[5], (DEPTH, RET_DIM), f32)
    w_out = jax.random.normal(ks[6], (DEPTH, MIX_DIM, D_MODEL), f32) * MIX_DIM ** -0.5
    norm2_gain = 1.0 + 0.01 * jax.random.normal(ks[7], (DEPTH, D_MODEL), f32)
    w_up = jax.random.normal(ks[8], (DEPTH, D_MODEL, 2 * D_FF), f32) * D_MODEL ** -0.5
    conv_w = jax.random.normal(ks[9], (DEPTH, CONV_WIDTH, 2 * D_FF), f32) * CONV_WIDTH ** -0.5
    conv_b = 0.01 * jax.random.normal(ks[10], (DEPTH, 2 * D_FF), f32)
    w_down = jax.random.normal(ks[11], (DEPTH, D_FF, D_MODEL), f32) * D_FF ** -0.5
    final_norm_gain = 1.0 + 0.01 * jax.random.normal(ks[12], (D_MODEL,), f32)
    return {"x": x, "meta_tokens": meta_tokens, "norm1_gain": norm1_gain, "w_in": w_in,
            "b_forget": b_forget, "ret_norm_gain": ret_norm_gain, "w_out": w_out,
            "norm2_gain": norm2_gain, "w_up": w_up, "conv_w": conv_w, "conv_b": conv_b,
            "w_down": w_down, "final_norm_gain": final_norm_gain}


def reference(x, meta_tokens, norm1_gain, w_in, b_forget, ret_norm_gain, w_out,
              norm2_gain, w_up, conv_w, conv_b, w_down, final_norm_gain):
    b = x.shape[0]
    f32 = jnp.float32
    h = jnp.concatenate([jnp.broadcast_to(meta_tokens[None].astype(x.dtype), (b, N_META, D_MODEL)), x], axis=1)
    l = h.shape[1]
    pos = jnp.arange(l, dtype=f32)
    log_g = jnp.log1p(-jnp.exp2(-5.0 - jnp.arange(N_RET_HEADS, dtype=f32)))
    split_at = np.cumsum([RET_DIM] * 4 + [FOX_DIM] * 3)[:].tolist()

    for layer in range(DEPTH):
        a = _rmsnorm(h, norm1_gain[layer])
        proj = a @ w_in[layer]
        r_q, r_k, r_v, r_g, f_q, f_k, f_v, f_f = jnp.split(proj, split_at, axis=-1)

        rq = _rotary(_heads(r_q, N_RET_HEADS).astype(f32), pos)
        rk = _rotary(_heads(r_k, N_RET_HEADS).astype(f32), pos) * RET_HEAD_DIM ** -0.5
        rv = _heads(r_v, N_RET_HEADS).astype(f32)
        ret = _head_groupnorm(_retention(rq, rk, rv, log_g), ret_norm_gain[layer])
        ret = (jax.nn.silu(r_g.astype(f32)) * ret).astype(x.dtype)

        log_f = jax.nn.log_sigmoid(f_f.astype(f32) + b_forget[layer].astype(f32)).transpose(0, 2, 1)
        fox = _forgetting_attention(_heads(f_q, N_FOX_HEADS), _heads(f_k, N_FOX_HEADS),
                                    _heads(f_v, N_FOX_HEADS), log_f)
        fox = fox.transpose(0, 2, 1, 3).reshape(b, l, FOX_DIM).astype(x.dtype)

        h = h + jnp.concatenate([ret, fox], axis=-1) @ w_out[layer]

        c = _rmsnorm(h, norm2_gain[layer])
        u = _causal_dwconv(c @ w_up[layer], conv_w[layer], conv_b[layer])
        gate, val = jnp.split(u, 2, axis=-1)
        h = h + (jax.nn.silu(gate) * val) @ w_down[layer]

    out = _rmsnorm(h, final_norm_gain)
    return out[:, N_META:]
```

```python
import functools

import jax
import jax.numpy as jnp
from jax import lax
from jax.experimental import pallas as pl
from jax.experimental.pallas import tpu as pltpu

F32 = jnp.float32
BF16 = jnp.bfloat16

D_MODEL = 2048
N_META = 16
CHUNK = 128
N_HEADS = 8
HEAD_DIM = 128
RET_DIM = N_HEADS * HEAD_DIM
FOX_DIM = N_HEADS * HEAD_DIM
MAIN_COLS = 4 * RET_DIM + 3 * FOX_DIM
D_FF = 5632
ROPE_BASE = 10000.0
NORM_EPS = 1e-6
META_PAD = CHUNK - N_META
MASKED = -1e30
LANES = 128
MIB = 1024 * 1024

SEC_RQ, SEC_RK, SEC_RV, SEC_RG, SEC_FQ, SEC_FK, SEC_FV = (s * N_HEADS for s in range(7))


def _params(sem, vmem_mib):
    return pltpu.CompilerParams(dimension_semantics=sem, vmem_limit_bytes=vmem_mib * MIB)


def _rms(x, gain):
    ms = jnp.mean(x * x, axis=-1, keepdims=True)
    return (x * lax.rsqrt(ms + NORM_EPS)) * gain


def _inproj_kernel(x_ref, g_ref, w_ref, wf_ref, cos_ref, sin_ref, o_ref, ff_ref, a_scr, *, tn):
    j = pl.program_id(1)

    @pl.when(j == 0)
    def _():
        a_scr[...] = _rms(x_ref[...], g_ref[...]).astype(BF16)
        ff_ref[...] = jnp.dot(a_scr[...], wf_ref[...], preferred_element_type=F32)

    acc = jnp.dot(a_scr[...], w_ref[...], preferred_element_type=F32)
    n_rot = 2 * RET_DIM // tn

    @pl.when(j < n_rot)
    def _():
        scale = jnp.where(j >= n_rot // 2, HEAD_DIM ** -0.5, 1.0).astype(F32)
        cos = cos_ref[...]
        sin = sin_ref[...]
        for hh in range(tn // LANES):
            t = acc[:, hh * LANES:(hh + 1) * LANES]
            r = pltpu.roll(t, HEAD_DIM // 2, 1)
            o_ref[:, hh * LANES:(hh + 1) * LANES] = ((t * cos + r * sin) * scale).astype(BF16)

    @pl.when(j >= n_rot)
    def _():
        o_ref[...] = acc.astype(BF16)


def _inproj(x2d, gain, w_main, w_f, cos_t, sin_t, *, tm, tn):
    m = x2d.shape[0]
    nb = cos_t.shape[0] // tm
    return pl.pallas_call(
        functools.partial(_inproj_kernel, tn=tn),
        grid=(m // tm, MAIN_COLS // tn),
        in_specs=[
            pl.BlockSpec((tm, D_MODEL), lambda i, j: (i, 0)),
            pl.BlockSpec((1, D_MODEL), lambda i, j: (0, 0)),
            pl.BlockSpec((D_MODEL, tn), lambda i, j: (0, j)),
            pl.BlockSpec((D_MODEL, LANES), lambda i, j: (0, 0)),
            pl.BlockSpec((tm, LANES), lambda i, j: (i % nb, 0)),
            pl.BlockSpec((tm, LANES), lambda i, j: (i % nb, 0)),
        ],
        out_specs=[
            pl.BlockSpec((tm, tn), lambda i, j: (i, j)),
            pl.BlockSpec((tm, LANES), lambda i, j: (i, 0)),
        ],
        out_shape=[
            jax.ShapeDtypeStruct((m, MAIN_COLS), BF16),
            jax.ShapeDtypeStruct((m, LANES), F32),
        ],
        scratch_shapes=[pltpu.VMEM((tm, D_MODEL), BF16)],
        compiler_params=_params(("arbitrary", "arbitrary"), 48),
        name="inproj",
    )(x2d, gain, w_main, w_f, cos_t, sin_t)


def _cum_kernel(ff_ref, b_ref, o_ref, *, n_blk, valid_from, rel_last):
    row = lax.broadcasted_iota(jnp.int32, (CHUNK, CHUNK), 0)
    col = lax.broadcasted_iota(jnp.int32, (CHUNK, CHUNK), 1)
    tri = (row >= col).astype(BF16)
    rows = lax.broadcasted_iota(jnp.int32, (CHUNK, LANES), 0)
    carry = jnp.zeros((1, LANES), F32)
    for blk in range(n_blk):
        z = ff_ref[blk * CHUNK:(blk + 1) * CHUNK, :] + b_ref[...]
        lf = jnp.minimum(z, 0.0) - jnp.log1p(jnp.exp(-jnp.abs(z)))
        if valid_from:
            lf = jnp.where(rows >= valid_from, lf, 0.0)
        hi = lf.astype(BF16)
        r1 = lf - hi.astype(F32)
        mid = r1.astype(BF16)
        lo = (r1 - mid.astype(F32)).astype(BF16)
        cum = (jnp.dot(tri, hi, preferred_element_type=F32)
               + jnp.dot(tri, mid, preferred_element_type=F32)
               + jnp.dot(tri, lo, preferred_element_type=F32)) + carry
        o_ref[blk * CHUNK:(blk + 1) * CHUNK, :] = cum
        carry = cum[CHUNK - 1:CHUNK, :]
    if rel_last:
        o_ref[...] = o_ref[...] - carry


def _cum(ff, b_row, *, rows_per_batch, valid_from=0, rel_last=False):
    m = ff.shape[0]
    return pl.pallas_call(
        functools.partial(_cum_kernel, n_blk=rows_per_batch // CHUNK,
                          valid_from=valid_from, rel_last=rel_last),
        grid=(m // rows_per_batch,),
        in_specs=[
            pl.BlockSpec((rows_per_batch, LANES), lambda b: (b, 0)),
            pl.BlockSpec((1, LANES), lambda b: (0, 0)),
        ],
        out_specs=pl.BlockSpec((rows_per_batch, LANES), lambda b: (b, 0)),
        out_shape=jax.ShapeDtypeStruct((m, LANES), F32),
        compiler_params=_params(("arbitrary",), 32),
        name="cumgate",
    )(ff, b_row)


def _ret_kernel(*refs, n_chunks, meta_prefix):
    if meta_prefix:
        logg_ref, q_ref, k_ref, v_ref, g_ref, km_ref, vm_ref, gain_ref, o_ref, state_scr = refs
    else:
        logg_ref, q_ref, k_ref, v_ref, g_ref, gain_ref, o_ref, state_scr = refs
    lg = logg_ref[pl.program_id(1)]
    ri = lax.broadcasted_iota(jnp.int32, (CHUNK, CHUNK), 0)
    ci = lax.broadcasted_iota(jnp.int32, (CHUNK, CHUNK), 1)
    diff = (ri - ci).astype(F32)
    dmat = jnp.where(diff >= 0, jnp.exp(jnp.maximum(diff, 0.0) * lg), 0.0)
    pos = lax.broadcasted_iota(jnp.int32, (CHUNK, 1), 0).astype(F32)
    xi = jnp.exp((pos + 1.0) * lg)
    zeta = jnp.exp((CHUNK - 1.0 - pos) * lg)
    g_chunk = jnp.exp(jnp.full((1, 1), float(CHUNK), F32) * lg)

    def advance(state, kc, vc):
        kz = (kc.astype(F32) * zeta).astype(BF16)
        return g_chunk * state + lax.dot_general(
            kz, vc, (((0,), (0,)), ((), ())), preferred_element_type=F32)

    state0 = jnp.zeros((HEAD_DIM, HEAD_DIM), F32)
    if meta_prefix:
        state0 = advance(state0, km_ref[...], vm_ref[...])
    state_scr[...] = state0

    def body(c, carry):
        r = pl.multiple_of(c * CHUNK, CHUNK)
        qc = q_ref[pl.ds(r, CHUNK), :]
        kc = k_ref[pl.ds(r, CHUNK), :]
        vc = v_ref[pl.ds(r, CHUNK), :]
        st = state_scr[...]
        s = lax.dot_general(qc, kc, (((1,), (1,)), ((), ())), preferred_element_type=F32) * dmat
        o = (jnp.dot(s.astype(BF16), vc, preferred_element_type=F32)
             + xi * jnp.dot(qc, st.astype(BF16), preferred_element_type=F32))
        state_scr[...] = advance(st, kc, vc)
        mu = jnp.mean(o, axis=-1, keepdims=True)
        d = o - mu
        var = jnp.mean(d * d, axis=-1, keepdims=True)
        y = (d * lax.rsqrt(var + NORM_EPS)) * gain_ref[...]
        gt = g_ref[pl.ds(r, CHUNK), :].astype(F32)
        silu = gt / (1.0 + jnp.exp(-gt))
        o_ref[pl.ds(r, CHUNK), :] = (silu * y).astype(BF16)
        return carry

    lax.fori_loop(0, n_chunks, body, 0)


def _retention(log_g, proj, proj_meta, gain, *, n_batch, rows_per_batch, meta_prefix):
    nb = n_batch
    blk = lambda sec: pl.BlockSpec((rows_per_batch, HEAD_DIM), lambda b, h, s=sec: (b, s + h))
    in_specs = [pl.BlockSpec(memory_space=pltpu.SMEM),
                blk(SEC_RQ), blk(SEC_RK), blk(SEC_RV), blk(SEC_RG)]
    args = [log_g, proj, proj, proj, proj]
    if meta_prefix:
        mblk = lambda sec: pl.BlockSpec((CHUNK, HEAD_DIM), lambda b, h, s=sec: (0, s + h))
        in_specs += [mblk(SEC_RK), mblk(SEC_RV)]
        args += [proj_meta, proj_meta]
    in_specs.append(pl.BlockSpec((1, HEAD_DIM), lambda b, h: (0, h)))
    args.append(gain)
    return pl.pallas_call(
        functools.partial(_ret_kernel, n_chunks=rows_per_batch // CHUNK, meta_prefix=meta_prefix),
        grid=(nb, N_HEADS),
        in_specs=in_specs,
        out_specs=pl.BlockSpec((rows_per_batch, HEAD_DIM), lambda b, h: (b, h)),
        out_shape=jax.ShapeDtypeStruct((nb * rows_per_batch, RET_DIM), BF16),
        scratch_shapes=[pltpu.VMEM((HEAD_DIM, HEAD_DIM), F32)],
        compiler_params=_params(("arbitrary", "arbitrary"), 32),
        name="retention",
    )(*args)


def _fox_kernel(*refs, n_q, tq, meta_only):
    if meta_only:
        q_ref, km_ref, vm_ref, cq_ref, ckm_ref, o_ref = refs
    else:
        q_ref, k_ref, v_ref, km_ref, vm_ref, cq_ref, ck_ref, ckm_ref, o_ref = refs
    h = pl.program_id(1)
    scale = HEAD_DIM ** -0.5
    lane = lax.broadcasted_iota(jnp.int32, (1, LANES), 1)
    km = km_ref[...]
    vm = vm_ref[...]
    ckm = ckm_ref[0]
    col_m = lax.broadcasted_iota(jnp.int32, (tq, CHUNK), 1)
    row_m = lax.broadcasted_iota(jnp.int32, (tq, CHUNK), 0)
    mask_m = col_m >= META_PAD
    if meta_only:
        mask_m = mask_m & (row_m >= col_m)
    tri = (lax.broadcasted_iota(jnp.int32, (tq, tq), 0)
           >= lax.broadcasted_iota(jnp.int32, (tq, tq), 1))

    def block(q, kb, vb, ckb, mask, cq, m, l, acc):
        s = lax.dot_general(q, kb, (((1,), (1,)), ((), ())), preferred_element_type=F32) * scale
        t = s - ckb
        if mask is not None:
            t = jnp.where(mask, t, MASKED)
        m_new = jnp.maximum(m, jnp.max(t, axis=1, keepdims=True) + cq)
        p = jnp.exp(t - (m_new - cq))
        alpha = jnp.exp(m - m_new)
        l = alpha * l + jnp.sum(p, axis=1, keepdims=True)
        acc = alpha * acc + jnp.dot(p.astype(BF16), vb, preferred_element_type=F32)
        return m_new, l, acc

    for qi in range(n_q):
        rows = slice(qi * tq, (qi + 1) * tq)
        q = q_ref[rows, :]
        cq = jnp.sum(jnp.where(lane == h, cq_ref[rows, :], 0.0), axis=1, keepdims=True)
        m = jnp.full((tq, 1), MASKED, F32)
        l = jnp.zeros((tq, 1), F32)
        acc = jnp.zeros((tq, HEAD_DIM), F32)
        m, l, acc = block(q, km, vm, ckm, mask_m, cq, m, l, acc)
        if not meta_only:
            def body(j, carry, q=q, cq=cq):
                c0 = pl.multiple_of(j * tq, tq)
                return block(q, k_ref[pl.ds(c0, tq), :], v_ref[pl.ds(c0, tq), :],
                             ck_ref[0, j], None, cq, *carry)
            m, l, acc = lax.fori_loop(0, qi, body, (m, l, acc))
            m, l, acc = block(q, k_ref[rows, :], v_ref[rows, :], ck_ref[0, qi], tri, cq, m, l, acc)
        o_ref[rows, :] = (acc / l).astype(BF16)


def _fox(proj, proj_meta, cum_col, ck_rows, ckm_rows, *, n_batch, rows_per_batch, tq, meta_only):
    n_q = rows_per_batch // tq
    qblk = lambda sec: pl.BlockSpec((rows_per_batch, HEAD_DIM), lambda b, h, s=sec: (b, s + h))
    mblk = lambda sec: pl.BlockSpec((CHUNK, HEAD_DIM), lambda b, h, s=sec: (0, s + h))
    cq_spec = pl.BlockSpec((rows_per_batch, LANES), lambda b, h: (b, 0))
    ckm_spec = pl.BlockSpec((1, 1, CHUNK), lambda b, h: (h, 0, 0))
    if meta_only:
        in_specs = [qblk(SEC_FQ), mblk(SEC_FK), mblk(SEC_FV), cq_spec, ckm_spec]
        args = [proj, proj_meta, proj_meta, cum_col, ckm_rows]
    else:
        ck_spec = pl.BlockSpec((1, n_q, 1, tq), lambda b, h: (b * N_HEADS + h, 0, 0, 0))
        in_specs = [qblk(SEC_FQ), qblk(SEC_FK), qblk(SEC_FV), mblk(SEC_FK), mblk(SEC_FV),
                    cq_spec, ck_spec, ckm_spec]
        args = [proj, proj, proj, proj_meta, proj_meta, cum_col, ck_rows, ckm_rows]
    return pl.pallas_call(
        functools.partial(_fox_kernel, n_q=n_q, tq=tq, meta_only=meta_only),
        grid=(n_batch, N_HEADS),
        in_specs=in_specs,
        out_specs=pl.BlockSpec((rows_per_batch, HEAD_DIM), lambda b, h: (b, h)),
        out_shape=jax.ShapeDtypeStruct((n_batch * rows_per_batch, FOX_DIM), BF16),
        compiler_params=_params(("arbitrary", "arbitrary"), 32),
        name="foxattn",
    )(*args)


def _outproj_kernel(ret_ref, fox_ref, w1_ref, w2_ref, x_ref, g_ref, h_ref, c_ref):
    hcur = (x_ref[...]
            + jnp.dot(ret_ref[...], w1_ref[...], preferred_element_type=F32)
            + jnp.dot(fox_ref[...], w2_ref[...], preferred_element_type=F32))
    h_ref[...] = hcur
    c_ref[...] = _rms(hcur, g_ref[...]).astype(BF16)


def _outproj(ret, fox, w_out, x2d, gain, *, tm):
    m = x2d.shape[0]
    return pl.pallas_call(
        _outproj_kernel,
        grid=(m // tm,),
        in_specs=[
            pl.BlockSpec((tm, RET_DIM), lambda i: (i, 0)),
            pl.BlockSpec((tm, FOX_DIM), lambda i: (i, 0)),
            pl.BlockSpec((RET_DIM, D_MODEL), lambda i: (0, 0)),
            pl.BlockSpec((FOX_DIM, D_MODEL), lambda i: (1, 0)),
            pl.BlockSpec((tm, D_MODEL), lambda i: (i, 0)),
            pl.BlockSpec((1, D_MODEL), lambda i: (0, 0)),
        ],
        out_specs=[
            pl.BlockSpec((tm, D_MODEL), lambda i: (i, 0)),
            pl.BlockSpec((tm, D_MODEL), lambda i: (i, 0)),
        ],
        out_shape=[
            jax.ShapeDtypeStruct((m, D_MODEL), F32),
            jax.ShapeDtypeStruct((m, D_MODEL), BF16),
        ],
        compiler_params=_params(("arbitrary",), 48),
        name="outproj",
    )(ret, fox, w_out, w_out, x2d, gain)


def _up_kernel(c_ref, cm_ref, wg_ref, wv_ref, cwg_ref, cwv_ref, cbg_ref, cbv_ref, o_ref,
               wg_s, wv_s, ug_s, uv_s, *, tm, tiles_per_batch):
    i = pl.program_id(1)

    @pl.when(i == 0)
    def _():
        wg_s[...] = wg_ref[...].astype(BF16)
        wv_s[...] = wv_ref[...].astype(BF16)

    @pl.when(i % tiles_per_batch == 0)
    def _():
        cm = cm_ref[...]
        ug_s[0:8, :] = jnp.dot(cm, wg_s[...], preferred_element_type=F32)[8:16, :]
        uv_s[0:8, :] = jnp.dot(cm, wv_s[...], preferred_element_type=F32)[8:16, :]

    c = c_ref[...]

    def conv(w_s, u_s, cw_ref, cb_ref):
        u_s[8:8 + tm, :] = jnp.dot(c, w_s[...], preferred_element_type=F32)
        y = (cb_ref[...]
             + cw_ref[0:1, :] * u_s[6:6 + tm, :]
             + cw_ref[1:2, :] * u_s[7:7 + tm, :]
             + cw_ref[2:3, :] * u_s[8:8 + tm, :])
        u_s[0:8, :] = u_s[tm:tm + 8, :]
        return y

    gate = conv(wg_s, ug_s, cwg_ref, cbg_ref)
    val = conv(wv_s, uv_s, cwv_ref, cbv_ref)
    o_ref[...] = ((gate / (1.0 + jnp.exp(-gate))) * val).astype(BF16)


def _upconv(c, c_meta, w_up, conv_w, conv_b, *, rows_per_batch, tm, tn):
    m = c.shape[0]
    nj = D_FF // tn
    tiles_per_batch = rows_per_batch // tm
    halo_blk = c_meta.shape[0] // 16 - 1
    return pl.pallas_call(
        functools.partial(_up_kernel, tm=tm, tiles_per_batch=tiles_per_batch),
        grid=(nj, m // tm),
        in_specs=[
            pl.BlockSpec((tm, D_MODEL), lambda j, i: (i, 0)),
            pl.BlockSpec((16, D_MODEL), lambda j, i: (halo_blk, 0)),
            pl.BlockSpec((D_MODEL, tn), lambda j, i: (0, j)),
            pl.BlockSpec((D_MODEL, tn), lambda j, i: (0, j + nj)),
            pl.BlockSpec((3, tn), lambda j, i: (0, j)),
            pl.BlockSpec((3, tn), lambda j, i: (0, j + nj)),
            pl.BlockSpec((1, tn), lambda j, i: (0, j)),
            pl.BlockSpec((1, tn), lambda j, i: (0, j + nj)),
        ],
        out_specs=pl.BlockSpec((tm, tn), lambda j, i: (i, j)),
        out_shape=jax.ShapeDtypeStruct((m, D_FF), BF16),
        scratch_shapes=[
            pltpu.VMEM((D_MODEL, tn), BF16),
            pltpu.VMEM((D_MODEL, tn), BF16),
            pltpu.VMEM((tm + 8, tn), F32),
            pltpu.VMEM((tm + 8, tn), F32),
        ],
        compiler_params=_params(("arbitrary", "arbitrary"), 56),
        name="upconv",
    )(c, c_meta, w_up, w_up, conv_w, conv_w, conv_b, conv_b)


def _down_kernel(a_ref, w_ref, h_ref, g_ref, o_ref, *, n_k):
    k = pl.program_id(1)

    @pl.when(k == 0)
    def _():
        o_ref[...] = h_ref[...]

    o_ref[...] += jnp.dot(a_ref[...], w_ref[...], preferred_element_type=F32)

    @pl.when(k == n_k - 1)
    def _():
        o_ref[...] = _rms(o_ref[...], g_ref[...])


def _down(act, w_down, h1, gain, *, tm, tk):
    m = act.shape[0]
    n_k = D_FF // tk
    return pl.pallas_call(
        functools.partial(_down_kernel, n_k=n_k),
        grid=(m // tm, n_k),
        in_specs=[
            pl.BlockSpec((tm, tk), lambda i, k: (i, k)),
            pl.BlockSpec((tk, D_MODEL), lambda i, k: (k, 0)),
            pl.BlockSpec((tm, D_MODEL), lambda i, k: (i, 0)),
            pl.BlockSpec((1, D_MODEL), lambda i, k: (0, 0)),
        ],
        out_specs=pl.BlockSpec((tm, D_MODEL), lambda i, k: (i, 0)),
        out_shape=jax.ShapeDtypeStruct((m, D_MODEL), F32),
        compiler_params=_params(("arbitrary", "arbitrary"), 48),
        name="downproj",
    )(act, w_down, h1, gain)


def _rotary_tables(pos):
    inv_freq = 1.0 / (ROPE_BASE ** (jnp.arange(0, HEAD_DIM, 2, dtype=F32) / HEAD_DIM))
    ang = pos[:, None] * inv_freq[None, :]
    cos, sin = jnp.cos(ang), jnp.sin(ang)
    return jnp.concatenate([cos, cos], axis=1), jnp.concatenate([-sin, sin], axis=1)


def kernel(x, meta_tokens, norm1_gain, w_in, b_forget, ret_norm_gain, w_out, norm2_gain,
           w_up, conv_w, conv_b, w_down, final_norm_gain):
    n_batch, seq, d_model = x.shape
    assert d_model == D_MODEL and seq % CHUNK == 0 and w_in.shape[0] == 1
    assert meta_tokens.shape == (N_META, D_MODEL)
    x2d = x.reshape(n_batch * seq, D_MODEL)
    xm = jnp.concatenate([jnp.zeros((META_PAD, D_MODEL), F32), meta_tokens.astype(F32)], axis=0)

    w_main = w_in[0, :, :MAIN_COLS].astype(BF16)
    w_f = jnp.pad(w_in[0, :, MAIN_COLS:], ((0, 0), (0, LANES - N_HEADS))).astype(BF16)
    w_out_b = w_out[0].astype(BF16)
    w_down_b = w_down[0].astype(BF16)
    b_row = jnp.pad(b_forget[0], (0, LANES - N_HEADS)).reshape(1, LANES)
    log_g = jnp.log1p(-jnp.exp2(-5.0 - jnp.arange(N_HEADS, dtype=F32)))
    g1 = norm1_gain[0].reshape(1, D_MODEL)
    g2 = norm2_gain[0].reshape(1, D_MODEL)
    gf = final_norm_gain.reshape(1, D_MODEL)
    gr = ret_norm_gain[0].reshape(1, RET_DIM)

    cos_r, sin_r = _rotary_tables(N_META + jnp.arange(seq, dtype=F32))
    cos_m, sin_m = _rotary_tables(jnp.maximum(jnp.arange(CHUNK, dtype=F32) - META_PAD, 0.0))

    proj_m, ff_m = _inproj(xm, g1, w_main, w_f, cos_m, sin_m, tm=CHUNK, tn=512)
    cum_m = _cum(ff_m, b_row, rows_per_batch=CHUNK, valid_from=META_PAD, rel_last=True)
    ckm_rows = cum_m[:, :N_HEADS].T.reshape(N_HEADS, 1, CHUNK)
    ret_m = _retention(log_g, proj_m, None, gr, n_batch=1, rows_per_batch=CHUNK, meta_prefix=False)
    fox_m = _fox(proj_m, proj_m, cum_m, None, ckm_rows, n_batch=1, rows_per_batch=CHUNK,
                 tq=CHUNK, meta_only=True)
    _, c_m = _outproj(ret_m, fox_m, w_out_b, xm, g2, tm=CHUNK)

    tq = 256
    proj, ff = _inproj(x2d, g1, w_main, w_f, cos_r, sin_r, tm=1024, tn=512)
    cum = _cum(ff, b_row, rows_per_batch=seq)
    ck_rows = (cum[:, :N_HEADS].reshape(n_batch, seq, N_HEADS).transpose(0, 2, 1)
               .reshape(n_batch * N_HEADS, seq // tq, 1, tq))
    ret = _retention(log_g, proj, proj_m, gr, n_batch=n_batch, rows_per_batch=seq, meta_prefix=True)
    fox = _fox(proj, proj_m, cum, ck_rows, ckm_rows, n_batch=n_batch, rows_per_batch=seq,
               tq=tq, meta_only=False)
    h1, c = _outproj(ret, fox, w_out_b, x2d, g2, tm=256)
    act = _upconv(c, c_m, w_up[0], conv_w[0], conv_b, rows_per_batch=seq, tm=1024, tn=512)
    out = _down(act, w_down_b, h1, gf, tm=512, tk=1408)
    return out.reshape(n_batch, seq, D_MODEL)
```

```python
import functools

import jax
import jax.numpy as jnp
from jax import lax
from jax.experimental import pallas as pl
from jax.experimental.pallas import tpu as pltpu

F32 = jnp.float32
BF16 = jnp.bfloat16

D_MODEL = 2048
N_META = 16
CHUNK = 128
N_HEADS = 8
HEAD_DIM = 128
RET_DIM = N_HEADS * HEAD_DIM
FOX_DIM = N_HEADS * HEAD_DIM
MAIN_COLS = 4 * RET_DIM + 3 * FOX_DIM
D_FF = 5632
ROPE_BASE = 10000.0
NORM_EPS = 1e-6
META_PAD = CHUNK - N_META
MASKED = -1e30
LANES = 128
MIB = 1024 * 1024

SEC_RQ, SEC_RK, SEC_RV, SEC_RG, SEC_FQ, SEC_FK, SEC_FV = (s * N_HEADS for s in range(7))


def _params(sem, vmem_mib):
    return pltpu.CompilerParams(dimension_semantics=sem, vmem_limit_bytes=vmem_mib * MIB)


def _rms(x, gain):
    ms = jnp.mean(x * x, axis=-1, keepdims=True)
    return (x * lax.rsqrt(ms + NORM_EPS)) * gain


def _inproj_kernel(x_ref, g_ref, w_ref, wf_ref, cos_ref, sin_ref, o_ref, ff_ref, a_scr, *, tn):
    j = pl.program_id(1)

    @pl.when(j == 0)
    def _():
        a_scr[...] = _rms(x_ref[...], g_ref[...]).astype(BF16)
        ff_ref[...] = jnp.dot(a_scr[...], wf_ref[...], preferred_element_type=F32)

    acc = jnp.dot(a_scr[...], w_ref[...], preferred_element_type=F32)
    n_rot = 2 * RET_DIM // tn

    @pl.when(j < n_rot)
    def _():
        scale = jnp.where(j >= n_rot // 2, HEAD_DIM ** -0.5, 1.0).astype(F32)
        cos = cos_ref[...]
        sin = sin_ref[...]
        for hh in range(tn // LANES):
            t = acc[:, hh * LANES:(hh + 1) * LANES]
            r = pltpu.roll(t, HEAD_DIM // 2, 1)
            o_ref[:, hh * LANES:(hh + 1) * LANES] = ((t * cos + r * sin) * scale).astype(BF16)

    @pl.when(j >= n_rot)
    def _():
        o_ref[...] = acc.astype(BF16)


def _inproj(x2d, gain, w_main, w_f, cos_t, sin_t, *, tm, tn):
    m = x2d.shape[0]
    nb = cos_t.shape[0] // tm
    return pl.pallas_call(
        functools.partial(_inproj_kernel, tn=tn),
        grid=(m // tm, MAIN_COLS // tn),
        in_specs=[
            pl.BlockSpec((tm, D_MODEL), lambda i, j: (i, 0)),
            pl.BlockSpec((1, D_MODEL), lambda i, j: (0, 0)),
            pl.BlockSpec((D_MODEL, tn), lambda i, j: (0, j)),
            pl.BlockSpec((D_MODEL, LANES), lambda i, j: (0, 0)),
            pl.BlockSpec((tm, LANES), lambda i, j: (i % nb, 0)),
            pl.BlockSpec((tm, LANES), lambda i, j: (i % nb, 0)),
        ],
        out_specs=[
            pl.BlockSpec((tm, tn), lambda i, j: (i, j)),
            pl.BlockSpec((tm, LANES), lambda i, j: (i, 0)),
        ],
        out_shape=[
            jax.ShapeDtypeStruct((m, MAIN_COLS), BF16),
            jax.ShapeDtypeStruct((m, LANES), F32),
        ],
        scratch_shapes=[pltpu.VMEM((tm, D_MODEL), BF16)],
        compiler_params=_params(("arbitrary", "arbitrary"), 48),
        name="inproj",
    )(x2d, gain, w_main, w_f, cos_t, sin_t)


def _cum_kernel(ff_ref, b_ref, o_ref, *, n_blk, valid_from, rel_last):
    row = lax.broadcasted_iota(jnp.int32, (CHUNK, CHUNK), 0)
    col = lax.broadcasted_iota(jnp.int32, (CHUNK, CHUNK), 1)
    tri = (row >= col).astype(BF16)
    rows = lax.broadcasted_iota(jnp.int32, (CHUNK, LANES), 0)
    carry = jnp.zeros((1, LANES), F32)
    for blk in range(n_blk):
        z = ff_ref[blk * CHUNK:(blk + 1) * CHUNK, :] + b_ref[...]
        lf = jnp.minimum(z, 0.0) - jnp.log1p(jnp.exp(-jnp.abs(z)))
        if valid_from:
            lf = jnp.where(rows >= valid_from, lf, 0.0)
        hi = lf.astype(BF16)
        r1 = lf - hi.astype(F32)
        mid = r1.astype(BF16)
        lo = (r1 - mid.astype(F32)).astype(BF16)
        cum = (jnp.dot(tri, hi, preferred_element_type=F32)
               + jnp.dot(tri, mid, preferred_element_type=F32)
               + jnp.dot(tri, lo, preferred_element_type=F32)) + carry
        o_ref[blk * CHUNK:(blk + 1) * CHUNK, :] = cum
        carry = cum[CHUNK - 1:CHUNK, :]
    if rel_last:
        o_ref[...] = o_ref[...] - carry


def _cum(ff, b_row, *, rows_per_batch, valid_from=0, rel_last=False):
    m = ff.shape[0]
    return pl.pallas_call(
        functools.partial(_cum_kernel, n_blk=rows_per_batch // CHUNK,
                          valid_from=valid_from, rel_last=rel_last),
        grid=(m // rows_per_batch,),
        in_specs=[
            pl.BlockSpec((rows_per_batch, LANES), lambda b: (b, 0)),
            pl.BlockSpec((1, LANES), lambda b: (0, 0)),
        ],
        out_specs=pl.BlockSpec((rows_per_batch, LANES), lambda b: (b, 0)),
        out_shape=jax.ShapeDtypeStruct((m, LANES), F32),
        compiler_params=_params(("arbitrary",), 32),
        name="cumgate",
    )(ff, b_row)


def _ret_kernel(*refs, n_chunks, meta_prefix):
    if meta_prefix:
        logg_ref, q_ref, k_ref, v_ref, g_ref, km_ref, vm_ref, gain_ref, o_ref, state_scr = refs
    else:
        logg_ref, q_ref, k_ref, v_ref, g_ref, gain_ref, o_ref, state_scr = refs
    lg = logg_ref[pl.program_id(1)]
    ri = lax.broadcasted_iota(jnp.int32, (CHUNK, CHUNK), 0)
    ci = lax.broadcasted_iota(jnp.int32, (CHUNK, CHUNK), 1)
    diff = (ri - ci).astype(F32)
    dmat = jnp.where(diff >= 0, jnp.exp(jnp.maximum(diff, 0.0) * lg), 0.0)
    pos = lax.broadcasted_iota(jnp.int32, (CHUNK, 1), 0).astype(F32)
    xi = jnp.exp((pos + 1.0) * lg)
    zeta = jnp.exp((CHUNK - 1.0 - pos) * lg)
    g_chunk = jnp.exp(jnp.full((1, 1), float(CHUNK), F32) * lg)

    def advance(state, kc, vc):
        kz = (kc.astype(F32) * zeta).astype(BF16)
        return g_chunk * state + lax.dot_general(
            kz, vc, (((0,), (0,)), ((), ())), preferred_element_type=F32)

    state = jnp.zeros((HEAD_DIM, HEAD_DIM), F32)
    if meta_prefix:
        state = advance(state, km_ref[...], vm_ref[...])
    for c in range(n_chunks):
        state_scr[c] = state.astype(BF16)
        if c + 1 < n_chunks:
            rows = slice(c * CHUNK, (c + 1) * CHUNK)
            state = advance(state, k_ref[rows, :], v_ref[rows, :])

    gain = gain_ref[...]
    for c in range(n_chunks):
        rows = slice(c * CHUNK, (c + 1) * CHUNK)
        qc = q_ref[rows, :]
        kc = k_ref[rows, :]
        vc = v_ref[rows, :]
        s = lax.dot_general(qc, kc, (((1,), (1,)), ((), ())), preferred_element_type=F32) * dmat
        o = (jnp.dot(s.astype(BF16), vc, preferred_element_type=F32)
             + xi * jnp.dot(qc, state_scr[c], preferred_element_type=F32))
        mu = jnp.mean(o, axis=-1, keepdims=True)
        d = o - mu
        var = jnp.mean(d * d, axis=-1, keepdims=True)
        y = (d * lax.rsqrt(var + NORM_EPS)) * gain
        gt = g_ref[rows, :].astype(F32)
        silu = gt / (1.0 + jnp.exp(-gt))
        o_ref[rows, :] = (silu * y).astype(BF16)


def _retention(log_g, proj, proj_meta, gain, *, n_batch, rows_per_batch, meta_prefix):
    nb = n_batch
    blk = lambda sec: pl.BlockSpec((rows_per_batch, HEAD_DIM), lambda b, h, s=sec: (b, s + h))
    in_specs = [pl.BlockSpec(memory_space=pltpu.SMEM),
                blk(SEC_RQ), blk(SEC_RK), blk(SEC_RV), blk(SEC_RG)]
    args = [log_g, proj, proj, proj, proj]
    if meta_prefix:
        mblk = lambda sec: pl.BlockSpec((CHUNK, HEAD_DIM), lambda b, h, s=sec: (0, s + h))
        in_specs += [mblk(SEC_RK), mblk(SEC_RV)]
        args += [proj_meta, proj_meta]
    in_specs.append(pl.BlockSpec((1, HEAD_DIM), lambda b, h: (0, h)))
    args.append(gain)
    return pl.pallas_call(
        functools.partial(_ret_kernel, n_chunks=rows_per_batch // CHUNK, meta_prefix=meta_prefix),
        grid=(nb, N_HEADS),
        in_specs=in_specs,
        out_specs=pl.BlockSpec((rows_per_batch, HEAD_DIM), lambda b, h: (b, h)),
        out_shape=jax.ShapeDtypeStruct((nb * rows_per_batch, RET_DIM), BF16),
        scratch_shapes=[pltpu.VMEM((rows_per_batch // CHUNK, HEAD_DIM, HEAD_DIM), BF16)],
        compiler_params=_params(("arbitrary", "arbitrary"), 32),
        name="retention",
    )(*args)


def _fox_kernel(*refs, n_q, tq, meta_only):
    if meta_only:
        q_ref, km_ref, vm_ref, cq_ref, ckm_ref, o_ref = refs
    else:
        q_ref, k_ref, v_ref, km_ref, vm_ref, cq_ref, ck_ref, ckm_ref, o_ref = refs
    h = pl.program_id(1)
    scale = HEAD_DIM ** -0.5
    lane = lax.broadcasted_iota(jnp.int32, (1, LANES), 1)
    km = km_ref[...]
    vm = vm_ref[...]
    ckm = ckm_ref[0]
    col_m = lax.broadcasted_iota(jnp.int32, (tq, CHUNK), 1)
    row_m = lax.broadcasted_iota(jnp.int32, (tq, CHUNK), 0)
    mask_m = col_m >= META_PAD
    if meta_only:
        mask_m = mask_m & (row_m >= col_m)
    tri = (lax.broadcasted_iota(jnp.int32, (tq, tq), 0)
           >= lax.broadcasted_iota(jnp.int32, (tq, tq), 1))

    def logits(q, kb, ckb, mask):
        s = lax.dot_general(q, kb, (((1,), (1,)), ((), ())), preferred_element_type=F32) * scale
        t = s - ckb
        return t if mask is None else jnp.where(mask, t, MASKED)

    def rowmax(t):
        return jnp.max(t, axis=1, keepdims=True)

    def rowsum(p):
        return jnp.sum(p, axis=1, keepdims=True)

    for qi in range(n_q):
        rows = slice(qi * tq, (qi + 1) * tq)
        n_prev = qi * tq
        q = q_ref[rows, :]
        cq = jnp.sum(jnp.where(lane == h, cq_ref[rows, :], 0.0), axis=1, keepdims=True)
        t_m = logits(q, km, ckm, mask_m)
        mx = rowmax(t_m)
        if not meta_only:
            t_d = logits(q, k_ref[rows, :], ck_ref[0, :, rows], tri)
            mx = jnp.maximum(mx, rowmax(t_d))
            if qi > 0:
                t_o = logits(q, k_ref[0:n_prev, :], ck_ref[0, :, 0:n_prev], None)
                mx = jnp.maximum(mx, rowmax(t_o))
        m_row = mx + cq
        shift = m_row - cq
        p_m = jnp.exp(t_m - shift)
        l = rowsum(p_m)
        acc = jnp.dot(p_m.astype(BF16), vm, preferred_element_type=F32)
        if not meta_only:
            p_d = jnp.exp(t_d - shift)
            l = l + rowsum(p_d)
            acc = acc + jnp.dot(p_d.astype(BF16), v_ref[rows, :], preferred_element_type=F32)
            if qi > 0:
                p_o = jnp.exp(t_o - shift)
                l = l + rowsum(p_o)
                acc = acc + jnp.dot(p_o.astype(BF16), v_ref[0:n_prev, :], preferred_element_type=F32)
        o_ref[rows, :] = (acc / l).astype(BF16)


def _fox(proj, proj_meta, cum_col, ck_rows, ckm_rows, *, n_batch, rows_per_batch, tq, meta_only):
    n_q = rows_per_batch // tq
    qblk = lambda sec: pl.BlockSpec((rows_per_batch, HEAD_DIM), lambda b, h, s=sec: (b, s + h))
    mblk = lambda sec: pl.BlockSpec((CHUNK, HEAD_DIM), lambda b, h, s=sec: (0, s + h))
    cq_spec = pl.BlockSpec((rows_per_batch, LANES), lambda b, h: (b, 0))
    ckm_spec = pl.BlockSpec((1, 1, CHUNK), lambda b, h: (h, 0, 0))
    if meta_only:
        in_specs = [qblk(SEC_FQ), mblk(SEC_FK), mblk(SEC_FV), cq_spec, ckm_spec]
        args = [proj, proj_meta, proj_meta, cum_col, ckm_rows]
    else:
        ck_spec = pl.BlockSpec((1, 1, rows_per_batch), lambda b, h: (b * N_HEADS + h, 0, 0))
        in_specs = [qblk(SEC_FQ), qblk(SEC_FK), qblk(SEC_FV), mblk(SEC_FK), mblk(SEC_FV),
                    cq_spec, ck_spec, ckm_spec]
        args = [proj, proj, proj, proj_meta, proj_meta, cum_col, ck_rows, ckm_rows]
    return pl.pallas_call(
        functools.partial(_fox_kernel, n_q=n_q, tq=tq, meta_only=meta_only),
        grid=(n_batch, N_HEADS),
        in_specs=in_specs,
        out_specs=pl.BlockSpec((rows_per_batch, HEAD_DIM), lambda b, h: (b, h)),
        out_shape=jax.ShapeDtypeStruct((n_batch * rows_per_batch, FOX_DIM), BF16),
        compiler_params=_params(("arbitrary", "arbitrary"), 32),
        name="foxattn",
    )(*args)


def _outproj_kernel(ret_ref, fox_ref, w1_ref, w2_ref, x_ref, g_ref, h_ref, c_ref):
    hcur = (x_ref[...]
            + jnp.dot(ret_ref[...], w1_ref[...], preferred_element_type=F32)
            + jnp.dot(fox_ref[...], w2_ref[...], preferred_element_type=F32))
    h_ref[...] = hcur
    c_ref[...] = _rms(hcur, g_ref[...]).astype(BF16)


def _outproj(ret, fox, w_out, x2d, gain, *, tm):
    m = x2d.shape[0]
    return pl.pallas_call(
        _outproj_kernel,
        grid=(m // tm,),
        in_specs=[
            pl.BlockSpec((tm, RET_DIM), lambda i: (i, 0)),
            pl.BlockSpec((tm, FOX_DIM), lambda i: (i, 0)),
            pl.BlockSpec((RET_DIM, D_MODEL), lambda i: (0, 0)),
            pl.BlockSpec((FOX_DIM, D_MODEL), lambda i: (1, 0)),
            pl.BlockSpec((tm, D_MODEL), lambda i: (i, 0)),
            pl.BlockSpec((1, D_MODEL), lambda i: (0, 0)),
        ],
        out_specs=[
            pl.BlockSpec((tm, D_MODEL), lambda i: (i, 0)),
            pl.BlockSpec((tm, D_MODEL), lambda i: (i, 0)),
        ],
        out_shape=[
            jax.ShapeDtypeStruct((m, D_MODEL), F32),
            jax.ShapeDtypeStruct((m, D_MODEL), BF16),
        ],
        compiler_params=_params(("arbitrary",), 48),
        name="outproj",
    )(ret, fox, w_out, w_out, x2d, gain)


def _up_kernel(c_ref, cm_ref, wg_ref, wv_ref, cwg_ref, cwv_ref, cbg_ref, cbv_ref, o_ref,
               wg_s, wv_s, ug_s, uv_s, *, tm, tiles_per_batch):
    i = pl.program_id(1)

    @pl.when(i == 0)
    def _():
        wg_s[...] = wg_ref[...].astype(BF16)
        wv_s[...] = wv_ref[...].astype(BF16)

    @pl.when(i % tiles_per_batch == 0)
    def _():
        cm = cm_ref[...]
        ug_s[0:8, :] = jnp.dot(cm, wg_s[...], preferred_element_type=F32)[8:16, :]
        uv_s[0:8, :] = jnp.dot(cm, wv_s[...], preferred_element_type=F32)[8:16, :]

    c = c_ref[...]

    def conv(w_s, u_s, cw_ref, cb_ref):
        u_s[8:8 + tm, :] = jnp.dot(c, w_s[...], preferred_element_type=F32)
        y = (cb_ref[...]
             + cw_ref[0:1, :] * u_s[6:6 + tm, :]
             + cw_ref[1:2, :] * u_s[7:7 + tm, :]
             + cw_ref[2:3, :] * u_s[8:8 + tm, :])
        u_s[0:8, :] = u_s[tm:tm + 8, :]
        return y

    gate = conv(wg_s, ug_s, cwg_ref, cbg_ref)
    val = conv(wv_s, uv_s, cwv_ref, cbv_ref)
    o_ref[...] = ((gate / (1.0 + jnp.exp(-gate))) * val).astype(BF16)


def _upconv(c, c_meta, w_up, conv_w, conv_b, *, rows_per_batch, tm, tn):
    m = c.shape[0]
    nj = D_FF // tn
    tiles_per_batch = rows_per_batch // tm
    halo_blk = c_meta.shape[0] // 16 - 1
    return pl.pallas_call(
        functools.partial(_up_kernel, tm=tm, tiles_per_batch=tiles_per_batch),
        grid=(nj, m // tm),
        in_specs=[
            pl.BlockSpec((tm, D_MODEL), lambda j, i: (i, 0)),
            pl.BlockSpec((16, D_MODEL), lambda j, i: (halo_blk, 0)),
            pl.BlockSpec((D_MODEL, tn), lambda j, i: (0, j)),
            pl.BlockSpec((D_MODEL, tn), lambda j, i: (0, j + nj)),
            pl.BlockSpec((3, tn), lambda j, i: (0, j)),
            pl.BlockSpec((3, tn), lambda j, i: (0, j + nj)),
            pl.BlockSpec((1, tn), lambda j, i: (0, j)),
            pl.BlockSpec((1, tn), lambda j, i: (0, j + nj)),
        ],
        out_specs=pl.BlockSpec((tm, tn), lambda j, i: (i, j)),
        out_shape=jax.ShapeDtypeStruct((m, D_FF), BF16),
        scratch_shapes=[
            pltpu.VMEM((D_MODEL, tn), BF16),
            pltpu.VMEM((D_MODEL, tn), BF16),
            pltpu.VMEM((tm + 8, tn), F32),
            pltpu.VMEM((tm + 8, tn), F32),
        ],
        compiler_params=_params(("arbitrary", "arbitrary"), 56),
        name="upconv",
    )(c, c_meta, w_up, w_up, conv_w, conv_w, conv_b, conv_b)


def _down_kernel(a_ref, w_ref, h_ref, g_ref, o_ref, *, n_k):
    k = pl.program_id(1)

    @pl.when(k == 0)
    def _():
        o_ref[...] = h_ref[...]

    o_ref[...] += jnp.dot(a_ref[...], w_ref[...], preferred_element_type=F32)

    @pl.when(k == n_k - 1)
    def _():
        o_ref[...] = _rms(o_ref[...], g_ref[...])


def _down(act, w_down, h1, gain, *, tm, tk):
    m = act.shape[0]
    n_k = D_FF // tk
    return pl.pallas_call(
        functools.partial(_down_kernel, n_k=n_k),
        grid=(m // tm, n_k),
        in_specs=[
            pl.BlockSpec((tm, tk), lambda i, k: (i, k)),
            pl.BlockSpec((tk, D_MODEL), lambda i, k: (k, 0)),
            pl.BlockSpec((tm, D_MODEL), lambda i, k: (i, 0)),
            pl.BlockSpec((1, D_MODEL), lambda i, k: (0, 0)),
        ],
        out_specs=pl.BlockSpec((tm, D_MODEL), lambda i, k: (i, 0)),
        out_shape=jax.ShapeDtypeStruct((m, D_MODEL), F32),
        compiler_params=_params(("arbitrary", "arbitrary"), 48),
        name="downproj",
    )(act, w_down, h1, gain)


def _rotary_tables(pos):
    inv_freq = 1.0 / (ROPE_BASE ** (jnp.arange(0, HEAD_DIM, 2, dtype=F32) / HEAD_DIM))
    ang = pos[:, None] * inv_freq[None, :]
    cos, sin = jnp.cos(ang), jnp.sin(ang)
    return jnp.concatenate([cos, cos], axis=1), jnp.concatenate([-sin, sin], axis=1)


def kernel(x, meta_tokens, norm1_gain, w_in, b_forget, ret_norm_gain, w_out, norm2_gain,
           w_up, conv_w, conv_b, w_down, final_norm_gain):
    n_batch, seq, d_model = x.shape
    assert d_model == D_MODEL and seq % CHUNK == 0 and w_in.shape[0] == 1
    assert meta_tokens.shape == (N_META, D_MODEL)
    x2d = x.reshape(n_batch * seq, D_MODEL)
    xm = jnp.concatenate([jnp.zeros((META_PAD, D_MODEL), F32), meta_tokens.astype(F32)], axis=0)

    w_main = w_in[0, :, :MAIN_COLS].astype(BF16)
    w_f = jnp.pad(w_in[0, :, MAIN_COLS:], ((0, 0), (0, LANES - N_HEADS))).astype(BF16)
    w_out_b = w_out[0].astype(BF16)
    w_down_b = w_down[0].astype(BF16)
    b_row = jnp.pad(b_forget[0], (0, LANES - N_HEADS)).reshape(1, LANES)
    log_g = jnp.log1p(-jnp.exp2(-5.0 - jnp.arange(N_HEADS, dtype=F32)))
    g1 = norm1_gain[0].reshape(1, D_MODEL)
    g2 = norm2_gain[0].reshape(1, D_MODEL)
    gf = final_norm_gain.reshape(1, D_MODEL)
    gr = ret_norm_gain[0].reshape(1, RET_DIM)

    cos_r, sin_r = _rotary_tables(N_META + jnp.arange(seq, dtype=F32))
    cos_m, sin_m = _rotary_tables(jnp.maximum(jnp.arange(CHUNK, dtype=F32) - META_PAD, 0.0))

    proj_m, ff_m = _inproj(xm, g1, w_main, w_f, cos_m, sin_m, tm=CHUNK, tn=512)
    cum_m = _cum(ff_m, b_row, rows_per_batch=CHUNK, valid_from=META_PAD, rel_last=True)
    ckm_rows = cum_m[:, :N_HEADS].T.reshape(N_HEADS, 1, CHUNK)
    ret_m = _retention(log_g, proj_m, None, gr, n_batch=1, rows_per_batch=CHUNK, meta_prefix=False)
    fox_m = _fox(proj_m, proj_m, cum_m, None, ckm_rows, n_batch=1, rows_per_batch=CHUNK,
                 tq=CHUNK, meta_only=True)
    _, c_m = _outproj(ret_m, fox_m, w_out_b, xm, g2, tm=CHUNK)

    tq = 256
    proj, ff = _inproj(x2d, g1, w_main, w_f, cos_r, sin_r, tm=1024, tn=512)
    cum = _cum(ff, b_row, rows_per_batch=seq)
    ck_rows = (cum[:, :N_HEADS].reshape(n_batch, seq, N_HEADS).transpose(0, 2, 1)
               .reshape(n_batch * N_HEADS, 1, seq))
    ret = _retention(log_g, proj, proj_m, gr, n_batch=n_batch, rows_per_batch=seq, meta_prefix=True)
    fox = _fox(proj, proj_m, cum, ck_rows, ckm_rows, n_batch=n_batch, rows_per_batch=seq,
               tq=tq, meta_only=False)
    h1, c = _outproj(ret, fox, w_out_b, x2d, g2, tm=256)
    act = _upconv(c, c_m, w_up[0], conv_w[0], conv_b, rows_per_batch=seq, tm=1024, tn=512)
    out = _down(act, w_down_b, h1, gf, tm=512, tk=1408)
    return out.reshape(n_batch, seq, D_MODEL)
```

```python
import functools

import jax
import jax.numpy as jnp
from jax import lax
from jax.experimental import pallas as pl
from jax.experimental.pallas import tpu as pltpu

F32 = jnp.float32
BF16 = jnp.bfloat16

D_MODEL = 2048
N_META = 16
CHUNK = 128
N_HEADS = 8
HEAD_DIM = 128
RET_DIM = N_HEADS * HEAD_DIM
FOX_DIM = N_HEADS * HEAD_DIM
MAIN_COLS = 4 * RET_DIM + 3 * FOX_DIM
D_FF = 5632
ROPE_BASE = 10000.0
NORM_EPS = 1e-6
META_PAD = CHUNK - N_META
MASKED = -1e30
LANES = 128
MIB = 1024 * 1024
LOG2E = 1.4426950408889634
FOXQ_SCALE = HEAD_DIM ** -0.5 * LOG2E

(SEC_RQ, SEC_RK, SEC_RV, SEC_RG, SEC_FQ, SEC_FK, SEC_FV) = range(7)
BLK = {s: s * N_HEADS for s in range(7)}


def _params(sem, vmem_mib):
    return pltpu.CompilerParams(dimension_semantics=sem, vmem_limit_bytes=vmem_mib * MIB)


def _rms(x, gain):
    ms = jnp.mean(x * x, axis=-1, keepdims=True)
    return (x * lax.rsqrt(ms + NORM_EPS)) * gain


def _row_tiles(tm, sub):
    return [slice(r * sub, (r + 1) * sub) for r in range(tm // sub)]


def _inproj_kernel(*refs, tm, tn, sub, cast_weights):
    if cast_weights:
        x_ref, g_ref, w_ref, wf_ref, cos_ref, sin_ref, o_ref, ff_ref, wq_ref, a_scr = refs
        wq_ref[...] = w_ref[...].astype(BF16)
        w_bf = wq_ref
    else:
        x_ref, g_ref, w_ref, wf_ref, cos_ref, sin_ref, o_ref, ff_ref, a_scr = refs
        w_bf = w_ref
    j = pl.program_id(1)
    sec = j // (RET_DIM // tn)
    tiles = _row_tiles(tm, sub)

    def rotary_store(rs, acc, scale):
        cos = cos_ref[rs, :]
        sin = sin_ref[rs, :]
        for hh in range(tn // LANES):
            cols = slice(hh * LANES, (hh + 1) * LANES)
            t = acc[:, cols]
            y = t * cos + pltpu.roll(t, HEAD_DIM // 2, 1) * sin
            if scale is not None:
                y = y * scale
            o_ref[rs, cols] = y.astype(BF16)

    @pl.when(j == 0)
    def _():
        for rs in tiles:
            a = _rms(x_ref[rs, :], g_ref[...]).astype(BF16)
            a_scr[rs, :] = a
            ff_ref[rs, :] = jnp.dot(a, wf_ref[...], preferred_element_type=F32)
            rotary_store(rs, jnp.dot(a, w_bf[...], preferred_element_type=F32), None)

    @pl.when((j > 0) & (sec <= SEC_RK))
    def _():
        scale = jnp.where(sec == SEC_RK, HEAD_DIM ** -0.5, 1.0).astype(F32)
        for rs in tiles:
            rotary_store(rs, jnp.dot(a_scr[rs, :], w_bf[...], preferred_element_type=F32), scale)

    @pl.when(sec > SEC_RK)
    def _():
        scale = jnp.where(sec == SEC_FQ, FOXQ_SCALE, 1.0).astype(F32)
        for rs in tiles:
            acc = jnp.dot(a_scr[rs, :], w_bf[...], preferred_element_type=F32)
            o_ref[rs, :] = (acc * scale).astype(BF16)


def _inproj(x2d, gain, w, w_f, cos_t, sin_t, *, tm, tn, sub, cast_weights):
    m = x2d.shape[0]
    nb = cos_t.shape[0] // tm
    out_specs = [
        pl.BlockSpec((tm, tn), lambda i, j: (i, j)),
        pl.BlockSpec((tm, LANES), lambda i, j: (i, 0)),
    ]
    out_shape = [
        jax.ShapeDtypeStruct((m, MAIN_COLS), BF16),
        jax.ShapeDtypeStruct((m, LANES), F32),
    ]
    if cast_weights:
        assert m == tm
        out_specs.append(pl.BlockSpec((D_MODEL, tn), lambda i, j: (0, j)))
        out_shape.append(jax.ShapeDtypeStruct((D_MODEL, MAIN_COLS), BF16))
    return pl.pallas_call(
        functools.partial(_inproj_kernel, tm=tm, tn=tn, sub=sub, cast_weights=cast_weights),
        grid=(m // tm, MAIN_COLS // tn),
        in_specs=[
            pl.BlockSpec((tm, D_MODEL), lambda i, j: (i, 0)),
            pl.BlockSpec((1, D_MODEL), lambda i, j: (0, 0)),
            pl.BlockSpec((D_MODEL, tn), lambda i, j: (0, j)),
            pl.BlockSpec((D_MODEL, LANES), lambda i, j: (0, 0)),
            pl.BlockSpec((tm, LANES), lambda i, j: (i % nb, 0)),
            pl.BlockSpec((tm, LANES), lambda i, j: (i % nb, 0)),
        ],
        out_specs=out_specs,
        out_shape=out_shape,
        scratch_shapes=[pltpu.VMEM((tm, D_MODEL), BF16)],
        compiler_params=_params(("arbitrary", "arbitrary"), 56),
        name="inproj",
    )(x2d, gain, w, w_f, cos_t, sin_t)


def _cum_kernel(ff_ref, b_ref, o_ref, *, n_blk, valid_from, rel_last):
    row = lax.broadcasted_iota(jnp.int32, (CHUNK, CHUNK), 0)
    col = lax.broadcasted_iota(jnp.int32, (CHUNK, CHUNK), 1)
    tri = (row >= col).astype(BF16)
    rows = lax.broadcasted_iota(jnp.int32, (CHUNK, LANES), 0)
    carry = jnp.zeros((1, LANES), F32)
    for blk in range(n_blk):
        z = ff_ref[blk * CHUNK:(blk + 1) * CHUNK, :] + b_ref[...]
        lf = jnp.minimum(z, 0.0) - jnp.log1p(jnp.exp(-jnp.abs(z)))
        if valid_from:
            lf = jnp.where(rows >= valid_from, lf, 0.0)
        hi = lf.astype(BF16)
        r1 = lf - hi.astype(F32)
        mid = r1.astype(BF16)
        lo = (r1 - mid.astype(F32)).astype(BF16)
        cum = (jnp.dot(tri, hi, preferred_element_type=F32)
               + jnp.dot(tri, mid, preferred_element_type=F32)
               + jnp.dot(tri, lo, preferred_element_type=F32)) + carry
        o_ref[blk * CHUNK:(blk + 1) * CHUNK, :] = cum
        carry = cum[CHUNK - 1:CHUNK, :]
    if rel_last:
        o_ref[...] = o_ref[...] - carry


def _cum(ff, b_row, *, rows_per_batch, valid_from=0, rel_last=False):
    m = ff.shape[0]
    return pl.pallas_call(
        functools.partial(_cum_kernel, n_blk=rows_per_batch // CHUNK,
                          valid_from=valid_from, rel_last=rel_last),
        grid=(m // rows_per_batch,),
        in_specs=[
            pl.BlockSpec((rows_per_batch, LANES), lambda b: (b, 0)),
            pl.BlockSpec((1, LANES), lambda b: (0, 0)),
        ],
        out_specs=pl.BlockSpec((rows_per_batch, LANES), lambda b: (b, 0)),
        out_shape=jax.ShapeDtypeStruct((m, LANES), F32),
        compiler_params=_params(("arbitrary",), 32),
        name="cumgate",
    )(ff, b_row)


def _ret_kernel(*refs, n_chunks, meta_prefix):
    if meta_prefix:
        logg_ref, q_ref, k_ref, v_ref, g_ref, km_ref, vm_ref, gain_ref, o_ref, state_scr = refs
    else:
        logg_ref, q_ref, k_ref, v_ref, g_ref, gain_ref, o_ref, state_scr = refs
    lg = logg_ref[pl.program_id(1)]
    ri = lax.broadcasted_iota(jnp.int32, (CHUNK, CHUNK), 0)
    ci = lax.broadcasted_iota(jnp.int32, (CHUNK, CHUNK), 1)
    diff = (ri - ci).astype(F32)
    dmat = jnp.where(diff >= 0, jnp.exp(jnp.maximum(diff, 0.0) * lg), 0.0)
    pos = lax.broadcasted_iota(jnp.int32, (CHUNK, 1), 0).astype(F32)
    xi = jnp.exp((pos + 1.0) * lg)
    zeta = jnp.exp((CHUNK - 1.0 - pos) * lg)
    g_chunk = jnp.exp(jnp.full((1, 1), float(CHUNK), F32) * lg)

    def advance(state, kc, vc):
        kz = (kc.astype(F32) * zeta).astype(BF16)
        return g_chunk * state + lax.dot_general(
            kz, vc, (((0,), (0,)), ((), ())), preferred_element_type=F32)

    state = jnp.zeros((HEAD_DIM, HEAD_DIM), F32)
    if meta_prefix:
        state = advance(state, km_ref[...], vm_ref[...])
    for c in range(n_chunks):
        state_scr[c] = state.astype(BF16)
        if c + 1 < n_chunks:
            rows = slice(c * CHUNK, (c + 1) * CHUNK)
            state = advance(state, k_ref[rows, :], v_ref[rows, :])

    gain = gain_ref[...]
    for c in range(n_chunks):
        rows = slice(c * CHUNK, (c + 1) * CHUNK)
        qc = q_ref[rows, :]
        kc = k_ref[rows, :]
        vc = v_ref[rows, :]
        s = lax.dot_general(qc, kc, (((1,), (1,)), ((), ())), preferred_element_type=F32) * dmat
        o = (jnp.dot(s.astype(BF16), vc, preferred_element_type=F32)
             + xi * jnp.dot(qc, state_scr[c], preferred_element_type=F32))
        mu = jnp.mean(o, axis=-1, keepdims=True)
        d = o - mu
        var = jnp.mean(d * d, axis=-1, keepdims=True)
        y = (d * lax.rsqrt(var + NORM_EPS)) * gain
        gt = g_ref[rows, :].astype(F32)
        silu = gt / (1.0 + jnp.exp(-gt))
        o_ref[rows, :] = (silu * y).astype(BF16)


def _retention(log_g, proj, proj_meta, gain, *, n_batch, rows_per_batch, meta_prefix):
    blk = lambda sec: pl.BlockSpec((rows_per_batch, HEAD_DIM), lambda b, h, s=BLK[sec]: (b, s + h))
    in_specs = [pl.BlockSpec(memory_space=pltpu.SMEM),
                blk(SEC_RQ), blk(SEC_RK), blk(SEC_RV), blk(SEC_RG)]
    args = [log_g, proj, proj, proj, proj]
    if meta_prefix:
        mblk = lambda sec: pl.BlockSpec((CHUNK, HEAD_DIM), lambda b, h, s=BLK[sec]: (0, s + h))
        in_specs += [mblk(SEC_RK), mblk(SEC_RV)]
        args += [proj_meta, proj_meta]
    in_specs.append(pl.BlockSpec((1, HEAD_DIM), lambda b, h: (0, h)))
    args.append(gain)
    return pl.pallas_call(
        functools.partial(_ret_kernel, n_chunks=rows_per_batch // CHUNK, meta_prefix=meta_prefix),
        grid=(n_batch, N_HEADS),
        in_specs=in_specs,
        out_specs=pl.BlockSpec((rows_per_batch, HEAD_DIM), lambda b, h: (b, h)),
        out_shape=jax.ShapeDtypeStruct((n_batch * rows_per_batch, RET_DIM), BF16),
        scratch_shapes=[pltpu.VMEM((rows_per_batch // CHUNK, HEAD_DIM, HEAD_DIM), BF16)],
        compiler_params=_params(("arbitrary", "arbitrary"), 32),
        name="retention",
    )(*args)


def _fox_kernel(*refs, n_q, tq, meta_only):
    if meta_only:
        q_ref, km_ref, vm_ref, cq_ref, ckm_ref, o_ref = refs
    else:
        q_ref, k_ref, v_ref, km_ref, vm_ref, cq_ref, ck_ref, ckm_ref, o_ref = refs
        ck2 = ck_ref[0] * LOG2E
    h = pl.program_id(1)
    lane = lax.broadcasted_iota(jnp.int32, (1, LANES), 1)
    km = km_ref[...]
    vm = vm_ref[...]
    ckm2 = ckm_ref[0] * LOG2E
    col_m = lax.broadcasted_iota(jnp.int32, (tq, CHUNK), 1)
    row_m = lax.broadcasted_iota(jnp.int32, (tq, CHUNK), 0)
    mask_m = col_m >= META_PAD
    if meta_only:
        mask_m = mask_m & (row_m >= col_m)
    tri = (lax.broadcasted_iota(jnp.int32, (tq, tq), 0)
           >= lax.broadcasted_iota(jnp.int32, (tq, tq), 1))

    def logits(q, kb, ckb, mask):
        t = lax.dot_general(q, kb, (((1,), (1,)), ((), ())), preferred_element_type=F32) - ckb
        return t if mask is None else jnp.where(mask, t, MASKED)

    def rowmax(t):
        return jnp.max(t, axis=1, keepdims=True)

    def rowsum(p):
        return jnp.sum(p, axis=1, keepdims=True)

    for qi in range(n_q):
        rows = slice(qi * tq, (qi + 1) * tq)
        n_prev = qi * tq
        q = q_ref[rows, :]
        cq2 = LOG2E * jnp.sum(jnp.where(lane == h, cq_ref[rows, :], 0.0), axis=1, keepdims=True)
        t_m = logits(q, km, ckm2, mask_m)
        mx = rowmax(t_m)
        if not meta_only:
            t_d = logits(q, k_ref[rows, :], ck2[:, rows], tri)
            mx = jnp.maximum(mx, rowmax(t_d))
            if qi > 0:
                t_o = logits(q, k_ref[0:n_prev, :], ck2[:, 0:n_prev], None)
                mx = jnp.maximum(mx, rowmax(t_o))
        m_row = mx + cq2
        shift = m_row - cq2
        p_m = jnp.exp2(t_m - shift)
        l = rowsum(p_m)
        acc = jnp.dot(p_m.astype(BF16), vm, preferred_element_type=F32)
        if not meta_only:
            p_d = jnp.exp2(t_d - shift)
            l = l + rowsum(p_d)
            acc = acc + jnp.dot(p_d.astype(BF16), v_ref[rows, :], preferred_element_type=F32)
            if qi > 0:
                p_o = jnp.exp2(t_o - shift)
                l = l + rowsum(p_o)
                acc = acc + jnp.dot(p_o.astype(BF16), v_ref[0:n_prev, :], preferred_element_type=F32)
        o_ref[rows, :] = (acc / l).astype(BF16)


def _fox(proj, proj_meta, cum_col, ck_rows, ckm_rows, *, n_batch, rows_per_batch, tq, meta_only):
    n_q = rows_per_batch // tq
    qblk = lambda sec: pl.BlockSpec((rows_per_batch, HEAD_DIM), lambda b, h, s=BLK[sec]: (b, s + h))
    mblk = lambda sec: pl.BlockSpec((CHUNK, HEAD_DIM), lambda b, h, s=BLK[sec]: (0, s + h))
    cq_spec = pl.BlockSpec((rows_per_batch, LANES), lambda b, h: (b, 0))
    ckm_spec = pl.BlockSpec((1, 1, CHUNK), lambda b, h: (h, 0, 0))
    if meta_only:
        in_specs = [qblk(SEC_FQ), mblk(SEC_FK), mblk(SEC_FV), cq_spec, ckm_spec]
        args = [proj, proj_meta, proj_meta, cum_col, ckm_rows]
    else:
        ck_spec = pl.BlockSpec((1, 1, rows_per_batch), lambda b, h: (b * N_HEADS + h, 0, 0))
        in_specs = [qblk(SEC_FQ), qblk(SEC_FK), qblk(SEC_FV), mblk(SEC_FK), mblk(SEC_FV),
                    cq_spec, ck_spec, ckm_spec]
        args = [proj, proj, proj, proj_meta, proj_meta, cum_col, ck_rows, ckm_rows]
    return pl.pallas_call(
        functools.partial(_fox_kernel, n_q=n_q, tq=tq, meta_only=meta_only),
        grid=(n_batch, N_HEADS),
        in_specs=in_specs,
        out_specs=pl.BlockSpec((rows_per_batch, HEAD_DIM), lambda b, h: (b, h)),
        out_shape=jax.ShapeDtypeStruct((n_batch * rows_per_batch, FOX_DIM), BF16),
        compiler_params=_params(("arbitrary", "arbitrary"), 32),
        name="foxattn",
    )(*args)


def _outproj_kernel(ret_ref, fox_ref, w1_ref, w2_ref, x_ref, g_ref, h_ref, c_ref, *, tm, sub):
    for rs in _row_tiles(tm, sub):
        hcur = (x_ref[rs, :]
                + jnp.dot(ret_ref[rs, :], w1_ref[...], preferred_element_type=F32)
                + jnp.dot(fox_ref[rs, :], w2_ref[...], preferred_element_type=F32))
        h_ref[rs, :] = hcur
        c_ref[rs, :] = _rms(hcur, g_ref[...]).astype(BF16)


def _outproj(ret, fox, w_out_b, x2d, gain, *, tm, sub):
    m = x2d.shape[0]
    return pl.pallas_call(
        functools.partial(_outproj_kernel, tm=tm, sub=sub),
        grid=(m // tm,),
        in_specs=[
            pl.BlockSpec((tm, RET_DIM), lambda i: (i, 0)),
            pl.BlockSpec((tm, FOX_DIM), lambda i: (i, 0)),
            pl.BlockSpec((RET_DIM, D_MODEL), lambda i: (0, 0)),
            pl.BlockSpec((FOX_DIM, D_MODEL), lambda i: (1, 0)),
            pl.BlockSpec((tm, D_MODEL), lambda i: (i, 0)),
            pl.BlockSpec((1, D_MODEL), lambda i: (0, 0)),
        ],
        out_specs=[
            pl.BlockSpec((tm, D_MODEL), lambda i: (i, 0)),
            pl.BlockSpec((tm, D_MODEL), lambda i: (i, 0)),
        ],
        out_shape=[
            jax.ShapeDtypeStruct((m, D_MODEL), F32),
            jax.ShapeDtypeStruct((m, D_MODEL), BF16),
        ],
        compiler_params=_params(("arbitrary",), 56),
        name="outproj",
    )(ret, fox, w_out_b, w_out_b, x2d, gain)


def _outproj_meta_kernel(mix_ref, w_ref, x_ref, g_ref, c_ref, wq_ref, acc_scr):
    k = pl.program_id(0)
    wq_ref[...] = w_ref[...].astype(BF16)
    part = jnp.dot(mix_ref[...], wq_ref[...], preferred_element_type=F32)

    @pl.when(k == 0)
    def _():
        acc_scr[...] = x_ref[...] + part

    @pl.when(k == 1)
    def _():
        c_ref[...] = _rms(acc_scr[...] + part, g_ref[...]).astype(BF16)


def _outproj_meta(mix_m, w_out, xm, gain):
    half = D_MODEL // 2
    return pl.pallas_call(
        _outproj_meta_kernel,
        grid=(2,),
        in_specs=[
            pl.BlockSpec((CHUNK, half), lambda k: (0, k)),
            pl.BlockSpec((half, D_MODEL), lambda k: (k, 0)),
            pl.BlockSpec((CHUNK, D_MODEL), lambda k: (0, 0)),
            pl.BlockSpec((1, D_MODEL), lambda k: (0, 0)),
        ],
        out_specs=[
            pl.BlockSpec((CHUNK, D_MODEL), lambda k: (0, 0)),
            pl.BlockSpec((half, D_MODEL), lambda k: (k, 0)),
        ],
        out_shape=[
            jax.ShapeDtypeStruct((CHUNK, D_MODEL), BF16),
            jax.ShapeDtypeStruct((D_MODEL, D_MODEL), BF16),
        ],
        scratch_shapes=[pltpu.VMEM((CHUNK, D_MODEL), F32)],
        compiler_params=_params(("arbitrary",), 48),
        name="outproj_meta",
    )(mix_m, w_out, xm, gain)


def _up_kernel(c_ref, cm_ref, wg_ref, wv_ref, cwg_ref, cwv_ref, cbg_ref, cbv_ref, o_ref,
               wg_s, wv_s, ug_s, uv_s, *, tm, sub, tiles_per_batch):
    i = pl.program_id(1)

    @pl.when(i == 0)
    def _():
        wg_s[...] = wg_ref[...].astype(BF16)
        wv_s[...] = wv_ref[...].astype(BF16)

    @pl.when(i % tiles_per_batch == 0)
    def _():
        cm = cm_ref[...]
        ug_s[0:8, :] = jnp.dot(cm, wg_s[...], preferred_element_type=F32)[8:16, :]
        uv_s[0:8, :] = jnp.dot(cm, wv_s[...], preferred_element_type=F32)[8:16, :]

    def conv(c, r0, w_s, u_s, cw_ref, cb_ref):
        u_s[8 + r0:8 + r0 + sub, :] = jnp.dot(c, w_s[...], preferred_element_type=F32)
        return (cb_ref[...]
                + cw_ref[0:1, :] * u_s[6 + r0:6 + r0 + sub, :]
                + cw_ref[1:2, :] * u_s[7 + r0:7 + r0 + sub, :]
                + cw_ref[2:3, :] * u_s[8 + r0:8 + r0 + sub, :])

    for rs in _row_tiles(tm, sub):
        c = c_ref[rs, :]
        gate = conv(c, rs.start, wg_s, ug_s, cwg_ref, cbg_ref)
        val = conv(c, rs.start, wv_s, uv_s, cwv_ref, cbv_ref)
        o_ref[rs, :] = ((gate / (1.0 + jnp.exp(-gate))) * val).astype(BF16)
    ug_s[0:8, :] = ug_s[tm:tm + 8, :]
    uv_s[0:8, :] = uv_s[tm:tm + 8, :]


def _upconv(c, c_meta, w_up, conv_w, conv_b, *, rows_per_batch, tm, tn, sub):
    m = c.shape[0]
    nj = D_FF // tn
    tiles_per_batch = rows_per_batch // tm
    halo_blk = c_meta.shape[0] // 16 - 1
    return pl.pallas_call(
        functools.partial(_up_kernel, tm=tm, sub=sub, tiles_per_batch=tiles_per_batch),
        grid=(nj, m // tm),
        in_specs=[
            pl.BlockSpec((tm, D_MODEL), lambda j, i: (i, 0)),
            pl.BlockSpec((16, D_MODEL), lambda j, i: (halo_blk, 0)),
            pl.BlockSpec((D_MODEL, tn), lambda j, i: (0, j)),
            pl.BlockSpec((D_MODEL, tn), lambda j, i: (0, j + nj)),
            pl.BlockSpec((3, tn), lambda j, i: (0, j)),
            pl.BlockSpec((3, tn), lambda j, i: (0, j + nj)),
            pl.BlockSpec((1, tn), lambda j, i: (0, j)),
            pl.BlockSpec((1, tn), lambda j, i: (0, j + nj)),
        ],
        out_specs=pl.BlockSpec((tm, tn), lambda j, i: (i, j)),
        out_shape=jax.ShapeDtypeStruct((m, D_FF), BF16),
        scratch_shapes=[
            pltpu.VMEM((D_MODEL, tn), BF16),
            pltpu.VMEM((D_MODEL, tn), BF16),
            pltpu.VMEM((tm + 8, tn), F32),
            pltpu.VMEM((tm + 8, tn), F32),
        ],
        compiler_params=_params(("arbitrary", "arbitrary"), 56),
        name="upconv",
    )(c, c_meta, w_up, w_up, conv_w, conv_w, conv_b, conv_b)


def _down_kernel(a_ref, w_ref, h_ref, g_ref, o_ref, *, tm, sub):
    for rs in _row_tiles(tm, sub):
        hcur = h_ref[rs, :] + jnp.dot(a_ref[rs, :], w_ref[...], preferred_element_type=F32)
        o_ref[rs, :] = _rms(hcur, g_ref[...])


def _down(act, w_down_b, h1, gain, *, tm, sub):
    m = act.shape[0]
    return pl.pallas_call(
        functools.partial(_down_kernel, tm=tm, sub=sub),
        grid=(m // tm,),
        in_specs=[
            pl.BlockSpec((tm, D_FF), lambda i: (i, 0)),
            pl.BlockSpec((D_FF, D_MODEL), lambda i: (0, 0), pipeline_mode=pl.Buffered(1)),
            pl.BlockSpec((tm, D_MODEL), lambda i: (i, 0)),
            pl.BlockSpec((1, D_MODEL), lambda i: (0, 0)),
        ],
        out_specs=pl.BlockSpec((tm, D_MODEL), lambda i: (i, 0)),
        out_shape=jax.ShapeDtypeStruct((m, D_MODEL), F32),
        compiler_params=_params(("arbitrary",), 58),
        name="downproj",
    )(act, w_down_b, h1, gain)


def _rotary_tables(pos):
    inv_freq = 1.0 / (ROPE_BASE ** (jnp.arange(0, HEAD_DIM, 2, dtype=F32) / HEAD_DIM))
    ang = pos[:, None] * inv_freq[None, :]
    cos, sin = jnp.cos(ang), jnp.sin(ang)
    return jnp.concatenate([cos, cos], axis=1), jnp.concatenate([-sin, sin], axis=1)


def kernel(x, meta_tokens, norm1_gain, w_in, b_forget, ret_norm_gain, w_out, norm2_gain,
           w_up, conv_w, conv_b, w_down, final_norm_gain):
    n_batch, seq, d_model = x.shape
    assert d_model == D_MODEL and seq % CHUNK == 0 and w_in.shape[0] == 1
    assert meta_tokens.shape == (N_META, D_MODEL)
    x2d = x.reshape(n_batch * seq, D_MODEL)
    xm = jnp.concatenate([jnp.zeros((META_PAD, D_MODEL), F32), meta_tokens.astype(F32)], axis=0)

    w_f = jnp.pad(w_in[0, :, MAIN_COLS:], ((0, 0), (0, LANES - N_HEADS))).astype(BF16)
    w_down_b = w_down[0].astype(BF16)
    b_row = jnp.pad(b_forget[0], (0, LANES - N_HEADS)).reshape(1, LANES)
    log_g = jnp.log1p(-jnp.exp2(-5.0 - jnp.arange(N_HEADS, dtype=F32)))
    g1 = norm1_gain[0].reshape(1, D_MODEL)
    g2 = norm2_gain[0].reshape(1, D_MODEL)
    gf = final_norm_gain.reshape(1, D_MODEL)
    gr = ret_norm_gain[0].reshape(1, RET_DIM)

    cos_r, sin_r = _rotary_tables(N_META + jnp.arange(seq, dtype=F32))
    cos_m, sin_m = _rotary_tables(jnp.maximum(jnp.arange(CHUNK, dtype=F32) - META_PAD, 0.0))

    proj_m, ff_m, w_main_b = _inproj(xm, g1, w_in[0], w_f, cos_m, sin_m,
                                     tm=CHUNK, tn=512, sub=CHUNK, cast_weights=True)
    cum_m = _cum(ff_m, b_row, rows_per_batch=CHUNK, valid_from=META_PAD, rel_last=True)
    ckm_rows = cum_m[:, :N_HEADS].T.reshape(N_HEADS, 1, CHUNK)
    ret_m = _retention(log_g, proj_m, None, gr, n_batch=1, rows_per_batch=CHUNK, meta_prefix=False)
    fox_m = _fox(proj_m, proj_m, cum_m, None, ckm_rows, n_batch=1, rows_per_batch=CHUNK,
                 tq=CHUNK, meta_only=True)
    c_m, w_out_b = _outproj_meta(jnp.concatenate([ret_m, fox_m], axis=1), w_out[0], xm, g2)

    proj, ff = _inproj(x2d, g1, w_main_b, w_f, cos_r, sin_r,
                       tm=1024, tn=1024, sub=256, cast_weights=False)
    cum = _cum(ff, b_row, rows_per_batch=seq)
    ck_rows = (cum[:, :N_HEADS].reshape(n_batch, seq, N_HEADS).transpose(0, 2, 1)
               .reshape(n_batch * N_HEADS, 1, seq))
    ret = _retention(log_g, proj, proj_m, gr, n_batch=n_batch, rows_per_batch=seq, meta_prefix=True)
    fox = _fox(proj, proj_m, cum, ck_rows, ckm_rows, n_batch=n_batch, rows_per_batch=seq,
               tq=256, meta_only=False)
    h1, c = _outproj(ret, fox, w_out_b, x2d, g2, tm=512, sub=256)
    act = _upconv(c, c_m, w_up[0], conv_w[0], conv_b, rows_per_batch=seq, tm=1024, tn=512, sub=256)
    out = _down(act, w_down_b, h1, gf, tm=512, sub=256)
    return out.reshape(n_batch, seq, D_MODEL)
```

```python
import functools

import jax
import jax.numpy as jnp
from jax import lax
from jax.experimental import pallas as pl
from jax.experimental.pallas import tpu as pltpu

F32 = jnp.float32
BF16 = jnp.bfloat16

D_MODEL = 2048
N_META = 16
CHUNK = 128
N_HEADS = 8
HEAD_DIM = 128
RET_DIM = N_HEADS * HEAD_DIM
FOX_DIM = N_HEADS * HEAD_DIM
MAIN_COLS = 4 * RET_DIM + 3 * FOX_DIM
D_FF = 5632
ROPE_BASE = 10000.0
NORM_EPS = 1e-6
META_PAD = CHUNK - N_META
MASKED = -1e30
LANES = 128
MIB = 1024 * 1024
LOG2E = 1.4426950408889634
FOXQ_SCALE = HEAD_DIM ** -0.5 * LOG2E

(SEC_RQ, SEC_RK, SEC_RV, SEC_RG, SEC_FQ, SEC_FK, SEC_FV) = range(7)
BLK = {s: s * N_HEADS for s in range(7)}


def _params(sem, vmem_mib):
    return pltpu.CompilerParams(dimension_semantics=sem, vmem_limit_bytes=vmem_mib * MIB)


def _rms(x, gain):
    ms = jnp.mean(x * x, axis=-1, keepdims=True)
    return (x * lax.rsqrt(ms + NORM_EPS)) * gain


def _row_tiles(tm, sub):
    return [slice(r * sub, (r + 1) * sub) for r in range(tm // sub)]


def _inproj_kernel(*refs, tm, tn, sub, cast_weights):
    j = pl.program_id(1)
    if cast_weights:
        x_ref, g_ref, w_ref, wf8_ref, cos_ref, sin_ref, o_ref, ff_ref, wq_ref, wf_ref, a_scr = refs
        wq_ref[...] = w_ref[...].T.astype(BF16)
        w_bf = wq_ref

        @pl.when(j == 0)
        def _():
            pad = jnp.zeros((LANES - N_HEADS, D_MODEL), F32)
            wf_ref[...] = jnp.concatenate([wf8_ref[...], pad], axis=0).astype(BF16)
    else:
        x_ref, g_ref, w_ref, wf_ref, cos_ref, sin_ref, o_ref, ff_ref, a_scr = refs
        w_bf = w_ref
    sec = j // (RET_DIM // tn)
    tiles = _row_tiles(tm, sub)

    def rotary_store(rs, acc, scale):
        cos = cos_ref[rs, :]
        sin = sin_ref[rs, :]
        for hh in range(tn // LANES):
            cols = slice(hh * LANES, (hh + 1) * LANES)
            t = acc[:, cols]
            y = t * cos + pltpu.roll(t, HEAD_DIM // 2, 1) * sin
            if scale is not None:
                y = y * scale
            o_ref[rs, cols] = y.astype(BF16)

    @pl.when(j == 0)
    def _():
        for rs in tiles:
            a = _rms(x_ref[rs, :], g_ref[...]).astype(BF16)
            a_scr[rs, :] = a
            ff_ref[rs, :] = lax.dot_general(a, wf_ref[...], (((1,), (1,)), ((), ())),
                                            preferred_element_type=F32)
            rotary_store(rs, jnp.dot(a, w_bf[...], preferred_element_type=F32), None)

    @pl.when((j > 0) & (sec <= SEC_RK))
    def _():
        scale = jnp.where(sec == SEC_RK, HEAD_DIM ** -0.5, 1.0).astype(F32)
        for rs in tiles:
            rotary_store(rs, jnp.dot(a_scr[rs, :], w_bf[...], preferred_element_type=F32), scale)

    @pl.when(sec > SEC_RK)
    def _():
        scale = jnp.where(sec == SEC_FQ, FOXQ_SCALE, 1.0).astype(F32)
        for rs in tiles:
            acc = jnp.dot(a_scr[rs, :], w_bf[...], preferred_element_type=F32)
            o_ref[rs, :] = (acc * scale).astype(BF16)


def _inproj(x2d, gain, w, w_f, cos_t, sin_t, *, tm, tn, sub, cast_weights):
    m = x2d.shape[0]
    nb = cos_t.shape[0] // tm
    out_specs = [
        pl.BlockSpec((tm, tn), lambda i, j: (i, j)),
        pl.BlockSpec((tm, LANES), lambda i, j: (i, 0)),
    ]
    out_shape = [
        jax.ShapeDtypeStruct((m, MAIN_COLS), BF16),
        jax.ShapeDtypeStruct((m, LANES), F32),
    ]
    if cast_weights:
        assert m == tm
        out_specs += [pl.BlockSpec((D_MODEL, tn), lambda i, j: (0, j)),
                      pl.BlockSpec((LANES, D_MODEL), lambda i, j: (0, 0))]
        out_shape += [jax.ShapeDtypeStruct((D_MODEL, MAIN_COLS), BF16),
                      jax.ShapeDtypeStruct((LANES, D_MODEL), BF16)]
    return pl.pallas_call(
        functools.partial(_inproj_kernel, tm=tm, tn=tn, sub=sub, cast_weights=cast_weights),
        grid=(m // tm, MAIN_COLS // tn),
        in_specs=[
            pl.BlockSpec((tm, D_MODEL), lambda i, j: (i, 0)),
            pl.BlockSpec((1, D_MODEL), lambda i, j: (0, 0)),
            (pl.BlockSpec((tn, D_MODEL), lambda i, j: (j, 0)) if cast_weights
             else pl.BlockSpec((D_MODEL, tn), lambda i, j: (0, j))),
            (pl.BlockSpec((N_HEADS, D_MODEL), lambda i, j: (MAIN_COLS // N_HEADS, 0)) if cast_weights
             else pl.BlockSpec((LANES, D_MODEL), lambda i, j: (0, 0))),
            pl.BlockSpec((tm, LANES), lambda i, j: (i % nb, 0)),
            pl.BlockSpec((tm, LANES), lambda i, j: (i % nb, 0)),
        ],
        out_specs=out_specs,
        out_shape=out_shape,
        scratch_shapes=[pltpu.VMEM((tm, D_MODEL), BF16)],
        compiler_params=_params(("arbitrary", "arbitrary"), 56),
        name="inproj",
    )(x2d, gain, w, w_f, cos_t, sin_t)


def _cum_kernel(ff_ref, b_ref, o_ref, *, n_blk, valid_from, rel_last):
    row = lax.broadcasted_iota(jnp.int32, (CHUNK, CHUNK), 0)
    col = lax.broadcasted_iota(jnp.int32, (CHUNK, CHUNK), 1)
    tri = (row >= col).astype(BF16)
    rows = lax.broadcasted_iota(jnp.int32, (CHUNK, LANES), 0)
    carry = jnp.zeros((1, LANES), F32)
    for blk in range(n_blk):
        z = ff_ref[blk * CHUNK:(blk + 1) * CHUNK, :] + b_ref[...]
        lf = jnp.minimum(z, 0.0) - jnp.log1p(jnp.exp(-jnp.abs(z)))
        if valid_from:
            lf = jnp.where(rows >= valid_from, lf, 0.0)
        hi = lf.astype(BF16)
        r1 = lf - hi.astype(F32)
        mid = r1.astype(BF16)
        lo = (r1 - mid.astype(F32)).astype(BF16)
        cum = (jnp.dot(tri, hi, preferred_element_type=F32)
               + jnp.dot(tri, mid, preferred_element_type=F32)
               + jnp.dot(tri, lo, preferred_element_type=F32)) + carry
        o_ref[blk * CHUNK:(blk + 1) * CHUNK, :] = cum
        carry = cum[CHUNK - 1:CHUNK, :]
    if rel_last:
        o_ref[...] = o_ref[...] - carry


def _cum(ff, b_row, *, rows_per_batch, valid_from=0, rel_last=False):
    m = ff.shape[0]
    return pl.pallas_call(
        functools.partial(_cum_kernel, n_blk=rows_per_batch // CHUNK,
                          valid_from=valid_from, rel_last=rel_last),
        grid=(m // rows_per_batch,),
        in_specs=[
            pl.BlockSpec((rows_per_batch, LANES), lambda b: (b, 0)),
            pl.BlockSpec((1, LANES), lambda b: (0, 0)),
        ],
        out_specs=pl.BlockSpec((rows_per_batch, LANES), lambda b: (b, 0)),
        out_shape=jax.ShapeDtypeStruct((m, LANES), F32),
        compiler_params=_params(("arbitrary",), 32),
        name="cumgate",
    )(ff, b_row)


def _ret_kernel(*refs, n_chunks, meta_prefix):
    if meta_prefix:
        logg_ref, q_ref, k_ref, v_ref, g_ref, km_ref, vm_ref, gain_ref, o_ref, state_scr = refs
    else:
        logg_ref, q_ref, k_ref, v_ref, g_ref, gain_ref, o_ref, state_scr = refs
    lg = logg_ref[pl.program_id(1)]
    ri = lax.broadcasted_iota(jnp.int32, (CHUNK, CHUNK), 0)
    ci = lax.broadcasted_iota(jnp.int32, (CHUNK, CHUNK), 1)
    diff = (ri - ci).astype(F32)
    dmat = jnp.where(diff >= 0, jnp.exp(jnp.maximum(diff, 0.0) * lg), 0.0)
    pos = lax.broadcasted_iota(jnp.int32, (CHUNK, 1), 0).astype(F32)
    xi = jnp.exp((pos + 1.0) * lg)
    zeta = jnp.exp((CHUNK - 1.0 - pos) * lg)
    g_chunk = jnp.exp(jnp.full((1, 1), float(CHUNK), F32) * lg)

    def advance(state, kc, vc):
        kz = (kc.astype(F32) * zeta).astype(BF16)
        return g_chunk * state + lax.dot_general(
            kz, vc, (((0,), (0,)), ((), ())), preferred_element_type=F32)

    state = jnp.zeros((HEAD_DIM, HEAD_DIM), F32)
    if meta_prefix:
        state = advance(state, km_ref[...], vm_ref[...])
    for c in range(n_chunks):
        state_scr[c] = state.astype(BF16)
        if c + 1 < n_chunks:
            rows = slice(c * CHUNK, (c + 1) * CHUNK)
            state = advance(state, k_ref[rows, :], v_ref[rows, :])

    gain = gain_ref[...]
    for c in range(n_chunks):
        rows = slice(c * CHUNK, (c + 1) * CHUNK)
        qc = q_ref[rows, :]
        kc = k_ref[rows, :]
        vc = v_ref[rows, :]
        s = lax.dot_general(qc, kc, (((1,), (1,)), ((), ())), preferred_element_type=F32) * dmat
        o = (jnp.dot(s.astype(BF16), vc, preferred_element_type=F32)
             + xi * jnp.dot(qc, state_scr[c], preferred_element_type=F32))
        mu = jnp.mean(o, axis=-1, keepdims=True)
        d = o - mu
        var = jnp.mean(d * d, axis=-1, keepdims=True)
        y = (d * lax.rsqrt(var + NORM_EPS)) * gain
        gt = g_ref[rows, :].astype(F32)
        silu = gt / (1.0 + jnp.exp(-gt))
        o_ref[rows, :] = (silu * y).astype(BF16)


def _retention(log_g, proj, proj_meta, gain, *, n_batch, rows_per_batch, meta_prefix):
    blk = lambda sec: pl.BlockSpec((rows_per_batch, HEAD_DIM), lambda b, h, s=BLK[sec]: (b, s + h))
    in_specs = [pl.BlockSpec(memory_space=pltpu.SMEM),
                blk(SEC_RQ), blk(SEC_RK), blk(SEC_RV), blk(SEC_RG)]
    args = [log_g, proj, proj, proj, proj]
    if meta_prefix:
        mblk = lambda sec: pl.BlockSpec((CHUNK, HEAD_DIM), lambda b, h, s=BLK[sec]: (0, s + h))
        in_specs += [mblk(SEC_RK), mblk(SEC_RV)]
        args += [proj_meta, proj_meta]
    in_specs.append(pl.BlockSpec((1, HEAD_DIM), lambda b, h: (0, h)))
    args.append(gain)
    return pl.pallas_call(
        functools.partial(_ret_kernel, n_chunks=rows_per_batch // CHUNK, meta_prefix=meta_prefix),
        grid=(n_batch, N_HEADS),
        in_specs=in_specs,
        out_specs=pl.BlockSpec((rows_per_batch, HEAD_DIM), lambda b, h: (b, h)),
        out_shape=jax.ShapeDtypeStruct((n_batch * rows_per_batch, RET_DIM), BF16),
        scratch_shapes=[pltpu.VMEM((rows_per_batch // CHUNK, HEAD_DIM, HEAD_DIM), BF16)],
        compiler_params=_params(("arbitrary", "arbitrary"), 32),
        name="retention",
    )(*args)


def _fox_kernel(*refs, n_q, tq, meta_only):
    if meta_only:
        q_ref, km_ref, vm_ref, cq_ref, ckm_ref, o_ref = refs
    else:
        q_ref, k_ref, v_ref, km_ref, vm_ref, cq_ref, ck_ref, ckm_ref, o_ref = refs
        ck2 = ck_ref[0] * LOG2E
    h = pl.program_id(1)
    lane = lax.broadcasted_iota(jnp.int32, (1, LANES), 1)
    km = km_ref[...]
    vm = vm_ref[...]
    ckm2 = ckm_ref[0] * LOG2E
    col_m = lax.broadcasted_iota(jnp.int32, (tq, CHUNK), 1)
    row_m = lax.broadcasted_iota(jnp.int32, (tq, CHUNK), 0)
    mask_m = col_m >= META_PAD
    if meta_only:
        mask_m = mask_m & (row_m >= col_m)
    tri = (lax.broadcasted_iota(jnp.int32, (tq, tq), 0)
           >= lax.broadcasted_iota(jnp.int32, (tq, tq), 1))

    def logits(q, kb, ckb, mask):
        t = lax.dot_general(q, kb, (((1,), (1,)), ((), ())), preferred_element_type=F32) - ckb
        return t if mask is None else jnp.where(mask, t, MASKED)

    def rowmax(t):
        return jnp.max(t, axis=1, keepdims=True)

    def rowsum(p):
        return jnp.sum(p, axis=1, keepdims=True)

    for qi in range(n_q):
        rows = slice(qi * tq, (qi + 1) * tq)
        n_prev = qi * tq
        q = q_ref[rows, :]
        cq2 = LOG2E * jnp.sum(jnp.where(lane == h, cq_ref[rows, :], 0.0), axis=1, keepdims=True)
        t_m = logits(q, km, ckm2, mask_m)
        mx = rowmax(t_m)
        if not meta_only:
            t_d = logits(q, k_ref[rows, :], ck2[:, rows], tri)
            mx = jnp.maximum(mx, rowmax(t_d))
            if qi > 0:
                t_o = logits(q, k_ref[0:n_prev, :], ck2[:, 0:n_prev], None)
                mx = jnp.maximum(mx, rowmax(t_o))
        m_row = mx + cq2
        shift = m_row - cq2
        p_m = jnp.exp2(t_m - shift)
        l = rowsum(p_m)
        acc = jnp.dot(p_m.astype(BF16), vm, preferred_element_type=F32)
        if not meta_only:
            p_d = jnp.exp2(t_d - shift)
            l = l + rowsum(p_d)
            acc = acc + jnp.dot(p_d.astype(BF16), v_ref[rows, :], preferred_element_type=F32)
            if qi > 0:
                p_o = jnp.exp2(t_o - shift)
                l = l + rowsum(p_o)
                acc = acc + jnp.dot(p_o.astype(BF16), v_ref[0:n_prev, :], preferred_element_type=F32)
        o_ref[rows, :] = (acc / l).astype(BF16)


def _fox(proj, proj_meta, cum_col, ck_rows, ckm_rows, *, n_batch, rows_per_batch, tq, meta_only):
    n_q = rows_per_batch // tq
    qblk = lambda sec: pl.BlockSpec((rows_per_batch, HEAD_DIM), lambda b, h, s=BLK[sec]: (b, s + h))
    mblk = lambda sec: pl.BlockSpec((CHUNK, HEAD_DIM), lambda b, h, s=BLK[sec]: (0, s + h))
    cq_spec = pl.BlockSpec((rows_per_batch, LANES), lambda b, h: (b, 0))
    ckm_spec = pl.BlockSpec((1, 1, CHUNK), lambda b, h: (h, 0, 0))
    if meta_only:
        in_specs = [qblk(SEC_FQ), mblk(SEC_FK), mblk(SEC_FV), cq_spec, ckm_spec]
        args = [proj, proj_meta, proj_meta, cum_col, ckm_rows]
    else:
        ck_spec = pl.BlockSpec((1, 1, rows_per_batch), lambda b, h: (b * N_HEADS + h, 0, 0))
        in_specs = [qblk(SEC_FQ), qblk(SEC_FK), qblk(SEC_FV), mblk(SEC_FK), mblk(SEC_FV),
                    cq_spec, ck_spec, ckm_spec]
        args = [proj, proj, proj, proj_meta, proj_meta, cum_col, ck_rows, ckm_rows]
    return pl.pallas_call(
        functools.partial(_fox_kernel, n_q=n_q, tq=tq, meta_only=meta_only),
        grid=(n_batch, N_HEADS),
        in_specs=in_specs,
        out_specs=pl.BlockSpec((rows_per_batch, HEAD_DIM), lambda b, h: (b, h)),
        out_shape=jax.ShapeDtypeStruct((n_batch * rows_per_batch, FOX_DIM), BF16),
        compiler_params=_params(("arbitrary", "arbitrary"), 32),
        name="foxattn",
    )(*args)


def _outproj_kernel(ret_ref, fox_ref, w1_ref, w2_ref, x_ref, g_ref, h_ref, c_ref, *, tm, sub):
    for rs in _row_tiles(tm, sub):
        hcur = (x_ref[rs, :]
                + jnp.dot(ret_ref[rs, :], w1_ref[...], preferred_element_type=F32)
                + jnp.dot(fox_ref[rs, :], w2_ref[...], preferred_element_type=F32))
        h_ref[rs, :] = hcur
        c_ref[rs, :] = _rms(hcur, g_ref[...]).astype(BF16)


def _outproj(ret, fox, w_out_b, x2d, gain, *, tm, sub):
    m = x2d.shape[0]
    return pl.pallas_call(
        functools.partial(_outproj_kernel, tm=tm, sub=sub),
        grid=(m // tm,),
        in_specs=[
            pl.BlockSpec((tm, RET_DIM), lambda i: (i, 0)),
            pl.BlockSpec((tm, FOX_DIM), lambda i: (i, 0)),
            pl.BlockSpec((RET_DIM, D_MODEL), lambda i: (0, 0)),
            pl.BlockSpec((FOX_DIM, D_MODEL), lambda i: (1, 0)),
            pl.BlockSpec((tm, D_MODEL), lambda i: (i, 0)),
            pl.BlockSpec((1, D_MODEL), lambda i: (0, 0)),
        ],
        out_specs=[
            pl.BlockSpec((tm, D_MODEL), lambda i: (i, 0)),
            pl.BlockSpec((tm, D_MODEL), lambda i: (i, 0)),
        ],
        out_shape=[
            jax.ShapeDtypeStruct((m, D_MODEL), F32),
            jax.ShapeDtypeStruct((m, D_MODEL), BF16),
        ],
        compiler_params=_params(("arbitrary",), 56),
        name="outproj",
    )(ret, fox, w_out_b, w_out_b, x2d, gain)


def _outproj_meta_kernel(mix_ref, w_ref, x_ref, g_ref, c_ref, wq_ref, acc_scr):
    k = pl.program_id(0)
    wq_ref[...] = w_ref[...].astype(BF16)
    part = jnp.dot(mix_ref[...], wq_ref[...], preferred_element_type=F32)

    @pl.when(k == 0)
    def _():
        acc_scr[...] = x_ref[...] + part

    @pl.when(k == 1)
    def _():
        c_ref[...] = _rms(acc_scr[...] + part, g_ref[...]).astype(BF16)


def _outproj_meta(mix_m, w_out, xm, gain):
    half = D_MODEL // 2
    return pl.pallas_call(
        _outproj_meta_kernel,
        grid=(2,),
        in_specs=[
            pl.BlockSpec((CHUNK, half), lambda k: (0, k)),
            pl.BlockSpec((half, D_MODEL), lambda k: (k, 0)),
            pl.BlockSpec((CHUNK, D_MODEL), lambda k: (0, 0)),
            pl.BlockSpec((1, D_MODEL), lambda k: (0, 0)),
        ],
        out_specs=[
            pl.BlockSpec((CHUNK, D_MODEL), lambda k: (0, 0)),
            pl.BlockSpec((half, D_MODEL), lambda k: (k, 0)),
        ],
        out_shape=[
            jax.ShapeDtypeStruct((CHUNK, D_MODEL), BF16),
            jax.ShapeDtypeStruct((D_MODEL, D_MODEL), BF16),
        ],
        scratch_shapes=[pltpu.VMEM((CHUNK, D_MODEL), F32)],
        compiler_params=_params(("arbitrary",), 48),
        name="outproj_meta",
    )(mix_m, w_out, xm, gain)


def _up_kernel(c_ref, cm_ref, wg_ref, wv_ref, cwg_ref, cwv_ref, cbg_ref, cbv_ref, o_ref,
               wg_s, wv_s, *u_scr, tm, sub, tiles_per_batch):
    n_sub = tm // sub
    ug_s, uv_s = u_scr[:n_sub], u_scr[n_sub:]
    i = pl.program_id(1)

    @pl.when(i == 0)
    def _():
        wg_s[...] = wg_ref[...].astype(BF16)
        wv_s[...] = wv_ref[...].astype(BF16)

    @pl.when(i % tiles_per_batch == 0)
    def _():
        cm = cm_ref[...]
        ug_s[0][0:8, :] = jnp.dot(cm, wg_s[...], preferred_element_type=F32)[8:16, :]
        uv_s[0][0:8, :] = jnp.dot(cm, wv_s[...], preferred_element_type=F32)[8:16, :]

    def conv(c, r, w_s, u_s, cw_ref, cb_ref):
        cur, nxt = u_s[r], u_s[(r + 1) % n_sub]
        u = jnp.dot(c, w_s[...], preferred_element_type=F32)
        cur[8:8 + sub, :] = u
        nxt[0:8, :] = u[sub - 8:sub, :]
        return (cb_ref[...]
                + cw_ref[0:1, :] * cur[6:6 + sub, :]
                + cw_ref[1:2, :] * cur[7:7 + sub, :]
                + cw_ref[2:3, :] * u)

    for r, rs in enumerate(_row_tiles(tm, sub)):
        c = c_ref[rs, :]
        gate = conv(c, r, wg_s, ug_s, cwg_ref, cbg_ref)
        val = conv(c, r, wv_s, uv_s, cwv_ref, cbv_ref)
        o_ref[rs, :] = ((gate / (1.0 + jnp.exp(-gate))) * val).astype(BF16)


def _upconv(c, c_meta, w_up, conv_w, conv_b, *, rows_per_batch, tm, tn, sub):
    m = c.shape[0]
    nj = D_FF // tn
    tiles_per_batch = rows_per_batch // tm
    halo_blk = c_meta.shape[0] // 16 - 1
    return pl.pallas_call(
        functools.partial(_up_kernel, tm=tm, sub=sub, tiles_per_batch=tiles_per_batch),
        grid=(nj, m // tm),
        in_specs=[
            pl.BlockSpec((tm, D_MODEL), lambda j, i: (i, 0)),
            pl.BlockSpec((16, D_MODEL), lambda j, i: (halo_blk, 0)),
            pl.BlockSpec((D_MODEL, tn), lambda j, i: (0, j)),
            pl.BlockSpec((D_MODEL, tn), lambda j, i: (0, j + nj)),
            pl.BlockSpec((3, tn), lambda j, i: (0, j)),
            pl.BlockSpec((3, tn), lambda j, i: (0, j + nj)),
            pl.BlockSpec((1, tn), lambda j, i: (0, j)),
            pl.BlockSpec((1, tn), lambda j, i: (0, j + nj)),
        ],
        out_specs=pl.BlockSpec((tm, tn), lambda j, i: (i, j)),
        out_shape=jax.ShapeDtypeStruct((m, D_FF), BF16),
        scratch_shapes=[
            pltpu.VMEM((D_MODEL, tn), BF16),
            pltpu.VMEM((D_MODEL, tn), BF16),
        ] + [pltpu.VMEM((sub + 8, tn), F32)] * (2 * (tm // sub)),
        compiler_params=_params(("arbitrary", "arbitrary"), 56),
        name="upconv",
    )(c, c_meta, w_up, w_up, conv_w, conv_w, conv_b, conv_b)


def _down_kernel(a_ref, w_ref, h_ref, g_ref, o_ref, *, tm, sub):
    for rs in _row_tiles(tm, sub):
        hcur = h_ref[rs, :] + jnp.dot(a_ref[rs, :], w_ref[...], preferred_element_type=F32)
        o_ref[rs, :] = _rms(hcur, g_ref[...])


def _down(act, w_down_b, h1, gain, *, tm, sub):
    m = act.shape[0]
    return pl.pallas_call(
        functools.partial(_down_kernel, tm=tm, sub=sub),
        grid=(m // tm,),
        in_specs=[
            pl.BlockSpec((tm, D_FF), lambda i: (i, 0)),
            pl.BlockSpec((D_FF, D_MODEL), lambda i: (0, 0), pipeline_mode=pl.Buffered(1)),
            pl.BlockSpec((tm, D_MODEL), lambda i: (i, 0)),
            pl.BlockSpec((1, D_MODEL), lambda i: (0, 0)),
        ],
        out_specs=pl.BlockSpec((tm, D_MODEL), lambda i: (i, 0)),
        out_shape=jax.ShapeDtypeStruct((m, D_MODEL), F32),
        compiler_params=_params(("arbitrary",), 58),
        name="downproj",
    )(act, w_down_b, h1, gain)


def _rotary_tables(pos):
    inv_freq = 1.0 / (ROPE_BASE ** (jnp.arange(0, HEAD_DIM, 2, dtype=F32) / HEAD_DIM))
    ang = pos[:, None] * inv_freq[None, :]
    cos, sin = jnp.cos(ang), jnp.sin(ang)
    return jnp.concatenate([cos, cos], axis=1), jnp.concatenate([-sin, sin], axis=1)


def kernel(x, meta_tokens, norm1_gain, w_in, b_forget, ret_norm_gain, w_out, norm2_gain,
           w_up, conv_w, conv_b, w_down, final_norm_gain):
    n_batch, seq, d_model = x.shape
    assert d_model == D_MODEL and seq % CHUNK == 0 and w_in.shape[0] == 1
    assert meta_tokens.shape == (N_META, D_MODEL)
    x2d = x.reshape(n_batch * seq, D_MODEL)
    xm = jnp.concatenate([jnp.zeros((META_PAD, D_MODEL), F32), meta_tokens.astype(F32)], axis=0)

    w_in_t = w_in[0].T
    w_down_b = w_down[0].astype(BF16)
    b_row = jnp.pad(b_forget[0], (0, LANES - N_HEADS)).reshape(1, LANES)
    log_g = jnp.log1p(-jnp.exp2(-5.0 - jnp.arange(N_HEADS, dtype=F32)))
    g1 = norm1_gain[0].reshape(1, D_MODEL)
    g2 = norm2_gain[0].reshape(1, D_MODEL)
    gf = final_norm_gain.reshape(1, D_MODEL)
    gr = ret_norm_gain[0].reshape(1, RET_DIM)

    cos_r, sin_r = _rotary_tables(N_META + jnp.arange(seq, dtype=F32))
    cos_m, sin_m = _rotary_tables(jnp.maximum(jnp.arange(CHUNK, dtype=F32) - META_PAD, 0.0))

    proj_m, ff_m, w_main_b, w_f = _inproj(xm, g1, w_in_t, w_in_t, cos_m, sin_m,
                                          tm=CHUNK, tn=512, sub=CHUNK, cast_weights=True)
    cum_m = _cum(ff_m, b_row, rows_per_batch=CHUNK, valid_from=META_PAD, rel_last=True)
    ckm_rows = cum_m[:, :N_HEADS].T.reshape(N_HEADS, 1, CHUNK)
    ret_m = _retention(log_g, proj_m, None, gr, n_batch=1, rows_per_batch=CHUNK, meta_prefix=False)
    fox_m = _fox(proj_m, proj_m, cum_m, None, ckm_rows, n_batch=1, rows_per_batch=CHUNK,
                 tq=CHUNK, meta_only=True)
    c_m, w_out_b = _outproj_meta(jnp.concatenate([ret_m, fox_m], axis=1), w_out[0], xm, g2)

    proj, ff = _inproj(x2d, g1, w_main_b, w_f, cos_r, sin_r,
                       tm=1024, tn=1024, sub=256, cast_weights=False)
    cum = _cum(ff, b_row, rows_per_batch=seq)
    ck_rows = (cum[:, :N_HEADS].reshape(n_batch, seq, N_HEADS).transpose(0, 2, 1)
               .reshape(n_batch * N_HEADS, 1, seq))
    ret = _retention(log_g, proj, proj_m, gr, n_batch=n_batch, rows_per_batch=seq, meta_prefix=True)
    fox = _fox(proj, proj_m, cum, ck_rows, ckm_rows, n_batch=n_batch, rows_per_batch=seq,
               tq=256, meta_only=False)
    h1, c = _outproj(ret, fox, w_out_b, x2d, g2, tm=512, sub=256)
    act = _upconv(c, c_m, w_up[0], conv_w[0], conv_b, rows_per_batch=seq, tm=1024, tn=512, sub=256)
    out = _down(act, w_down_b, h1, gf, tm=512, sub=256)
    return out.reshape(n_batch, seq, D_MODEL)
```

```python
import functools

import jax
import jax.numpy as jnp
from jax import lax
from jax.experimental import pallas as pl
from jax.experimental.pallas import tpu as pltpu

F32 = jnp.float32
BF16 = jnp.bfloat16

D_MODEL = 2048
N_META = 16
CHUNK = 128
N_HEADS = 8
HEAD_DIM = 128
RET_DIM = N_HEADS * HEAD_DIM
FOX_DIM = N_HEADS * HEAD_DIM
MAIN_COLS = 4 * RET_DIM + 3 * FOX_DIM
D_FF = 5632
ROPE_BASE = 10000.0
NORM_EPS = 1e-6
META_PAD = CHUNK - N_META
MASKED = -1e30
LANES = 128
MIB = 1024 * 1024
LOG2E = 1.4426950408889634
FOXQ_SCALE = HEAD_DIM ** -0.5 * LOG2E

(SEC_RQ, SEC_RK, SEC_RV, SEC_RG, SEC_FQ, SEC_FK, SEC_FV) = range(7)
BLK = {s: s * N_HEADS for s in range(7)}


def _params(sem, vmem_mib):
    return pltpu.CompilerParams(dimension_semantics=sem, vmem_limit_bytes=vmem_mib * MIB)


def _rms(x, gain):
    ms = jnp.mean(x * x, axis=-1, keepdims=True)
    return (x * lax.rsqrt(ms + NORM_EPS)) * gain


def _row_tiles(tm, sub):
    return [slice(r * sub, (r + 1) * sub) for r in range(tm // sub)]


def _inproj_kernel(*refs, tm, tn, sub, cast_weights):
    j = pl.program_id(1)
    if cast_weights:
        x_ref, g_ref, w_ref, wf8_ref, cos_ref, sin_ref, o_ref, ff_ref, wq_ref, wf_ref, a_scr = refs
        wq_ref[...] = w_ref[...].T.astype(BF16)
        w_bf = wq_ref

        @pl.when(j == 0)
        def _():
            pad = jnp.zeros((LANES - N_HEADS, D_MODEL), F32)
            wf_ref[...] = jnp.concatenate([wf8_ref[...], pad], axis=0).astype(BF16)
    else:
        x_ref, g_ref, w_ref, wf_ref, cos_ref, sin_ref, o_ref, ff_ref, a_scr = refs
        w_bf = w_ref
    sec = j // (RET_DIM // tn)
    tiles = _row_tiles(tm, sub)

    def rotary_store(rs, acc, scale):
        cos = cos_ref[rs, :]
        sin = sin_ref[rs, :]
        for hh in range(tn // LANES):
            cols = slice(hh * LANES, (hh + 1) * LANES)
            t = acc[:, cols]
            y = t * cos + pltpu.roll(t, HEAD_DIM // 2, 1) * sin
            if scale is not None:
                y = y * scale
            o_ref[rs, cols] = y.astype(BF16)

    @pl.when(j == 0)
    def _():
        for rs in tiles:
            a = _rms(x_ref[rs, :], g_ref[...]).astype(BF16)
            a_scr[rs, :] = a
            ff_ref[rs, :] = lax.dot_general(a, wf_ref[...], (((1,), (1,)), ((), ())),
                                            preferred_element_type=F32)
            rotary_store(rs, jnp.dot(a, w_bf[...], preferred_element_type=F32), None)

    @pl.when((j > 0) & (sec <= SEC_RK))
    def _():
        scale = jnp.where(sec == SEC_RK, HEAD_DIM ** -0.5, 1.0).astype(F32)
        for rs in tiles:
            rotary_store(rs, jnp.dot(a_scr[rs, :], w_bf[...], preferred_element_type=F32), scale)

    @pl.when(sec > SEC_RK)
    def _():
        scale = jnp.where(sec == SEC_FQ, FOXQ_SCALE, 1.0).astype(F32)
        for rs in tiles:
            acc = jnp.dot(a_scr[rs, :], w_bf[...], preferred_element_type=F32)
            o_ref[rs, :] = (acc * scale).astype(BF16)


def _inproj(x2d, gain, w, w_f, cos_t, sin_t, *, tm, tn, sub, cast_weights):
    m = x2d.shape[0]
    nb = cos_t.shape[0] // tm
    out_specs = [
        pl.BlockSpec((tm, tn), lambda i, j: (i, j)),
        pl.BlockSpec((tm, LANES), lambda i, j: (i, 0)),
    ]
    out_shape = [
        jax.ShapeDtypeStruct((m, MAIN_COLS), BF16),
        jax.ShapeDtypeStruct((m, LANES), F32),
    ]
    if cast_weights:
        assert m == tm
        out_specs += [pl.BlockSpec((D_MODEL, tn), lambda i, j: (0, j)),
                      pl.BlockSpec((LANES, D_MODEL), lambda i, j: (0, 0))]
        out_shape += [jax.ShapeDtypeStruct((D_MODEL, MAIN_COLS), BF16),
                      jax.ShapeDtypeStruct((LANES, D_MODEL), BF16)]
    return pl.pallas_call(
        functools.partial(_inproj_kernel, tm=tm, tn=tn, sub=sub, cast_weights=cast_weights),
        grid=(m // tm, MAIN_COLS // tn),
        in_specs=[
            pl.BlockSpec((tm, D_MODEL), lambda i, j: (i, 0)),
            pl.BlockSpec((1, D_MODEL), lambda i, j: (0, 0)),
            (pl.BlockSpec((tn, D_MODEL), lambda i, j: (j, 0)) if cast_weights
             else pl.BlockSpec((D_MODEL, tn), lambda i, j: (0, j))),
            (pl.BlockSpec((N_HEADS, D_MODEL), lambda i, j: (MAIN_COLS // N_HEADS, 0)) if cast_weights
             else pl.BlockSpec((LANES, D_MODEL), lambda i, j: (0, 0))),
            pl.BlockSpec((tm, LANES), lambda i, j: (i % nb, 0)),
            pl.BlockSpec((tm, LANES), lambda i, j: (i % nb, 0)),
        ],
        out_specs=out_specs,
        out_shape=out_shape,
        scratch_shapes=[pltpu.VMEM((tm, D_MODEL), BF16)],
        compiler_params=_params(("arbitrary", "arbitrary"), 56),
        name="inproj",
    )(x2d, gain, w, w_f, cos_t, sin_t)


def _cum_kernel(ff_ref, b_ref, o_ref, *, n_blk, valid_from, rel_last):
    row = lax.broadcasted_iota(jnp.int32, (CHUNK, CHUNK), 0)
    col = lax.broadcasted_iota(jnp.int32, (CHUNK, CHUNK), 1)
    tri = (row >= col).astype(BF16)
    rows = lax.broadcasted_iota(jnp.int32, (CHUNK, LANES), 0)
    carry = jnp.zeros((1, LANES), F32)
    for blk in range(n_blk):
        z = ff_ref[blk * CHUNK:(blk + 1) * CHUNK, :] + b_ref[...]
        lf = jnp.minimum(z, 0.0) - jnp.log1p(jnp.exp(-jnp.abs(z)))
        if valid_from:
            lf = jnp.where(rows >= valid_from, lf, 0.0)
        hi = lf.astype(BF16)
        r1 = lf - hi.astype(F32)
        mid = r1.astype(BF16)
        lo = (r1 - mid.astype(F32)).astype(BF16)
        cum = (jnp.dot(tri, hi, preferred_element_type=F32)
               + jnp.dot(tri, mid, preferred_element_type=F32)
               + jnp.dot(tri, lo, preferred_element_type=F32)) + carry
        o_ref[blk * CHUNK:(blk + 1) * CHUNK, :] = cum
        carry = cum[CHUNK - 1:CHUNK, :]
    if rel_last:
        o_ref[...] = o_ref[...] - carry


def _cum(ff, b_row, *, rows_per_batch, valid_from=0, rel_last=False):
    m = ff.shape[0]
    return pl.pallas_call(
        functools.partial(_cum_kernel, n_blk=rows_per_batch // CHUNK,
                          valid_from=valid_from, rel_last=rel_last),
        grid=(m // rows_per_batch,),
        in_specs=[
            pl.BlockSpec((rows_per_batch, LANES), lambda b: (b, 0)),
            pl.BlockSpec((1, LANES), lambda b: (0, 0)),
        ],
        out_specs=pl.BlockSpec((rows_per_batch, LANES), lambda b: (b, 0)),
        out_shape=jax.ShapeDtypeStruct((m, LANES), F32),
        compiler_params=_params(("arbitrary",), 32),
        name="cumgate",
    )(ff, b_row)


def _ret_kernel(*refs, n_chunks, meta_prefix):
    if meta_prefix:
        logg_ref, q_ref, k_ref, v_ref, g_ref, km_ref, vm_ref, gain_ref, o_ref, state_scr = refs
    else:
        logg_ref, q_ref, k_ref, v_ref, g_ref, gain_ref, o_ref, state_scr = refs
    lg = logg_ref[pl.program_id(1)]
    ri = lax.broadcasted_iota(jnp.int32, (CHUNK, CHUNK), 0)
    ci = lax.broadcasted_iota(jnp.int32, (CHUNK, CHUNK), 1)
    diff = (ri - ci).astype(F32)
    dmat = jnp.where(diff >= 0, jnp.exp(jnp.maximum(diff, 0.0) * lg), 0.0)
    pos = lax.broadcasted_iota(jnp.int32, (CHUNK, 1), 0).astype(F32)
    xi = jnp.exp((pos + 1.0) * lg)
    zeta = jnp.exp((CHUNK - 1.0 - pos) * lg)
    g_chunk = jnp.exp(jnp.full((1, 1), float(CHUNK), F32) * lg)

    def advance(state, kc, vc):
        kz = (kc.astype(F32) * zeta).astype(BF16)
        return g_chunk * state + lax.dot_general(
            kz, vc, (((0,), (0,)), ((), ())), preferred_element_type=F32)

    state = jnp.zeros((HEAD_DIM, HEAD_DIM), F32)
    if meta_prefix:
        state = advance(state, km_ref[...], vm_ref[...])
    for c in range(n_chunks):
        state_scr[c] = state.astype(BF16)
        if c + 1 < n_chunks:
            rows = slice(c * CHUNK, (c + 1) * CHUNK)
            state = advance(state, k_ref[rows, :], v_ref[rows, :])

    gain = gain_ref[...]
    for c in range(n_chunks):
        rows = slice(c * CHUNK, (c + 1) * CHUNK)
        qc = q_ref[rows, :]
        kc = k_ref[rows, :]
        vc = v_ref[rows, :]
        s = lax.dot_general(qc, kc, (((1,), (1,)), ((), ())), preferred_element_type=F32) * dmat
        o = (jnp.dot(s.astype(BF16), vc, preferred_element_type=F32)
             + xi * jnp.dot(qc, state_scr[c], preferred_element_type=F32))
        mu = jnp.mean(o, axis=-1, keepdims=True)
        d = o - mu
        var = jnp.mean(d * d, axis=-1, keepdims=True)
        y = (d * lax.rsqrt(var + NORM_EPS)) * gain
        gt = g_ref[rows, :].astype(F32)
        silu = gt / (1.0 + jnp.exp(-gt))
        o_ref[rows, :] = (silu * y).astype(BF16)


def _retention(log_g, proj, proj_meta, gain, *, n_batch, rows_per_batch, meta_prefix):
    blk = lambda sec: pl.BlockSpec((rows_per_batch, HEAD_DIM), lambda b, h, s=BLK[sec]: (b, s + h))
    in_specs = [pl.BlockSpec(memory_space=pltpu.SMEM),
                blk(SEC_RQ), blk(SEC_RK), blk(SEC_RV), blk(SEC_RG)]
    args = [log_g, proj, proj, proj, proj]
    if meta_prefix:
        mblk = lambda sec: pl.BlockSpec((CHUNK, HEAD_DIM), lambda b, h, s=BLK[sec]: (0, s + h))
        in_specs += [mblk(SEC_RK), mblk(SEC_RV)]
        args += [proj_meta, proj_meta]
    in_specs.append(pl.BlockSpec((1, HEAD_DIM), lambda b, h: (0, h)))
    args.append(gain)
    return pl.pallas_call(
        functools.partial(_ret_kernel, n_chunks=rows_per_batch // CHUNK, meta_prefix=meta_prefix),
        grid=(n_batch, N_HEADS),
        in_specs=in_specs,
        out_specs=pl.BlockSpec((rows_per_batch, HEAD_DIM), lambda b, h: (b, h)),
        out_shape=jax.ShapeDtypeStruct((n_batch * rows_per_batch, RET_DIM), BF16),
        scratch_shapes=[pltpu.VMEM((rows_per_batch // CHUNK, HEAD_DIM, HEAD_DIM), BF16)],
        compiler_params=_params(("arbitrary", "arbitrary"), 32),
        name="retention",
    )(*args)


def _fox_kernel(*refs, n_q, tq, meta_only):
    if meta_only:
        q_ref, km_ref, vm_ref, cq_ref, ckm_ref, o_ref = refs
    else:
        q_ref, k_ref, v_ref, km_ref, vm_ref, cq_ref, ck_ref, ckm_ref, o_ref = refs
        ck2 = ck_ref[0] * LOG2E
    h = pl.program_id(1)
    lane = lax.broadcasted_iota(jnp.int32, (1, LANES), 1)
    km = km_ref[...]
    vm = vm_ref[...]
    ckm2 = ckm_ref[0] * LOG2E
    col_m = lax.broadcasted_iota(jnp.int32, (tq, CHUNK), 1)
    row_m = lax.broadcasted_iota(jnp.int32, (tq, CHUNK), 0)
    mask_m = col_m >= META_PAD
    if meta_only:
        mask_m = mask_m & (row_m >= col_m)
    tri = (lax.broadcasted_iota(jnp.int32, (tq, tq), 0)
           >= lax.broadcasted_iota(jnp.int32, (tq, tq), 1))

    def logits(q, kb, ckb, mask):
        t = lax.dot_general(q, kb, (((1,), (1,)), ((), ())), preferred_element_type=F32) - ckb
        return t if mask is None else jnp.where(mask, t, MASKED)

    def rowmax(t):
        return jnp.max(t, axis=1, keepdims=True)

    def rowsum(p):
        return jnp.sum(p, axis=1, keepdims=True)

    def logit_pass(qi):
        rows = slice(qi * tq, (qi + 1) * tq)
        n_prev = qi * tq
        q = q_ref[rows, :]
        ts = [logits(q, km, ckm2, mask_m)]
        if not meta_only:
            ts.append(logits(q, k_ref[rows, :], ck2[:, rows], tri))
            if qi > 0:
                ts.append(logits(q, k_ref[0:n_prev, :], ck2[:, 0:n_prev], None))
        mx = rowmax(ts[0])
        for t in ts[1:]:
            mx = jnp.maximum(mx, rowmax(t))
        return ts, mx

    def exp_pass(qi, ts, mx):
        rows = slice(qi * tq, (qi + 1) * tq)
        cq2 = LOG2E * jnp.sum(jnp.where(lane == h, cq_ref[rows, :], 0.0), axis=1, keepdims=True)
        m_row = mx + cq2
        shift = m_row - cq2
        vs = [vm]
        if not meta_only:
            vs.append(v_ref[rows, :])
            if qi > 0:
                vs.append(v_ref[0:qi * tq, :])
        l = None
        acc = None
        for t, vb in zip(ts, vs):
            p = jnp.exp2(t - shift)
            pv = jnp.dot(p.astype(BF16), vb, preferred_element_type=F32)
            l = rowsum(p) if l is None else l + rowsum(p)
            acc = pv if acc is None else acc + pv
        o_ref[rows, :] = (acc / l).astype(BF16)

    cur = logit_pass(0)
    for qi in range(n_q):
        nxt = logit_pass(qi + 1) if qi + 1 < n_q else None
        exp_pass(qi, *cur)
        cur = nxt


def _fox(proj, proj_meta, cum_col, ck_rows, ckm_rows, *, n_batch, rows_per_batch, tq, meta_only):
    n_q = rows_per_batch // tq
    qblk = lambda sec: pl.BlockSpec((rows_per_batch, HEAD_DIM), lambda b, h, s=BLK[sec]: (b, s + h))
    mblk = lambda sec: pl.BlockSpec((CHUNK, HEAD_DIM), lambda b, h, s=BLK[sec]: (0, s + h))
    cq_spec = pl.BlockSpec((rows_per_batch, LANES), lambda b, h: (b, 0))
    ckm_spec = pl.BlockSpec((1, 1, CHUNK), lambda b, h: (h, 0, 0))
    if meta_only:
        in_specs = [qblk(SEC_FQ), mblk(SEC_FK), mblk(SEC_FV), cq_spec, ckm_spec]
        args = [proj, proj_meta, proj_meta, cum_col, ckm_rows]
    else:
        ck_spec = pl.BlockSpec((1, 1, rows_per_batch), lambda b, h: (b * N_HEADS + h, 0, 0))
        in_specs = [qblk(SEC_FQ), qblk(SEC_FK), qblk(SEC_FV), mblk(SEC_FK), mblk(SEC_FV),
                    cq_spec, ck_spec, ckm_spec]
        args = [proj, proj, proj, proj_meta, proj_meta, cum_col, ck_rows, ckm_rows]
    return pl.pallas_call(
        functools.partial(_fox_kernel, n_q=n_q, tq=tq, meta_only=meta_only),
        grid=(n_batch, N_HEADS),
        in_specs=in_specs,
        out_specs=pl.BlockSpec((rows_per_batch, HEAD_DIM), lambda b, h: (b, h)),
        out_shape=jax.ShapeDtypeStruct((n_batch * rows_per_batch, FOX_DIM), BF16),
        compiler_params=_params(("arbitrary", "arbitrary"), 32),
        name="foxattn",
    )(*args)


def _outproj_kernel(ret_ref, fox_ref, w1_ref, w2_ref, x_ref, g_ref, h_ref, c_ref, *, tm, sub):
    for rs in _row_tiles(tm, sub):
        hcur = (x_ref[rs, :]
                + jnp.dot(ret_ref[rs, :], w1_ref[...], preferred_element_type=F32)
                + jnp.dot(fox_ref[rs, :], w2_ref[...], preferred_element_type=F32))
        h_ref[rs, :] = hcur
        c_ref[rs, :] = _rms(hcur, g_ref[...]).astype(BF16)


def _outproj(ret, fox, w_out_b, x2d, gain, *, tm, sub):
    m = x2d.shape[0]
    return pl.pallas_call(
        functools.partial(_outproj_kernel, tm=tm, sub=sub),
        grid=(m // tm,),
        in_specs=[
            pl.BlockSpec((tm, RET_DIM), lambda i: (i, 0)),
            pl.BlockSpec((tm, FOX_DIM), lambda i: (i, 0)),
            pl.BlockSpec((RET_DIM, D_MODEL), lambda i: (0, 0)),
            pl.BlockSpec((FOX_DIM, D_MODEL), lambda i: (1, 0)),
            pl.BlockSpec((tm, D_MODEL), lambda i: (i, 0)),
            pl.BlockSpec((1, D_MODEL), lambda i: (0, 0)),
        ],
        out_specs=[
            pl.BlockSpec((tm, D_MODEL), lambda i: (i, 0)),
            pl.BlockSpec((tm, D_MODEL), lambda i: (i, 0)),
        ],
        out_shape=[
            jax.ShapeDtypeStruct((m, D_MODEL), F32),
            jax.ShapeDtypeStruct((m, D_MODEL), BF16),
        ],
        compiler_params=_params(("arbitrary",), 56),
        name="outproj",
    )(ret, fox, w_out_b, w_out_b, x2d, gain)


def _outproj_meta_kernel(mix_ref, w_ref, x_ref, g_ref, c_ref, wq_ref, acc_scr):
    k = pl.program_id(0)
    wq_ref[...] = w_ref[...].astype(BF16)
    part = jnp.dot(mix_ref[...], wq_ref[...], preferred_element_type=F32)

    @pl.when(k == 0)
    def _():
        acc_scr[...] = x_ref[...] + part

    @pl.when(k == 1)
    def _():
        c_ref[...] = _rms(acc_scr[...] + part, g_ref[...]).astype(BF16)


def _outproj_meta(mix_m, w_out, xm, gain):
    half = D_MODEL // 2
    return pl.pallas_call(
        _outproj_meta_kernel,
        grid=(2,),
        in_specs=[
            pl.BlockSpec((CHUNK, half), lambda k: (0, k)),
            pl.BlockSpec((half, D_MODEL), lambda k: (k, 0)),
            pl.BlockSpec((CHUNK, D_MODEL), lambda k: (0, 0)),
            pl.BlockSpec((1, D_MODEL), lambda k: (0, 0)),
        ],
        out_specs=[
            pl.BlockSpec((CHUNK, D_MODEL), lambda k: (0, 0)),
            pl.BlockSpec((half, D_MODEL), lambda k: (k, 0)),
        ],
        out_shape=[
            jax.ShapeDtypeStruct((CHUNK, D_MODEL), BF16),
            jax.ShapeDtypeStruct((D_MODEL, D_MODEL), BF16),
        ],
        scratch_shapes=[pltpu.VMEM((CHUNK, D_MODEL), F32)],
        compiler_params=_params(("arbitrary",), 48),
        name="outproj_meta",
    )(mix_m, w_out, xm, gain)


def _up_kernel(c_ref, cm_ref, wg_ref, wv_ref, cwg_ref, cwv_ref, cbg_ref, cbv_ref, o_ref,
               wg_s, wv_s, ug_s, uv_s, *, tm, sub, tiles_per_batch):
    i = pl.program_id(1)

    @pl.when(i == 0)
    def _():
        wg_s[...] = wg_ref[...].astype(BF16)
        wv_s[...] = wv_ref[...].astype(BF16)

    @pl.when(i % tiles_per_batch == 0)
    def _():
        cm = cm_ref[...]
        ug_s[0:8, :] = jnp.dot(cm, wg_s[...], preferred_element_type=F32)[8:16, :]
        uv_s[0:8, :] = jnp.dot(cm, wv_s[...], preferred_element_type=F32)[8:16, :]

    def conv(c, rs, w_s, u_s, cw_ref, cb_ref):
        u_s[8 + rs.start:8 + rs.stop, :] = jnp.dot(c, w_s[...], preferred_element_type=F32)
        return (cb_ref[...]
                + cw_ref[0:1, :] * u_s[6 + rs.start:6 + rs.stop, :]
                + cw_ref[1:2, :] * u_s[7 + rs.start:7 + rs.stop, :]
                + cw_ref[2:3, :] * u_s[8 + rs.start:8 + rs.stop, :])

    for rs in _row_tiles(tm, sub):
        c = c_ref[rs, :]
        gate = conv(c, rs, wg_s, ug_s, cwg_ref, cbg_ref)
        val = conv(c, rs, wv_s, uv_s, cwv_ref, cbv_ref)
        o_ref[rs, :] = ((gate / (1.0 + jnp.exp(-gate))) * val).astype(BF16)
    ug_s[0:8, :] = ug_s[tm:tm + 8, :]
    uv_s[0:8, :] = uv_s[tm:tm + 8, :]


def _upconv(c, c_meta, w_up, conv_w, conv_b, *, rows_per_batch, tm, tn, sub):
    m = c.shape[0]
    nj = D_FF // tn
    tiles_per_batch = rows_per_batch // tm
    halo_blk = c_meta.shape[0] // 16 - 1
    return pl.pallas_call(
        functools.partial(_up_kernel, tm=tm, sub=sub, tiles_per_batch=tiles_per_batch),
        grid=(nj, m // tm),
        in_specs=[
            pl.BlockSpec((tm, D_MODEL), lambda j, i: (i, 0)),
            pl.BlockSpec((16, D_MODEL), lambda j, i: (halo_blk, 0)),
            pl.BlockSpec((D_MODEL, tn), lambda j, i: (0, j)),
            pl.BlockSpec((D_MODEL, tn), lambda j, i: (0, j + nj)),
            pl.BlockSpec((3, tn), lambda j, i: (0, j)),
            pl.BlockSpec((3, tn), lambda j, i: (0, j + nj)),
            pl.BlockSpec((1, tn), lambda j, i: (0, j)),
            pl.BlockSpec((1, tn), lambda j, i: (0, j + nj)),
        ],
        out_specs=pl.BlockSpec((tm, tn), lambda j, i: (i, j)),
        out_shape=jax.ShapeDtypeStruct((m, D_FF), BF16),
        scratch_shapes=[
            pltpu.VMEM((D_MODEL, tn), BF16),
            pltpu.VMEM((D_MODEL, tn), BF16),
            pltpu.VMEM((tm + 8, tn), F32),
            pltpu.VMEM((tm + 8, tn), F32),
        ],
        compiler_params=_params(("arbitrary", "arbitrary"), 56),
        name="upconv",
    )(c, c_meta, w_up, w_up, conv_w, conv_w, conv_b, conv_b)


def _down_kernel(a_ref, w_ref, h_ref, g_ref, o_ref, *, tm, sub):
    for rs in _row_tiles(tm, sub):
        hcur = h_ref[rs, :] + jnp.dot(a_ref[rs, :], w_ref[...], preferred_element_type=F32)
        o_ref[rs, :] = _rms(hcur, g_ref[...])


def _down(act, w_down_b, h1, gain, *, tm, sub):
    m = act.shape[0]
    return pl.pallas_call(
        functools.partial(_down_kernel, tm=tm, sub=sub),
        grid=(m // tm,),
        in_specs=[
            pl.BlockSpec((tm, D_FF), lambda i: (i, 0)),
            pl.BlockSpec((D_FF, D_MODEL), lambda i: (0, 0), pipeline_mode=pl.Buffered(1)),
            pl.BlockSpec((tm, D_MODEL), lambda i: (i, 0)),
            pl.BlockSpec((1, D_MODEL), lambda i: (0, 0)),
        ],
        out_specs=pl.BlockSpec((tm, D_MODEL), lambda i: (i, 0)),
        out_shape=jax.ShapeDtypeStruct((m, D_MODEL), F32),
        compiler_params=_params(("arbitrary",), 58),
        name="downproj",
    )(act, w_down_b, h1, gain)


def _rotary_tables(pos):
    inv_freq = 1.0 / (ROPE_BASE ** (jnp.arange(0, HEAD_DIM, 2, dtype=F32) / HEAD_DIM))
    ang = pos[:, None] * inv_freq[None, :]
    cos, sin = jnp.cos(ang), jnp.sin(ang)
    return jnp.concatenate([cos, cos], axis=1), jnp.concatenate([-sin, sin], axis=1)


def kernel(x, meta_tokens, norm1_gain, w_in, b_forget, ret_norm_gain, w_out, norm2_gain,
           w_up, conv_w, conv_b, w_down, final_norm_gain):
    n_batch, seq, d_model = x.shape
    assert d_model == D_MODEL and seq % CHUNK == 0 and w_in.shape[0] == 1
    assert meta_tokens.shape == (N_META, D_MODEL)
    x2d = x.reshape(n_batch * seq, D_MODEL)
    xm = jnp.concatenate([jnp.zeros((META_PAD, D_MODEL), F32), meta_tokens.astype(F32)], axis=0)

    w_in_t = w_in[0].T
    w_down_b = w_down[0].astype(BF16)
    b_row = jnp.pad(b_forget[0], (0, LANES - N_HEADS)).reshape(1, LANES)
    log_g = jnp.log1p(-jnp.exp2(-5.0 - jnp.arange(N_HEADS, dtype=F32)))
    g1 = norm1_gain[0].reshape(1, D_MODEL)
    g2 = norm2_gain[0].reshape(1, D_MODEL)
    gf = final_norm_gain.reshape(1, D_MODEL)
    gr = ret_norm_gain[0].reshape(1, RET_DIM)

    cos_r, sin_r = _rotary_tables(N_META + jnp.arange(seq, dtype=F32))
    cos_m, sin_m = _rotary_tables(jnp.maximum(jnp.arange(CHUNK, dtype=F32) - META_PAD, 0.0))

    proj_m, ff_m, w_main_b, w_f = _inproj(xm, g1, w_in_t, w_in_t, cos_m, sin_m,
                                          tm=CHUNK, tn=512, sub=CHUNK, cast_weights=True)
    cum_m = _cum(ff_m, b_row, rows_per_batch=CHUNK, valid_from=META_PAD, rel_last=True)
    ckm_rows = cum_m[:, :N_HEADS].T.reshape(N_HEADS, 1, CHUNK)
    ret_m = _retention(log_g, proj_m, None, gr, n_batch=1, rows_per_batch=CHUNK, meta_prefix=False)
    fox_m = _fox(proj_m, proj_m, cum_m, None, ckm_rows, n_batch=1, rows_per_batch=CHUNK,
                 tq=CHUNK, meta_only=True)
    c_m, w_out_b = _outproj_meta(jnp.concatenate([ret_m, fox_m], axis=1), w_out[0], xm, g2)

    proj, ff = _inproj(x2d, g1, w_main_b, w_f, cos_r, sin_r,
                       tm=1024, tn=1024, sub=256, cast_weights=False)
    cum = _cum(ff, b_row, rows_per_batch=seq)
    ck_rows = (cum[:, :N_HEADS].reshape(n_batch, seq, N_HEADS).transpose(0, 2, 1)
               .reshape(n_batch * N_HEADS, 1, seq))
    ret = _retention(log_g, proj, proj_m, gr, n_batch=n_batch, rows_per_batch=seq, meta_prefix=True)
    fox = _fox(proj, proj_m, cum, ck_rows, ckm_rows, n_batch=n_batch, rows_per_batch=seq,
               tq=256, meta_only=False)
    h1, c = _outproj(ret, fox, w_out_b, x2d, g2, tm=512, sub=256)
    act = _upconv(c, c_m, w_up[0], conv_w[0], conv_b, rows_per_batch=seq, tm=1024, tn=512, sub=1024)
    out = _down(act, w_down_b, h1, gf, tm=512, sub=256)
    return out.reshape(n_batch, seq, D_MODEL)
```

```python
import functools

import jax
import jax.numpy as jnp
from jax import lax
from jax.experimental import pallas as pl
from jax.experimental.pallas import tpu as pltpu

F32 = jnp.float32
BF16 = jnp.bfloat16

D_MODEL = 2048
N_META = 16
CHUNK = 128
N_HEADS = 8
HEAD_DIM = 128
RET_DIM = N_HEADS * HEAD_DIM
FOX_DIM = N_HEADS * HEAD_DIM
MAIN_COLS = 4 * RET_DIM + 3 * FOX_DIM
D_FF = 5632
ROPE_BASE = 10000.0
NORM_EPS = 1e-6
META_PAD = CHUNK - N_META
MASKED = -1e30
LANES = 128
MIB = 1024 * 1024
LOG2E = 1.4426950408889634
FOXQ_SCALE = HEAD_DIM ** -0.5 * LOG2E

(SEC_RQ, SEC_RK, SEC_RV, SEC_RG, SEC_FQ, SEC_FK, SEC_FV) = range(7)
BLK = {s: s * N_HEADS for s in range(7)}


def _params(sem, vmem_mib):
    return pltpu.CompilerParams(dimension_semantics=sem, vmem_limit_bytes=vmem_mib * MIB)


def _rms(x, gain):
    ms = jnp.mean(x * x, axis=-1, keepdims=True)
    return (x * lax.rsqrt(ms + NORM_EPS)) * gain


def _row_tiles(tm, sub):
    return [slice(r * sub, (r + 1) * sub) for r in range(tm // sub)]


def _inproj_kernel(*refs, tm, tn, sub, cast_weights):
    j = pl.program_id(1)
    if cast_weights:
        x_ref, g_ref, w_ref, wf8_ref, cos_ref, sin_ref, o_ref, ff_ref, wq_ref, wf_ref, a_scr = refs
        wq_ref[...] = w_ref[...].T.astype(BF16)
        w_bf = wq_ref

        @pl.when(j == 0)
        def _():
            pad = jnp.zeros((LANES - N_HEADS, D_MODEL), F32)
            wf_ref[...] = jnp.concatenate([wf8_ref[...], pad], axis=0).astype(BF16)
    else:
        x_ref, g_ref, w_ref, wf_ref, cos_ref, sin_ref, o_ref, ff_ref, a_scr = refs
        w_bf = w_ref
    sec = j // (RET_DIM // tn)
    tiles = _row_tiles(tm, sub)

    def rotary_store(rs, acc, scale):
        cos = cos_ref[rs, :]
        sin = sin_ref[rs, :]
        for hh in range(tn // LANES):
            cols = slice(hh * LANES, (hh + 1) * LANES)
            t = acc[:, cols]
            y = t * cos + pltpu.roll(t, HEAD_DIM // 2, 1) * sin
            if scale is not None:
                y = y * scale
            o_ref[rs, cols] = y.astype(BF16)

    @pl.when(j == 0)
    def _():
        for rs in tiles:
            a = _rms(x_ref[rs, :], g_ref[...]).astype(BF16)
            a_scr[rs, :] = a
            ff_ref[rs, :] = lax.dot_general(a, wf_ref[...], (((1,), (1,)), ((), ())),
                                            preferred_element_type=F32)
            rotary_store(rs, jnp.dot(a, w_bf[...], preferred_element_type=F32), None)

    @pl.when((j > 0) & (sec <= SEC_RK))
    def _():
        scale = jnp.where(sec == SEC_RK, HEAD_DIM ** -0.5, 1.0).astype(F32)
        for rs in tiles:
            rotary_store(rs, jnp.dot(a_scr[rs, :], w_bf[...], preferred_element_type=F32), scale)

    @pl.when(sec > SEC_RK)
    def _():
        scale = jnp.where(sec == SEC_FQ, FOXQ_SCALE, 1.0).astype(F32)
        for rs in tiles:
            acc = jnp.dot(a_scr[rs, :], w_bf[...], preferred_element_type=F32)
            o_ref[rs, :] = (acc * scale).astype(BF16)


def _inproj(x2d, gain, w, w_f, cos_t, sin_t, *, tm, tn, sub, cast_weights):
    m = x2d.shape[0]
    nb = cos_t.shape[0] // tm
    out_specs = [
        pl.BlockSpec((tm, tn), lambda i, j: (i, j)),
        pl.BlockSpec((tm, LANES), lambda i, j: (i, 0)),
    ]
    out_shape = [
        jax.ShapeDtypeStruct((m, MAIN_COLS), BF16),
        jax.ShapeDtypeStruct((m, LANES), F32),
    ]
    if cast_weights:
        assert m == tm
        out_specs += [pl.BlockSpec((D_MODEL, tn), lambda i, j: (0, j)),
                      pl.BlockSpec((LANES, D_MODEL), lambda i, j: (0, 0))]
        out_shape += [jax.ShapeDtypeStruct((D_MODEL, MAIN_COLS), BF16),
                      jax.ShapeDtypeStruct((LANES, D_MODEL), BF16)]
    return pl.pallas_call(
        functools.partial(_inproj_kernel, tm=tm, tn=tn, sub=sub, cast_weights=cast_weights),
        grid=(m // tm, MAIN_COLS // tn),
        in_specs=[
            pl.BlockSpec((tm, D_MODEL), lambda i, j: (i, 0)),
            pl.BlockSpec((1, D_MODEL), lambda i, j: (0, 0)),
            (pl.BlockSpec((tn, D_MODEL), lambda i, j: (j, 0)) if cast_weights
             else pl.BlockSpec((D_MODEL, tn), lambda i, j: (0, j))),
            (pl.BlockSpec((N_HEADS, D_MODEL), lambda i, j: (MAIN_COLS // N_HEADS, 0)) if cast_weights
             else pl.BlockSpec((LANES, D_MODEL), lambda i, j: (0, 0))),
            pl.BlockSpec((tm, LANES), lambda i, j: (i % nb, 0)),
            pl.BlockSpec((tm, LANES), lambda i, j: (i % nb, 0)),
        ],
        out_specs=out_specs,
        out_shape=out_shape,
        scratch_shapes=[pltpu.VMEM((tm, D_MODEL), BF16)],
        compiler_params=_params(("arbitrary", "arbitrary"), 56),
        name="inproj",
    )(x2d, gain, w, w_f, cos_t, sin_t)


def _cum_kernel(ff_ref, b_ref, o_ref, *, n_blk, valid_from, rel_last):
    row = lax.broadcasted_iota(jnp.int32, (CHUNK, CHUNK), 0)
    col = lax.broadcasted_iota(jnp.int32, (CHUNK, CHUNK), 1)
    tri = (row >= col).astype(BF16)
    rows = lax.broadcasted_iota(jnp.int32, (CHUNK, LANES), 0)
    carry = jnp.zeros((1, LANES), F32)
    for blk in range(n_blk):
        z = ff_ref[blk * CHUNK:(blk + 1) * CHUNK, :] + b_ref[...]
        lf = jnp.minimum(z, 0.0) - jnp.log1p(jnp.exp(-jnp.abs(z)))
        if valid_from:
            lf = jnp.where(rows >= valid_from, lf, 0.0)
        hi = lf.astype(BF16)
        r1 = lf - hi.astype(F32)
        mid = r1.astype(BF16)
        lo = (r1 - mid.astype(F32)).astype(BF16)
        cum = (jnp.dot(tri, hi, preferred_element_type=F32)
               + jnp.dot(tri, mid, preferred_element_type=F32)
               + jnp.dot(tri, lo, preferred_element_type=F32)) + carry
        o_ref[blk * CHUNK:(blk + 1) * CHUNK, :] = cum
        carry = cum[CHUNK - 1:CHUNK, :]
    if rel_last:
        o_ref[...] = o_ref[...] - carry


def _cum(ff, b_row, *, rows_per_batch, valid_from=0, rel_last=False):
    m = ff.shape[0]
    return pl.pallas_call(
        functools.partial(_cum_kernel, n_blk=rows_per_batch // CHUNK,
                          valid_from=valid_from, rel_last=rel_last),
        grid=(m // rows_per_batch,),
        in_specs=[
            pl.BlockSpec((rows_per_batch, LANES), lambda b: (b, 0)),
            pl.BlockSpec((1, LANES), lambda b: (0, 0)),
        ],
        out_specs=pl.BlockSpec((rows_per_batch, LANES), lambda b: (b, 0)),
        out_shape=jax.ShapeDtypeStruct((m, LANES), F32),
        compiler_params=_params(("arbitrary",), 32),
        name="cumgate",
    )(ff, b_row)


def _ret_kernel(*refs, n_chunks, meta_prefix):
    if meta_prefix:
        logg_ref, q_ref, k_ref, v_ref, g_ref, km_ref, vm_ref, gain_ref, o_ref, state_scr = refs
    else:
        logg_ref, q_ref, k_ref, v_ref, g_ref, gain_ref, o_ref, state_scr = refs
    lg = logg_ref[pl.program_id(1)]
    ri = lax.broadcasted_iota(jnp.int32, (CHUNK, CHUNK), 0)
    ci = lax.broadcasted_iota(jnp.int32, (CHUNK, CHUNK), 1)
    diff = (ri - ci).astype(F32)
    dmat = jnp.where(diff >= 0, jnp.exp(jnp.maximum(diff, 0.0) * lg), 0.0)
    pos = lax.broadcasted_iota(jnp.int32, (CHUNK, 1), 0).astype(F32)
    xi = jnp.exp((pos + 1.0) * lg)
    zeta = jnp.exp((CHUNK - 1.0 - pos) * lg)
    g_chunk = jnp.exp(jnp.full((1, 1), float(CHUNK), F32) * lg)

    def advance(state, kc, vc):
        kz = (kc.astype(F32) * zeta).astype(BF16)
        return g_chunk * state + lax.dot_general(
            kz, vc, (((0,), (0,)), ((), ())), preferred_element_type=F32)

    state = jnp.zeros((HEAD_DIM, HEAD_DIM), F32)
    if meta_prefix:
        state = advance(state, km_ref[...], vm_ref[...])
    for c in range(n_chunks):
        state_scr[c] = state.astype(BF16)
        if c + 1 < n_chunks:
            rows = slice(c * CHUNK, (c + 1) * CHUNK)
            state = advance(state, k_ref[rows, :], v_ref[rows, :])

    gain = gain_ref[...]

    for c in range(n_chunks):
        rows = slice(c * CHUNK, (c + 1) * CHUNK)
        qc = q_ref[rows, :]
        kc = k_ref[rows, :]
        vc = v_ref[rows, :]
        s = lax.dot_general(qc, kc, (((1,), (1,)), ((), ())), preferred_element_type=F32) * dmat
        o = (jnp.dot(s.astype(BF16), vc, preferred_element_type=F32)
             + xi * jnp.dot(qc, state_scr[c], preferred_element_type=F32))
        mu = jnp.mean(o, axis=-1, keepdims=True)
        d = o - mu
        var = jnp.mean(d * d, axis=-1, keepdims=True)
        y = (d * lax.rsqrt(var + NORM_EPS)) * gain
        gt = g_ref[rows, :].astype(F32)
        silu = gt / (1.0 + jnp.exp(-gt))
        o_ref[rows, :] = (silu * y).astype(BF16)


def _retention(log_g, proj, proj_meta, gain, *, n_batch, rows_per_batch, meta_prefix):
    blk = lambda sec: pl.BlockSpec((rows_per_batch, HEAD_DIM), lambda b, h, s=BLK[sec]: (b, s + h))
    in_specs = [pl.BlockSpec(memory_space=pltpu.SMEM),
                blk(SEC_RQ), blk(SEC_RK), blk(SEC_RV), blk(SEC_RG)]
    args = [log_g, proj, proj, proj, proj]
    if meta_prefix:
        mblk = lambda sec: pl.BlockSpec((CHUNK, HEAD_DIM), lambda b, h, s=BLK[sec]: (0, s + h))
        in_specs += [mblk(SEC_RK), mblk(SEC_RV)]
        args += [proj_meta, proj_meta]
    in_specs.append(pl.BlockSpec((1, HEAD_DIM), lambda b, h: (0, h)))
    args.append(gain)
    return pl.pallas_call(
        functools.partial(_ret_kernel, n_chunks=rows_per_batch // CHUNK, meta_prefix=meta_prefix),
        grid=(n_batch, N_HEADS),
        in_specs=in_specs,
        out_specs=pl.BlockSpec((rows_per_batch, HEAD_DIM), lambda b, h: (b, h)),
        out_shape=jax.ShapeDtypeStruct((n_batch * rows_per_batch, RET_DIM), BF16),
        scratch_shapes=[pltpu.VMEM((rows_per_batch // CHUNK, HEAD_DIM, HEAD_DIM), BF16)],
        compiler_params=_params(("arbitrary", "arbitrary"), 32),
        name="retention",
    )(*args)


def _fox_kernel(*refs, n_q, tq, meta_only):
    if meta_only:
        q_ref, km_ref, vm_ref, cq_ref, ckm_ref, o_ref = refs
    else:
        q_ref, k_ref, v_ref, km_ref, vm_ref, cq_ref, ck_ref, ckm_ref, o_ref = refs
        ck2 = ck_ref[0] * LOG2E
    h = pl.program_id(1)
    lane = lax.broadcasted_iota(jnp.int32, (1, LANES), 1)
    km = km_ref[...]
    vm = vm_ref[...]
    ckm2 = ckm_ref[0] * LOG2E
    col_m = lax.broadcasted_iota(jnp.int32, (tq, CHUNK), 1)
    row_m = lax.broadcasted_iota(jnp.int32, (tq, CHUNK), 0)
    mask_m = col_m >= META_PAD
    if meta_only:
        mask_m = mask_m & (row_m >= col_m)
    tri = (lax.broadcasted_iota(jnp.int32, (tq, tq), 0)
           >= lax.broadcasted_iota(jnp.int32, (tq, tq), 1))

    def logits(q, kb, ckb, mask):
        t = lax.dot_general(q, kb, (((1,), (1,)), ((), ())), preferred_element_type=F32) - ckb
        return t if mask is None else jnp.where(mask, t, MASKED)

    def rowmax(t):
        return jnp.max(t, axis=1, keepdims=True)

    def rowsum(p):
        return jnp.sum(p, axis=1, keepdims=True)

    def logit_pass(qi):
        rows = slice(qi * tq, (qi + 1) * tq)
        n_prev = qi * tq
        q = q_ref[rows, :]
        ts = [logits(q, km, ckm2, mask_m)]
        if not meta_only:
            ts.append(logits(q, k_ref[rows, :], ck2[:, rows], tri))
            if qi > 0:
                ts.append(logits(q, k_ref[0:n_prev, :], ck2[:, 0:n_prev], None))
        mx = rowmax(ts[0])
        for t in ts[1:]:
            mx = jnp.maximum(mx, rowmax(t))
        return ts, mx

    def exp_pass(qi, ts, mx):
        rows = slice(qi * tq, (qi + 1) * tq)
        cq2 = LOG2E * jnp.sum(jnp.where(lane == h, cq_ref[rows, :], 0.0), axis=1, keepdims=True)
        m_row = mx + cq2
        shift = m_row - cq2
        vs = [vm]
        if not meta_only:
            vs.append(v_ref[rows, :])
            if qi > 0:
                vs.append(v_ref[0:qi * tq, :])
        l = None
        acc = None
        for t, vb in zip(ts, vs):
            p = jnp.exp2(t - shift)
            pv = jnp.dot(p.astype(BF16), vb, preferred_element_type=F32)
            l = rowsum(p) if l is None else l + rowsum(p)
            acc = pv if acc is None else acc + pv
        o_ref[rows, :] = (acc / l).astype(BF16)

    cur = logit_pass(0)
    for qi in range(n_q):
        nxt = logit_pass(qi + 1) if qi + 1 < n_q else None
        exp_pass(qi, *cur)
        cur = nxt


def _fox(proj, proj_meta, cum_col, ck_rows, ckm_rows, *, n_batch, rows_per_batch, tq, meta_only):
    n_q = rows_per_batch // tq
    qblk = lambda sec: pl.BlockSpec((rows_per_batch, HEAD_DIM), lambda b, h, s=BLK[sec]: (b, s + h))
    mblk = lambda sec: pl.BlockSpec((CHUNK, HEAD_DIM), lambda b, h, s=BLK[sec]: (0, s + h))
    cq_spec = pl.BlockSpec((rows_per_batch, LANES), lambda b, h: (b, 0))
    ckm_spec = pl.BlockSpec((1, 1, CHUNK), lambda b, h: (h, 0, 0))
    if meta_only:
        in_specs = [qblk(SEC_FQ), mblk(SEC_FK), mblk(SEC_FV), cq_spec, ckm_spec]
        args = [proj, proj_meta, proj_meta, cum_col, ckm_rows]
    else:
        ck_spec = pl.BlockSpec((1, 1, rows_per_batch), lambda b, h: (b * N_HEADS + h, 0, 0))
        in_specs = [qblk(SEC_FQ), qblk(SEC_FK), qblk(SEC_FV), mblk(SEC_FK), mblk(SEC_FV),
                    cq_spec, ck_spec, ckm_spec]
        args = [proj, proj, proj, proj_meta, proj_meta, cum_col, ck_rows, ckm_rows]
    return pl.pallas_call(
        functools.partial(_fox_kernel, n_q=n_q, tq=tq, meta_only=meta_only),
        grid=(n_batch, N_HEADS),
        in_specs=in_specs,
        out_specs=pl.BlockSpec((rows_per_batch, HEAD_DIM), lambda b, h: (b, h)),
        out_shape=jax.ShapeDtypeStruct((n_batch * rows_per_batch, FOX_DIM), BF16),
        compiler_params=_params(("arbitrary", "arbitrary"), 32),
        name="foxattn",
    )(*args)


def _outproj_kernel(ret_ref, fox_ref, w1_ref, w2_ref, x_ref, g_ref, h_ref, c_ref, *, tm, sub):
    for rs in _row_tiles(tm, sub):
        hcur = (x_ref[rs, :]
                + jnp.dot(ret_ref[rs, :], w1_ref[...], preferred_element_type=F32)
                + jnp.dot(fox_ref[rs, :], w2_ref[...], preferred_element_type=F32))
        h_ref[rs, :] = hcur
        c_ref[rs, :] = _rms(hcur, g_ref[...]).astype(BF16)


def _outproj(ret, fox, w_out_b, x2d, gain, *, tm, sub):
    m = x2d.shape[0]
    return pl.pallas_call(
        functools.partial(_outproj_kernel, tm=tm, sub=sub),
        grid=(m // tm,),
        in_specs=[
            pl.BlockSpec((tm, RET_DIM), lambda i: (i, 0)),
            pl.BlockSpec((tm, FOX_DIM), lambda i: (i, 0)),
            pl.BlockSpec((RET_DIM, D_MODEL), lambda i: (0, 0)),
            pl.BlockSpec((FOX_DIM, D_MODEL), lambda i: (1, 0)),
            pl.BlockSpec((tm, D_MODEL), lambda i: (i, 0)),
            pl.BlockSpec((1, D_MODEL), lambda i: (0, 0)),
        ],
        out_specs=[
            pl.BlockSpec((tm, D_MODEL), lambda i: (i, 0)),
            pl.BlockSpec((tm, D_MODEL), lambda i: (i, 0)),
        ],
        out_shape=[
            jax.ShapeDtypeStruct((m, D_MODEL), F32),
            jax.ShapeDtypeStruct((m, D_MODEL), BF16),
        ],
        compiler_params=_params(("arbitrary",), 56),
        name="outproj",
    )(ret, fox, w_out_b, w_out_b, x2d, gain)


def _outproj_meta_kernel(mix_ref, w_ref, x_ref, g_ref, c_ref, wq_ref, acc_scr):
    k = pl.program_id(0)
    wq_ref[...] = w_ref[...].astype(BF16)
    part = jnp.dot(mix_ref[...], wq_ref[...], preferred_element_type=F32)

    @pl.when(k == 0)
    def _():
        acc_scr[...] = x_ref[...] + part

    @pl.when(k == 1)
    def _():
        c_ref[...] = _rms(acc_scr[...] + part, g_ref[...]).astype(BF16)


def _outproj_meta(mix_m, w_out, xm, gain):
    half = D_MODEL // 2
    return pl.pallas_call(
        _outproj_meta_kernel,
        grid=(2,),
        in_specs=[
            pl.BlockSpec((CHUNK, half), lambda k: (0, k)),
            pl.BlockSpec((half, D_MODEL), lambda k: (k, 0)),
            pl.BlockSpec((CHUNK, D_MODEL), lambda k: (0, 0)),
            pl.BlockSpec((1, D_MODEL), lambda k: (0, 0)),
        ],
        out_specs=[
            pl.BlockSpec((CHUNK, D_MODEL), lambda k: (0, 0)),
            pl.BlockSpec((half, D_MODEL), lambda k: (k, 0)),
        ],
        out_shape=[
            jax.ShapeDtypeStruct((CHUNK, D_MODEL), BF16),
            jax.ShapeDtypeStruct((D_MODEL, D_MODEL), BF16),
        ],
        scratch_shapes=[pltpu.VMEM((CHUNK, D_MODEL), F32)],
        compiler_params=_params(("arbitrary",), 48),
        name="outproj_meta",
    )(mix_m, w_out, xm, gain)


def _up_kernel(c_ref, cm_ref, wg_ref, wv_ref, cwg_ref, cwv_ref, cbg_ref, cbv_ref, wd_ref,
               o_ref, wdq_ref, wg_s, wv_s, ug_s, uv_s, *, tm, sub, tiles_per_batch):
    i = pl.program_id(1)
    wdq_ref[...] = wd_ref[...].astype(BF16)

    @pl.when(i == 0)
    def _():
        wg_s[...] = wg_ref[...].astype(BF16)
        wv_s[...] = wv_ref[...].astype(BF16)

    @pl.when(i % tiles_per_batch == 0)
    def _():
        cm = cm_ref[...]
        ug_s[0:8, :] = jnp.dot(cm, wg_s[...], preferred_element_type=F32)[8:16, :]
        uv_s[0:8, :] = jnp.dot(cm, wv_s[...], preferred_element_type=F32)[8:16, :]

    def conv(c, rs, w_s, u_s, cw_ref, cb_ref):
        u_s[8 + rs.start:8 + rs.stop, :] = jnp.dot(c, w_s[...], preferred_element_type=F32)
        return (cb_ref[...]
                + cw_ref[0:1, :] * u_s[6 + rs.start:6 + rs.stop, :]
                + cw_ref[1:2, :] * u_s[7 + rs.start:7 + rs.stop, :]
                + cw_ref[2:3, :] * u_s[8 + rs.start:8 + rs.stop, :])

    for rs in _row_tiles(tm, sub):
        c = c_ref[rs, :]
        gate = conv(c, rs, wg_s, ug_s, cwg_ref, cbg_ref)
        val = conv(c, rs, wv_s, uv_s, cwv_ref, cbv_ref)
        o_ref[rs, :] = ((gate / (1.0 + jnp.exp(-gate))) * val).astype(BF16)
    ug_s[0:8, :] = ug_s[tm:tm + 8, :]
    uv_s[0:8, :] = uv_s[tm:tm + 8, :]


def _upconv(c, c_meta, w_up, conv_w, conv_b, w_down, *, rows_per_batch, tm, tn, sub):
    m = c.shape[0]
    nj = D_FF // tn
    ni = m // tm
    wd_rows = D_FF // (nj * ni)
    assert wd_rows * nj * ni == D_FF and wd_rows % 16 == 0
    tiles_per_batch = rows_per_batch // tm
    halo_blk = c_meta.shape[0] // 16 - 1
    return pl.pallas_call(
        functools.partial(_up_kernel, tm=tm, sub=sub, tiles_per_batch=tiles_per_batch),
        grid=(nj, m // tm),
        in_specs=[
            pl.BlockSpec((tm, D_MODEL), lambda j, i: (i, 0)),
            pl.BlockSpec((16, D_MODEL), lambda j, i: (halo_blk, 0)),
            pl.BlockSpec((D_MODEL, tn), lambda j, i: (0, j)),
            pl.BlockSpec((D_MODEL, tn), lambda j, i: (0, j + nj)),
            pl.BlockSpec((3, tn), lambda j, i: (0, j)),
            pl.BlockSpec((3, tn), lambda j, i: (0, j + nj)),
            pl.BlockSpec((1, tn), lambda j, i: (0, j)),
            pl.BlockSpec((1, tn), lambda j, i: (0, j + nj)),
            pl.BlockSpec((wd_rows, D_MODEL), lambda j, i: (j * ni + i, 0)),
        ],
        out_specs=[
            pl.BlockSpec((tm, tn), lambda j, i: (i, j)),
            pl.BlockSpec((wd_rows, D_MODEL), lambda j, i: (j * ni + i, 0)),
        ],
        out_shape=[
            jax.ShapeDtypeStruct((m, D_FF), BF16),
            jax.ShapeDtypeStruct((D_FF, D_MODEL), BF16),
        ],
        scratch_shapes=[
            pltpu.VMEM((D_MODEL, tn), BF16),
            pltpu.VMEM((D_MODEL, tn), BF16),
            pltpu.VMEM((tm + 8, tn), F32),
            pltpu.VMEM((tm + 8, tn), F32),
        ],
        compiler_params=_params(("arbitrary", "arbitrary"), 56),
        name="upconv",
    )(c, c_meta, w_up, w_up, conv_w, conv_w, conv_b, conv_b, w_down)


def _down_kernel(a_ref, w_ref, h_ref, g_ref, o_ref, *, tm, sub):
    for rs in _row_tiles(tm, sub):
        hcur = h_ref[rs, :] + jnp.dot(a_ref[rs, :], w_ref[...], preferred_element_type=F32)
        o_ref[rs, :] = _rms(hcur, g_ref[...])


def _down(act, w_down_b, h1, gain, *, tm, sub):
    m = act.shape[0]
    return pl.pallas_call(
        functools.partial(_down_kernel, tm=tm, sub=sub),
        grid=(m // tm,),
        in_specs=[
            pl.BlockSpec((tm, D_FF), lambda i: (i, 0)),
            pl.BlockSpec((D_FF, D_MODEL), lambda i: (0, 0), pipeline_mode=pl.Buffered(1)),
            pl.BlockSpec((tm, D_MODEL), lambda i: (i, 0)),
            pl.BlockSpec((1, D_MODEL), lambda i: (0, 0)),
        ],
        out_specs=pl.BlockSpec((tm, D_MODEL), lambda i: (i, 0)),
        out_shape=jax.ShapeDtypeStruct((m, D_MODEL), F32),
        compiler_params=_params(("arbitrary",), 58),
        name="downproj",
    )(act, w_down_b, h1, gain)


def _rotary_tables(pos):
    inv_freq = 1.0 / (ROPE_BASE ** (jnp.arange(0, HEAD_DIM, 2, dtype=F32) / HEAD_DIM))
    ang = pos[:, None] * inv_freq[None, :]
    cos, sin = jnp.cos(ang), jnp.sin(ang)
    return jnp.concatenate([cos, cos], axis=1), jnp.concatenate([-sin, sin], axis=1)


def kernel(x, meta_tokens, norm1_gain, w_in, b_forget, ret_norm_gain, w_out, norm2_gain,
           w_up, conv_w, conv_b, w_down, final_norm_gain):
    n_batch, seq, d_model = x.shape
    assert d_model == D_MODEL and seq % CHUNK == 0 and w_in.shape[0] == 1
    assert meta_tokens.shape == (N_META, D_MODEL)
    x2d = x.reshape(n_batch * seq, D_MODEL)
    xm = jnp.concatenate([jnp.zeros((META_PAD, D_MODEL), F32), meta_tokens.astype(F32)], axis=0)

    w_in_t = w_in[0].T
    b_row = jnp.pad(b_forget[0], (0, LANES - N_HEADS)).reshape(1, LANES)
    log_g = jnp.log1p(-jnp.exp2(-5.0 - jnp.arange(N_HEADS, dtype=F32)))
    g1 = norm1_gain[0].reshape(1, D_MODEL)
    g2 = norm2_gain[0].reshape(1, D_MODEL)
    gf = final_norm_gain.reshape(1, D_MODEL)
    gr = ret_norm_gain[0].reshape(1, RET_DIM)

    cos_r, sin_r = _rotary_tables(N_META + jnp.arange(seq, dtype=F32))
    cos_m, sin_m = _rotary_tables(jnp.maximum(jnp.arange(CHUNK, dtype=F32) - META_PAD, 0.0))

    proj_m, ff_m, w_main_b, w_f = _inproj(xm, g1, w_in_t, w_in_t, cos_m, sin_m,
                                          tm=CHUNK, tn=512, sub=CHUNK, cast_weights=True)
    cum_m = _cum(ff_m, b_row, rows_per_batch=CHUNK, valid_from=META_PAD, rel_last=True)
    ckm_rows = cum_m[:, :N_HEADS].T.reshape(N_HEADS, 1, CHUNK)
    ret_m = _retention(log_g, proj_m, None, gr, n_batch=1, rows_per_batch=CHUNK, meta_prefix=False)
    fox_m = _fox(proj_m, proj_m, cum_m, None, ckm_rows, n_batch=1, rows_per_batch=CHUNK,
                 tq=CHUNK, meta_only=True)
    c_m, w_out_b = _outproj_meta(jnp.concatenate([ret_m, fox_m], axis=1), w_out[0], xm, g2)

    proj, ff = _inproj(x2d, g1, w_main_b, w_f, cos_r, sin_r,
                       tm=1024, tn=1024, sub=256, cast_weights=False)
    cum = _cum(ff, b_row, rows_per_batch=seq)
    ck_rows = (cum[:, :N_HEADS].reshape(n_batch, seq, N_HEADS).transpose(0, 2, 1)
               .reshape(n_batch * N_HEADS, 1, seq))
    ret = _retention(log_g, proj, proj_m, gr, n_batch=n_batch, rows_per_batch=seq, meta_prefix=True)
    fox = _fox(proj, proj_m, cum, ck_rows, ckm_rows, n_batch=n_batch, rows_per_batch=seq,
               tq=256, meta_only=False)
    h1, c = _outproj(ret, fox, w_out_b, x2d, g2, tm=512, sub=256)
    act, w_down_b = _upconv(c, c_m, w_up[0], conv_w[0], conv_b, w_down[0],
                            rows_per_batch=seq, tm=1024, tn=512, sub=1024)
    out = _down(act, w_down_b, h1, gf, tm=512, sub=256)
    return out.reshape(n_batch, seq, D_MODEL)
```

```python
import functools

import jax
import jax.numpy as jnp
from jax import lax
from jax.experimental import pallas as pl
from jax.experimental.pallas import tpu as pltpu

F32 = jnp.float32
BF16 = jnp.bfloat16

D_MODEL = 2048
N_META = 16
CHUNK = 128
N_HEADS = 8
HEAD_DIM = 128
RET_DIM = N_HEADS * HEAD_DIM
FOX_DIM = N_HEADS * HEAD_DIM
MAIN_COLS = 4 * RET_DIM + 3 * FOX_DIM
D_FF = 5632
ROPE_BASE = 10000.0
NORM_EPS = 1e-6
META_PAD = CHUNK - N_META
MASKED = -1e30
LANES = 128
MIB = 1024 * 1024
LOG2E = 1.4426950408889634
FOXQ_SCALE = HEAD_DIM ** -0.5 * LOG2E

(SEC_RQ, SEC_RK, SEC_RV, SEC_RG, SEC_FQ, SEC_FK, SEC_FV) = range(7)
BLK = {s: s * N_HEADS for s in range(7)}


def _params(sem, vmem_mib):
    return pltpu.CompilerParams(dimension_semantics=sem, vmem_limit_bytes=vmem_mib * MIB)


def _rms(x, gain):
    ms = jnp.mean(x * x, axis=-1, keepdims=True)
    return (x * lax.rsqrt(ms + NORM_EPS)) * gain


def _row_tiles(tm, sub):
    return [slice(r * sub, (r + 1) * sub) for r in range(tm // sub)]


def _inproj_kernel(*refs, tm, tn, sub, cast_weights):
    j = pl.program_id(1)
    if cast_weights:
        x_ref, g_ref, w_ref, wf8_ref, cos_ref, sin_ref, o_ref, ff_ref, wq_ref, wf_ref, a_scr = refs
        wq_ref[...] = w_ref[...].T.astype(BF16)
        w_bf = wq_ref

        @pl.when(j == 0)
        def _():
            pad = jnp.zeros((LANES - N_HEADS, D_MODEL), F32)
            wf_ref[...] = jnp.concatenate([wf8_ref[...], pad], axis=0).astype(BF16)
    else:
        x_ref, g_ref, w_ref, wf_ref, cos_ref, sin_ref, o_ref, ff_ref, a_scr = refs
        w_bf = w_ref
    sec = j // (RET_DIM // tn)
    tiles = _row_tiles(tm, sub)

    def rotary_store(rs, acc, scale):
        cos = cos_ref[rs, :]
        sin = sin_ref[rs, :]
        for hh in range(tn // LANES):
            cols = slice(hh * LANES, (hh + 1) * LANES)
            t = acc[:, cols]
            y = t * cos + pltpu.roll(t, HEAD_DIM // 2, 1) * sin
            if scale is not None:
                y = y * scale
            o_ref[rs, cols] = y.astype(BF16)

    @pl.when(j == 0)
    def _():
        for rs in tiles:
            a = _rms(x_ref[rs, :], g_ref[...]).astype(BF16)
            a_scr[rs, :] = a
            ff_ref[rs, :] = lax.dot_general(a, wf_ref[...], (((1,), (1,)), ((), ())),
                                            preferred_element_type=F32)
            rotary_store(rs, jnp.dot(a, w_bf[...], preferred_element_type=F32), None)

    @pl.when((j > 0) & (sec <= SEC_RK))
    def _():
        scale = jnp.where(sec == SEC_RK, HEAD_DIM ** -0.5, 1.0).astype(F32)
        for rs in tiles:
            rotary_store(rs, jnp.dot(a_scr[rs, :], w_bf[...], preferred_element_type=F32), scale)

    @pl.when(sec > SEC_RK)
    def _():
        scale = jnp.where(sec == SEC_FQ, FOXQ_SCALE, 1.0).astype(F32)
        for rs in tiles:
            acc = jnp.dot(a_scr[rs, :], w_bf[...], preferred_element_type=F32)
            o_ref[rs, :] = (acc * scale).astype(BF16)


def _inproj(x2d, gain, w, w_f, cos_t, sin_t, *, tm, tn, sub, cast_weights):
    m = x2d.shape[0]
    nb = cos_t.shape[0] // tm
    out_specs = [
        pl.BlockSpec((tm, tn), lambda i, j: (i, j)),
        pl.BlockSpec((tm, LANES), lambda i, j: (i, 0)),
    ]
    out_shape = [
        jax.ShapeDtypeStruct((m, MAIN_COLS), BF16),
        jax.ShapeDtypeStruct((m, LANES), F32),
    ]
    if cast_weights:
        assert m == tm
        out_specs += [pl.BlockSpec((D_MODEL, tn), lambda i, j: (0, j)),
                      pl.BlockSpec((LANES, D_MODEL), lambda i, j: (0, 0))]
        out_shape += [jax.ShapeDtypeStruct((D_MODEL, MAIN_COLS), BF16),
                      jax.ShapeDtypeStruct((LANES, D_MODEL), BF16)]
    return pl.pallas_call(
        functools.partial(_inproj_kernel, tm=tm, tn=tn, sub=sub, cast_weights=cast_weights),
        grid=(m // tm, MAIN_COLS // tn),
        in_specs=[
            pl.BlockSpec((tm, D_MODEL), lambda i, j: (i, 0)),
            pl.BlockSpec((1, D_MODEL), lambda i, j: (0, 0)),
            (pl.BlockSpec((tn, D_MODEL), lambda i, j: (j, 0)) if cast_weights
             else pl.BlockSpec((D_MODEL, tn), lambda i, j: (0, j))),
            (pl.BlockSpec((N_HEADS, D_MODEL), lambda i, j: (MAIN_COLS // N_HEADS, 0)) if cast_weights
             else pl.BlockSpec((LANES, D_MODEL), lambda i, j: (0, 0))),
            pl.BlockSpec((tm, LANES), lambda i, j: (i % nb, 0)),
            pl.BlockSpec((tm, LANES), lambda i, j: (i % nb, 0)),
        ],
        out_specs=out_specs,
        out_shape=out_shape,
        scratch_shapes=[pltpu.VMEM((tm, D_MODEL), BF16)],
        compiler_params=_params(("arbitrary", "arbitrary"), 56),
        name="inproj",
    )(x2d, gain, w, w_f, cos_t, sin_t)


def _cum_kernel(ff_ref, b_ref, o_ref, *, n_blk, valid_from, rel_last):
    row = lax.broadcasted_iota(jnp.int32, (CHUNK, CHUNK), 0)
    col = lax.broadcasted_iota(jnp.int32, (CHUNK, CHUNK), 1)
    tri = (row >= col).astype(BF16)
    rows = lax.broadcasted_iota(jnp.int32, (CHUNK, LANES), 0)
    carry = jnp.zeros((1, LANES), F32)
    for blk in range(n_blk):
        z = ff_ref[blk * CHUNK:(blk + 1) * CHUNK, :] + b_ref[...]
        lf = jnp.minimum(z, 0.0) - jnp.log1p(jnp.exp(-jnp.abs(z)))
        if valid_from:
            lf = jnp.where(rows >= valid_from, lf, 0.0)
        hi = lf.astype(BF16)
        r1 = lf - hi.astype(F32)
        mid = r1.astype(BF16)
        lo = (r1 - mid.astype(F32)).astype(BF16)
        cum = (jnp.dot(tri, hi, preferred_element_type=F32)
               + jnp.dot(tri, mid, preferred_element_type=F32)
               + jnp.dot(tri, lo, preferred_element_type=F32)) + carry
        o_ref[blk * CHUNK:(blk + 1) * CHUNK, :] = cum
        carry = cum[CHUNK - 1:CHUNK, :]
    if rel_last:
        o_ref[...] = o_ref[...] - carry


def _cum(ff, b_row, *, rows_per_batch, valid_from=0, rel_last=False):
    m = ff.shape[0]
    return pl.pallas_call(
        functools.partial(_cum_kernel, n_blk=rows_per_batch // CHUNK,
                          valid_from=valid_from, rel_last=rel_last),
        grid=(m // rows_per_batch,),
        in_specs=[
            pl.BlockSpec((rows_per_batch, LANES), lambda b: (b, 0)),
            pl.BlockSpec((1, LANES), lambda b: (0, 0)),
        ],
        out_specs=pl.BlockSpec((rows_per_batch, LANES), lambda b: (b, 0)),
        out_shape=jax.ShapeDtypeStruct((m, LANES), F32),
        compiler_params=_params(("arbitrary",), 32),
        name="cumgate",
    )(ff, b_row)


def _ret_kernel(*refs, n_chunks, meta_prefix):
    if meta_prefix:
        logg_ref, q_ref, k_ref, v_ref, g_ref, km_ref, vm_ref, gain_ref, o_ref, state_scr = refs
    else:
        logg_ref, q_ref, k_ref, v_ref, g_ref, gain_ref, o_ref, state_scr = refs
    lg = logg_ref[pl.program_id(1)]
    ri = lax.broadcasted_iota(jnp.int32, (CHUNK, CHUNK), 0)
    ci = lax.broadcasted_iota(jnp.int32, (CHUNK, CHUNK), 1)
    diff = (ri - ci).astype(F32)
    dmat = jnp.where(diff >= 0, jnp.exp(jnp.maximum(diff, 0.0) * lg), 0.0)
    pos = lax.broadcasted_iota(jnp.int32, (CHUNK, 1), 0).astype(F32)
    xi = jnp.exp((pos + 1.0) * lg)
    zeta = jnp.exp((CHUNK - 1.0 - pos) * lg)
    g_chunk = jnp.exp(jnp.full((1, 1), float(CHUNK), F32) * lg)

    def advance(state, kc, vc):
        kz = (kc.astype(F32) * zeta).astype(BF16)
        return g_chunk * state + lax.dot_general(
            kz, vc, (((0,), (0,)), ((), ())), preferred_element_type=F32)

    state = jnp.zeros((HEAD_DIM, HEAD_DIM), F32)
    if meta_prefix:
        state = advance(state, km_ref[...], vm_ref[...])
    for c in range(n_chunks):
        state_scr[c] = state.astype(BF16)
        if c + 1 < n_chunks:
            rows = slice(c * CHUNK, (c + 1) * CHUNK)
            state = advance(state, k_ref[rows, :], v_ref[rows, :])

    gain = gain_ref[...]

    for c in range(n_chunks):
        rows = slice(c * CHUNK, (c + 1) * CHUNK)
        qc = q_ref[rows, :]
        kc = k_ref[rows, :]
        vc = v_ref[rows, :]
        s = lax.dot_general(qc, kc, (((1,), (1,)), ((), ())), preferred_element_type=F32) * dmat
        o = (jnp.dot(s.astype(BF16), vc, preferred_element_type=F32)
             + xi * jnp.dot(qc, state_scr[c], preferred_element_type=F32))
        mu = jnp.mean(o, axis=-1, keepdims=True)
        d = o - mu
        var = jnp.mean(d * d, axis=-1, keepdims=True)
        y = (d * lax.rsqrt(var + NORM_EPS)) * gain
        gt = g_ref[rows, :].astype(F32)
        silu = gt / (1.0 + jnp.exp(-gt))
        o_ref[rows, :] = (silu * y).astype(BF16)


def _retention(log_g, proj, proj_meta, gain, *, n_batch, rows_per_batch, meta_prefix):
    blk = lambda sec: pl.BlockSpec((rows_per_batch, HEAD_DIM), lambda b, h, s=BLK[sec]: (b, s + h))
    in_specs = [pl.BlockSpec(memory_space=pltpu.SMEM),
                blk(SEC_RQ), blk(SEC_RK), blk(SEC_RV), blk(SEC_RG)]
    args = [log_g, proj, proj, proj, proj]
    if meta_prefix:
        mblk = lambda sec: pl.BlockSpec((CHUNK, HEAD_DIM), lambda b, h, s=BLK[sec]: (0, s + h))
        in_specs += [mblk(SEC_RK), mblk(SEC_RV)]
        args += [proj_meta, proj_meta]
    in_specs.append(pl.BlockSpec((1, HEAD_DIM), lambda b, h: (0, h)))
    args.append(gain)
    return pl.pallas_call(
        functools.partial(_ret_kernel, n_chunks=rows_per_batch // CHUNK, meta_prefix=meta_prefix),
        grid=(n_batch, N_HEADS),
        in_specs=in_specs,
        out_specs=pl.BlockSpec((rows_per_batch, HEAD_DIM), lambda b, h: (b, h)),
        out_shape=jax.ShapeDtypeStruct((n_batch * rows_per_batch, RET_DIM), BF16),
        scratch_shapes=[pltpu.VMEM((rows_per_batch // CHUNK, HEAD_DIM, HEAD_DIM), BF16)],
        compiler_params=_params(("arbitrary", "arbitrary"), 32),
        name="retention",
    )(*args)


def _fox_kernel(*refs, n_q, tq, meta_only):
    if meta_only:
        q_ref, km_ref, vm_ref, cq_ref, ckm_ref, o_ref = refs
    else:
        q_ref, k_ref, v_ref, km_ref, vm_ref, cq_ref, ck_ref, ckm_ref, o_ref = refs
        ck2 = ck_ref[0] * LOG2E
    h = pl.program_id(1)
    lane = lax.broadcasted_iota(jnp.int32, (1, LANES), 1)
    km = km_ref[...]
    vm = vm_ref[...]
    ckm2 = ckm_ref[0] * LOG2E
    col_m = lax.broadcasted_iota(jnp.int32, (tq, CHUNK), 1)
    row_m = lax.broadcasted_iota(jnp.int32, (tq, CHUNK), 0)
    mask_m = col_m >= META_PAD
    if meta_only:
        mask_m = mask_m & (row_m >= col_m)
    tri = (lax.broadcasted_iota(jnp.int32, (tq, tq), 0)
           >= lax.broadcasted_iota(jnp.int32, (tq, tq), 1))

    def logits(q, kb, ckb, mask):
        t = lax.dot_general(q, kb, (((1,), (1,)), ((), ())), preferred_element_type=F32) - ckb
        return t if mask is None else jnp.where(mask, t, MASKED)

    def rowmax(t):
        return jnp.max(t, axis=1, keepdims=True)

    def rowsum(p):
        return jnp.sum(p, axis=1, keepdims=True)

    def logit_pass(qi):
        rows = slice(qi * tq, (qi + 1) * tq)
        n_prev = qi * tq
        q = q_ref[rows, :]
        ts = [logits(q, km, ckm2, mask_m)]
        if not meta_only:
            ts.append(logits(q, k_ref[rows, :], ck2[:, rows], tri))
            if qi > 0:
                ts.append(logits(q, k_ref[0:n_prev, :], ck2[:, 0:n_prev], None))
        mx = rowmax(ts[0])
        for t in ts[1:]:
            mx = jnp.maximum(mx, rowmax(t))
        return ts, mx

    def exp_pass(qi, ts, mx):
        rows = slice(qi * tq, (qi + 1) * tq)
        cq2 = LOG2E * jnp.sum(jnp.where(lane == h, cq_ref[rows, :], 0.0), axis=1, keepdims=True)
        m_row = mx + cq2
        shift = m_row - cq2
        vs = [vm]
        if not meta_only:
            vs.append(v_ref[rows, :])
            if qi > 0:
                vs.append(v_ref[0:qi * tq, :])
        l = None
        acc = None
        for t, vb in zip(ts, vs):
            p = jnp.exp2(t - shift)
            pv = jnp.dot(p.astype(BF16), vb, preferred_element_type=F32)
            l = rowsum(p) if l is None else l + rowsum(p)
            acc = pv if acc is None else acc + pv
        o_ref[rows, :] = (acc / l).astype(BF16)

    cur = logit_pass(0)
    for qi in range(n_q):
        nxt = logit_pass(qi + 1) if qi + 1 < n_q else None
        exp_pass(qi, *cur)
        cur = nxt


def _fox(proj, proj_meta, cum_col, ck_rows, ckm_rows, *, n_batch, rows_per_batch, tq, meta_only):
    n_q = rows_per_batch // tq
    qblk = lambda sec: pl.BlockSpec((rows_per_batch, HEAD_DIM), lambda b, h, s=BLK[sec]: (b, s + h))
    mblk = lambda sec: pl.BlockSpec((CHUNK, HEAD_DIM), lambda b, h, s=BLK[sec]: (0, s + h))
    cq_spec = pl.BlockSpec((rows_per_batch, LANES), lambda b, h: (b, 0))
    ckm_spec = pl.BlockSpec((1, 1, CHUNK), lambda b, h: (h, 0, 0))
    if meta_only:
        in_specs = [qblk(SEC_FQ), mblk(SEC_FK), mblk(SEC_FV), cq_spec, ckm_spec]
        args = [proj, proj_meta, proj_meta, cum_col, ckm_rows]
    else:
        ck_spec = pl.BlockSpec((1, 1, rows_per_batch), lambda b, h: (b * N_HEADS + h, 0, 0))
        in_specs = [qblk(SEC_FQ), qblk(SEC_FK), qblk(SEC_FV), mblk(SEC_FK), mblk(SEC_FV),
                    cq_spec, ck_spec, ckm_spec]
        args = [proj, proj, proj, proj_meta, proj_meta, cum_col, ck_rows, ckm_rows]
    return pl.pallas_call(
        functools.partial(_fox_kernel, n_q=n_q, tq=tq, meta_only=meta_only),
        grid=(n_batch, N_HEADS),
        in_specs=in_specs,
        out_specs=pl.BlockSpec((rows_per_batch, HEAD_DIM), lambda b, h: (b, h)),
        out_shape=jax.ShapeDtypeStruct((n_batch * rows_per_batch, FOX_DIM), BF16),
        compiler_params=_params(("arbitrary", "arbitrary"), 32),
        name="foxattn",
    )(*args)


def _outproj_kernel(ret_ref, fox_ref, w1_ref, w2_ref, x_ref, g_ref, h_ref, c_ref, *, tm, sub):
    for rs in _row_tiles(tm, sub):
        hcur = (x_ref[rs, :]
                + jnp.dot(ret_ref[rs, :], w1_ref[...], preferred_element_type=F32)
                + jnp.dot(fox_ref[rs, :], w2_ref[...], preferred_element_type=F32))
        h_ref[rs, :] = hcur
        c_ref[rs, :] = _rms(hcur, g_ref[...]).astype(BF16)


def _outproj(ret, fox, w_out_b, x2d, gain, *, tm, sub):
    m = x2d.shape[0]
    return pl.pallas_call(
        functools.partial(_outproj_kernel, tm=tm, sub=sub),
        grid=(m // tm,),
        in_specs=[
            pl.BlockSpec((tm, RET_DIM), lambda i: (i, 0)),
            pl.BlockSpec((tm, FOX_DIM), lambda i: (i, 0)),
            pl.BlockSpec((RET_DIM, D_MODEL), lambda i: (0, 0)),
            pl.BlockSpec((FOX_DIM, D_MODEL), lambda i: (1, 0)),
            pl.BlockSpec((tm, D_MODEL), lambda i: (i, 0)),
            pl.BlockSpec((1, D_MODEL), lambda i: (0, 0)),
        ],
        out_specs=[
            pl.BlockSpec((tm, D_MODEL), lambda i: (i, 0)),
            pl.BlockSpec((tm, D_MODEL), lambda i: (i, 0)),
        ],
        out_shape=[
            jax.ShapeDtypeStruct((m, D_MODEL), F32),
            jax.ShapeDtypeStruct((m, D_MODEL), BF16),
        ],
        compiler_params=_params(("arbitrary",), 56),
        name="outproj",
    )(ret, fox, w_out_b, w_out_b, x2d, gain)


def _outproj_meta_kernel(mix_ref, w_ref, x_ref, g_ref, c_ref, wq_ref, acc_scr):
    k = pl.program_id(0)
    wq_ref[...] = w_ref[...].astype(BF16)
    part = jnp.dot(mix_ref[...], wq_ref[...], preferred_element_type=F32)

    @pl.when(k == 0)
    def _():
        acc_scr[...] = x_ref[...] + part

    @pl.when(k == 1)
    def _():
        c_ref[...] = _rms(acc_scr[...] + part, g_ref[...]).astype(BF16)


def _outproj_meta(mix_m, w_out, xm, gain):
    half = D_MODEL // 2
    return pl.pallas_call(
        _outproj_meta_kernel,
        grid=(2,),
        in_specs=[
            pl.BlockSpec((CHUNK, half), lambda k: (0, k)),
            pl.BlockSpec((half, D_MODEL), lambda k: (k, 0)),
            pl.BlockSpec((CHUNK, D_MODEL), lambda k: (0, 0)),
            pl.BlockSpec((1, D_MODEL), lambda k: (0, 0)),
        ],
        out_specs=[
            pl.BlockSpec((CHUNK, D_MODEL), lambda k: (0, 0)),
            pl.BlockSpec((half, D_MODEL), lambda k: (k, 0)),
        ],
        out_shape=[
            jax.ShapeDtypeStruct((CHUNK, D_MODEL), BF16),
            jax.ShapeDtypeStruct((D_MODEL, D_MODEL), BF16),
        ],
        scratch_shapes=[pltpu.VMEM((CHUNK, D_MODEL), F32)],
        compiler_params=_params(("arbitrary",), 48),
        name="outproj_meta",
    )(mix_m, w_out, xm, gain)


def _up_kernel(c_ref, cm_ref, wg_ref, wv_ref, cwg_ref, cwv_ref, cbg_ref, cbv_ref, wd_ref,
               o_ref, wdq_ref, wg_s, wv_s, ug_s, uv_s, *, tm, sub, tiles_per_batch):
    i = pl.program_id(1)
    wdq_ref[...] = wd_ref[...].astype(BF16)

    @pl.when(i == 0)
    def _():
        wg_s[...] = wg_ref[...].astype(BF16)
        wv_s[...] = wv_ref[...].astype(BF16)

    @pl.when(i % tiles_per_batch == 0)
    def _():
        cm = cm_ref[...]
        ug_s[0:8, :] = jnp.dot(cm, wg_s[...], preferred_element_type=F32)[8:16, :]
        uv_s[0:8, :] = jnp.dot(cm, wv_s[...], preferred_element_type=F32)[8:16, :]

    def conv(c, rs, w_s, u_s, cw_ref, cb_ref):
        u_s[8 + rs.start:8 + rs.stop, :] = jnp.dot(c, w_s[...], preferred_element_type=F32)
        ext = u_s[rs.start:8 + rs.stop, :]
        n = rs.stop - rs.start
        return (cb_ref[...]
                + cw_ref[0:1, :] * pltpu.roll(ext, 2, 0)[8:8 + n, :]
                + cw_ref[1:2, :] * pltpu.roll(ext, 1, 0)[8:8 + n, :]
                + cw_ref[2:3, :] * ext[8:8 + n, :])

    for rs in _row_tiles(tm, sub):
        c = c_ref[rs, :]
        gate = conv(c, rs, wg_s, ug_s, cwg_ref, cbg_ref)
        val = conv(c, rs, wv_s, uv_s, cwv_ref, cbv_ref)
        o_ref[rs, :] = ((gate / (1.0 + jnp.exp(-gate))) * val).astype(BF16)
    ug_s[0:8, :] = ug_s[tm:tm + 8, :]
    uv_s[0:8, :] = uv_s[tm:tm + 8, :]


def _upconv(c, c_meta, w_up, conv_w, conv_b, w_down, *, rows_per_batch, tm, tn, sub):
    m = c.shape[0]
    nj = D_FF // tn
    ni = m // tm
    wd_rows = D_FF // (nj * ni)
    assert wd_rows * nj * ni == D_FF and wd_rows % 16 == 0
    tiles_per_batch = rows_per_batch // tm
    halo_blk = c_meta.shape[0] // 16 - 1
    return pl.pallas_call(
        functools.partial(_up_kernel, tm=tm, sub=sub, tiles_per_batch=tiles_per_batch),
        grid=(nj, m // tm),
        in_specs=[
            pl.BlockSpec((tm, D_MODEL), lambda j, i: (i, 0)),
            pl.BlockSpec((16, D_MODEL), lambda j, i: (halo_blk, 0)),
            pl.BlockSpec((D_MODEL, tn), lambda j, i: (0, j)),
            pl.BlockSpec((D_MODEL, tn), lambda j, i: (0, j + nj)),
            pl.BlockSpec((3, tn), lambda j, i: (0, j)),
            pl.BlockSpec((3, tn), lambda j, i: (0, j + nj)),
            pl.BlockSpec((1, tn), lambda j, i: (0, j)),
            pl.BlockSpec((1, tn), lambda j, i: (0, j + nj)),
            pl.BlockSpec((wd_rows, D_MODEL), lambda j, i: (j * ni + i, 0)),
        ],
        out_specs=[
            pl.BlockSpec((tm, tn), lambda j, i: (i, j)),
            pl.BlockSpec((wd_rows, D_MODEL), lambda j, i: (j * ni + i, 0)),
        ],
        out_shape=[
            jax.ShapeDtypeStruct((m, D_FF), BF16),
            jax.ShapeDtypeStruct((D_FF, D_MODEL), BF16),
        ],
        scratch_shapes=[
            pltpu.VMEM((D_MODEL, tn), BF16),
            pltpu.VMEM((D_MODEL, tn), BF16),
            pltpu.VMEM((tm + 8, tn), F32),
            pltpu.VMEM((tm + 8, tn), F32),
        ],
        compiler_params=_params(("arbitrary", "arbitrary"), 56),
        name="upconv",
    )(c, c_meta, w_up, w_up, conv_w, conv_w, conv_b, conv_b, w_down)


def _down_kernel(a_ref, w_ref, h_ref, g_ref, o_ref, *, tm, sub):
    for rs in _row_tiles(tm, sub):
        hcur = h_ref[rs, :] + jnp.dot(a_ref[rs, :], w_ref[...], preferred_element_type=F32)
        o_ref[rs, :] = _rms(hcur, g_ref[...])


def _down(act, w_down_b, h1, gain, *, tm, sub):
    m = act.shape[0]
    return pl.pallas_call(
        functools.partial(_down_kernel, tm=tm, sub=sub),
        grid=(m // tm,),
        in_specs=[
            pl.BlockSpec((tm, D_FF), lambda i: (i, 0)),
            pl.BlockSpec((D_FF, D_MODEL), lambda i: (0, 0), pipeline_mode=pl.Buffered(1)),
            pl.BlockSpec((tm, D_MODEL), lambda i: (i, 0)),
            pl.BlockSpec((1, D_MODEL), lambda i: (0, 0)),
        ],
        out_specs=pl.BlockSpec((tm, D_MODEL), lambda i: (i, 0)),
        out_shape=jax.ShapeDtypeStruct((m, D_MODEL), F32),
        compiler_params=_params(("arbitrary",), 58),
        name="downproj",
    )(act, w_down_b, h1, gain)


def _rotary_tables(pos):
    inv_freq = 1.0 / (ROPE_BASE ** (jnp.arange(0, HEAD_DIM, 2, dtype=F32) / HEAD_DIM))
    ang = pos[:, None] * inv_freq[None, :]
    cos, sin = jnp.cos(ang), jnp.sin(ang)
    return jnp.concatenate([cos, cos], axis=1), jnp.concatenate([-sin, sin], axis=1)


def kernel(x, meta_tokens, norm1_gain, w_in, b_forget, ret_norm_gain, w_out, norm2_gain,
           w_up, conv_w, conv_b, w_down, final_norm_gain):
    n_batch, seq, d_model = x.shape
    assert d_model == D_MODEL and seq % CHUNK == 0 and w_in.shape[0] == 1
    assert meta_tokens.shape == (N_META, D_MODEL)
    x2d = x.reshape(n_batch * seq, D_MODEL)
    xm = jnp.concatenate([jnp.zeros((META_PAD, D_MODEL), F32), meta_tokens.astype(F32)], axis=0)

    w_in_t = w_in[0].T
    b_row = jnp.pad(b_forget[0], (0, LANES - N_HEADS)).reshape(1, LANES)
    log_g = jnp.log1p(-jnp.exp2(-5.0 - jnp.arange(N_HEADS, dtype=F32)))
    g1 = norm1_gain[0].reshape(1, D_MODEL)
    g2 = norm2_gain[0].reshape(1, D_MODEL)
    gf = final_norm_gain.reshape(1, D_MODEL)
    gr = ret_norm_gain[0].reshape(1, RET_DIM)

    cos_r, sin_r = _rotary_tables(N_META + jnp.arange(seq, dtype=F32))
    cos_m, sin_m = _rotary_tables(jnp.maximum(jnp.arange(CHUNK, dtype=F32) - META_PAD, 0.0))

    proj_m, ff_m, w_main_b, w_f = _inproj(xm, g1, w_in_t, w_in_t, cos_m, sin_m,
                                          tm=CHUNK, tn=512, sub=CHUNK, cast_weights=True)
    cum_m = _cum(ff_m, b_row, rows_per_batch=CHUNK, valid_from=META_PAD, rel_last=True)
    ckm_rows = cum_m[:, :N_HEADS].T.reshape(N_HEADS, 1, CHUNK)
    ret_m = _retention(log_g, proj_m, None, gr, n_batch=1, rows_per_batch=CHUNK, meta_prefix=False)
    fox_m = _fox(proj_m, proj_m, cum_m, None, ckm_rows, n_batch=1, rows_per_batch=CHUNK,
                 tq=CHUNK, meta_only=True)
    c_m, w_out_b = _outproj_meta(jnp.concatenate([ret_m, fox_m], axis=1), w_out[0], xm, g2)

    proj, ff = _inproj(x2d, g1, w_main_b, w_f, cos_r, sin_r,
                       tm=1024, tn=1024, sub=256, cast_weights=False)
    cum = _cum(ff, b_row, rows_per_batch=seq)
    ck_rows = (cum[:, :N_HEADS].reshape(n_batch, seq, N_HEADS).transpose(0, 2, 1)
               .reshape(n_batch * N_HEADS, 1, seq))
    ret = _retention(log_g, proj, proj_m, gr, n_batch=n_batch, rows_per_batch=seq, meta_prefix=True)
    fox = _fox(proj, proj_m, cum, ck_rows, ckm_rows, n_batch=n_batch, rows_per_batch=seq,
               tq=256, meta_only=False)
    h1, c = _outproj(ret, fox, w_out_b, x2d, g2, tm=512, sub=256)
    act, w_down_b = _upconv(c, c_m, w_up[0], conv_w[0], conv_b, w_down[0],
                            rows_per_batch=seq, tm=2048, tn=512, sub=1024)
    out = _down(act, w_down_b, h1, gf, tm=512, sub=256)
    return out.reshape(n_batch, seq, D_MODEL)
```

```python
import functools
from typing import NamedTuple

import jax
import jax.numpy as jnp
from jax import lax
from jax.experimental import pallas as pl
from jax.experimental.pallas import tpu as pltpu

F32 = jnp.float32
BF16 = jnp.bfloat16

D_MODEL = 2048
N_META = 16
CHUNK = 128
N_HEADS = 8
HEAD_DIM = 128
RET_DIM = N_HEADS * HEAD_DIM
FOX_DIM = N_HEADS * HEAD_DIM
MAIN_COLS = 4 * RET_DIM + 3 * FOX_DIM
D_FF = 5632
ROPE_BASE = 10000.0
NORM_EPS = 1e-6
META_PAD = CHUNK - N_META
MASKED = -1e30
LANES = 128
MIB = 1024 * 1024
LOG2E = 1.4426950408889634
FOXQ_SCALE = HEAD_DIM ** -0.5 * LOG2E

(SEC_RQ, SEC_RK, SEC_RV, SEC_RG, SEC_FQ, SEC_FK, SEC_FV) = range(7)
BLK = {s: s * N_HEADS for s in range(7)}


def _params(sem, vmem_mib):
    return pltpu.CompilerParams(dimension_semantics=sem, vmem_limit_bytes=vmem_mib * MIB)


def _rms(x, gain):
    ms = jnp.mean(x * x, axis=-1, keepdims=True)
    return (x * lax.rsqrt(ms + NORM_EPS)) * gain


def _row_tiles(tm, sub):
    return [slice(r * sub, (r + 1) * sub) for r in range(tm // sub)]


class _Rows(NamedTuple):
    x_ref: object
    cos_ref: object
    sin_ref: object
    o_ref: object
    ff_ref: object
    a_scr: object
    tiles: list


def _inproj_body(j, tn, groups, g_ref, w_bf, wf_ref):
    sec = j // (RET_DIM // tn)

    def rotary_store(grp, rs, acc, scale):
        cos = grp.cos_ref[rs, :]
        sin = grp.sin_ref[rs, :]
        for hh in range(tn // LANES):
            cols = slice(hh * LANES, (hh + 1) * LANES)
            t = acc[:, cols]
            y = t * cos + pltpu.roll(t, HEAD_DIM // 2, 1) * sin
            if scale is not None:
                y = y * scale
            grp.o_ref[rs, cols] = y.astype(BF16)

    @pl.when(j == 0)
    def _():
        for grp in groups:
            for rs in grp.tiles:
                a = _rms(grp.x_ref[rs, :], g_ref[...]).astype(BF16)
                grp.a_scr[rs, :] = a
                grp.ff_ref[rs, :] = lax.dot_general(a, wf_ref[...], (((1,), (1,)), ((), ())),
                                                    preferred_element_type=F32)
                rotary_store(grp, rs, jnp.dot(a, w_bf[...], preferred_element_type=F32), None)

    @pl.when((j > 0) & (sec <= SEC_RK))
    def _():
        scale = jnp.where(sec == SEC_RK, HEAD_DIM ** -0.5, 1.0).astype(F32)
        for grp in groups:
            for rs in grp.tiles:
                acc = jnp.dot(grp.a_scr[rs, :], w_bf[...], preferred_element_type=F32)
                rotary_store(grp, rs, acc, scale)

    @pl.when(sec > SEC_RK)
    def _():
        scale = jnp.where(sec == SEC_FQ, FOXQ_SCALE, 1.0).astype(F32)
        for grp in groups:
            for rs in grp.tiles:
                acc = jnp.dot(grp.a_scr[rs, :], w_bf[...], preferred_element_type=F32)
                grp.o_ref[rs, :] = (acc * scale).astype(BF16)


def _inproj_first_kernel(xm_ref, x_ref, g_ref, w_ref, wf8_ref, cosm_ref, sinm_ref, cos_ref, sin_ref,
                         om_ref, ffm_ref, o_ref, ff_ref, wq_ref, wf_ref, am_scr, a_scr,
                         *, tm, tn, sub):
    j = pl.program_id(0)
    wq_ref[...] = w_ref[...].T.astype(BF16)

    @pl.when(j == 0)
    def _():
        pad = jnp.zeros((LANES - N_HEADS, D_MODEL), F32)
        wf_ref[...] = jnp.concatenate([wf8_ref[...], pad], axis=0).astype(BF16)

    groups = [_Rows(xm_ref, cosm_ref, sinm_ref, om_ref, ffm_ref, am_scr, _row_tiles(CHUNK, CHUNK)),
              _Rows(x_ref, cos_ref, sin_ref, o_ref, ff_ref, a_scr, _row_tiles(tm, sub))]
    _inproj_body(j, tn, groups, g_ref, wq_ref, wf_ref)


def _inproj_rest_kernel(x_ref, g_ref, w_ref, wf_ref, cos_ref, sin_ref, proj_hbm, ff_hbm,
                        o_ref, ff_ref, a_scr, *, tm, tn, sub):
    del proj_hbm, ff_hbm
    groups = [_Rows(x_ref, cos_ref, sin_ref, o_ref, ff_ref, a_scr, _row_tiles(tm, sub))]
    _inproj_body(pl.program_id(1), tn, groups, g_ref, w_ref, wf_ref)


def _inproj_first(xm, x2d, gain, w_in_t, cos_m, sin_m, cos_t, sin_t, *, tm, tn, sub):
    m = x2d.shape[0]
    const = lambda shape: pl.BlockSpec(shape, lambda j: (0, 0))
    once = lambda shape: pl.BlockSpec(shape, lambda j: (0, 0), pipeline_mode=pl.Buffered(1))
    return pl.pallas_call(
        functools.partial(_inproj_first_kernel, tm=tm, tn=tn, sub=sub),
        grid=(MAIN_COLS // tn,),
        in_specs=[
            const((CHUNK, D_MODEL)),
            once((tm, D_MODEL)),
            const((1, D_MODEL)),
            pl.BlockSpec((tn, D_MODEL), lambda j: (j, 0)),
            pl.BlockSpec((N_HEADS, D_MODEL), lambda j: (MAIN_COLS // N_HEADS, 0)),
            const((CHUNK, LANES)),
            const((CHUNK, LANES)),
            const((tm, LANES)),
            const((tm, LANES)),
        ],
        out_specs=[
            pl.BlockSpec((CHUNK, tn), lambda j: (0, j)),
            const((CHUNK, LANES)),
            pl.BlockSpec((tm, tn), lambda j: (0, j)),
            const((tm, LANES)),
            pl.BlockSpec((D_MODEL, tn), lambda j: (0, j)),
            const((LANES, D_MODEL)),
        ],
        out_shape=[
            jax.ShapeDtypeStruct((CHUNK, MAIN_COLS), BF16),
            jax.ShapeDtypeStruct((CHUNK, LANES), F32),
            jax.ShapeDtypeStruct((m, MAIN_COLS), BF16),
            jax.ShapeDtypeStruct((m, LANES), F32),
            jax.ShapeDtypeStruct((D_MODEL, MAIN_COLS), BF16),
            jax.ShapeDtypeStruct((LANES, D_MODEL), BF16),
        ],
        scratch_shapes=[pltpu.VMEM((CHUNK, D_MODEL), BF16), pltpu.VMEM((tm, D_MODEL), BF16)],
        compiler_params=_params(("arbitrary",), 56),
        name="inproj_first",
    )(xm, x2d, gain, w_in_t, w_in_t, cos_m, sin_m, cos_t, sin_t)


def _inproj_rest(x2d, gain, w_main_b, w_f, cos_t, sin_t, proj, ff, *, tm, tn, sub):
    m = x2d.shape[0]
    nb = cos_t.shape[0] // tm
    return pl.pallas_call(
        functools.partial(_inproj_rest_kernel, tm=tm, tn=tn, sub=sub),
        grid=(m // tm - 1, MAIN_COLS // tn),
        in_specs=[
            pl.BlockSpec((tm, D_MODEL), lambda i, j: (i + 1, 0)),
            pl.BlockSpec((1, D_MODEL), lambda i, j: (0, 0)),
            pl.BlockSpec((D_MODEL, tn), lambda i, j: (0, j)),
            pl.BlockSpec((LANES, D_MODEL), lambda i, j: (0, 0)),
            pl.BlockSpec((tm, LANES), lambda i, j: ((i + 1) % nb, 0)),
            pl.BlockSpec((tm, LANES), lambda i, j: ((i + 1) % nb, 0)),
            pl.BlockSpec(memory_space=pl.ANY),
            pl.BlockSpec(memory_space=pl.ANY),
        ],
        out_specs=[
            pl.BlockSpec((tm, tn), lambda i, j: (i + 1, j)),
            pl.BlockSpec((tm, LANES), lambda i, j: (i + 1, 0)),
        ],
        out_shape=[
            jax.ShapeDtypeStruct((m, MAIN_COLS), BF16),
            jax.ShapeDtypeStruct((m, LANES), F32),
        ],
        input_output_aliases={6: 0, 7: 1},
        scratch_shapes=[pltpu.VMEM((tm, D_MODEL), BF16)],
        compiler_params=_params(("arbitrary", "arbitrary"), 56),
        name="inproj_rest",
    )(x2d, gain, w_main_b, w_f, cos_t, sin_t, proj, ff)


def _cum_kernel(ff_ref, b_ref, o_ref, *, n_blk, valid_from, rel_last):
    row = lax.broadcasted_iota(jnp.int32, (CHUNK, CHUNK), 0)
    col = lax.broadcasted_iota(jnp.int32, (CHUNK, CHUNK), 1)
    tri = (row >= col).astype(BF16)
    rows = lax.broadcasted_iota(jnp.int32, (CHUNK, LANES), 0)
    carry = jnp.zeros((1, LANES), F32)
    for blk in range(n_blk):
        z = ff_ref[blk * CHUNK:(blk + 1) * CHUNK, :] + b_ref[...]
        lf = jnp.minimum(z, 0.0) - jnp.log1p(jnp.exp(-jnp.abs(z)))
        if valid_from:
            lf = jnp.where(rows >= valid_from, lf, 0.0)
        hi = lf.astype(BF16)
        r1 = lf - hi.astype(F32)
        mid = r1.astype(BF16)
        lo = (r1 - mid.astype(F32)).astype(BF16)
        cum = (jnp.dot(tri, hi, preferred_element_type=F32)
               + jnp.dot(tri, mid, preferred_element_type=F32)
               + jnp.dot(tri, lo, preferred_element_type=F32)) + carry
        o_ref[blk * CHUNK:(blk + 1) * CHUNK, :] = cum
        carry = cum[CHUNK - 1:CHUNK, :]
    if rel_last:
        o_ref[...] = o_ref[...] - carry


def _cum(ff, b_row, *, rows_per_batch, valid_from=0, rel_last=False):
    m = ff.shape[0]
    return pl.pallas_call(
        functools.partial(_cum_kernel, n_blk=rows_per_batch // CHUNK,
                          valid_from=valid_from, rel_last=rel_last),
        grid=(m // rows_per_batch,),
        in_specs=[
            pl.BlockSpec((rows_per_batch, LANES), lambda b: (b, 0)),
            pl.BlockSpec((1, LANES), lambda b: (0, 0)),
        ],
        out_specs=pl.BlockSpec((rows_per_batch, LANES), lambda b: (b, 0)),
        out_shape=jax.ShapeDtypeStruct((m, LANES), F32),
        compiler_params=_params(("arbitrary",), 32),
        name="cumgate",
    )(ff, b_row)


def _ret_kernel(*refs, n_chunks, meta_prefix):
    if meta_prefix:
        logg_ref, q_ref, k_ref, v_ref, g_ref, km_ref, vm_ref, gain_ref, o_ref, state_scr = refs
    else:
        logg_ref, q_ref, k_ref, v_ref, g_ref, gain_ref, o_ref, state_scr = refs
    lg = logg_ref[pl.program_id(1)]
    ri = lax.broadcasted_iota(jnp.int32, (CHUNK, CHUNK), 0)
    ci = lax.broadcasted_iota(jnp.int32, (CHUNK, CHUNK), 1)
    diff = (ri - ci).astype(F32)
    dmat = jnp.where(diff >= 0, jnp.exp(jnp.maximum(diff, 0.0) * lg), 0.0)
    pos = lax.broadcasted_iota(jnp.int32, (CHUNK, 1), 0).astype(F32)
    xi = jnp.exp((pos + 1.0) * lg)
    zeta = jnp.exp((CHUNK - 1.0 - pos) * lg)
    g_chunk = jnp.exp(jnp.full((1, 1), float(CHUNK), F32) * lg)

    def advance(state, kc, vc):
        kz = (kc.astype(F32) * zeta).astype(BF16)
        return g_chunk * state + lax.dot_general(
            kz, vc, (((0,), (0,)), ((), ())), preferred_element_type=F32)

    state = jnp.zeros((HEAD_DIM, HEAD_DIM), F32)
    if meta_prefix:
        state = advance(state, km_ref[...], vm_ref[...])
    for c in range(n_chunks):
        state_scr[c] = state.astype(BF16)
        if c + 1 < n_chunks:
            rows = slice(c * CHUNK, (c + 1) * CHUNK)
            state = advance(state, k_ref[rows, :], v_ref[rows, :])

    gain = gain_ref[...]

    for c in range(n_chunks):
        rows = slice(c * CHUNK, (c + 1) * CHUNK)
        qc = q_ref[rows, :]
        kc = k_ref[rows, :]
        vc = v_ref[rows, :]
        s = lax.dot_general(qc, kc, (((1,), (1,)), ((), ())), preferred_element_type=F32) * dmat
        o = (jnp.dot(s.astype(BF16), vc, preferred_element_type=F32)
             + xi * jnp.dot(qc, state_scr[c], preferred_element_type=F32))
        mu = jnp.mean(o, axis=-1, keepdims=True)
        d = o - mu
        var = jnp.mean(d * d, axis=-1, keepdims=True)
        y = (d * lax.rsqrt(var + NORM_EPS)) * gain
        gt = g_ref[rows, :].astype(F32)
        silu = gt / (1.0 + jnp.exp(-gt))
        o_ref[rows, :] = (silu * y).astype(BF16)


def _retention(log_g, proj, proj_meta, gain, *, n_batch, rows_per_batch, meta_prefix):
    blk = lambda sec: pl.BlockSpec((rows_per_batch, HEAD_DIM), lambda b, h, s=BLK[sec]: (b, s + h))
    in_specs = [pl.BlockSpec(memory_space=pltpu.SMEM),
                blk(SEC_RQ), blk(SEC_RK), blk(SEC_RV), blk(SEC_RG)]
    args = [log_g, proj, proj, proj, proj]
    if meta_prefix:
        mblk = lambda sec: pl.BlockSpec((CHUNK, HEAD_DIM), lambda b, h, s=BLK[sec]: (0, s + h))
        in_specs += [mblk(SEC_RK), mblk(SEC_RV)]
        args += [proj_meta, proj_meta]
    in_specs.append(pl.BlockSpec((1, HEAD_DIM), lambda b, h: (0, h)))
    args.append(gain)
    return pl.pallas_call(
        functools.partial(_ret_kernel, n_chunks=rows_per_batch // CHUNK, meta_prefix=meta_prefix),
        grid=(n_batch, N_HEADS),
        in_specs=in_specs,
        out_specs=pl.BlockSpec((rows_per_batch, HEAD_DIM), lambda b, h: (b, h)),
        out_shape=jax.ShapeDtypeStruct((n_batch * rows_per_batch, RET_DIM), BF16),
        scratch_shapes=[pltpu.VMEM((rows_per_batch // CHUNK, HEAD_DIM, HEAD_DIM), BF16)],
        compiler_params=_params(("arbitrary", "arbitrary"), 32),
        name="retention",
    )(*args)


def _fox_kernel(*refs, n_q, tq, meta_only):
    if meta_only:
        q_ref, km_ref, vm_ref, cq_ref, ckm_ref, o_ref = refs
    else:
        q_ref, k_ref, v_ref, km_ref, vm_ref, cq_ref, ck_ref, ckm_ref, o_ref = refs
        ck2 = ck_ref[0] * LOG2E
    h = pl.program_id(1)
    lane = lax.broadcasted_iota(jnp.int32, (1, LANES), 1)
    km = km_ref[...]
    vm = vm_ref[...]
    ckm2 = ckm_ref[0] * LOG2E
    col_m = lax.broadcasted_iota(jnp.int32, (tq, CHUNK), 1)
    row_m = lax.broadcasted_iota(jnp.int32, (tq, CHUNK), 0)
    mask_m = col_m >= META_PAD
    if meta_only:
        mask_m = mask_m & (row_m >= col_m)
    tri = (lax.broadcasted_iota(jnp.int32, (tq, tq), 0)
           >= lax.broadcasted_iota(jnp.int32, (tq, tq), 1))

    def logits(q, kb, ckb, mask):
        t = lax.dot_general(q, kb, (((1,), (1,)), ((), ())), preferred_element_type=F32) - ckb
        return t if mask is None else jnp.where(mask, t, MASKED)

    def rowmax(t):
        return jnp.max(t, axis=1, keepdims=True)

    def rowsum(p):
        return jnp.sum(p, axis=1, keepdims=True)

    def logit_pass(qi):
        rows = slice(qi * tq, (qi + 1) * tq)
        n_prev = qi * tq
        q = q_ref[rows, :]
        ts = [logits(q, km, ckm2, mask_m)]
        if not meta_only:
            ts.append(logits(q, k_ref[rows, :], ck2[:, rows], tri))
            if qi > 0:
                ts.append(logits(q, k_ref[0:n_prev, :], ck2[:, 0:n_prev], None))
        mx = rowmax(ts[0])
        for t in ts[1:]:
            mx = jnp.maximum(mx, rowmax(t))
        return ts, mx

    def exp_pass(qi, ts, mx):
        rows = slice(qi * tq, (qi + 1) * tq)
        cq2 = LOG2E * jnp.sum(jnp.where(lane == h, cq_ref[rows, :], 0.0), axis=1, keepdims=True)
        m_row = mx + cq2
        shift = m_row - cq2
        vs = [vm]
        if not meta_only:
            vs.append(v_ref[rows, :])
            if qi > 0:
                vs.append(v_ref[0:qi * tq, :])
        l = None
        acc = None
        for t, vb in zip(ts, vs):
            p = jnp.exp2(t - shift)
            pv = jnp.dot(p.astype(BF16), vb, preferred_element_type=F32)
            l = rowsum(p) if l is None else l + rowsum(p)
            acc = pv if acc is None else acc + pv
        o_ref[rows, :] = (acc / l).astype(BF16)

    cur = logit_pass(0)
    for qi in range(n_q):
        nxt = logit_pass(qi + 1) if qi + 1 < n_q else None
        exp_pass(qi, *cur)
        cur = nxt


def _fox(proj, proj_meta, cum_col, ck_rows, ckm_rows, *, n_batch, rows_per_batch, tq, meta_only):
    n_q = rows_per_batch // tq
    qblk = lambda sec: pl.BlockSpec((rows_per_batch, HEAD_DIM), lambda b, h, s=BLK[sec]: (b, s + h))
    mblk = lambda sec: pl.BlockSpec((CHUNK, HEAD_DIM), lambda b, h, s=BLK[sec]: (0, s + h))
    cq_spec = pl.BlockSpec((rows_per_batch, LANES), lambda b, h: (b, 0))
    ckm_spec = pl.BlockSpec((1, 1, CHUNK), lambda b, h: (h, 0, 0))
    if meta_only:
        in_specs = [qblk(SEC_FQ), mblk(SEC_FK), mblk(SEC_FV), cq_spec, ckm_spec]
        args = [proj, proj_meta, proj_meta, cum_col, ckm_rows]
    else:
        ck_spec = pl.BlockSpec((1, 1, rows_per_batch), lambda b, h: (b * N_HEADS + h, 0, 0))
        in_specs = [qblk(SEC_FQ), qblk(SEC_FK), qblk(SEC_FV), mblk(SEC_FK), mblk(SEC_FV),
                    cq_spec, ck_spec, ckm_spec]
        args = [proj, proj, proj, proj_meta, proj_meta, cum_col, ck_rows, ckm_rows]
    return pl.pallas_call(
        functools.partial(_fox_kernel, n_q=n_q, tq=tq, meta_only=meta_only),
        grid=(n_batch, N_HEADS),
        in_specs=in_specs,
        out_specs=pl.BlockSpec((rows_per_batch, HEAD_DIM), lambda b, h: (b, h)),
        out_shape=jax.ShapeDtypeStruct((n_batch * rows_per_batch, FOX_DIM), BF16),
        compiler_params=_params(("arbitrary", "arbitrary"), 32),
        name="foxattn",
    )(*args)


def _outproj_kernel(ret_ref, fox_ref, w1_ref, w2_ref, x_ref, g_ref, h_ref, c_ref, *, tm, sub):
    for rs in _row_tiles(tm, sub):
        hcur = (x_ref[rs, :]
                + jnp.dot(ret_ref[rs, :], w1_ref[...], preferred_element_type=F32)
                + jnp.dot(fox_ref[rs, :], w2_ref[...], preferred_element_type=F32))
        h_ref[rs, :] = hcur
        c_ref[rs, :] = _rms(hcur, g_ref[...]).astype(BF16)


def _outproj(ret, fox, w_out_b, x2d, gain, *, tm, sub):
    m = x2d.shape[0]
    return pl.pallas_call(
        functools.partial(_outproj_kernel, tm=tm, sub=sub),
        grid=(m // tm,),
        in_specs=[
            pl.BlockSpec((tm, RET_DIM), lambda i: (i, 0)),
            pl.BlockSpec((tm, FOX_DIM), lambda i: (i, 0)),
            pl.BlockSpec((RET_DIM, D_MODEL), lambda i: (0, 0)),
            pl.BlockSpec((FOX_DIM, D_MODEL), lambda i: (1, 0)),
            pl.BlockSpec((tm, D_MODEL), lambda i: (i, 0)),
            pl.BlockSpec((1, D_MODEL), lambda i: (0, 0)),
        ],
        out_specs=[
            pl.BlockSpec((tm, D_MODEL), lambda i: (i, 0)),
            pl.BlockSpec((tm, D_MODEL), lambda i: (i, 0)),
        ],
        out_shape=[
            jax.ShapeDtypeStruct((m, D_MODEL), F32),
            jax.ShapeDtypeStruct((m, D_MODEL), BF16),
        ],
        compiler_params=_params(("arbitrary",), 56),
        name="outproj",
    )(ret, fox, w_out_b, w_out_b, x2d, gain)


def _outproj_meta_kernel(mix_ref, w_ref, x_ref, g_ref, c_ref, wq_ref, acc_scr):
    k = pl.program_id(0)
    wq_ref[...] = w_ref[...].astype(BF16)
    part = jnp.dot(mix_ref[...], wq_ref[...], preferred_element_type=F32)

    @pl.when(k == 0)
    def _():
        acc_scr[...] = x_ref[...] + part

    @pl.when(k == 1)
    def _():
        c_ref[...] = _rms(acc_scr[...] + part, g_ref[...]).astype(BF16)


def _outproj_meta(mix_m, w_out, xm, gain):
    half = D_MODEL // 2
    return pl.pallas_call(
        _outproj_meta_kernel,
        grid=(2,),
        in_specs=[
            pl.BlockSpec((CHUNK, half), lambda k: (0, k)),
            pl.BlockSpec((half, D_MODEL), lambda k: (k, 0)),
            pl.BlockSpec((CHUNK, D_MODEL), lambda k: (0, 0)),
            pl.BlockSpec((1, D_MODEL), lambda k: (0, 0)),
        ],
        out_specs=[
            pl.BlockSpec((CHUNK, D_MODEL), lambda k: (0, 0)),
            pl.BlockSpec((half, D_MODEL), lambda k: (k, 0)),
        ],
        out_shape=[
            jax.ShapeDtypeStruct((CHUNK, D_MODEL), BF16),
            jax.ShapeDtypeStruct((D_MODEL, D_MODEL), BF16),
        ],
        scratch_shapes=[pltpu.VMEM((CHUNK, D_MODEL), F32)],
        compiler_params=_params(("arbitrary",), 48),
        name="outproj_meta",
    )(mix_m, w_out, xm, gain)


def _up_kernel(c_ref, cm_ref, wg_ref, wv_ref, cwg_ref, cwv_ref, cbg_ref, cbv_ref, wd_ref,
               o_ref, wdq_ref, wg_s, wv_s, ug_s, uv_s, *, tm, sub, tiles_per_batch):
    i = pl.program_id(1)
    wdq_ref[...] = wd_ref[...].astype(BF16)

    @pl.when(i == 0)
    def _():
        wg_s[...] = wg_ref[...].astype(BF16)
        wv_s[...] = wv_ref[...].astype(BF16)

    @pl.when(i % tiles_per_batch == 0)
    def _():
        cm = cm_ref[...]
        ug_s[0:8, :] = jnp.dot(cm, wg_s[...], preferred_element_type=F32)[8:16, :]
        uv_s[0:8, :] = jnp.dot(cm, wv_s[...], preferred_element_type=F32)[8:16, :]

    def conv(c, rs, w_s, u_s, cw_ref, cb_ref):
        u_s[8 + rs.start:8 + rs.stop, :] = jnp.dot(c, w_s[...], preferred_element_type=F32)
        ext = u_s[rs.start:8 + rs.stop, :]
        n = rs.stop - rs.start
        return (cb_ref[...]
                + cw_ref[0:1, :] * pltpu.roll(ext, 2, 0)[8:8 + n, :]
                + cw_ref[1:2, :] * pltpu.roll(ext, 1, 0)[8:8 + n, :]
                + cw_ref[2:3, :] * ext[8:8 + n, :])

    for rs in _row_tiles(tm, sub):
        c = c_ref[rs, :]
        gate = conv(c, rs, wg_s, ug_s, cwg_ref, cbg_ref)
        val = conv(c, rs, wv_s, uv_s, cwv_ref, cbv_ref)
        o_ref[rs, :] = ((gate / (1.0 + jnp.exp(-gate))) * val).astype(BF16)
    ug_s[0:8, :] = ug_s[tm:tm + 8, :]
    uv_s[0:8, :] = uv_s[tm:tm + 8, :]


def _upconv(c, c_meta, w_up, conv_w, conv_b, w_down, *, rows_per_batch, tm, tn, sub):
    m = c.shape[0]
    nj = D_FF // tn
    ni = m // tm
    wd_rows = D_FF // (nj * ni)
    assert wd_rows * nj * ni == D_FF and wd_rows % 16 == 0
    tiles_per_batch = rows_per_batch // tm
    halo_blk = c_meta.shape[0] // 16 - 1
    return pl.pallas_call(
        functools.partial(_up_kernel, tm=tm, sub=sub, tiles_per_batch=tiles_per_batch),
        grid=(nj, m // tm),
        in_specs=[
            pl.BlockSpec((tm, D_MODEL), lambda j, i: (i, 0)),
            pl.BlockSpec((16, D_MODEL), lambda j, i: (halo_blk, 0)),
            pl.BlockSpec((D_MODEL, tn), lambda j, i: (0, j)),
            pl.BlockSpec((D_MODEL, tn), lambda j, i: (0, j + nj)),
            pl.BlockSpec((3, tn), lambda j, i: (0, j)),
            pl.BlockSpec((3, tn), lambda j, i: (0, j + nj)),
            pl.BlockSpec((1, tn), lambda j, i: (0, j)),
            pl.BlockSpec((1, tn), lambda j, i: (0, j + nj)),
            pl.BlockSpec((wd_rows, D_MODEL), lambda j, i: (j * ni + i, 0)),
        ],
        out_specs=[
            pl.BlockSpec((tm, tn), lambda j, i: (i, j)),
            pl.BlockSpec((wd_rows, D_MODEL), lambda j, i: (j * ni + i, 0)),
        ],
        out_shape=[
            jax.ShapeDtypeStruct((m, D_FF), BF16),
            jax.ShapeDtypeStruct((D_FF, D_MODEL), BF16),
        ],
        scratch_shapes=[
            pltpu.VMEM((D_MODEL, tn), BF16),
            pltpu.VMEM((D_MODEL, tn), BF16),
            pltpu.VMEM((tm + 8, tn), F32),
            pltpu.VMEM((tm + 8, tn), F32),
        ],
        compiler_params=_params(("arbitrary", "arbitrary"), 56),
        name="upconv",
    )(c, c_meta, w_up, w_up, conv_w, conv_w, conv_b, conv_b, w_down)


def _down_kernel(a_ref, w_ref, h_ref, g_ref, o_ref, *, tm, sub):
    for rs in _row_tiles(tm, sub):
        hcur = h_ref[rs, :] + jnp.dot(a_ref[rs, :], w_ref[...], preferred_element_type=F32)
        o_ref[rs, :] = _rms(hcur, g_ref[...])


def _down(act, w_down_b, h1, gain, *, tm, sub):
    m = act.shape[0]
    return pl.pallas_call(
        functools.partial(_down_kernel, tm=tm, sub=sub),
        grid=(m // tm,),
        in_specs=[
            pl.BlockSpec((tm, D_FF), lambda i: (i, 0)),
            pl.BlockSpec((D_FF, D_MODEL), lambda i: (0, 0), pipeline_mode=pl.Buffered(1)),
            pl.BlockSpec((tm, D_MODEL), lambda i: (i, 0)),
            pl.BlockSpec((1, D_MODEL), lambda i: (0, 0)),
        ],
        out_specs=pl.BlockSpec((tm, D_MODEL), lambda i: (i, 0)),
        out_shape=jax.ShapeDtypeStruct((m, D_MODEL), F32),
        compiler_params=_params(("arbitrary",), 58),
        name="downproj",
    )(act, w_down_b, h1, gain)


def _rotary_tables(pos):
    inv_freq = 1.0 / (ROPE_BASE ** (jnp.arange(0, HEAD_DIM, 2, dtype=F32) / HEAD_DIM))
    ang = pos[:, None] * inv_freq[None, :]
    cos, sin = jnp.cos(ang), jnp.sin(ang)
    return jnp.concatenate([cos, cos], axis=1), jnp.concatenate([-sin, sin], axis=1)


def kernel(x, meta_tokens, norm1_gain, w_in, b_forget, ret_norm_gain, w_out, norm2_gain,
           w_up, conv_w, conv_b, w_down, final_norm_gain):
    n_batch, seq, d_model = x.shape
    assert d_model == D_MODEL and seq % CHUNK == 0 and w_in.shape[0] == 1
    assert meta_tokens.shape == (N_META, D_MODEL)
    x2d = x.reshape(n_batch * seq, D_MODEL)
    xm = jnp.concatenate([jnp.zeros((META_PAD, D_MODEL), F32), meta_tokens.astype(F32)], axis=0)

    w_in_t = w_in[0].T
    b_row = jnp.pad(b_forget[0], (0, LANES - N_HEADS)).reshape(1, LANES)
    log_g = jnp.log1p(-jnp.exp2(-5.0 - jnp.arange(N_HEADS, dtype=F32)))
    g1 = norm1_gain[0].reshape(1, D_MODEL)
    g2 = norm2_gain[0].reshape(1, D_MODEL)
    gf = final_norm_gain.reshape(1, D_MODEL)
    gr = ret_norm_gain[0].reshape(1, RET_DIM)

    cos_r, sin_r = _rotary_tables(N_META + jnp.arange(seq, dtype=F32))
    cos_m, sin_m = _rotary_tables(jnp.maximum(jnp.arange(CHUNK, dtype=F32) - META_PAD, 0.0))

    proj_m, ff_m, proj, ff, w_main_b, w_f = _inproj_first(
        xm, x2d, g1, w_in_t, cos_m, sin_m, cos_r, sin_r, tm=1024, tn=512, sub=256)
    cum_m = _cum(ff_m, b_row, rows_per_batch=CHUNK, valid_from=META_PAD, rel_last=True)
    ckm_rows = cum_m[:, :N_HEADS].T.reshape(N_HEADS, 1, CHUNK)
    ret_m = _retention(log_g, proj_m, None, gr, n_batch=1, rows_per_batch=CHUNK, meta_prefix=False)
    fox_m = _fox(proj_m, proj_m, cum_m, None, ckm_rows, n_batch=1, rows_per_batch=CHUNK,
                 tq=CHUNK, meta_only=True)
    c_m, w_out_b = _outproj_meta(jnp.concatenate([ret_m, fox_m], axis=1), w_out[0], xm, g2)

    proj, ff = _inproj_rest(x2d, g1, w_main_b, w_f, cos_r, sin_r, proj, ff,
                            tm=1024, tn=1024, sub=256)
    cum = _cum(ff, b_row, rows_per_batch=seq)
    ck_rows = (cum[:, :N_HEADS].reshape(n_batch, seq, N_HEADS).transpose(0, 2, 1)
               .reshape(n_batch * N_HEADS, 1, seq))
    ret = _retention(log_g, proj, proj_m, gr, n_batch=n_batch, rows_per_batch=seq, meta_prefix=True)
    fox = _fox(proj, proj_m, cum, ck_rows, ckm_rows, n_batch=n_batch, rows_per_batch=seq,
               tq=256, meta_only=False)
    h1, c = _outproj(ret, fox, w_out_b, x2d, g2, tm=512, sub=256)
    act, w_down_b = _upconv(c, c_m, w_up[0], conv_w[0], conv_b, w_down[0],
                            rows_per_batch=seq, tm=2048, tn=512, sub=1024)
    out = _down(act, w_down_b, h1, gf, tm=512, sub=256)
    return out.reshape(n_batch, seq, D_MODEL)
```

```python
import functools
from typing import NamedTuple

import jax
import jax.numpy as jnp
from jax import lax
from jax.experimental import pallas as pl
from jax.experimental.pallas import tpu as pltpu

F32 = jnp.float32
BF16 = jnp.bfloat16

D_MODEL = 2048
N_META = 16
CHUNK = 128
N_HEADS = 8
HEAD_DIM = 128
RET_DIM = N_HEADS * HEAD_DIM
FOX_DIM = N_HEADS * HEAD_DIM
MAIN_COLS = 4 * RET_DIM + 3 * FOX_DIM
D_FF = 5632
ROPE_BASE = 10000.0
NORM_EPS = 1e-6
META_PAD = CHUNK - N_META
MASKED = -1e30
LANES = 128
MIB = 1024 * 1024
LOG2E = 1.4426950408889634
FOXQ_SCALE = HEAD_DIM ** -0.5 * LOG2E

(SEC_RQ, SEC_RK, SEC_RV, SEC_RG, SEC_FQ, SEC_FK, SEC_FV) = range(7)
BLK = {s: s * N_HEADS for s in range(7)}


def _params(sem, vmem_mib):
    return pltpu.CompilerParams(dimension_semantics=sem, vmem_limit_bytes=vmem_mib * MIB)


def _rms(x, gain):
    ms = jnp.mean(x * x, axis=-1, keepdims=True)
    return (x * lax.rsqrt(ms + NORM_EPS)) * gain


def _silu(x):
    h = 0.5 * x
    return h + h * jnp.tanh(h)


def _row_tiles(tm, sub):
    return [slice(r * sub, (r + 1) * sub) for r in range(tm // sub)]


class _Rows(NamedTuple):
    x_ref: object
    cos_ref: object
    sin_ref: object
    o_ref: object
    ff_ref: object
    a_scr: object
    tiles: list


def _inproj_body(j, tn, groups, g_ref, w_bf, wf_ref):
    sec = j // (RET_DIM // tn)

    def rotary_store(grp, rs, acc, scale):
        cos = grp.cos_ref[rs, :]
        sin = grp.sin_ref[rs, :]
        for hh in range(tn // LANES):
            cols = slice(hh * LANES, (hh + 1) * LANES)
            t = acc[:, cols]
            y = t * cos + pltpu.roll(t, HEAD_DIM // 2, 1) * sin
            if scale is not None:
                y = y * scale
            grp.o_ref[rs, cols] = y.astype(BF16)

    @pl.when(j == 0)
    def _():
        for grp in groups:
            for rs in grp.tiles:
                a = _rms(grp.x_ref[rs, :], g_ref[...]).astype(BF16)
                grp.a_scr[rs, :] = a
                grp.ff_ref[rs, :] = lax.dot_general(a, wf_ref[...], (((1,), (1,)), ((), ())),
                                                    preferred_element_type=F32)
                rotary_store(grp, rs, jnp.dot(a, w_bf[...], preferred_element_type=F32), None)

    @pl.when((j > 0) & (sec <= SEC_RK))
    def _():
        scale = jnp.where(sec == SEC_RK, HEAD_DIM ** -0.5, 1.0).astype(F32)
        for grp in groups:
            for rs in grp.tiles:
                acc = jnp.dot(grp.a_scr[rs, :], w_bf[...], preferred_element_type=F32)
                rotary_store(grp, rs, acc, scale)

    @pl.when(sec > SEC_RK)
    def _():
        scale = jnp.where(sec == SEC_FQ, FOXQ_SCALE, 1.0).astype(F32)
        for grp in groups:
            for rs in grp.tiles:
                acc = jnp.dot(grp.a_scr[rs, :], w_bf[...], preferred_element_type=F32)
                grp.o_ref[rs, :] = (acc * scale).astype(BF16)


def _inproj_first_kernel(xm_ref, x_ref, g_ref, w_ref, wf8_ref, cosm_ref, sinm_ref, cos_ref, sin_ref,
                         om_ref, ffm_ref, o_ref, ff_ref, wq_ref, wf_ref, am_scr, a_scr,
                         *, tm, tn, sub):
    j = pl.program_id(0)
    wq_ref[...] = w_ref[...].T.astype(BF16)

    @pl.when(j == 0)
    def _():
        pad = jnp.zeros((LANES - N_HEADS, D_MODEL), F32)
        wf_ref[...] = jnp.concatenate([wf8_ref[...], pad], axis=0).astype(BF16)

    groups = [_Rows(xm_ref, cosm_ref, sinm_ref, om_ref, ffm_ref, am_scr, _row_tiles(CHUNK, CHUNK)),
              _Rows(x_ref, cos_ref, sin_ref, o_ref, ff_ref, a_scr, _row_tiles(tm, sub))]
    _inproj_body(j, tn, groups, g_ref, wq_ref, wf_ref)


def _inproj_rest_kernel(x_ref, g_ref, w_ref, wf_ref, cos_ref, sin_ref, proj_hbm, ff_hbm,
                        o_ref, ff_ref, a_scr, *, tm, tn, sub):
    del proj_hbm, ff_hbm
    groups = [_Rows(x_ref, cos_ref, sin_ref, o_ref, ff_ref, a_scr, _row_tiles(tm, sub))]
    _inproj_body(pl.program_id(1), tn, groups, g_ref, w_ref, wf_ref)


def _inproj_first(xm, x2d, gain, w_in_t, cos_m, sin_m, cos_t, sin_t, *, tm, tn, sub):
    m = x2d.shape[0]
    const = lambda shape: pl.BlockSpec(shape, lambda j: (0, 0))
    once = lambda shape: pl.BlockSpec(shape, lambda j: (0, 0), pipeline_mode=pl.Buffered(1))
    return pl.pallas_call(
        functools.partial(_inproj_first_kernel, tm=tm, tn=tn, sub=sub),
        grid=(MAIN_COLS // tn,),
        in_specs=[
            const((CHUNK, D_MODEL)),
            once((tm, D_MODEL)),
            const((1, D_MODEL)),
            pl.BlockSpec((tn, D_MODEL), lambda j: (j, 0)),
            pl.BlockSpec((N_HEADS, D_MODEL), lambda j: (MAIN_COLS // N_HEADS, 0)),
            const((CHUNK, LANES)),
            const((CHUNK, LANES)),
            const((tm, LANES)),
            const((tm, LANES)),
        ],
        out_specs=[
            pl.BlockSpec((CHUNK, tn), lambda j: (0, j)),
            const((CHUNK, LANES)),
            pl.BlockSpec((tm, tn), lambda j: (0, j)),
            const((tm, LANES)),
            pl.BlockSpec((D_MODEL, tn), lambda j: (0, j)),
            const((LANES, D_MODEL)),
        ],
        out_shape=[
            jax.ShapeDtypeStruct((CHUNK, MAIN_COLS), BF16),
            jax.ShapeDtypeStruct((CHUNK, LANES), F32),
            jax.ShapeDtypeStruct((m, MAIN_COLS), BF16),
            jax.ShapeDtypeStruct((m, LANES), F32),
            jax.ShapeDtypeStruct((D_MODEL, MAIN_COLS), BF16),
            jax.ShapeDtypeStruct((LANES, D_MODEL), BF16),
        ],
        scratch_shapes=[pltpu.VMEM((CHUNK, D_MODEL), BF16), pltpu.VMEM((tm, D_MODEL), BF16)],
        compiler_params=_params(("arbitrary",), 56),
        name="inproj_first",
    )(xm, x2d, gain, w_in_t, w_in_t, cos_m, sin_m, cos_t, sin_t)


def _inproj_rest(x2d, gain, w_main_b, w_f, cos_t, sin_t, proj, ff, *, tm, tn, sub):
    m = x2d.shape[0]
    nb = cos_t.shape[0] // tm
    return pl.pallas_call(
        functools.partial(_inproj_rest_kernel, tm=tm, tn=tn, sub=sub),
        grid=(m // tm - 1, MAIN_COLS // tn),
        in_specs=[
            pl.BlockSpec((tm, D_MODEL), lambda i, j: (i + 1, 0)),
            pl.BlockSpec((1, D_MODEL), lambda i, j: (0, 0)),
            pl.BlockSpec((D_MODEL, tn), lambda i, j: (0, j)),
            pl.BlockSpec((LANES, D_MODEL), lambda i, j: (0, 0)),
            pl.BlockSpec((tm, LANES), lambda i, j: ((i + 1) % nb, 0)),
            pl.BlockSpec((tm, LANES), lambda i, j: ((i + 1) % nb, 0)),
            pl.BlockSpec(memory_space=pl.ANY),
            pl.BlockSpec(memory_space=pl.ANY),
        ],
        out_specs=[
            pl.BlockSpec((tm, tn), lambda i, j: (i + 1, j)),
            pl.BlockSpec((tm, LANES), lambda i, j: (i + 1, 0)),
        ],
        out_shape=[
            jax.ShapeDtypeStruct((m, MAIN_COLS), BF16),
            jax.ShapeDtypeStruct((m, LANES), F32),
        ],
        input_output_aliases={6: 0, 7: 1},
        scratch_shapes=[pltpu.VMEM((tm, D_MODEL), BF16)],
        compiler_params=_params(("arbitrary", "arbitrary"), 56),
        name="inproj_rest",
    )(x2d, gain, w_main_b, w_f, cos_t, sin_t, proj, ff)


def _cum_kernel(ff_ref, b_ref, o_ref, *, n_blk, valid_from, rel_last):
    row = lax.broadcasted_iota(jnp.int32, (CHUNK, CHUNK), 0)
    col = lax.broadcasted_iota(jnp.int32, (CHUNK, CHUNK), 1)
    tri = (row >= col).astype(BF16)
    rows = lax.broadcasted_iota(jnp.int32, (CHUNK, LANES), 0)
    carry = jnp.zeros((1, LANES), F32)
    for blk in range(n_blk):
        z = ff_ref[blk * CHUNK:(blk + 1) * CHUNK, :] + b_ref[...]
        lf = jnp.minimum(z, 0.0) - jnp.log1p(jnp.exp(-jnp.abs(z)))
        if valid_from:
            lf = jnp.where(rows >= valid_from, lf, 0.0)
        hi = lf.astype(BF16)
        r1 = lf - hi.astype(F32)
        mid = r1.astype(BF16)
        lo = (r1 - mid.astype(F32)).astype(BF16)
        cum = (jnp.dot(tri, hi, preferred_element_type=F32)
               + jnp.dot(tri, mid, preferred_element_type=F32)
               + jnp.dot(tri, lo, preferred_element_type=F32)) + carry
        o_ref[blk * CHUNK:(blk + 1) * CHUNK, :] = cum
        carry = cum[CHUNK - 1:CHUNK, :]
    if rel_last:
        o_ref[...] = o_ref[...] - carry


def _cum(ff, b_row, *, rows_per_batch, valid_from=0, rel_last=False):
    m = ff.shape[0]
    return pl.pallas_call(
        functools.partial(_cum_kernel, n_blk=rows_per_batch // CHUNK,
                          valid_from=valid_from, rel_last=rel_last),
        grid=(m // rows_per_batch,),
        in_specs=[
            pl.BlockSpec((rows_per_batch, LANES), lambda b: (b, 0)),
            pl.BlockSpec((1, LANES), lambda b: (0, 0)),
        ],
        out_specs=pl.BlockSpec((rows_per_batch, LANES), lambda b: (b, 0)),
        out_shape=jax.ShapeDtypeStruct((m, LANES), F32),
        compiler_params=_params(("arbitrary",), 32),
        name="cumgate",
    )(ff, b_row)


def _ret_kernel(*refs, n_chunks, meta_prefix):
    if meta_prefix:
        logg_ref, q_ref, k_ref, v_ref, g_ref, km_ref, vm_ref, gain_ref, o_ref, state_scr = refs
    else:
        logg_ref, q_ref, k_ref, v_ref, g_ref, gain_ref, o_ref, state_scr = refs
    lg = logg_ref[pl.program_id(1)]
    ri = lax.broadcasted_iota(jnp.int32, (CHUNK, CHUNK), 0)
    ci = lax.broadcasted_iota(jnp.int32, (CHUNK, CHUNK), 1)
    diff = (ri - ci).astype(F32)
    dmat = jnp.where(diff >= 0, jnp.exp(jnp.maximum(diff, 0.0) * lg), 0.0)
    pos = lax.broadcasted_iota(jnp.int32, (CHUNK, 1), 0).astype(F32)
    xi = jnp.exp((pos + 1.0) * lg)
    zeta = jnp.exp((CHUNK - 1.0 - pos) * lg)
    g_chunk = jnp.exp(jnp.full((1, 1), float(CHUNK), F32) * lg)

    def advance(state, kc, vc):
        kz = (kc.astype(F32) * zeta).astype(BF16)
        return g_chunk * state + lax.dot_general(
            kz, vc, (((0,), (0,)), ((), ())), preferred_element_type=F32)

    state = jnp.zeros((HEAD_DIM, HEAD_DIM), F32)
    if meta_prefix:
        state = advance(state, km_ref[...], vm_ref[...])
    for c in range(n_chunks):
        state_scr[c] = state.astype(BF16)
        if c + 1 < n_chunks:
            rows = slice(c * CHUNK, (c + 1) * CHUNK)
            state = advance(state, k_ref[rows, :], v_ref[rows, :])

    gain = gain_ref[...]

    for c in range(n_chunks):
        rows = slice(c * CHUNK, (c + 1) * CHUNK)
        qc = q_ref[rows, :]
        kc = k_ref[rows, :]
        vc = v_ref[rows, :]
        s = lax.dot_general(qc, kc, (((1,), (1,)), ((), ())), preferred_element_type=F32) * dmat
        o = (jnp.dot(s.astype(BF16), vc, preferred_element_type=F32)
             + xi * jnp.dot(qc, state_scr[c], preferred_element_type=F32))
        mu = jnp.mean(o, axis=-1, keepdims=True)
        d = o - mu
        var = jnp.mean(d * d, axis=-1, keepdims=True)
        y = (d * lax.rsqrt(var + NORM_EPS)) * gain
        gt = g_ref[rows, :].astype(F32)
        o_ref[rows, :] = (_silu(gt) * y).astype(BF16)


def _retention(log_g, proj, proj_meta, gain, *, n_batch, rows_per_batch, meta_prefix):
    blk = lambda sec: pl.BlockSpec((rows_per_batch, HEAD_DIM), lambda b, h, s=BLK[sec]: (b, s + h))
    in_specs = [pl.BlockSpec(memory_space=pltpu.SMEM),
                blk(SEC_RQ), blk(SEC_RK), blk(SEC_RV), blk(SEC_RG)]
    args = [log_g, proj, proj, proj, proj]
    if meta_prefix:
        mblk = lambda sec: pl.BlockSpec((CHUNK, HEAD_DIM), lambda b, h, s=BLK[sec]: (0, s + h))
        in_specs += [mblk(SEC_RK), mblk(SEC_RV)]
        args += [proj_meta, proj_meta]
    in_specs.append(pl.BlockSpec((1, HEAD_DIM), lambda b, h: (0, h)))
    args.append(gain)
    return pl.pallas_call(
        functools.partial(_ret_kernel, n_chunks=rows_per_batch // CHUNK, meta_prefix=meta_prefix),
        grid=(n_batch, N_HEADS),
        in_specs=in_specs,
        out_specs=pl.BlockSpec((rows_per_batch, HEAD_DIM), lambda b, h: (b, h)),
        out_shape=jax.ShapeDtypeStruct((n_batch * rows_per_batch, RET_DIM), BF16),
        scratch_shapes=[pltpu.VMEM((rows_per_batch // CHUNK, HEAD_DIM, HEAD_DIM), BF16)],
        compiler_params=_params(("arbitrary", "arbitrary"), 32),
        name="retention",
    )(*args)


def _fox_kernel(*refs, n_q, tq, meta_only):
    if meta_only:
        q_ref, km_ref, vm_ref, cq_ref, ckm_ref, o_ref = refs
    else:
        q_ref, k_ref, v_ref, km_ref, vm_ref, cq_ref, ck_ref, ckm_ref, o_ref = refs
        ck2 = ck_ref[0] * LOG2E
    h = pl.program_id(1)
    lane = lax.broadcasted_iota(jnp.int32, (1, LANES), 1)
    km = km_ref[...]
    vm = vm_ref[...]
    ckm2 = ckm_ref[0] * LOG2E
    col_m = lax.broadcasted_iota(jnp.int32, (tq, CHUNK), 1)
    row_m = lax.broadcasted_iota(jnp.int32, (tq, CHUNK), 0)
    mask_m = col_m >= META_PAD
    if meta_only:
        mask_m = mask_m & (row_m >= col_m)
    tri = (lax.broadcasted_iota(jnp.int32, (tq, tq), 0)
           >= lax.broadcasted_iota(jnp.int32, (tq, tq), 1))

    def logits(q, kb, ckb, mask):
        t = lax.dot_general(q, kb, (((1,), (1,)), ((), ())), preferred_element_type=F32) - ckb
        return t if mask is None else jnp.where(mask, t, MASKED)

    def rowmax(t):
        return jnp.max(t, axis=1, keepdims=True)

    def rowsum(p):
        return jnp.sum(p, axis=1, keepdims=True)

    def logit_pass(qi):
        rows = slice(qi * tq, (qi + 1) * tq)
        n_prev = qi * tq
        q = q_ref[rows, :]
        ts = [logits(q, km, ckm2, mask_m)]
        if not meta_only:
            ts.append(logits(q, k_ref[rows, :], ck2[:, rows], tri))
            if qi > 0:
                ts.append(logits(q, k_ref[0:n_prev, :], ck2[:, 0:n_prev], None))
        mx = rowmax(ts[0])
        for t in ts[1:]:
            mx = jnp.maximum(mx, rowmax(t))
        return ts, mx

    def exp_pass(qi, ts, mx):
        rows = slice(qi * tq, (qi + 1) * tq)
        cq2 = LOG2E * jnp.sum(jnp.where(lane == h, cq_ref[rows, :], 0.0), axis=1, keepdims=True)
        m_row = mx + cq2
        shift = m_row - cq2
        vs = [vm]
        if not meta_only:
            vs.append(v_ref[rows, :])
            if qi > 0:
                vs.append(v_ref[0:qi * tq, :])
        l = None
        acc = None
        for t, vb in zip(ts, vs):
            p = jnp.exp2(t - shift)
            pv = jnp.dot(p.astype(BF16), vb, preferred_element_type=F32)
            l = rowsum(p) if l is None else l + rowsum(p)
            acc = pv if acc is None else acc + pv
        o_ref[rows, :] = (acc / l).astype(BF16)

    cur = logit_pass(0)
    for qi in range(n_q):
        nxt = logit_pass(qi + 1) if qi + 1 < n_q else None
        exp_pass(qi, *cur)
        cur = nxt


def _fox(proj, proj_meta, cum_col, ck_rows, ckm_rows, *, n_batch, rows_per_batch, tq, meta_only):
    n_q = rows_per_batch // tq
    qblk = lambda sec: pl.BlockSpec((rows_per_batch, HEAD_DIM), lambda b, h, s=BLK[sec]: (b, s + h))
    mblk = lambda sec: pl.BlockSpec((CHUNK, HEAD_DIM), lambda b, h, s=BLK[sec]: (0, s + h))
    cq_spec = pl.BlockSpec((rows_per_batch, LANES), lambda b, h: (b, 0))
    ckm_spec = pl.BlockSpec((1, 1, CHUNK), lambda b, h: (h, 0, 0))
    if meta_only:
        in_specs = [qblk(SEC_FQ), mblk(SEC_FK), mblk(SEC_FV), cq_spec, ckm_spec]
        args = [proj, proj_meta, proj_meta, cum_col, ckm_rows]
    else:
        ck_spec = pl.BlockSpec((1, 1, rows_per_batch), lambda b, h: (b * N_HEADS + h, 0, 0))
        in_specs = [qblk(SEC_FQ), qblk(SEC_FK), qblk(SEC_FV), mblk(SEC_FK), mblk(SEC_FV),
                    cq_spec, ck_spec, ckm_spec]
        args = [proj, proj, proj, proj_meta, proj_meta, cum_col, ck_rows, ckm_rows]
    return pl.pallas_call(
        functools.partial(_fox_kernel, n_q=n_q, tq=tq, meta_only=meta_only),
        grid=(n_batch, N_HEADS),
        in_specs=in_specs,
        out_specs=pl.BlockSpec((rows_per_batch, HEAD_DIM), lambda b, h: (b, h)),
        out_shape=jax.ShapeDtypeStruct((n_batch * rows_per_batch, FOX_DIM), BF16),
        compiler_params=_params(("arbitrary", "arbitrary"), 32),
        name="foxattn",
    )(*args)


def _outproj_kernel(ret_ref, fox_ref, w1_ref, w2_ref, x_ref, g_ref, h_ref, c_ref, *, tm, sub):
    for rs in _row_tiles(tm, sub):
        hcur = (x_ref[rs, :]
                + jnp.dot(ret_ref[rs, :], w1_ref[...], preferred_element_type=F32)
                + jnp.dot(fox_ref[rs, :], w2_ref[...], preferred_element_type=F32))
        h_ref[rs, :] = hcur
        c_ref[rs, :] = _rms(hcur, g_ref[...]).astype(BF16)


def _outproj(ret, fox, w_out_b, x2d, gain, *, tm, sub):
    m = x2d.shape[0]
    return pl.pallas_call(
        functools.partial(_outproj_kernel, tm=tm, sub=sub),
        grid=(m // tm,),
        in_specs=[
            pl.BlockSpec((tm, RET_DIM), lambda i: (i, 0)),
            pl.BlockSpec((tm, FOX_DIM), lambda i: (i, 0)),
            pl.BlockSpec((RET_DIM, D_MODEL), lambda i: (0, 0)),
            pl.BlockSpec((FOX_DIM, D_MODEL), lambda i: (1, 0)),
            pl.BlockSpec((tm, D_MODEL), lambda i: (i, 0)),
            pl.BlockSpec((1, D_MODEL), lambda i: (0, 0)),
        ],
        out_specs=[
            pl.BlockSpec((tm, D_MODEL), lambda i: (i, 0)),
            pl.BlockSpec((tm, D_MODEL), lambda i: (i, 0)),
        ],
        out_shape=[
            jax.ShapeDtypeStruct((m, D_MODEL), F32),
            jax.ShapeDtypeStruct((m, D_MODEL), BF16),
        ],
        compiler_params=_params(("arbitrary",), 56),
        name="outproj",
    )(ret, fox, w_out_b, w_out_b, x2d, gain)


def _outproj_meta_kernel(mix_ref, w_ref, x_ref, g_ref, c_ref, wq_ref, acc_scr):
    k = pl.program_id(0)
    wq_ref[...] = w_ref[...].astype(BF16)
    part = jnp.dot(mix_ref[...], wq_ref[...], preferred_element_type=F32)

    @pl.when(k == 0)
    def _():
        acc_scr[...] = x_ref[...] + part

    @pl.when(k == 1)
    def _():
        c_ref[...] = _rms(acc_scr[...] + part, g_ref[...]).astype(BF16)


def _outproj_meta(mix_m, w_out, xm, gain):
    half = D_MODEL // 2
    return pl.pallas_call(
        _outproj_meta_kernel,
        grid=(2,),
        in_specs=[
            pl.BlockSpec((CHUNK, half), lambda k: (0, k)),
            pl.BlockSpec((half, D_MODEL), lambda k: (k, 0)),
            pl.BlockSpec((CHUNK, D_MODEL), lambda k: (0, 0)),
            pl.BlockSpec((1, D_MODEL), lambda k: (0, 0)),
        ],
        out_specs=[
            pl.BlockSpec((CHUNK, D_MODEL), lambda k: (0, 0)),
            pl.BlockSpec((half, D_MODEL), lambda k: (k, 0)),
        ],
        out_shape=[
            jax.ShapeDtypeStruct((CHUNK, D_MODEL), BF16),
            jax.ShapeDtypeStruct((D_MODEL, D_MODEL), BF16),
        ],
        scratch_shapes=[pltpu.VMEM((CHUNK, D_MODEL), F32)],
        compiler_params=_params(("arbitrary",), 48),
        name="outproj_meta",
    )(mix_m, w_out, xm, gain)


def _up_kernel(c_ref, cm_ref, wg_ref, wv_ref, cwg_ref, cwv_ref, cbg_ref, cbv_ref, wd_ref,
               o_ref, wdq_ref, wg_s, wv_s, ug_s, uv_s, *, tm, sub, tiles_per_batch):
    i = pl.program_id(1)
    wdq_ref[...] = wd_ref[...].astype(BF16)

    @pl.when(i == 0)
    def _():
        wg_s[...] = wg_ref[...].astype(BF16)
        wv_s[...] = wv_ref[...].astype(BF16)

    @pl.when(i % tiles_per_batch == 0)
    def _():
        cm = cm_ref[...]
        ug_s[0:8, :] = jnp.dot(cm, wg_s[...], preferred_element_type=F32)[8:16, :]
        uv_s[0:8, :] = jnp.dot(cm, wv_s[...], preferred_element_type=F32)[8:16, :]

    def conv(c, rs, w_s, u_s, cw_ref, cb_ref):
        u_s[8 + rs.start:8 + rs.stop, :] = jnp.dot(c, w_s[...], preferred_element_type=F32)
        ext = u_s[rs.start:8 + rs.stop, :]
        n = rs.stop - rs.start
        return (cb_ref[...]
                + cw_ref[0:1, :] * pltpu.roll(ext, 2, 0)[8:8 + n, :]
                + cw_ref[1:2, :] * pltpu.roll(ext, 1, 0)[8:8 + n, :]
                + cw_ref[2:3, :] * ext[8:8 + n, :])

    for rs in _row_tiles(tm, sub):
        c = c_ref[rs, :]
        gate = conv(c, rs, wg_s, ug_s, cwg_ref, cbg_ref)
        val = conv(c, rs, wv_s, uv_s, cwv_ref, cbv_ref)
        o_ref[rs, :] = ((gate / (1.0 + jnp.exp(-gate))) * val).astype(BF16)
    ug_s[0:8, :] = ug_s[tm:tm + 8, :]
    uv_s[0:8, :] = uv_s[tm:tm + 8, :]


def _upconv(c, c_meta, w_up, conv_w, conv_b, w_down, *, rows_per_batch, tm, tn, sub):
    m = c.shape[0]
    nj = D_FF // tn
    ni = m // tm
    wd_rows = D_FF // (nj * ni)
    assert wd_rows * nj * ni == D_FF and wd_rows % 16 == 0
    tiles_per_batch = rows_per_batch // tm
    halo_blk = c_meta.shape[0] // 16 - 1
    return pl.pallas_call(
        functools.partial(_up_kernel, tm=tm, sub=sub, tiles_per_batch=tiles_per_batch),
        grid=(nj, m // tm),
        in_specs=[
            pl.BlockSpec((tm, D_MODEL), lambda j, i: (i, 0)),
            pl.BlockSpec((16, D_MODEL), lambda j, i: (halo_blk, 0)),
            pl.BlockSpec((D_MODEL, tn), lambda j, i: (0, j)),
            pl.BlockSpec((D_MODEL, tn), lambda j, i: (0, j + nj)),
            pl.BlockSpec((3, tn), lambda j, i: (0, j)),
            pl.BlockSpec((3, tn), lambda j, i: (0, j + nj)),
            pl.BlockSpec((1, tn), lambda j, i: (0, j)),
            pl.BlockSpec((1, tn), lambda j, i: (0, j + nj)),
            pl.BlockSpec((wd_rows, D_MODEL), lambda j, i: (j * ni + i, 0)),
        ],
        out_specs=[
            pl.BlockSpec((tm, tn), lambda j, i: (i, j)),
            pl.BlockSpec((wd_rows, D_MODEL), lambda j, i: (j * ni + i, 0)),
        ],
        out_shape=[
            jax.ShapeDtypeStruct((m, D_FF), BF16),
            jax.ShapeDtypeStruct((D_FF, D_MODEL), BF16),
        ],
        scratch_shapes=[
            pltpu.VMEM((D_MODEL, tn), BF16),
            pltpu.VMEM((D_MODEL, tn), BF16),
            pltpu.VMEM((tm + 8, tn), F32),
            pltpu.VMEM((tm + 8, tn), F32),
        ],
        compiler_params=_params(("arbitrary", "arbitrary"), 56),
        name="upconv",
    )(c, c_meta, w_up, w_up, conv_w, conv_w, conv_b, conv_b, w_down)


def _down_kernel(a_ref, w_ref, h_ref, g_ref, o_ref, *, tm, sub):
    for rs in _row_tiles(tm, sub):
        hcur = h_ref[rs, :] + jnp.dot(a_ref[rs, :], w_ref[...], preferred_element_type=F32)
        o_ref[rs, :] = _rms(hcur, g_ref[...])


def _down(act, w_down_b, h1, gain, *, tm, sub):
    m = act.shape[0]
    return pl.pallas_call(
        functools.partial(_down_kernel, tm=tm, sub=sub),
        grid=(m // tm,),
        in_specs=[
            pl.BlockSpec((tm, D_FF), lambda i: (i, 0)),
            pl.BlockSpec((D_FF, D_MODEL), lambda i: (0, 0), pipeline_mode=pl.Buffered(1)),
            pl.BlockSpec((tm, D_MODEL), lambda i: (i, 0)),
            pl.BlockSpec((1, D_MODEL), lambda i: (0, 0)),
        ],
        out_specs=pl.BlockSpec((tm, D_MODEL), lambda i: (i, 0)),
        out_shape=jax.ShapeDtypeStruct((m, D_MODEL), F32),
        compiler_params=_params(("arbitrary",), 58),
        name="downproj",
    )(act, w_down_b, h1, gain)


def _rotary_tables(pos):
    inv_freq = 1.0 / (ROPE_BASE ** (jnp.arange(0, HEAD_DIM, 2, dtype=F32) / HEAD_DIM))
    ang = pos[:, None] * inv_freq[None, :]
    cos, sin = jnp.cos(ang), jnp.sin(ang)
    return jnp.concatenate([cos, cos], axis=1), jnp.concatenate([-sin, sin], axis=1)


def kernel(x, meta_tokens, norm1_gain, w_in, b_forget, ret_norm_gain, w_out, norm2_gain,
           w_up, conv_w, conv_b, w_down, final_norm_gain):
    n_batch, seq, d_model = x.shape
    assert d_model == D_MODEL and seq % CHUNK == 0 and w_in.shape[0] == 1
    assert meta_tokens.shape == (N_META, D_MODEL)
    x2d = x.reshape(n_batch * seq, D_MODEL)
    xm = jnp.concatenate([jnp.zeros((META_PAD, D_MODEL), F32), meta_tokens.astype(F32)], axis=0)

    w_in_t = w_in[0].T
    b_row = jnp.pad(b_forget[0], (0, LANES - N_HEADS)).reshape(1, LANES)
    log_g = jnp.log1p(-jnp.exp2(-5.0 - jnp.arange(N_HEADS, dtype=F32)))
    g1 = norm1_gain[0].reshape(1, D_MODEL)
    g2 = norm2_gain[0].reshape(1, D_MODEL)
    gf = final_norm_gain.reshape(1, D_MODEL)
    gr = ret_norm_gain[0].reshape(1, RET_DIM)

    cos_r, sin_r = _rotary_tables(N_META + jnp.arange(seq, dtype=F32))
    cos_m, sin_m = _rotary_tables(jnp.maximum(jnp.arange(CHUNK, dtype=F32) - META_PAD, 0.0))

    proj_m, ff_m, proj, ff, w_main_b, w_f = _inproj_first(
        xm, x2d, g1, w_in_t, cos_m, sin_m, cos_r, sin_r, tm=1024, tn=512, sub=256)
    cum_m = _cum(ff_m, b_row, rows_per_batch=CHUNK, valid_from=META_PAD, rel_last=True)
    ckm_rows = cum_m[:, :N_HEADS].T.reshape(N_HEADS, 1, CHUNK)
    ret_m = _retention(log_g, proj_m, None, gr, n_batch=1, rows_per_batch=CHUNK, meta_prefix=False)
    fox_m = _fox(proj_m, proj_m, cum_m, None, ckm_rows, n_batch=1, rows_per_batch=CHUNK,
                 tq=CHUNK, meta_only=True)
    c_m, w_out_b = _outproj_meta(jnp.concatenate([ret_m, fox_m], axis=1), w_out[0], xm, g2)

    proj, ff = _inproj_rest(x2d, g1, w_main_b, w_f, cos_r, sin_r, proj, ff,
                            tm=1024, tn=1024, sub=256)
    cum = _cum(ff, b_row, rows_per_batch=seq)
    ck_rows = (cum[:, :N_HEADS].reshape(n_batch, seq, N_HEADS).transpose(0, 2, 1)
               .reshape(n_batch * N_HEADS, 1, seq))
    ret = _retention(log_g, proj, proj_m, gr, n_batch=n_batch, rows_per_batch=seq, meta_prefix=True)
    fox = _fox(proj, proj_m, cum, ck_rows, ckm_rows, n_batch=n_batch, rows_per_batch=seq,
               tq=256, meta_only=False)
    h1, c = _outproj(ret, fox, w_out_b, x2d, g2, tm=512, sub=256)
    act, w_down_b = _upconv(c, c_m, w_up[0], conv_w[0], conv_b, w_down[0],
                            rows_per_batch=seq, tm=2048, tn=512, sub=1024)
    out = _down(act, w_down_b, h1, gf, tm=512, sub=256)
    return out.reshape(n_batch, seq, D_MODEL)
```

```python
import functools
from typing import NamedTuple

import jax
import jax.numpy as jnp
from jax import lax
from jax.experimental import pallas as pl
from jax.experimental.pallas import tpu as pltpu

F32 = jnp.float32
BF16 = jnp.bfloat16

D_MODEL = 2048
N_META = 16
CHUNK = 128
N_HEADS = 8
HEAD_DIM = 128
RET_DIM = N_HEADS * HEAD_DIM
FOX_DIM = N_HEADS * HEAD_DIM
MAIN_COLS = 4 * RET_DIM + 3 * FOX_DIM
D_FF = 5632
ROPE_BASE = 10000.0
NORM_EPS = 1e-6
META_PAD = CHUNK - N_META
MASKED = -1e30
LANES = 128
MIB = 1024 * 1024
LOG2E = 1.4426950408889634
FOXQ_SCALE = HEAD_DIM ** -0.5 * LOG2E

(SEC_RQ, SEC_RK, SEC_RV, SEC_RG, SEC_FQ, SEC_FK, SEC_FV) = range(7)
BLK = {s: s * N_HEADS for s in range(7)}


def _params(sem, vmem_mib):
    return pltpu.CompilerParams(dimension_semantics=sem, vmem_limit_bytes=vmem_mib * MIB)


def _rms(x, gain):
    ms = jnp.mean(x * x, axis=-1, keepdims=True)
    return (x * lax.rsqrt(ms + NORM_EPS)) * gain


def _silu(x):
    h = 0.5 * x
    return h + h * jnp.tanh(h)


def _row_tiles(tm, sub):
    return [slice(r * sub, (r + 1) * sub) for r in range(tm // sub)]


class _Rows(NamedTuple):
    x_ref: object
    cos_ref: object
    sin_ref: object
    o_ref: object
    ff_ref: object
    a_scr: object
    tiles: list


def _inproj_body(j, tn, groups, g_ref, w_bf, wf_ref):
    sec = j // (RET_DIM // tn)

    def rotary_store(grp, rs, acc, scale):
        cos = grp.cos_ref[rs, :]
        sin = grp.sin_ref[rs, :]
        for hh in range(tn // LANES):
            cols = slice(hh * LANES, (hh + 1) * LANES)
            t = acc[:, cols]
            y = t * cos + pltpu.roll(t, HEAD_DIM // 2, 1) * sin
            if scale is not None:
                y = y * scale
            grp.o_ref[rs, cols] = y.astype(BF16)

    @pl.when(j == 0)
    def _():
        for grp in groups:
            for rs in grp.tiles:
                a = _rms(grp.x_ref[rs, :], g_ref[...]).astype(BF16)
                grp.a_scr[rs, :] = a
                grp.ff_ref[rs, :] = lax.dot_general(a, wf_ref[...], (((1,), (1,)), ((), ())),
                                                    preferred_element_type=F32)
                rotary_store(grp, rs, jnp.dot(a, w_bf[...], preferred_element_type=F32), None)

    @pl.when((j > 0) & (sec <= SEC_RK))
    def _():
        scale = jnp.where(sec == SEC_RK, HEAD_DIM ** -0.5, 1.0).astype(F32)
        for grp in groups:
            for rs in grp.tiles:
                acc = jnp.dot(grp.a_scr[rs, :], w_bf[...], preferred_element_type=F32)
                rotary_store(grp, rs, acc, scale)

    @pl.when(sec > SEC_RK)
    def _():
        scale = jnp.where(sec == SEC_FQ, FOXQ_SCALE, 1.0).astype(F32)
        for grp in groups:
            for rs in grp.tiles:
                acc = jnp.dot(grp.a_scr[rs, :], w_bf[...], preferred_element_type=F32)
                grp.o_ref[rs, :] = (acc * scale).astype(BF16)


def _inproj_first_kernel(xm_ref, x_ref, g_ref, w_ref, wf8_ref, cosm_ref, sinm_ref, cos_ref, sin_ref,
                         om_ref, ffm_ref, o_ref, ff_ref, wq_ref, wf_ref, am_scr, a_scr,
                         *, tm, tn, sub):
    j = pl.program_id(0)
    wq_ref[...] = w_ref[...].T.astype(BF16)

    @pl.when(j == 0)
    def _():
        pad = jnp.zeros((LANES - N_HEADS, D_MODEL), F32)
        wf_ref[...] = jnp.concatenate([wf8_ref[...], pad], axis=0).astype(BF16)

    groups = [_Rows(xm_ref, cosm_ref, sinm_ref, om_ref, ffm_ref, am_scr, _row_tiles(CHUNK, CHUNK)),
              _Rows(x_ref, cos_ref, sin_ref, o_ref, ff_ref, a_scr, _row_tiles(tm, sub))]
    _inproj_body(j, tn, groups, g_ref, wq_ref, wf_ref)


def _inproj_rest_kernel(x_ref, g_ref, w_ref, wf_ref, cos_ref, sin_ref, proj_hbm, ff_hbm,
                        o_ref, ff_ref, a_scr, *, tm, tn, sub):
    del proj_hbm, ff_hbm
    groups = [_Rows(x_ref, cos_ref, sin_ref, o_ref, ff_ref, a_scr, _row_tiles(tm, sub))]
    _inproj_body(pl.program_id(1), tn, groups, g_ref, w_ref, wf_ref)


def _inproj_first(xm, x2d, gain, w_in_t, cos_m, sin_m, cos_t, sin_t, *, tm, tn, sub):
    m = x2d.shape[0]
    const = lambda shape: pl.BlockSpec(shape, lambda j: (0, 0))
    once = lambda shape: pl.BlockSpec(shape, lambda j: (0, 0), pipeline_mode=pl.Buffered(1))
    return pl.pallas_call(
        functools.partial(_inproj_first_kernel, tm=tm, tn=tn, sub=sub),
        grid=(MAIN_COLS // tn,),
        in_specs=[
            const((CHUNK, D_MODEL)),
            once((tm, D_MODEL)),
            const((1, D_MODEL)),
            pl.BlockSpec((tn, D_MODEL), lambda j: (j, 0)),
            pl.BlockSpec((N_HEADS, D_MODEL), lambda j: (MAIN_COLS // N_HEADS, 0)),
            const((CHUNK, LANES)),
            const((CHUNK, LANES)),
            const((tm, LANES)),
            const((tm, LANES)),
        ],
        out_specs=[
            pl.BlockSpec((CHUNK, tn), lambda j: (0, j)),
            const((CHUNK, LANES)),
            pl.BlockSpec((tm, tn), lambda j: (0, j)),
            const((tm, LANES)),
            pl.BlockSpec((D_MODEL, tn), lambda j: (0, j)),
            const((LANES, D_MODEL)),
        ],
        out_shape=[
            jax.ShapeDtypeStruct((CHUNK, MAIN_COLS), BF16),
            jax.ShapeDtypeStruct((CHUNK, LANES), F32),
            jax.ShapeDtypeStruct((m, MAIN_COLS), BF16),
            jax.ShapeDtypeStruct((m, LANES), F32),
            jax.ShapeDtypeStruct((D_MODEL, MAIN_COLS), BF16),
            jax.ShapeDtypeStruct((LANES, D_MODEL), BF16),
        ],
        scratch_shapes=[pltpu.VMEM((CHUNK, D_MODEL), BF16), pltpu.VMEM((tm, D_MODEL), BF16)],
        compiler_params=_params(("arbitrary",), 56),
        name="inproj_first",
    )(xm, x2d, gain, w_in_t, w_in_t, cos_m, sin_m, cos_t, sin_t)


def _inproj_rest(x2d, gain, w_main_b, w_f, cos_t, sin_t, proj, ff, *, tm, tn, sub):
    m = x2d.shape[0]
    nb = cos_t.shape[0] // tm
    return pl.pallas_call(
        functools.partial(_inproj_rest_kernel, tm=tm, tn=tn, sub=sub),
        grid=(m // tm - 1, MAIN_COLS // tn),
        in_specs=[
            pl.BlockSpec((tm, D_MODEL), lambda i, j: (i + 1, 0)),
            pl.BlockSpec((1, D_MODEL), lambda i, j: (0, 0)),
            pl.BlockSpec((D_MODEL, tn), lambda i, j: (0, j)),
            pl.BlockSpec((LANES, D_MODEL), lambda i, j: (0, 0)),
            pl.BlockSpec((tm, LANES), lambda i, j: ((i + 1) % nb, 0)),
            pl.BlockSpec((tm, LANES), lambda i, j: ((i + 1) % nb, 0)),
            pl.BlockSpec(memory_space=pl.ANY),
            pl.BlockSpec(memory_space=pl.ANY),
        ],
        out_specs=[
            pl.BlockSpec((tm, tn), lambda i, j: (i + 1, j)),
            pl.BlockSpec((tm, LANES), lambda i, j: (i + 1, 0)),
        ],
        out_shape=[
            jax.ShapeDtypeStruct((m, MAIN_COLS), BF16),
            jax.ShapeDtypeStruct((m, LANES), F32),
        ],
        input_output_aliases={6: 0, 7: 1},
        scratch_shapes=[pltpu.VMEM((tm, D_MODEL), BF16)],
        compiler_params=_params(("arbitrary", "arbitrary"), 56),
        name="inproj_rest",
    )(x2d, gain, w_main_b, w_f, cos_t, sin_t, proj, ff)


def _cum_kernel(ff_ref, b_ref, o_ref, *, n_blk, valid_from, rel_last):
    row = lax.broadcasted_iota(jnp.int32, (CHUNK, CHUNK), 0)
    col = lax.broadcasted_iota(jnp.int32, (CHUNK, CHUNK), 1)
    tri = (row >= col).astype(BF16)
    rows = lax.broadcasted_iota(jnp.int32, (CHUNK, LANES), 0)
    carry = jnp.zeros((1, LANES), F32)
    for blk in range(n_blk):
        z = ff_ref[blk * CHUNK:(blk + 1) * CHUNK, :] + b_ref[...]
        lf = jnp.minimum(z, 0.0) - jnp.log1p(jnp.exp(-jnp.abs(z)))
        if valid_from:
            lf = jnp.where(rows >= valid_from, lf, 0.0)
        hi = lf.astype(BF16)
        r1 = lf - hi.astype(F32)
        mid = r1.astype(BF16)
        lo = (r1 - mid.astype(F32)).astype(BF16)
        cum = (jnp.dot(tri, hi, preferred_element_type=F32)
               + jnp.dot(tri, mid, preferred_element_type=F32)
               + jnp.dot(tri, lo, preferred_element_type=F32)) + carry
        o_ref[blk * CHUNK:(blk + 1) * CHUNK, :] = cum
        carry = cum[CHUNK - 1:CHUNK, :]
    if rel_last:
        o_ref[...] = o_ref[...] - carry


def _cum(ff, b_row, *, rows_per_batch, valid_from=0, rel_last=False):
    m = ff.shape[0]
    return pl.pallas_call(
        functools.partial(_cum_kernel, n_blk=rows_per_batch // CHUNK,
                          valid_from=valid_from, rel_last=rel_last),
        grid=(m // rows_per_batch,),
        in_specs=[
            pl.BlockSpec((rows_per_batch, LANES), lambda b: (b, 0)),
            pl.BlockSpec((1, LANES), lambda b: (0, 0)),
        ],
        out_specs=pl.BlockSpec((rows_per_batch, LANES), lambda b: (b, 0)),
        out_shape=jax.ShapeDtypeStruct((m, LANES), F32),
        compiler_params=_params(("arbitrary",), 32),
        name="cumgate",
    )(ff, b_row)


def _ret_kernel(*refs, n_chunks, meta_prefix):
    if meta_prefix:
        logg_ref, q_ref, k_ref, v_ref, g_ref, km_ref, vm_ref, gain_ref, o_ref, state_scr = refs
    else:
        logg_ref, q_ref, k_ref, v_ref, g_ref, gain_ref, o_ref, state_scr = refs
    lg = logg_ref[pl.program_id(1)]
    ri = lax.broadcasted_iota(jnp.int32, (CHUNK, CHUNK), 0)
    ci = lax.broadcasted_iota(jnp.int32, (CHUNK, CHUNK), 1)
    diff = (ri - ci).astype(F32)
    dmat = jnp.where(diff >= 0, jnp.exp(jnp.maximum(diff, 0.0) * lg), 0.0)
    pos = lax.broadcasted_iota(jnp.int32, (CHUNK, 1), 0).astype(F32)
    xi = jnp.exp((pos + 1.0) * lg)
    zeta = jnp.exp((CHUNK - 1.0 - pos) * lg)
    g_chunk = jnp.exp(jnp.full((1, 1), float(CHUNK), F32) * lg)

    def advance(state, kc, vc):
        kz = (kc.astype(F32) * zeta).astype(BF16)
        return g_chunk * state + lax.dot_general(
            kz, vc, (((0,), (0,)), ((), ())), preferred_element_type=F32)

    state = jnp.zeros((HEAD_DIM, HEAD_DIM), F32)
    if meta_prefix:
        state = advance(state, km_ref[...], vm_ref[...])
    for c in range(n_chunks):
        state_scr[c] = state.astype(BF16)
        if c + 1 < n_chunks:
            rows = slice(c * CHUNK, (c + 1) * CHUNK)
            state = advance(state, k_ref[rows, :], v_ref[rows, :])

    gain = gain_ref[...]

    for c in range(n_chunks):
        rows = slice(c * CHUNK, (c + 1) * CHUNK)
        qc = q_ref[rows, :]
        kc = k_ref[rows, :]
        vc = v_ref[rows, :]
        s = lax.dot_general(qc, kc, (((1,), (1,)), ((), ())), preferred_element_type=F32) * dmat
        o = (jnp.dot(s.astype(BF16), vc, preferred_element_type=F32)
             + xi * jnp.dot(qc, state_scr[c], preferred_element_type=F32))
        mu = jnp.mean(o, axis=-1, keepdims=True)
        d = o - mu
        var = jnp.mean(d * d, axis=-1, keepdims=True)
        y = (d * lax.rsqrt(var + NORM_EPS)) * gain
        gt = g_ref[rows, :].astype(F32)
        o_ref[rows, :] = (_silu(gt) * y).astype(BF16)


def _retention(log_g, proj, proj_meta, gain, *, n_batch, rows_per_batch, meta_prefix):
    blk = lambda sec: pl.BlockSpec((rows_per_batch, HEAD_DIM), lambda b, h, s=BLK[sec]: (b, s + h))
    in_specs = [pl.BlockSpec(memory_space=pltpu.SMEM),
                blk(SEC_RQ), blk(SEC_RK), blk(SEC_RV), blk(SEC_RG)]
    args = [log_g, proj, proj, proj, proj]
    if meta_prefix:
        mblk = lambda sec: pl.BlockSpec((CHUNK, HEAD_DIM), lambda b, h, s=BLK[sec]: (0, s + h))
        in_specs += [mblk(SEC_RK), mblk(SEC_RV)]
        args += [proj_meta, proj_meta]
    in_specs.append(pl.BlockSpec((1, HEAD_DIM), lambda b, h: (0, h)))
    args.append(gain)
    return pl.pallas_call(
        functools.partial(_ret_kernel, n_chunks=rows_per_batch // CHUNK, meta_prefix=meta_prefix),
        grid=(n_batch, N_HEADS),
        in_specs=in_specs,
        out_specs=pl.BlockSpec((rows_per_batch, HEAD_DIM), lambda b, h: (b, h)),
        out_shape=jax.ShapeDtypeStruct((n_batch * rows_per_batch, RET_DIM), BF16),
        scratch_shapes=[pltpu.VMEM((rows_per_batch // CHUNK, HEAD_DIM, HEAD_DIM), BF16)],
        compiler_params=_params(("arbitrary", "arbitrary"), 32),
        name="retention",
    )(*args)


def _fox_kernel(*refs, n_q, tq, meta_only):
    if meta_only:
        q_ref, km_ref, vm_ref, cq_ref, ckm_ref, o_ref, k_all, v_all = refs
        ck_all = ckm_ref[0] * LOG2E
    else:
        q_ref, k_ref, v_ref, km_ref, vm_ref, cq_ref, ck_ref, ckm_ref, o_ref, k_all, v_all = refs
        k_all[CHUNK:, :] = k_ref[...]
        v_all[CHUNK:, :] = v_ref[...]
        ck_all = jnp.concatenate([ckm_ref[0], ck_ref[0]], axis=1) * LOG2E
    k_all[0:CHUNK, :] = km_ref[...]
    v_all[0:CHUNK, :] = vm_ref[...]
    h = pl.program_id(1)
    lane = lax.broadcasted_iota(jnp.int32, (1, LANES), 1)
    col_m = lax.broadcasted_iota(jnp.int32, (tq, CHUNK), 1)
    row_m = lax.broadcasted_iota(jnp.int32, (tq, CHUNK), 0)
    mask_m = col_m >= META_PAD
    if meta_only:
        mask_m = mask_m & (row_m >= col_m)
    tri = (lax.broadcasted_iota(jnp.int32, (tq, tq), 0)
           >= lax.broadcasted_iota(jnp.int32, (tq, tq), 1))

    def n_keys(qi):
        return CHUNK if meta_only else CHUNK + (qi + 1) * tq

    def logit_pass(qi):
        n = n_keys(qi)
        q = q_ref[qi * tq:(qi + 1) * tq, :]
        t = lax.dot_general(q, k_all[0:n, :], (((1,), (1,)), ((), ())),
                            preferred_element_type=F32) - ck_all[:, 0:n]
        parts = [jnp.where(mask_m, t[:, 0:CHUNK], MASKED)]
        if not meta_only:
            if qi > 0:
                parts.append(t[:, CHUNK:n - tq])
            parts.append(jnp.where(tri, t[:, n - tq:n], MASKED))
        t = parts[0] if len(parts) == 1 else jnp.concatenate(parts, axis=1)
        return t, jnp.max(t, axis=1, keepdims=True)

    def exp_pass(qi, t, mx):
        rows = slice(qi * tq, (qi + 1) * tq)
        cq2 = LOG2E * jnp.sum(jnp.where(lane == h, cq_ref[rows, :], 0.0), axis=1, keepdims=True)
        m_row = mx + cq2
        p = jnp.exp2(t - (m_row - cq2))
        l = jnp.sum(p, axis=1, keepdims=True)
        acc = jnp.dot(p.astype(BF16), v_all[0:n_keys(qi), :], preferred_element_type=F32)
        o_ref[rows, :] = (acc / l).astype(BF16)

    cur = logit_pass(0)
    for qi in range(n_q):
        nxt = logit_pass(qi + 1) if qi + 1 < n_q else None
        exp_pass(qi, *cur)
        cur = nxt


def _fox(proj, proj_meta, cum_col, ck_rows, ckm_rows, *, n_batch, rows_per_batch, tq, meta_only):
    n_q = rows_per_batch // tq
    n_all = CHUNK if meta_only else CHUNK + rows_per_batch
    qblk = lambda sec: pl.BlockSpec((rows_per_batch, HEAD_DIM), lambda b, h, s=BLK[sec]: (b, s + h))
    mblk = lambda sec: pl.BlockSpec((CHUNK, HEAD_DIM), lambda b, h, s=BLK[sec]: (0, s + h))
    cq_spec = pl.BlockSpec((rows_per_batch, LANES), lambda b, h: (b, 0))
    ckm_spec = pl.BlockSpec((1, 1, CHUNK), lambda b, h: (h, 0, 0))
    if meta_only:
        in_specs = [qblk(SEC_FQ), mblk(SEC_FK), mblk(SEC_FV), cq_spec, ckm_spec]
        args = [proj, proj_meta, proj_meta, cum_col, ckm_rows]
    else:
        ck_spec = pl.BlockSpec((1, 1, rows_per_batch), lambda b, h: (b * N_HEADS + h, 0, 0))
        in_specs = [qblk(SEC_FQ), qblk(SEC_FK), qblk(SEC_FV), mblk(SEC_FK), mblk(SEC_FV),
                    cq_spec, ck_spec, ckm_spec]
        args = [proj, proj, proj, proj_meta, proj_meta, cum_col, ck_rows, ckm_rows]
    return pl.pallas_call(
        functools.partial(_fox_kernel, n_q=n_q, tq=tq, meta_only=meta_only),
        grid=(n_batch, N_HEADS),
        in_specs=in_specs,
        out_specs=pl.BlockSpec((rows_per_batch, HEAD_DIM), lambda b, h: (b, h)),
        out_shape=jax.ShapeDtypeStruct((n_batch * rows_per_batch, FOX_DIM), BF16),
        scratch_shapes=[pltpu.VMEM((n_all, HEAD_DIM), BF16)] * 2,
        compiler_params=_params(("arbitrary", "arbitrary"), 32),
        name="foxattn",
    )(*args)


def _outproj_kernel(ret_ref, fox_ref, w1_ref, w2_ref, x_ref, g_ref, h_ref, c_ref, *, tm, sub):
    for rs in _row_tiles(tm, sub):
        hcur = (x_ref[rs, :]
                + jnp.dot(ret_ref[rs, :], w1_ref[...], preferred_element_type=F32)
                + jnp.dot(fox_ref[rs, :], w2_ref[...], preferred_element_type=F32))
        h_ref[rs, :] = hcur
        c_ref[rs, :] = _rms(hcur, g_ref[...]).astype(BF16)


def _outproj(ret, fox, w_out_b, x2d, gain, *, tm, sub):
    m = x2d.shape[0]
    return pl.pallas_call(
        functools.partial(_outproj_kernel, tm=tm, sub=sub),
        grid=(m // tm,),
        in_specs=[
            pl.BlockSpec((tm, RET_DIM), lambda i: (i, 0)),
            pl.BlockSpec((tm, FOX_DIM), lambda i: (i, 0)),
            pl.BlockSpec((RET_DIM, D_MODEL), lambda i: (0, 0)),
            pl.BlockSpec((FOX_DIM, D_MODEL), lambda i: (1, 0)),
            pl.BlockSpec((tm, D_MODEL), lambda i: (i, 0)),
            pl.BlockSpec((1, D_MODEL), lambda i: (0, 0)),
        ],
        out_specs=[
            pl.BlockSpec((tm, D_MODEL), lambda i: (i, 0)),
            pl.BlockSpec((tm, D_MODEL), lambda i: (i, 0)),
        ],
        out_shape=[
            jax.ShapeDtypeStruct((m, D_MODEL), F32),
            jax.ShapeDtypeStruct((m, D_MODEL), BF16),
        ],
        compiler_params=_params(("arbitrary",), 56),
        name="outproj",
    )(ret, fox, w_out_b, w_out_b, x2d, gain)


def _outproj_meta_kernel(mix_ref, w_ref, x_ref, g_ref, c_ref, wq_ref, acc_scr):
    k = pl.program_id(0)
    wq_ref[...] = w_ref[...].astype(BF16)
    part = jnp.dot(mix_ref[...], wq_ref[...], preferred_element_type=F32)

    @pl.when(k == 0)
    def _():
        acc_scr[...] = x_ref[...] + part

    @pl.when(k == 1)
    def _():
        c_ref[...] = _rms(acc_scr[...] + part, g_ref[...]).astype(BF16)


def _outproj_meta(mix_m, w_out, xm, gain):
    half = D_MODEL // 2
    return pl.pallas_call(
        _outproj_meta_kernel,
        grid=(2,),
        in_specs=[
            pl.BlockSpec((CHUNK, half), lambda k: (0, k)),
            pl.BlockSpec((half, D_MODEL), lambda k: (k, 0)),
            pl.BlockSpec((CHUNK, D_MODEL), lambda k: (0, 0)),
            pl.BlockSpec((1, D_MODEL), lambda k: (0, 0)),
        ],
        out_specs=[
            pl.BlockSpec((CHUNK, D_MODEL), lambda k: (0, 0)),
            pl.BlockSpec((half, D_MODEL), lambda k: (k, 0)),
        ],
        out_shape=[
            jax.ShapeDtypeStruct((CHUNK, D_MODEL), BF16),
            jax.ShapeDtypeStruct((D_MODEL, D_MODEL), BF16),
        ],
        scratch_shapes=[pltpu.VMEM((CHUNK, D_MODEL), F32)],
        compiler_params=_params(("arbitrary",), 48),
        name="outproj_meta",
    )(mix_m, w_out, xm, gain)


def _up_kernel(c_ref, cm_ref, wg_ref, wv_ref, cwg_ref, cwv_ref, cbg_ref, cbv_ref, wd_ref,
               o_ref, wdq_ref, wg_s, wv_s, ug_s, uv_s, *, tm, sub, tiles_per_batch):
    i = pl.program_id(1)
    wdq_ref[...] = wd_ref[...].astype(BF16)

    @pl.when(i == 0)
    def _():
        wg_s[...] = wg_ref[...].astype(BF16)
        wv_s[...] = wv_ref[...].astype(BF16)

    @pl.when(i % tiles_per_batch == 0)
    def _():
        cm = cm_ref[...]
        ug_s[0:8, :] = jnp.dot(cm, wg_s[...], preferred_element_type=F32)[8:16, :]
        uv_s[0:8, :] = jnp.dot(cm, wv_s[...], preferred_element_type=F32)[8:16, :]

    def conv(c, rs, w_s, u_s, cw_ref, cb_ref):
        u_s[8 + rs.start:8 + rs.stop, :] = jnp.dot(c, w_s[...], preferred_element_type=F32)
        ext = u_s[rs.start:8 + rs.stop, :]
        n = rs.stop - rs.start
        return (cb_ref[...]
                + cw_ref[0:1, :] * pltpu.roll(ext, 2, 0)[8:8 + n, :]
                + cw_ref[1:2, :] * pltpu.roll(ext, 1, 0)[8:8 + n, :]
                + cw_ref[2:3, :] * ext[8:8 + n, :])

    for rs in _row_tiles(tm, sub):
        c = c_ref[rs, :]
        gate = conv(c, rs, wg_s, ug_s, cwg_ref, cbg_ref)
        val = conv(c, rs, wv_s, uv_s, cwv_ref, cbv_ref)
        o_ref[rs, :] = ((gate / (1.0 + jnp.exp(-gate))) * val).astype(BF16)
    ug_s[0:8, :] = ug_s[tm:tm + 8, :]
    uv_s[0:8, :] = uv_s[tm:tm + 8, :]


def _upconv(c, c_meta, w_up, conv_w, conv_b, w_down, *, rows_per_batch, tm, tn, sub):
    m = c.shape[0]
    nj = D_FF // tn
    ni = m // tm
    wd_rows = D_FF // (nj * ni)
    assert wd_rows * nj * ni == D_FF and wd_rows % 16 == 0
    tiles_per_batch = rows_per_batch // tm
    halo_blk = c_meta.shape[0] // 16 - 1
    return pl.pallas_call(
        functools.partial(_up_kernel, tm=tm, sub=sub, tiles_per_batch=tiles_per_batch),
        grid=(nj, m // tm),
        in_specs=[
            pl.BlockSpec((tm, D_MODEL), lambda j, i: (i, 0)),
            pl.BlockSpec((16, D_MODEL), lambda j, i: (halo_blk, 0)),
            pl.BlockSpec((D_MODEL, tn), lambda j, i: (0, j)),
            pl.BlockSpec((D_MODEL, tn), lambda j, i: (0, j + nj)),
            pl.BlockSpec((3, tn), lambda j, i: (0, j)),
            pl.BlockSpec((3, tn), lambda j, i: (0, j + nj)),
            pl.BlockSpec((1, tn), lambda j, i: (0, j)),
            pl.BlockSpec((1, tn), lambda j, i: (0, j + nj)),
            pl.BlockSpec((wd_rows, D_MODEL), lambda j, i: (j * ni + i, 0)),
        ],
        out_specs=[
            pl.BlockSpec((tm, tn), lambda j, i: (i, j)),
            pl.BlockSpec((wd_rows, D_MODEL), lambda j, i: (j * ni + i, 0)),
        ],
        out_shape=[
            jax.ShapeDtypeStruct((m, D_FF), BF16),
            jax.ShapeDtypeStruct((D_FF, D_MODEL), BF16),
        ],
        scratch_shapes=[
            pltpu.VMEM((D_MODEL, tn), BF16),
            pltpu.VMEM((D_MODEL, tn), BF16),
            pltpu.VMEM((tm + 8, tn), F32),
            pltpu.VMEM((tm + 8, tn), F32),
        ],
        compiler_params=_params(("arbitrary", "arbitrary"), 56),
        name="upconv",
    )(c, c_meta, w_up, w_up, conv_w, conv_w, conv_b, conv_b, w_down)


def _down_kernel(a_ref, w_ref, h_ref, g_ref, o_ref, *, tm, sub):
    for rs in _row_tiles(tm, sub):
        hcur = h_ref[rs, :] + jnp.dot(a_ref[rs, :], w_ref[...], preferred_element_type=F32)
        o_ref[rs, :] = _rms(hcur, g_ref[...])


def _down(act, w_down_b, h1, gain, *, tm, sub):
    m = act.shape[0]
    return pl.pallas_call(
        functools.partial(_down_kernel, tm=tm, sub=sub),
        grid=(m // tm,),
        in_specs=[
            pl.BlockSpec((tm, D_FF), lambda i: (i, 0)),
            pl.BlockSpec((D_FF, D_MODEL), lambda i: (0, 0), pipeline_mode=pl.Buffered(1)),
            pl.BlockSpec((tm, D_MODEL), lambda i: (i, 0)),
            pl.BlockSpec((1, D_MODEL), lambda i: (0, 0)),
        ],
        out_specs=pl.BlockSpec((tm, D_MODEL), lambda i: (i, 0)),
        out_shape=jax.ShapeDtypeStruct((m, D_MODEL), F32),
        compiler_params=_params(("arbitrary",), 58),
        name="downproj",
    )(act, w_down_b, h1, gain)


def _rotary_tables(pos):
    inv_freq = 1.0 / (ROPE_BASE ** (jnp.arange(0, HEAD_DIM, 2, dtype=F32) / HEAD_DIM))
    ang = pos[:, None] * inv_freq[None, :]
    cos, sin = jnp.cos(ang), jnp.sin(ang)
    return jnp.concatenate([cos, cos], axis=1), jnp.concatenate([-sin, sin], axis=1)


def kernel(x, meta_tokens, norm1_gain, w_in, b_forget, ret_norm_gain, w_out, norm2_gain,
           w_up, conv_w, conv_b, w_down, final_norm_gain):
    n_batch, seq, d_model = x.shape
    assert d_model == D_MODEL and seq % CHUNK == 0 and w_in.shape[0] == 1
    assert meta_tokens.shape == (N_META, D_MODEL)
    x2d = x.reshape(n_batch * seq, D_MODEL)
    xm = jnp.concatenate([jnp.zeros((META_PAD, D_MODEL), F32), meta_tokens.astype(F32)], axis=0)

    w_in_t = w_in[0].T
    b_row = jnp.pad(b_forget[0], (0, LANES - N_HEADS)).reshape(1, LANES)
    log_g = jnp.log1p(-jnp.exp2(-5.0 - jnp.arange(N_HEADS, dtype=F32)))
    g1 = norm1_gain[0].reshape(1, D_MODEL)
    g2 = norm2_gain[0].reshape(1, D_MODEL)
    gf = final_norm_gain.reshape(1, D_MODEL)
    gr = ret_norm_gain[0].reshape(1, RET_DIM)

    cos_r, sin_r = _rotary_tables(N_META + jnp.arange(seq, dtype=F32))
    cos_m, sin_m = _rotary_tables(jnp.maximum(jnp.arange(CHUNK, dtype=F32) - META_PAD, 0.0))

    proj_m, ff_m, proj, ff, w_main_b, w_f = _inproj_first(
        xm, x2d, g1, w_in_t, cos_m, sin_m, cos_r, sin_r, tm=1024, tn=512, sub=256)
    cum_m = _cum(ff_m, b_row, rows_per_batch=CHUNK, valid_from=META_PAD, rel_last=True)
    ckm_rows = cum_m[:, :N_HEADS].T.reshape(N_HEADS, 1, CHUNK)
    ret_m = _retention(log_g, proj_m, None, gr, n_batch=1, rows_per_batch=CHUNK, meta_prefix=False)
    fox_m = _fox(proj_m, proj_m, cum_m, None, ckm_rows, n_batch=1, rows_per_batch=CHUNK,
                 tq=CHUNK, meta_only=True)
    c_m, w_out_b = _outproj_meta(jnp.concatenate([ret_m, fox_m], axis=1), w_out[0], xm, g2)

    proj, ff = _inproj_rest(x2d, g1, w_main_b, w_f, cos_r, sin_r, proj, ff,
                            tm=1024, tn=1024, sub=256)
    cum = _cum(ff, b_row, rows_per_batch=seq)
    ck_rows = (cum[:, :N_HEADS].reshape(n_batch, seq, N_HEADS).transpose(0, 2, 1)
               .reshape(n_batch * N_HEADS, 1, seq))
    ret = _retention(log_g, proj, proj_m, gr, n_batch=n_batch, rows_per_batch=seq, meta_prefix=True)
    fox = _fox(proj, proj_m, cum, ck_rows, ckm_rows, n_batch=n_batch, rows_per_batch=seq,
               tq=256, meta_only=False)
    h1, c = _outproj(ret, fox, w_out_b, x2d, g2, tm=512, sub=256)
    act, w_down_b = _upconv(c, c_m, w_up[0], conv_w[0], conv_b, w_down[0],
                            rows_per_batch=seq, tm=2048, tn=512, sub=1024)
    out = _down(act, w_down_b, h1, gf, tm=512, sub=256)
    return out.reshape(n_batch, seq, D_MODEL)
```

```python
import functools
from typing import NamedTuple

import jax
import jax.numpy as jnp
from jax import lax
from jax.experimental import pallas as pl
from jax.experimental.pallas import tpu as pltpu

F32 = jnp.float32
BF16 = jnp.bfloat16

D_MODEL = 2048
N_META = 16
CHUNK = 128
N_HEADS = 8
HEAD_DIM = 128
RET_DIM = N_HEADS * HEAD_DIM
FOX_DIM = N_HEADS * HEAD_DIM
MAIN_COLS = 4 * RET_DIM + 3 * FOX_DIM
D_FF = 5632
ROPE_BASE = 10000.0
NORM_EPS = 1e-6
META_PAD = CHUNK - N_META
MASKED = -1e30
LANES = 128
MIB = 1024 * 1024
LOG2E = 1.4426950408889634
FOXQ_SCALE = HEAD_DIM ** -0.5 * LOG2E

(SEC_RQ, SEC_RK, SEC_RV, SEC_RG, SEC_FQ, SEC_FK, SEC_FV) = range(7)
BLK = {s: s * N_HEADS for s in range(7)}


V7X_VMEM_BYTES = 64 * MIB
V7X_VMEM_UNSCOPED = 6 * MIB


class _Tiles(NamedTuple):
    inproj_rows: int = 1024
    inproj_first_cols: int = 512
    inproj_cols: int = 1024
    proj_sub_rows: int = 256
    fox_q_rows: int = 256
    mixer_heads: int = 2
    outproj_rows: int = 512
    up_rows: int = 2048
    up_sub_rows: int = 1024
    up_cols: int = 512
    down_rows: int = 512


def _nbytes(shape, dtype):
    n = jnp.dtype(dtype).itemsize
    for d in shape:
        n *= d
    return n


def _params(sem, pipelined, resident=(), temps=()):
    need = (2 * sum(_nbytes(*b) for b in pipelined) + sum(_nbytes(*b) for b in resident)
            + sum(_nbytes(*b) for b in temps))
    limit = -(-need // MIB) * MIB
    assert limit <= V7X_VMEM_BYTES - V7X_VMEM_UNSCOPED, (limit, sem)
    return pltpu.CompilerParams(dimension_semantics=sem, vmem_limit_bytes=limit)


def _rms(x, gain):
    ms = jnp.mean(x * x, axis=-1, keepdims=True)
    return (x * lax.rsqrt(ms + NORM_EPS)) * gain


def _silu(x):
    h = 0.5 * x
    return h + h * jnp.tanh(h)


def _row_tiles(tm, sub):
    return [slice(r * sub, (r + 1) * sub) for r in range(tm // sub)]


class _Rows(NamedTuple):
    x_ref: object
    cos_ref: object
    sin_ref: object
    o_ref: object
    ff_ref: object
    a_scr: object
    tiles: list


def _inproj_body(j, tn, groups, g_ref, w_bf, wf_ref):
    sec = j // (RET_DIM // tn)

    def rotary_store(grp, rs, acc, scale):
        cos = grp.cos_ref[rs, :]
        sin = grp.sin_ref[rs, :]
        for hh in range(tn // LANES):
            cols = slice(hh * LANES, (hh + 1) * LANES)
            t = acc[:, cols]
            y = t * cos + pltpu.roll(t, HEAD_DIM // 2, 1) * sin
            if scale is not None:
                y = y * scale
            grp.o_ref[rs, cols] = y.astype(BF16)

    @pl.when(j == 0)
    def _():
        for grp in groups:
            for rs in grp.tiles:
                a = _rms(grp.x_ref[rs, :], g_ref[...]).astype(BF16)
                grp.a_scr[rs, :] = a
                grp.ff_ref[rs, :] = lax.dot_general(a, wf_ref[...], (((1,), (1,)), ((), ())),
                                                    preferred_element_type=F32)
                rotary_store(grp, rs, jnp.dot(a, w_bf[...], preferred_element_type=F32), None)

    @pl.when((j > 0) & (sec <= SEC_RK))
    def _():
        scale = jnp.where(sec == SEC_RK, HEAD_DIM ** -0.5, 1.0).astype(F32)
        for grp in groups:
            for rs in grp.tiles:
                acc = jnp.dot(grp.a_scr[rs, :], w_bf[...], preferred_element_type=F32)
                rotary_store(grp, rs, acc, scale)

    @pl.when(sec > SEC_RK)
    def _():
        scale = jnp.where(sec == SEC_FQ, FOXQ_SCALE, 1.0).astype(F32)
        for grp in groups:
            for rs in grp.tiles:
                acc = jnp.dot(grp.a_scr[rs, :], w_bf[...], preferred_element_type=F32)
                grp.o_ref[rs, :] = (acc * scale).astype(BF16)


def _inproj_first_kernel(xm_ref, x_ref, g_ref, w_ref, wf8_ref, cosm_ref, sinm_ref, cos_ref, sin_ref,
                         om_ref, ffm_ref, o_ref, ff_ref, wq_ref, wf_ref, am_scr, a_scr,
                         *, tm, tn, sub):
    j = pl.program_id(0)
    wq_ref[...] = w_ref[...].T.astype(BF16)

    @pl.when(j == 0)
    def _():
        pad = jnp.zeros((LANES - N_HEADS, D_MODEL), F32)
        wf_ref[...] = jnp.concatenate([wf8_ref[...], pad], axis=0).astype(BF16)

    groups = [_Rows(xm_ref, cosm_ref, sinm_ref, om_ref, ffm_ref, am_scr, _row_tiles(CHUNK, CHUNK)),
              _Rows(x_ref, cos_ref, sin_ref, o_ref, ff_ref, a_scr, _row_tiles(tm, sub))]
    _inproj_body(j, tn, groups, g_ref, wq_ref, wf_ref)


def _inproj_rest_kernel(x_ref, g_ref, w_ref, wf_ref, cos_ref, sin_ref, proj_hbm, ff_hbm,
                        o_ref, ff_ref, a_scr, *, tm, tn, sub):
    del proj_hbm, ff_hbm
    groups = [_Rows(x_ref, cos_ref, sin_ref, o_ref, ff_ref, a_scr, _row_tiles(tm, sub))]
    _inproj_body(pl.program_id(1), tn, groups, g_ref, w_ref, wf_ref)


def _inproj_first(xm, x2d, gain, w_in_t, cos_m, sin_m, cos_t, sin_t, *, tm, tn, sub):
    m = x2d.shape[0]
    const = lambda shape: pl.BlockSpec(shape, lambda j: (0, 0))
    once = lambda shape: pl.BlockSpec(shape, lambda j: (0, 0), pipeline_mode=pl.Buffered(1))
    return pl.pallas_call(
        functools.partial(_inproj_first_kernel, tm=tm, tn=tn, sub=sub),
        grid=(MAIN_COLS // tn,),
        in_specs=[
            const((CHUNK, D_MODEL)),
            once((tm, D_MODEL)),
            const((1, D_MODEL)),
            pl.BlockSpec((tn, D_MODEL), lambda j: (j, 0)),
            pl.BlockSpec((N_HEADS, D_MODEL), lambda j: (MAIN_COLS // N_HEADS, 0)),
            const((CHUNK, LANES)),
            const((CHUNK, LANES)),
            const((tm, LANES)),
            const((tm, LANES)),
        ],
        out_specs=[
            pl.BlockSpec((CHUNK, tn), lambda j: (0, j)),
            const((CHUNK, LANES)),
            pl.BlockSpec((tm, tn), lambda j: (0, j)),
            const((tm, LANES)),
            pl.BlockSpec((D_MODEL, tn), lambda j: (0, j)),
            const((LANES, D_MODEL)),
        ],
        out_shape=[
            jax.ShapeDtypeStruct((CHUNK, MAIN_COLS), BF16),
            jax.ShapeDtypeStruct((CHUNK, LANES), F32),
            jax.ShapeDtypeStruct((m, MAIN_COLS), BF16),
            jax.ShapeDtypeStruct((m, LANES), F32),
            jax.ShapeDtypeStruct((D_MODEL, MAIN_COLS), BF16),
            jax.ShapeDtypeStruct((LANES, D_MODEL), BF16),
        ],
        scratch_shapes=[pltpu.VMEM((CHUNK, D_MODEL), BF16), pltpu.VMEM((tm, D_MODEL), BF16)],
        compiler_params=_params(
            ("arbitrary",),
            pipelined=[((tn, D_MODEL), F32), ((D_MODEL, tn), BF16), ((CHUNK, D_MODEL), F32),
                       ((tm + CHUNK, tn), BF16), ((2 * (tm + CHUNK), LANES), F32),
                       ((tm + CHUNK, LANES), F32), ((LANES, D_MODEL), BF16)],
            resident=[((tm, D_MODEL), F32), ((tm + CHUNK, D_MODEL), BF16)],
            temps=[((tn, D_MODEL), F32), ((sub, D_MODEL), F32), ((sub, tn), F32)]),
        name="inproj_first",
    )(xm, x2d, gain, w_in_t, w_in_t, cos_m, sin_m, cos_t, sin_t)


def _inproj_rest(x2d, gain, w_main_b, w_f, cos_t, sin_t, proj, ff, *, tm, tn, sub):
    m = x2d.shape[0]
    nb = cos_t.shape[0] // tm
    return pl.pallas_call(
        functools.partial(_inproj_rest_kernel, tm=tm, tn=tn, sub=sub),
        grid=(m // tm - 1, MAIN_COLS // tn),
        in_specs=[
            pl.BlockSpec((tm, D_MODEL), lambda i, j: (i + 1, 0)),
            pl.BlockSpec((1, D_MODEL), lambda i, j: (0, 0)),
            pl.BlockSpec((D_MODEL, tn), lambda i, j: (0, j)),
            pl.BlockSpec((LANES, D_MODEL), lambda i, j: (0, 0)),
            pl.BlockSpec((tm, LANES), lambda i, j: ((i + 1) % nb, 0)),
            pl.BlockSpec((tm, LANES), lambda i, j: ((i + 1) % nb, 0)),
            pl.BlockSpec(memory_space=pl.ANY),
            pl.BlockSpec(memory_space=pl.ANY),
        ],
        out_specs=[
            pl.BlockSpec((tm, tn), lambda i, j: (i + 1, j)),
            pl.BlockSpec((tm, LANES), lambda i, j: (i + 1, 0)),
        ],
        out_shape=[
            jax.ShapeDtypeStruct((m, MAIN_COLS), BF16),
            jax.ShapeDtypeStruct((m, LANES), F32),
        ],
        input_output_aliases={6: 0, 7: 1},
        scratch_shapes=[pltpu.VMEM((tm, D_MODEL), BF16)],
        compiler_params=_params(
            ("arbitrary", "arbitrary"),
            pipelined=[((tm, D_MODEL), F32), ((D_MODEL, tn), BF16), ((tm, tn), BF16),
                       ((3 * tm, LANES), F32), ((LANES, D_MODEL), BF16)],
            resident=[((tm, D_MODEL), BF16)],
            temps=[((sub, D_MODEL), F32), ((sub, tn), F32), ((sub, tn), F32)]),
        name="inproj_rest",
    )(x2d, gain, w_main_b, w_f, cos_t, sin_t, proj, ff)


def _cum_kernel(ff_ref, b_ref, o_ref, *, n_blk, valid_from, rel_last):
    row = lax.broadcasted_iota(jnp.int32, (CHUNK, CHUNK), 0)
    col = lax.broadcasted_iota(jnp.int32, (CHUNK, CHUNK), 1)
    tri = (row >= col).astype(BF16)
    rows = lax.broadcasted_iota(jnp.int32, (CHUNK, LANES), 0)
    carry = jnp.zeros((1, LANES), F32)
    for blk in range(n_blk):
        z = ff_ref[blk * CHUNK:(blk + 1) * CHUNK, :] + b_ref[...]
        lf = jnp.minimum(z, 0.0) - jnp.log1p(jnp.exp(-jnp.abs(z)))
        if valid_from:
            lf = jnp.where(rows >= valid_from, lf, 0.0)
        hi = lf.astype(BF16)
        r1 = lf - hi.astype(F32)
        mid = r1.astype(BF16)
        lo = (r1 - mid.astype(F32)).astype(BF16)
        cum = (jnp.dot(tri, hi, preferred_element_type=F32)
               + jnp.dot(tri, mid, preferred_element_type=F32)
               + jnp.dot(tri, lo, preferred_element_type=F32)) + carry
        o_ref[blk * CHUNK:(blk + 1) * CHUNK, :] = cum
        carry = cum[CHUNK - 1:CHUNK, :]
    if rel_last:
        o_ref[...] = o_ref[...] - carry


def _cum(ff, b_row, *, rows_per_batch, valid_from=0, rel_last=False):
    m = ff.shape[0]
    return pl.pallas_call(
        functools.partial(_cum_kernel, n_blk=rows_per_batch // CHUNK,
                          valid_from=valid_from, rel_last=rel_last),
        grid=(m // rows_per_batch,),
        in_specs=[
            pl.BlockSpec((rows_per_batch, LANES), lambda b: (b, 0)),
            pl.BlockSpec((1, LANES), lambda b: (0, 0)),
        ],
        out_specs=pl.BlockSpec((rows_per_batch, LANES), lambda b: (b, 0)),
        out_shape=jax.ShapeDtypeStruct((m, LANES), F32),
        compiler_params=_params(("arbitrary",), pipelined=[((2 * rows_per_batch, LANES), F32)],
                                temps=[((rows_per_batch, LANES), F32)]),
        name="cumgate",
    )(ff, b_row)


def _head_view(ref, hh):
    return ref.at[:, pl.ds(hh * HEAD_DIM, HEAD_DIM)]


def _ret_kernel(logg_ref, *refs, n_chunks, meta_prefix, heads):
    *io_refs, state_scr = refs
    for hh in range(heads):
        _ret_head(logg_ref[pl.program_id(1) * heads + hh], *[_head_view(r, hh) for r in io_refs],
                  state_scr.at[hh], n_chunks=n_chunks, meta_prefix=meta_prefix)


def _ret_head(lg, *refs, n_chunks, meta_prefix):
    if meta_prefix:
        q_ref, k_ref, v_ref, g_ref, km_ref, vm_ref, gain_ref, o_ref, state_scr = refs
    else:
        q_ref, k_ref, v_ref, g_ref, gain_ref, o_ref, state_scr = refs
    ri = lax.broadcasted_iota(jnp.int32, (CHUNK, CHUNK), 0)
    ci = lax.broadcasted_iota(jnp.int32, (CHUNK, CHUNK), 1)
    diff = (ri - ci).astype(F32)
    dmat = jnp.where(diff >= 0, jnp.exp(jnp.maximum(diff, 0.0) * lg), 0.0)
    pos = lax.broadcasted_iota(jnp.int32, (CHUNK, 1), 0).astype(F32)
    xi = jnp.exp((pos + 1.0) * lg)
    zeta = jnp.exp((CHUNK - 1.0 - pos) * lg)
    g_chunk = jnp.exp(jnp.full((1, 1), float(CHUNK), F32) * lg)

    def advance(state, kc, vc):
        kz = (kc.astype(F32) * zeta).astype(BF16)
        return g_chunk * state + lax.dot_general(
            kz, vc, (((0,), (0,)), ((), ())), preferred_element_type=F32)

    state = jnp.zeros((HEAD_DIM, HEAD_DIM), F32)
    if meta_prefix:
        state = advance(state, km_ref[...], vm_ref[...])
    for c in range(n_chunks):
        state_scr[c] = state.astype(BF16)
        if c + 1 < n_chunks:
            rows = slice(c * CHUNK, (c + 1) * CHUNK)
            state = advance(state, k_ref[rows, :], v_ref[rows, :])

    gain = gain_ref[...]

    for c in range(n_chunks):
        rows = slice(c * CHUNK, (c + 1) * CHUNK)
        qc = q_ref[rows, :]
        kc = k_ref[rows, :]
        vc = v_ref[rows, :]
        s = lax.dot_general(qc, kc, (((1,), (1,)), ((), ())), preferred_element_type=F32) * dmat
        o = (jnp.dot(s.astype(BF16), vc, preferred_element_type=F32)
             + xi * jnp.dot(qc, state_scr[c], preferred_element_type=F32))
        mu = jnp.mean(o, axis=-1, keepdims=True)
        d = o - mu
        var = jnp.mean(d * d, axis=-1, keepdims=True)
        y = (d * lax.rsqrt(var + NORM_EPS)) * gain
        gt = g_ref[rows, :].astype(F32)
        o_ref[rows, :] = (_silu(gt) * y).astype(BF16)


def _retention(log_g, proj, proj_meta, gain, *, n_batch, rows_per_batch, meta_prefix, heads):
    width = heads * HEAD_DIM
    blk = lambda sec: pl.BlockSpec((rows_per_batch, width),
                                   lambda b, h, s=BLK[sec] // heads: (b, s + h))
    in_specs = [pl.BlockSpec(memory_space=pltpu.SMEM),
                blk(SEC_RQ), blk(SEC_RK), blk(SEC_RV), blk(SEC_RG)]
    args = [log_g, proj, proj, proj, proj]
    if meta_prefix:
        mblk = lambda sec: pl.BlockSpec((CHUNK, width), lambda b, h, s=BLK[sec] // heads: (0, s + h))
        in_specs += [mblk(SEC_RK), mblk(SEC_RV)]
        args += [proj_meta, proj_meta]
    in_specs.append(pl.BlockSpec((1, width), lambda b, h: (0, h)))
    args.append(gain)
    return pl.pallas_call(
        functools.partial(_ret_kernel, n_chunks=rows_per_batch // CHUNK, meta_prefix=meta_prefix,
                          heads=heads),
        grid=(n_batch, N_HEADS // heads),
        in_specs=in_specs,
        out_specs=pl.BlockSpec((rows_per_batch, width), lambda b, h: (b, h)),
        out_shape=jax.ShapeDtypeStruct((n_batch * rows_per_batch, RET_DIM), BF16),
        scratch_shapes=[pltpu.VMEM((heads, rows_per_batch // CHUNK, HEAD_DIM, HEAD_DIM), BF16)],
        compiler_params=_params(
            ("arbitrary", "arbitrary"),
            pipelined=[((5 * rows_per_batch + 2 * CHUNK, width), BF16)],
            resident=[((rows_per_batch, width), BF16)],
            temps=[((rows_per_batch, HEAD_DIM), F32)]),
        name="retention",
    )(*args)


def _fox_kernel(*refs, n_q, tq, meta_only, heads):
    for hh in range(heads):
        _fox_head(pl.program_id(1) * heads + hh, hh, *refs, n_q=n_q, tq=tq, meta_only=meta_only)


def _fox_head(h, hh, *refs, n_q, tq, meta_only):
    if meta_only:
        q_ref, km_ref, vm_ref, cq_ref, ckm_ref, o_ref, k_all, v_all = refs
        ck_all = ckm_ref[hh] * LOG2E
    else:
        q_ref, k_ref, v_ref, km_ref, vm_ref, cq_ref, ck_ref, ckm_ref, o_ref, k_all, v_all = refs
        k_ref, v_ref = _head_view(k_ref, hh), _head_view(v_ref, hh)
        k_all[CHUNK:, :] = k_ref[...]
        v_all[CHUNK:, :] = v_ref[...]
        ck_all = jnp.concatenate([ckm_ref[hh], ck_ref[hh]], axis=1) * LOG2E
    q_ref, o_ref = _head_view(q_ref, hh), _head_view(o_ref, hh)
    km_ref, vm_ref = _head_view(km_ref, hh), _head_view(vm_ref, hh)
    k_all[0:CHUNK, :] = km_ref[...]
    v_all[0:CHUNK, :] = vm_ref[...]
    lane = lax.broadcasted_iota(jnp.int32, (1, LANES), 1)
    col_m = lax.broadcasted_iota(jnp.int32, (tq, CHUNK), 1)
    row_m = lax.broadcasted_iota(jnp.int32, (tq, CHUNK), 0)
    mask_m = col_m >= META_PAD
    if meta_only:
        mask_m = mask_m & (row_m >= col_m)
    tri = (lax.broadcasted_iota(jnp.int32, (tq, tq), 0)
           >= lax.broadcasted_iota(jnp.int32, (tq, tq), 1))

    def n_keys(qi):
        return CHUNK if meta_only else CHUNK + (qi + 1) * tq

    def logit_pass(qi):
        n = n_keys(qi)
        q = q_ref[qi * tq:(qi + 1) * tq, :]
        t = lax.dot_general(q, k_all[0:n, :], (((1,), (1,)), ((), ())),
                            preferred_element_type=F32) - ck_all[:, 0:n]
        parts = [jnp.where(mask_m, t[:, 0:CHUNK], MASKED)]
        if not meta_only:
            if qi > 0:
                parts.append(t[:, CHUNK:n - tq])
            parts.append(jnp.where(tri, t[:, n - tq:n], MASKED))
        t = parts[0] if len(parts) == 1 else jnp.concatenate(parts, axis=1)
        return t, jnp.max(t, axis=1, keepdims=True)

    def exp_pass(qi, t, mx):
        rows = slice(qi * tq, (qi + 1) * tq)
        cq2 = LOG2E * jnp.sum(jnp.where(lane == h, cq_ref[rows, :], 0.0), axis=1, keepdims=True)
        m_row = mx + cq2
        p = jnp.exp2(t - (m_row - cq2))
        l = jnp.sum(p, axis=1, keepdims=True)
        acc = jnp.dot(p.astype(BF16), v_all[0:n_keys(qi), :], preferred_element_type=F32)
        o_ref[rows, :] = (acc / l).astype(BF16)

    cur = logit_pass(0)
    for qi in range(n_q):
        nxt = logit_pass(qi + 1) if qi + 1 < n_q else None
        exp_pass(qi, *cur)
        cur = nxt


def _fox(proj, proj_meta, cum_col, ck_rows, ckm_rows, *, n_batch, rows_per_batch, tq, meta_only,
         heads):
    n_q = rows_per_batch // tq
    n_all = CHUNK if meta_only else CHUNK + rows_per_batch
    width = heads * HEAD_DIM
    groups = N_HEADS // heads
    qblk = lambda sec: pl.BlockSpec((rows_per_batch, width),
                                    lambda b, h, s=BLK[sec] // heads: (b, s + h))
    mblk = lambda sec: pl.BlockSpec((CHUNK, width), lambda b, h, s=BLK[sec] // heads: (0, s + h))
    cq_spec = pl.BlockSpec((rows_per_batch, LANES), lambda b, h: (b, 0))
    ckm_spec = pl.BlockSpec((heads, 1, CHUNK), lambda b, h: (h, 0, 0))
    if meta_only:
        in_specs = [qblk(SEC_FQ), mblk(SEC_FK), mblk(SEC_FV), cq_spec, ckm_spec]
        args = [proj, proj_meta, proj_meta, cum_col, ckm_rows]
    else:
        ck_spec = pl.BlockSpec((heads, 1, rows_per_batch), lambda b, h: (b * groups + h, 0, 0))
        in_specs = [qblk(SEC_FQ), qblk(SEC_FK), qblk(SEC_FV), mblk(SEC_FK), mblk(SEC_FV),
                    cq_spec, ck_spec, ckm_spec]
        args = [proj, proj, proj, proj_meta, proj_meta, cum_col, ck_rows, ckm_rows]
    return pl.pallas_call(
        functools.partial(_fox_kernel, n_q=n_q, tq=tq, meta_only=meta_only, heads=heads),
        grid=(n_batch, groups),
        in_specs=in_specs,
        out_specs=pl.BlockSpec((rows_per_batch, width), lambda b, h: (b, h)),
        out_shape=jax.ShapeDtypeStruct((n_batch * rows_per_batch, FOX_DIM), BF16),
        scratch_shapes=[pltpu.VMEM((n_all, HEAD_DIM), BF16)] * 2,
        compiler_params=_params(
            ("arbitrary", "arbitrary"),
            pipelined=[((4 * rows_per_batch + 2 * CHUNK, width), BF16),
                       ((rows_per_batch, LANES), F32), ((8 * heads, n_all), F32)],
            resident=[((2 * n_all, HEAD_DIM), BF16)],
            temps=[((4 * tq * heads, n_all), F32)]),
        name="foxattn",
    )(*args)


def _outproj_kernel(ret_ref, fox_ref, w1_ref, w2_ref, x_ref, g_ref, h_ref, c_ref, *, tm, sub):
    for rs in _row_tiles(tm, sub):
        hcur = (x_ref[rs, :]
                + jnp.dot(ret_ref[rs, :], w1_ref[...], preferred_element_type=F32)
                + jnp.dot(fox_ref[rs, :], w2_ref[...], preferred_element_type=F32))
        h_ref[rs, :] = hcur
        c_ref[rs, :] = _rms(hcur, g_ref[...]).astype(BF16)


def _outproj(ret, fox, w_out_b, x2d, gain, *, tm, sub):
    m = x2d.shape[0]
    return pl.pallas_call(
        functools.partial(_outproj_kernel, tm=tm, sub=sub),
        grid=(m // tm,),
        in_specs=[
            pl.BlockSpec((tm, RET_DIM), lambda i: (i, 0)),
            pl.BlockSpec((tm, FOX_DIM), lambda i: (i, 0)),
            pl.BlockSpec((RET_DIM, D_MODEL), lambda i: (0, 0)),
            pl.BlockSpec((FOX_DIM, D_MODEL), lambda i: (1, 0)),
            pl.BlockSpec((tm, D_MODEL), lambda i: (i, 0)),
            pl.BlockSpec((1, D_MODEL), lambda i: (0, 0)),
        ],
        out_specs=[
            pl.BlockSpec((tm, D_MODEL), lambda i: (i, 0)),
            pl.BlockSpec((tm, D_MODEL), lambda i: (i, 0)),
        ],
        out_shape=[
            jax.ShapeDtypeStruct((m, D_MODEL), F32),
            jax.ShapeDtypeStruct((m, D_MODEL), BF16),
        ],
        compiler_params=_params(
            ("arbitrary",),
            pipelined=[((tm, RET_DIM + FOX_DIM), BF16), ((tm, D_MODEL), F32), ((tm, D_MODEL), F32),
                       ((tm, D_MODEL), BF16)],
            resident=[((RET_DIM + FOX_DIM, D_MODEL), BF16)],
            temps=[((sub, D_MODEL), F32), ((sub, D_MODEL), F32)]),
        name="outproj",
    )(ret, fox, w_out_b, w_out_b, x2d, gain)


def _outproj_meta_kernel(mix_ref, w_ref, x_ref, g_ref, c_ref, wq_ref, acc_scr):
    k = pl.program_id(0)
    wq_ref[...] = w_ref[...].astype(BF16)
    part = jnp.dot(mix_ref[...], wq_ref[...], preferred_element_type=F32)

    @pl.when(k == 0)
    def _():
        acc_scr[...] = x_ref[...] + part

    @pl.when(k == 1)
    def _():
        c_ref[...] = _rms(acc_scr[...] + part, g_ref[...]).astype(BF16)


def _outproj_meta(mix_m, w_out, xm, gain):
    half = D_MODEL // 2
    return pl.pallas_call(
        _outproj_meta_kernel,
        grid=(2,),
        in_specs=[
            pl.BlockSpec((CHUNK, half), lambda k: (0, k)),
            pl.BlockSpec((half, D_MODEL), lambda k: (k, 0)),
            pl.BlockSpec((CHUNK, D_MODEL), lambda k: (0, 0)),
            pl.BlockSpec((1, D_MODEL), lambda k: (0, 0)),
        ],
        out_specs=[
            pl.BlockSpec((CHUNK, D_MODEL), lambda k: (0, 0)),
            pl.BlockSpec((half, D_MODEL), lambda k: (k, 0)),
        ],
        out_shape=[
            jax.ShapeDtypeStruct((CHUNK, D_MODEL), BF16),
            jax.ShapeDtypeStruct((D_MODEL, D_MODEL), BF16),
        ],
        scratch_shapes=[pltpu.VMEM((CHUNK, D_MODEL), F32)],
        compiler_params=_params(
            ("arbitrary",),
            pipelined=[((half, D_MODEL), F32), ((half, D_MODEL), BF16), ((CHUNK, 2 * D_MODEL), F32)],
            resident=[((CHUNK, D_MODEL), F32)],
            temps=[((CHUNK, D_MODEL), F32), ((CHUNK, D_MODEL), F32)]),
        name="outproj_meta",
    )(mix_m, w_out, xm, gain)


def _up_kernel(c_ref, cm_ref, wg_ref, wv_ref, cwg_ref, cwv_ref, cbg_ref, cbv_ref, wd_ref,
               o_ref, wdq_ref, wg_s, wv_s, ug_s, uv_s, *, tm, sub, tiles_per_batch):
    i = pl.program_id(1)
    wdq_ref[...] = wd_ref[...].astype(BF16)

    @pl.when(i == 0)
    def _():
        wg_s[...] = wg_ref[...].astype(BF16)
        wv_s[...] = wv_ref[...].astype(BF16)

    @pl.when(i % tiles_per_batch == 0)
    def _():
        cm = cm_ref[...]
        ug_s[0:8, :] = jnp.dot(cm, wg_s[...], preferred_element_type=F32)[8:16, :]
        uv_s[0:8, :] = jnp.dot(cm, wv_s[...], preferred_element_type=F32)[8:16, :]

    def conv(c, rs, w_s, u_s, cw_ref, cb_ref):
        u_s[8 + rs.start:8 + rs.stop, :] = jnp.dot(c, w_s[...], preferred_element_type=F32)
        ext = u_s[rs.start:8 + rs.stop, :]
        n = rs.stop - rs.start
        return (cb_ref[...]
                + cw_ref[0:1, :] * pltpu.roll(ext, 2, 0)[8:8 + n, :]
                + cw_ref[1:2, :] * pltpu.roll(ext, 1, 0)[8:8 + n, :]
                + cw_ref[2:3, :] * ext[8:8 + n, :])

    for rs in _row_tiles(tm, sub):
        c = c_ref[rs, :]
        gate = conv(c, rs, wg_s, ug_s, cwg_ref, cbg_ref)
        val = conv(c, rs, wv_s, uv_s, cwv_ref, cbv_ref)
        o_ref[rs, :] = ((gate / (1.0 + jnp.exp(-gate))) * val).astype(BF16)
    ug_s[0:8, :] = ug_s[tm:tm + 8, :]
    uv_s[0:8, :] = uv_s[tm:tm + 8, :]


def _upconv(c, c_meta, w_up, conv_w, conv_b, w_down, *, rows_per_batch, tm, tn, sub):
    m = c.shape[0]
    nj = D_FF // tn
    ni = m // tm
    wd_rows = D_FF // (nj * ni)
    assert wd_rows * nj * ni == D_FF and wd_rows % 16 == 0
    tiles_per_batch = rows_per_batch // tm
    halo_blk = c_meta.shape[0] // 16 - 1
    return pl.pallas_call(
        functools.partial(_up_kernel, tm=tm, sub=sub, tiles_per_batch=tiles_per_batch),
        grid=(nj, m // tm),
        in_specs=[
            pl.BlockSpec((tm, D_MODEL), lambda j, i: (i, 0)),
            pl.BlockSpec((16, D_MODEL), lambda j, i: (halo_blk, 0)),
            pl.BlockSpec((D_MODEL, tn), lambda j, i: (0, j)),
            pl.BlockSpec((D_MODEL, tn), lambda j, i: (0, j + nj)),
            pl.BlockSpec((3, tn), lambda j, i: (0, j)),
            pl.BlockSpec((3, tn), lambda j, i: (0, j + nj)),
            pl.BlockSpec((1, tn), lambda j, i: (0, j)),
            pl.BlockSpec((1, tn), lambda j, i: (0, j + nj)),
            pl.BlockSpec((wd_rows, D_MODEL), lambda j, i: (j * ni + i, 0)),
        ],
        out_specs=[
            pl.BlockSpec((tm, tn), lambda j, i: (i, j)),
            pl.BlockSpec((wd_rows, D_MODEL), lambda j, i: (j * ni + i, 0)),
        ],
        out_shape=[
            jax.ShapeDtypeStruct((m, D_FF), BF16),
            jax.ShapeDtypeStruct((D_FF, D_MODEL), BF16),
        ],
        scratch_shapes=[
            pltpu.VMEM((D_MODEL, tn), BF16),
            pltpu.VMEM((D_MODEL, tn), BF16),
            pltpu.VMEM((tm + 8, tn), F32),
            pltpu.VMEM((tm + 8, tn), F32),
        ],
        compiler_params=_params(
            ("arbitrary", "arbitrary"),
            pipelined=[((tm, D_MODEL), BF16), ((D_MODEL, 2 * tn), F32), ((tm, tn), BF16),
                       ((wd_rows, D_MODEL), F32), ((wd_rows, D_MODEL), BF16)],
            resident=[((D_MODEL, 2 * tn), BF16), ((2 * (tm + 8), tn), F32)],
            temps=[((sub, tn), F32), ((sub, tn), F32)]),
        name="upconv",
    )(c, c_meta, w_up, w_up, conv_w, conv_w, conv_b, conv_b, w_down)


def _down_kernel(a_ref, w_ref, h_ref, g_ref, o_ref, *, tm, sub):
    for rs in _row_tiles(tm, sub):
        hcur = h_ref[rs, :] + jnp.dot(a_ref[rs, :], w_ref[...], preferred_element_type=F32)
        o_ref[rs, :] = _rms(hcur, g_ref[...])


def _down(act, w_down_b, h1, gain, *, tm, sub):
    m = act.shape[0]
    return pl.pallas_call(
        functools.partial(_down_kernel, tm=tm, sub=sub),
        grid=(m // tm,),
        in_specs=[
            pl.BlockSpec((tm, D_FF), lambda i: (i, 0)),
            pl.BlockSpec((D_FF, D_MODEL), lambda i: (0, 0), pipeline_mode=pl.Buffered(1)),
            pl.BlockSpec((tm, D_MODEL), lambda i: (i, 0)),
            pl.BlockSpec((1, D_MODEL), lambda i: (0, 0)),
        ],
        out_specs=pl.BlockSpec((tm, D_MODEL), lambda i: (i, 0)),
        out_shape=jax.ShapeDtypeStruct((m, D_MODEL), F32),
        compiler_params=_params(
            ("arbitrary",),
            pipelined=[((tm, D_FF), BF16), ((tm, D_MODEL), F32), ((tm, D_MODEL), F32)],
            resident=[((D_FF, D_MODEL), BF16)],
            temps=[((sub, D_MODEL), F32), ((sub, D_MODEL), F32)]),
        name="downproj",
    )(act, w_down_b, h1, gain)


def _rotary_tables(pos):
    inv_freq = 1.0 / (ROPE_BASE ** (jnp.arange(0, HEAD_DIM, 2, dtype=F32) / HEAD_DIM))
    ang = pos[:, None] * inv_freq[None, :]
    cos, sin = jnp.cos(ang), jnp.sin(ang)
    return jnp.concatenate([cos, cos], axis=1), jnp.concatenate([-sin, sin], axis=1)


def kernel(x, meta_tokens, norm1_gain, w_in, b_forget, ret_norm_gain, w_out, norm2_gain,
           w_up, conv_w, conv_b, w_down, final_norm_gain):
    n_batch, seq, d_model = x.shape
    assert d_model == D_MODEL and seq % CHUNK == 0 and w_in.shape[0] == 1
    assert meta_tokens.shape == (N_META, D_MODEL)
    t = _Tiles()
    x2d = x.reshape(n_batch * seq, D_MODEL)
    xm = jnp.concatenate([jnp.zeros((META_PAD, D_MODEL), F32), meta_tokens.astype(F32)], axis=0)

    w_in_t = w_in[0].T
    b_row = jnp.pad(b_forget[0], (0, LANES - N_HEADS)).reshape(1, LANES)
    log_g = jnp.log1p(-jnp.exp2(-5.0 - jnp.arange(N_HEADS, dtype=F32)))
    g1 = norm1_gain[0].reshape(1, D_MODEL)
    g2 = norm2_gain[0].reshape(1, D_MODEL)
    gf = final_norm_gain.reshape(1, D_MODEL)
    gr = ret_norm_gain[0].reshape(1, RET_DIM)

    cos_r, sin_r = _rotary_tables(N_META + jnp.arange(seq, dtype=F32))
    cos_m, sin_m = _rotary_tables(jnp.maximum(jnp.arange(CHUNK, dtype=F32) - META_PAD, 0.0))

    proj_m, ff_m, proj, ff, w_main_b, w_f = _inproj_first(
        xm, x2d, g1, w_in_t, cos_m, sin_m, cos_r, sin_r,
        tm=t.inproj_rows, tn=t.inproj_first_cols, sub=t.proj_sub_rows)
    cum_m = _cum(ff_m, b_row, rows_per_batch=CHUNK, valid_from=META_PAD, rel_last=True)
    ckm_rows = cum_m[:, :N_HEADS].T.reshape(N_HEADS, 1, CHUNK)
    ret_m = _retention(log_g, proj_m, None, gr, n_batch=1, rows_per_batch=CHUNK, meta_prefix=False,
                       heads=t.mixer_heads)
    fox_m = _fox(proj_m, proj_m, cum_m, None, ckm_rows, n_batch=1, rows_per_batch=CHUNK,
                 tq=CHUNK, meta_only=True, heads=t.mixer_heads)
    c_m, w_out_b = _outproj_meta(jnp.concatenate([ret_m, fox_m], axis=1), w_out[0], xm, g2)

    proj, ff = _inproj_rest(x2d, g1, w_main_b, w_f, cos_r, sin_r, proj, ff,
                            tm=t.inproj_rows, tn=t.inproj_cols, sub=t.proj_sub_rows)
    cum = _cum(ff, b_row, rows_per_batch=seq)
    ck_rows = (cum[:, :N_HEADS].reshape(n_batch, seq, N_HEADS).transpose(0, 2, 1)
               .reshape(n_batch * N_HEADS, 1, seq))
    ret = _retention(log_g, proj, proj_m, gr, n_batch=n_batch, rows_per_batch=seq, meta_prefix=True,
                     heads=t.mixer_heads)
    fox = _fox(proj, proj_m, cum, ck_rows, ckm_rows, n_batch=n_batch, rows_per_batch=seq,
               tq=t.fox_q_rows, meta_only=False, heads=t.mixer_heads)
    h1, c = _outproj(ret, fox, w_out_b, x2d, g2, tm=t.outproj_rows, sub=t.proj_sub_rows)
    act, w_down_b = _upconv(c, c_m, w_up[0], conv_w[0], conv_b, w_down[0],
                            rows_per_batch=seq, tm=t.up_rows, tn=t.up_cols, sub=t.up_sub_rows)
    out = _down(act, w_down_b, h1, gf, tm=t.down_rows, sub=t.proj_sub_rows)
    return out.reshape(n_batch, seq, D_MODEL)
```

```python
import functools
from typing import NamedTuple

import jax
import jax.numpy as jnp
from jax import lax
from jax.experimental import pallas as pl
from jax.experimental.pallas import tpu as pltpu

F32 = jnp.float32
BF16 = jnp.bfloat16

D_MODEL = 2048
N_META = 16
CHUNK = 128
N_HEADS = 8
HEAD_DIM = 128
RET_DIM = N_HEADS * HEAD_DIM
FOX_DIM = N_HEADS * HEAD_DIM
MAIN_COLS = 4 * RET_DIM + 3 * FOX_DIM
D_FF = 5632
ROPE_BASE = 10000.0
NORM_EPS = 1e-6
META_PAD = CHUNK - N_META
MASKED = -1e30
LANES = 128
MIB = 1024 * 1024
LOG2E = 1.4426950408889634
FOXQ_SCALE = HEAD_DIM ** -0.5 * LOG2E

(SEC_RQ, SEC_RK, SEC_RV, SEC_RG, SEC_FQ, SEC_FK, SEC_FV) = range(7)
BLK = {s: s * N_HEADS for s in range(7)}


V7X_VMEM_BYTES = 64 * MIB
V7X_VMEM_UNSCOPED = 6 * MIB


class _Tiles(NamedTuple):
    inproj_rows: int = 1024
    inproj_first_cols: int = 512
    inproj_cols: int = 1024
    proj_sub_rows: int = 256
    fox_q_rows: int = 256
    mixer_heads: int = 2
    outproj_rows: int = 512
    up_rows: int = 2048
    up_sub_rows: int = 1024
    up_cols: int = 512
    down_rows: int = 512


def _nbytes(shape, dtype):
    n = jnp.dtype(dtype).itemsize
    for d in shape:
        n *= d
    return n


def _params(sem, pipelined, resident=(), temps=()):
    need = (2 * sum(_nbytes(*b) for b in pipelined) + sum(_nbytes(*b) for b in resident)
            + sum(_nbytes(*b) for b in temps))
    limit = -(-need // MIB) * MIB
    assert limit <= V7X_VMEM_BYTES - V7X_VMEM_UNSCOPED, (limit, sem)
    return pltpu.CompilerParams(dimension_semantics=sem, vmem_limit_bytes=limit)


def _rms(x, gain):
    ms = jnp.mean(x * x, axis=-1, keepdims=True)
    return (x * lax.rsqrt(ms + NORM_EPS)) * gain


def _silu(x):
    h = 0.5 * x
    return h + h * jnp.tanh(h)


def _row_tiles(tm, sub):
    return [slice(r * sub, (r + 1) * sub) for r in range(tm // sub)]


class _Rows(NamedTuple):
    x_ref: object
    cos_ref: object
    sin_ref: object
    o_ref: object
    ff_ref: object
    a_scr: object
    tiles: list


def _inproj_body(j, tn, groups, g_ref, w_bf, wf_ref):
    sec = j // (RET_DIM // tn)

    def rotary_store(grp, rs, acc, scale):
        cos = grp.cos_ref[rs, :]
        sin = grp.sin_ref[rs, :]
        for hh in range(tn // LANES):
            cols = slice(hh * LANES, (hh + 1) * LANES)
            t = acc[:, cols]
            y = t * cos + pltpu.roll(t, HEAD_DIM // 2, 1) * sin
            if scale is not None:
                y = y * scale
            grp.o_ref[rs, cols] = y.astype(BF16)

    @pl.when(j == 0)
    def _():
        for grp in groups:
            for rs in grp.tiles:
                a = _rms(grp.x_ref[rs, :], g_ref[...]).astype(BF16)
                grp.a_scr[rs, :] = a
                grp.ff_ref[rs, :] = lax.dot_general(a, wf_ref[...], (((1,), (1,)), ((), ())),
                                                    preferred_element_type=F32)
                rotary_store(grp, rs, jnp.dot(a, w_bf[...], preferred_element_type=F32), None)

    @pl.when((j > 0) & (sec <= SEC_RK))
    def _():
        scale = jnp.where(sec == SEC_RK, HEAD_DIM ** -0.5, 1.0).astype(F32)
        for grp in groups:
            for rs in grp.tiles:
                acc = jnp.dot(grp.a_scr[rs, :], w_bf[...], preferred_element_type=F32)
                rotary_store(grp, rs, acc, scale)

    @pl.when(sec > SEC_RK)
    def _():
        scale = jnp.where(sec == SEC_FQ, FOXQ_SCALE, 1.0).astype(F32)
        for grp in groups:
            for rs in grp.tiles:
                acc = jnp.dot(grp.a_scr[rs, :], w_bf[...], preferred_element_type=F32)
                grp.o_ref[rs, :] = (acc * scale).astype(BF16)


def _inproj_meta_kernel(xm_ref, g_ref, w_ref, wf8_ref, cosm_ref, sinm_ref,
                        om_ref, ffm_ref, wq_ref, wf_ref, am_scr, *, tn):
    j = pl.program_id(0)
    wq_ref[...] = w_ref[...].T.astype(BF16)

    @pl.when(j == 0)
    def _():
        pad = jnp.zeros((LANES - N_HEADS, D_MODEL), F32)
        wf_ref[...] = jnp.concatenate([wf8_ref[...], pad], axis=0).astype(BF16)

    groups = [_Rows(xm_ref, cosm_ref, sinm_ref, om_ref, ffm_ref, am_scr, _row_tiles(CHUNK, CHUNK))]
    _inproj_body(j, tn, groups, g_ref, wq_ref, wf_ref)


def _inproj_main_kernel(x_ref, g_ref, w_ref, wf_ref, cos_ref, sin_ref, o_ref, ff_ref, a_scr,
                        *, tm, tn, sub):
    groups = [_Rows(x_ref, cos_ref, sin_ref, o_ref, ff_ref, a_scr, _row_tiles(tm, sub))]
    _inproj_body(pl.program_id(1), tn, groups, g_ref, w_ref, wf_ref)


def _inproj_meta(xm, gain, w_in_t, cos_m, sin_m, *, tn):
    const = lambda shape: pl.BlockSpec(shape, lambda j: (0, 0))
    return pl.pallas_call(
        functools.partial(_inproj_meta_kernel, tn=tn),
        grid=(MAIN_COLS // tn,),
        in_specs=[
            const((CHUNK, D_MODEL)),
            const((1, D_MODEL)),
            pl.BlockSpec((tn, D_MODEL), lambda j: (j, 0)),
            pl.BlockSpec((N_HEADS, D_MODEL), lambda j: (MAIN_COLS // N_HEADS, 0)),
            const((CHUNK, LANES)),
            const((CHUNK, LANES)),
        ],
        out_specs=[
            pl.BlockSpec((CHUNK, tn), lambda j: (0, j)),
            const((CHUNK, LANES)),
            pl.BlockSpec((D_MODEL, tn), lambda j: (0, j)),
            const((LANES, D_MODEL)),
        ],
        out_shape=[
            jax.ShapeDtypeStruct((CHUNK, MAIN_COLS), BF16),
            jax.ShapeDtypeStruct((CHUNK, LANES), F32),
            jax.ShapeDtypeStruct((D_MODEL, MAIN_COLS), BF16),
            jax.ShapeDtypeStruct((LANES, D_MODEL), BF16),
        ],
        scratch_shapes=[pltpu.VMEM((CHUNK, D_MODEL), BF16)],
        compiler_params=_params(
            ("arbitrary",),
            pipelined=[((tn, D_MODEL), F32), ((D_MODEL, tn), BF16), ((CHUNK, D_MODEL), F32),
                       ((CHUNK, tn), BF16), ((3 * CHUNK, LANES), F32), ((LANES, D_MODEL), BF16)],
            resident=[((CHUNK, D_MODEL), BF16)],
            temps=[((tn, D_MODEL), F32), ((CHUNK, D_MODEL), F32), ((CHUNK, tn), F32)]),
        name="inproj_meta",
    )(xm, gain, w_in_t, w_in_t, cos_m, sin_m)


def _inproj_main(x2d, gain, w_main_b, w_f, cos_t, sin_t, *, tm, tn, sub):
    m = x2d.shape[0]
    nb = cos_t.shape[0] // tm
    return pl.pallas_call(
        functools.partial(_inproj_main_kernel, tm=tm, tn=tn, sub=sub),
        grid=(m // tm, MAIN_COLS // tn),
        in_specs=[
            pl.BlockSpec((tm, D_MODEL), lambda i, j: (i, 0)),
            pl.BlockSpec((1, D_MODEL), lambda i, j: (0, 0)),
            pl.BlockSpec((D_MODEL, tn), lambda i, j: (0, j)),
            pl.BlockSpec((LANES, D_MODEL), lambda i, j: (0, 0)),
            pl.BlockSpec((tm, LANES), lambda i, j: (i % nb, 0)),
            pl.BlockSpec((tm, LANES), lambda i, j: (i % nb, 0)),
        ],
        out_specs=[
            pl.BlockSpec((tm, tn), lambda i, j: (i, j)),
            pl.BlockSpec((tm, LANES), lambda i, j: (i, 0)),
        ],
        out_shape=[
            jax.ShapeDtypeStruct((m, MAIN_COLS), BF16),
            jax.ShapeDtypeStruct((m, LANES), F32),
        ],
        scratch_shapes=[pltpu.VMEM((tm, D_MODEL), BF16)],
        compiler_params=_params(
            ("arbitrary", "arbitrary"),
            pipelined=[((tm, D_MODEL), F32), ((D_MODEL, tn), BF16), ((tm, tn), BF16),
                       ((3 * tm, LANES), F32), ((LANES, D_MODEL), BF16)],
            resident=[((tm, D_MODEL), BF16)],
            temps=[((sub, D_MODEL), F32), ((sub, tn), F32), ((sub, tn), F32)]),
        name="inproj_main",
    )(x2d, gain, w_main_b, w_f, cos_t, sin_t)


def _cum_kernel(ff_ref, b_ref, o_ref, *, n_blk, valid_from, rel_last):
    row = lax.broadcasted_iota(jnp.int32, (CHUNK, CHUNK), 0)
    col = lax.broadcasted_iota(jnp.int32, (CHUNK, CHUNK), 1)
    tri = (row >= col).astype(BF16)
    rows = lax.broadcasted_iota(jnp.int32, (CHUNK, LANES), 0)
    carry = jnp.zeros((1, LANES), F32)
    for blk in range(n_blk):
        z = ff_ref[blk * CHUNK:(blk + 1) * CHUNK, :] + b_ref[...]
        lf = jnp.minimum(z, 0.0) - jnp.log1p(jnp.exp(-jnp.abs(z)))
        if valid_from:
            lf = jnp.where(rows >= valid_from, lf, 0.0)
        hi = lf.astype(BF16)
        r1 = lf - hi.astype(F32)
        mid = r1.astype(BF16)
        lo = (r1 - mid.astype(F32)).astype(BF16)
        cum = (jnp.dot(tri, hi, preferred_element_type=F32)
               + jnp.dot(tri, mid, preferred_element_type=F32)
               + jnp.dot(tri, lo, preferred_element_type=F32)) + carry
        o_ref[blk * CHUNK:(blk + 1) * CHUNK, :] = cum
        carry = cum[CHUNK - 1:CHUNK, :]
    if rel_last:
        o_ref[...] = o_ref[...] - carry


def _cum(ff, b_row, *, rows_per_batch, valid_from=0, rel_last=False):
    m = ff.shape[0]
    return pl.pallas_call(
        functools.partial(_cum_kernel, n_blk=rows_per_batch // CHUNK,
                          valid_from=valid_from, rel_last=rel_last),
        grid=(m // rows_per_batch,),
        in_specs=[
            pl.BlockSpec((rows_per_batch, LANES), lambda b: (b, 0)),
            pl.BlockSpec((1, LANES), lambda b: (0, 0)),
        ],
        out_specs=pl.BlockSpec((rows_per_batch, LANES), lambda b: (b, 0)),
        out_shape=jax.ShapeDtypeStruct((m, LANES), F32),
        compiler_params=_params(("arbitrary",), pipelined=[((2 * rows_per_batch, LANES), F32)],
                                temps=[((rows_per_batch, LANES), F32)]),
        name="cumgate",
    )(ff, b_row)


def _head_view(ref, hh):
    return ref.at[:, pl.ds(hh * HEAD_DIM, HEAD_DIM)]


def _ret_kernel(logg_ref, *refs, n_chunks, meta_prefix, heads):
    *io_refs, state_scr = refs
    for hh in range(heads):
        _ret_head(logg_ref[pl.program_id(1) * heads + hh], *[_head_view(r, hh) for r in io_refs],
                  state_scr.at[hh], n_chunks=n_chunks, meta_prefix=meta_prefix)


def _ret_head(lg, *refs, n_chunks, meta_prefix):
    if meta_prefix:
        q_ref, k_ref, v_ref, g_ref, km_ref, vm_ref, gain_ref, o_ref, state_scr = refs
    else:
        q_ref, k_ref, v_ref, g_ref, gain_ref, o_ref, state_scr = refs
    ri = lax.broadcasted_iota(jnp.int32, (CHUNK, CHUNK), 0)
    ci = lax.broadcasted_iota(jnp.int32, (CHUNK, CHUNK), 1)
    diff = (ri - ci).astype(F32)
    dmat = jnp.where(diff >= 0, jnp.exp(jnp.maximum(diff, 0.0) * lg), 0.0)
    pos = lax.broadcasted_iota(jnp.int32, (CHUNK, 1), 0).astype(F32)
    xi = jnp.exp((pos + 1.0) * lg)
    zeta = jnp.exp((CHUNK - 1.0 - pos) * lg)
    g_chunk = jnp.exp(jnp.full((1, 1), float(CHUNK), F32) * lg)

    def advance(state, kc, vc):
        kz = (kc.astype(F32) * zeta).astype(BF16)
        return g_chunk * state + lax.dot_general(
            kz, vc, (((0,), (0,)), ((), ())), preferred_element_type=F32)

    state = jnp.zeros((HEAD_DIM, HEAD_DIM), F32)
    if meta_prefix:
        state = advance(state, km_ref[...], vm_ref[...])
    for c in range(n_chunks):
        state_scr[c] = state.astype(BF16)
        if c + 1 < n_chunks:
            rows = slice(c * CHUNK, (c + 1) * CHUNK)
            state = advance(state, k_ref[rows, :], v_ref[rows, :])

    gain = gain_ref[...]

    for c in range(n_chunks):
        rows = slice(c * CHUNK, (c + 1) * CHUNK)
        qc = q_ref[rows, :]
        kc = k_ref[rows, :]
        vc = v_ref[rows, :]
        s = lax.dot_general(qc, kc, (((1,), (1,)), ((), ())), preferred_element_type=F32) * dmat
        o = (jnp.dot(s.astype(BF16), vc, preferred_element_type=F32)
             + xi * jnp.dot(qc, state_scr[c], preferred_element_type=F32))
        mu = jnp.mean(o, axis=-1, keepdims=True)
        d = o - mu
        var = jnp.mean(d * d, axis=-1, keepdims=True)
        y = (d * lax.rsqrt(var + NORM_EPS)) * gain
        gt = g_ref[rows, :].astype(F32)
        o_ref[rows, :] = (_silu(gt) * y).astype(BF16)


def _retention(log_g, proj, proj_meta, gain, *, n_batch, rows_per_batch, meta_prefix, heads):
    width = heads * HEAD_DIM
    blk = lambda sec: pl.BlockSpec((rows_per_batch, width),
                                   lambda b, h, s=BLK[sec] // heads: (b, s + h))
    in_specs = [pl.BlockSpec(memory_space=pltpu.SMEM),
                blk(SEC_RQ), blk(SEC_RK), blk(SEC_RV), blk(SEC_RG)]
    args = [log_g, proj, proj, proj, proj]
    if meta_prefix:
        mblk = lambda sec: pl.BlockSpec((CHUNK, width), lambda b, h, s=BLK[sec] // heads: (0, s + h))
        in_specs += [mblk(SEC_RK), mblk(SEC_RV)]
        args += [proj_meta, proj_meta]
    in_specs.append(pl.BlockSpec((1, width), lambda b, h: (0, h)))
    args.append(gain)
    return pl.pallas_call(
        functools.partial(_ret_kernel, n_chunks=rows_per_batch // CHUNK, meta_prefix=meta_prefix,
                          heads=heads),
        grid=(n_batch, N_HEADS // heads),
        in_specs=in_specs,
        out_specs=pl.BlockSpec((rows_per_batch, width), lambda b, h: (b, h)),
        out_shape=jax.ShapeDtypeStruct((n_batch * rows_per_batch, RET_DIM), BF16),
        scratch_shapes=[pltpu.VMEM((heads, rows_per_batch // CHUNK, HEAD_DIM, HEAD_DIM), BF16)],
        compiler_params=_params(
            ("arbitrary", "arbitrary"),
            pipelined=[((5 * rows_per_batch + 2 * CHUNK, width), BF16)],
            resident=[((rows_per_batch, width), BF16)],
            temps=[((rows_per_batch, HEAD_DIM), F32)]),
        name="retention",
    )(*args)


def _fox_kernel(*refs, n_q, tq, meta_only, heads):
    for hh in range(heads):
        _fox_head(pl.program_id(1) * heads + hh, hh, *refs, n_q=n_q, tq=tq, meta_only=meta_only)


def _fox_head(h, hh, *refs, n_q, tq, meta_only):
    if meta_only:
        q_ref, km_ref, vm_ref, cq_ref, ckm_ref, o_ref, k_all, v_all = refs
        ck_all = ckm_ref[hh] * LOG2E
    else:
        q_ref, k_ref, v_ref, km_ref, vm_ref, cq_ref, ck_ref, ckm_ref, o_ref, k_all, v_all = refs
        k_ref, v_ref = _head_view(k_ref, hh), _head_view(v_ref, hh)
        k_all[CHUNK:, :] = k_ref[...]
        v_all[CHUNK:, :] = v_ref[...]
        ck_all = jnp.concatenate([ckm_ref[hh], ck_ref[hh]], axis=1) * LOG2E
    q_ref, o_ref = _head_view(q_ref, hh), _head_view(o_ref, hh)
    km_ref, vm_ref = _head_view(km_ref, hh), _head_view(vm_ref, hh)
    k_all[0:CHUNK, :] = km_ref[...]
    v_all[0:CHUNK, :] = vm_ref[...]
    lane = lax.broadcasted_iota(jnp.int32, (1, LANES), 1)
    col_m = lax.broadcasted_iota(jnp.int32, (tq, CHUNK), 1)
    row_m = lax.broadcasted_iota(jnp.int32, (tq, CHUNK), 0)
    mask_m = col_m >= META_PAD
    if meta_only:
        mask_m = mask_m & (row_m >= col_m)
    tri = (lax.broadcasted_iota(jnp.int32, (tq, tq), 0)
           >= lax.broadcasted_iota(jnp.int32, (tq, tq), 1))

    def n_keys(qi):
        return CHUNK if meta_only else CHUNK + (qi + 1) * tq

    def logit_pass(qi):
        n = n_keys(qi)
        q = q_ref[qi * tq:(qi + 1) * tq, :]
        t = lax.dot_general(q, k_all[0:n, :], (((1,), (1,)), ((), ())),
                            preferred_element_type=F32) - ck_all[:, 0:n]
        parts = [jnp.where(mask_m, t[:, 0:CHUNK], MASKED)]
        if not meta_only:
            if qi > 0:
                parts.append(t[:, CHUNK:n - tq])
            parts.append(jnp.where(tri, t[:, n - tq:n], MASKED))
        t = parts[0] if len(parts) == 1 else jnp.concatenate(parts, axis=1)
        return t, jnp.max(t, axis=1, keepdims=True)

    def exp_pass(qi, t, mx):
        rows = slice(qi * tq, (qi + 1) * tq)
        cq2 = LOG2E * jnp.sum(jnp.where(lane == h, cq_ref[rows, :], 0.0), axis=1, keepdims=True)
        m_row = mx + cq2
        p = jnp.exp2(t - (m_row - cq2))
        l = jnp.sum(p, axis=1, keepdims=True)
        acc = jnp.dot(p.astype(BF16), v_all[0:n_keys(qi), :], preferred_element_type=F32)
        o_ref[rows, :] = (acc / l).astype(BF16)

    cur = logit_pass(0)
    for qi in range(n_q):
        nxt = logit_pass(qi + 1) if qi + 1 < n_q else None
        exp_pass(qi, *cur)
        cur = nxt


def _fox(proj, proj_meta, cum_col, ck_rows, ckm_rows, *, n_batch, rows_per_batch, tq, meta_only,
         heads):
    n_q = rows_per_batch // tq
    n_all = CHUNK if meta_only else CHUNK + rows_per_batch
    width = heads * HEAD_DIM
    groups = N_HEADS // heads
    qblk = lambda sec: pl.BlockSpec((rows_per_batch, width),
                                    lambda b, h, s=BLK[sec] // heads: (b, s + h))
    mblk = lambda sec: pl.BlockSpec((CHUNK, width), lambda b, h, s=BLK[sec] // heads: (0, s + h))
    cq_spec = pl.BlockSpec((rows_per_batch, LANES), lambda b, h: (b, 0))
    ckm_spec = pl.BlockSpec((heads, 1, CHUNK), lambda b, h: (h, 0, 0))
    if meta_only:
        in_specs = [qblk(SEC_FQ), mblk(SEC_FK), mblk(SEC_FV), cq_spec, ckm_spec]
        args = [proj, proj_meta, proj_meta, cum_col, ckm_rows]
    else:
        ck_spec = pl.BlockSpec((heads, 1, rows_per_batch), lambda b, h: (b * groups + h, 0, 0))
        in_specs = [qblk(SEC_FQ), qblk(SEC_FK), qblk(SEC_FV), mblk(SEC_FK), mblk(SEC_FV),
                    cq_spec, ck_spec, ckm_spec]
        args = [proj, proj, proj, proj_meta, proj_meta, cum_col, ck_rows, ckm_rows]
    return pl.pallas_call(
        functools.partial(_fox_kernel, n_q=n_q, tq=tq, meta_only=meta_only, heads=heads),
        grid=(n_batch, groups),
        in_specs=in_specs,
        out_specs=pl.BlockSpec((rows_per_batch, width), lambda b, h: (b, h)),
        out_shape=jax.ShapeDtypeStruct((n_batch * rows_per_batch, FOX_DIM), BF16),
        scratch_shapes=[pltpu.VMEM((n_all, HEAD_DIM), BF16)] * 2,
        compiler_params=_params(
            ("arbitrary", "arbitrary"),
            pipelined=[((4 * rows_per_batch + 2 * CHUNK, width), BF16),
                       ((rows_per_batch, LANES), F32), ((8 * heads, n_all), F32)],
            resident=[((2 * n_all, HEAD_DIM), BF16)],
            temps=[((4 * tq * heads, n_all), F32)]),
        name="foxattn",
    )(*args)


def _outproj_kernel(ret_ref, fox_ref, w1_ref, w2_ref, x_ref, g_ref, h_ref, c_ref, *, tm, sub):
    for rs in _row_tiles(tm, sub):
        hcur = (x_ref[rs, :]
                + jnp.dot(ret_ref[rs, :], w1_ref[...], preferred_element_type=F32)
                + jnp.dot(fox_ref[rs, :], w2_ref[...], preferred_element_type=F32))
        h_ref[rs, :] = hcur
        c_ref[rs, :] = _rms(hcur, g_ref[...]).astype(BF16)


def _outproj(ret, fox, w_out_b, x2d, gain, *, tm, sub):
    m = x2d.shape[0]
    return pl.pallas_call(
        functools.partial(_outproj_kernel, tm=tm, sub=sub),
        grid=(m // tm,),
        in_specs=[
            pl.BlockSpec((tm, RET_DIM), lambda i: (i, 0)),
            pl.BlockSpec((tm, FOX_DIM), lambda i: (i, 0)),
            pl.BlockSpec((RET_DIM, D_MODEL), lambda i: (0, 0)),
            pl.BlockSpec((FOX_DIM, D_MODEL), lambda i: (1, 0)),
            pl.BlockSpec((tm, D_MODEL), lambda i: (i, 0)),
            pl.BlockSpec((1, D_MODEL), lambda i: (0, 0)),
        ],
        out_specs=[
            pl.BlockSpec((tm, D_MODEL), lambda i: (i, 0)),
            pl.BlockSpec((tm, D_MODEL), lambda i: (i, 0)),
        ],
        out_shape=[
            jax.ShapeDtypeStruct((m, D_MODEL), F32),
            jax.ShapeDtypeStruct((m, D_MODEL), BF16),
        ],
        compiler_params=_params(
            ("arbitrary",),
            pipelined=[((tm, RET_DIM + FOX_DIM), BF16), ((tm, D_MODEL), F32), ((tm, D_MODEL), F32),
                       ((tm, D_MODEL), BF16)],
            resident=[((RET_DIM + FOX_DIM, D_MODEL), BF16)],
            temps=[((sub, D_MODEL), F32), ((sub, D_MODEL), F32)]),
        name="outproj",
    )(ret, fox, w_out_b, w_out_b, x2d, gain)


def _outproj_meta_kernel(mix_ref, w_ref, x_ref, g_ref, c_ref, wq_ref, acc_scr):
    k = pl.program_id(0)
    wq_ref[...] = w_ref[...].astype(BF16)
    part = jnp.dot(mix_ref[...], wq_ref[...], preferred_element_type=F32)

    @pl.when(k == 0)
    def _():
        acc_scr[...] = x_ref[...] + part

    @pl.when(k == 1)
    def _():
        c_ref[...] = _rms(acc_scr[...] + part, g_ref[...]).astype(BF16)


def _outproj_meta(mix_m, w_out, xm, gain):
    half = D_MODEL // 2
    return pl.pallas_call(
        _outproj_meta_kernel,
        grid=(2,),
        in_specs=[
            pl.BlockSpec((CHUNK, half), lambda k: (0, k)),
            pl.BlockSpec((half, D_MODEL), lambda k: (k, 0)),
            pl.BlockSpec((CHUNK, D_MODEL), lambda k: (0, 0)),
            pl.BlockSpec((1, D_MODEL), lambda k: (0, 0)),
        ],
        out_specs=[
            pl.BlockSpec((CHUNK, D_MODEL), lambda k: (0, 0)),
            pl.BlockSpec((half, D_MODEL), lambda k: (k, 0)),
        ],
        out_shape=[
            jax.ShapeDtypeStruct((CHUNK, D_MODEL), BF16),
            jax.ShapeDtypeStruct((D_MODEL, D_MODEL), BF16),
        ],
        scratch_shapes=[pltpu.VMEM((CHUNK, D_MODEL), F32)],
        compiler_params=_params(
            ("arbitrary",),
            pipelined=[((half, D_MODEL), F32), ((half, D_MODEL), BF16), ((CHUNK, 2 * D_MODEL), F32)],
            resident=[((CHUNK, D_MODEL), F32)],
            temps=[((CHUNK, D_MODEL), F32), ((CHUNK, D_MODEL), F32)]),
        name="outproj_meta",
    )(mix_m, w_out, xm, gain)


def _up_kernel(c_ref, cm_ref, wg_ref, wv_ref, cwg_ref, cwv_ref, cbg_ref, cbv_ref, wd_ref,
               o_ref, wdq_ref, wg_s, wv_s, ug_s, uv_s, *, tm, sub, tiles_per_batch):
    i = pl.program_id(1)
    wdq_ref[...] = wd_ref[...].astype(BF16)

    @pl.when(i == 0)
    def _():
        wg_s[...] = wg_ref[...].astype(BF16)
        wv_s[...] = wv_ref[...].astype(BF16)

    @pl.when(i % tiles_per_batch == 0)
    def _():
        cm = cm_ref[...]
        ug_s[0:8, :] = jnp.dot(cm, wg_s[...], preferred_element_type=F32)[8:16, :]
        uv_s[0:8, :] = jnp.dot(cm, wv_s[...], preferred_element_type=F32)[8:16, :]

    def conv(c, rs, w_s, u_s, cw_ref, cb_ref):
        u_s[8 + rs.start:8 + rs.stop, :] = jnp.dot(c, w_s[...], preferred_element_type=F32)
        ext = u_s[rs.start:8 + rs.stop, :]
        n = rs.stop - rs.start
        return (cb_ref[...]
                + cw_ref[0:1, :] * pltpu.roll(ext, 2, 0)[8:8 + n, :]
                + cw_ref[1:2, :] * pltpu.roll(ext, 1, 0)[8:8 + n, :]
                + cw_ref[2:3, :] * ext[8:8 + n, :])

    for rs in _row_tiles(tm, sub):
        c = c_ref[rs, :]
        gate = conv(c, rs, wg_s, ug_s, cwg_ref, cbg_ref)
        val = conv(c, rs, wv_s, uv_s, cwv_ref, cbv_ref)
        o_ref[rs, :] = ((gate / (1.0 + jnp.exp(-gate))) * val).astype(BF16)
    ug_s[0:8, :] = ug_s[tm:tm + 8, :]
    uv_s[0:8, :] = uv_s[tm:tm + 8, :]


def _upconv(c, c_meta, w_up, conv_w, conv_b, w_down, *, rows_per_batch, tm, tn, sub):
    m = c.shape[0]
    nj = D_FF // tn
    ni = m // tm
    wd_rows = D_FF // (nj * ni)
    assert wd_rows * nj * ni == D_FF and wd_rows % 16 == 0
    tiles_per_batch = rows_per_batch // tm
    halo_blk = c_meta.shape[0] // 16 - 1
    return pl.pallas_call(
        functools.partial(_up_kernel, tm=tm, sub=sub, tiles_per_batch=tiles_per_batch),
        grid=(nj, m // tm),
        in_specs=[
            pl.BlockSpec((tm, D_MODEL), lambda j, i: (i, 0)),
            pl.BlockSpec((16, D_MODEL), lambda j, i: (halo_blk, 0)),
            pl.BlockSpec((D_MODEL, tn), lambda j, i: (0, j)),
            pl.BlockSpec((D_MODEL, tn), lambda j, i: (0, j + nj)),
            pl.BlockSpec((3, tn), lambda j, i: (0, j)),
            pl.BlockSpec((3, tn), lambda j, i: (0, j + nj)),
            pl.BlockSpec((1, tn), lambda j, i: (0, j)),
            pl.BlockSpec((1, tn), lambda j, i: (0, j + nj)),
            pl.BlockSpec((wd_rows, D_MODEL), lambda j, i: (j * ni + i, 0)),
        ],
        out_specs=[
            pl.BlockSpec((tm, tn), lambda j, i: (i, j)),
            pl.BlockSpec((wd_rows, D_MODEL), lambda j, i: (j * ni + i, 0)),
        ],
        out_shape=[
            jax.ShapeDtypeStruct((m, D_FF), BF16),
            jax.ShapeDtypeStruct((D_FF, D_MODEL), BF16),
        ],
        scratch_shapes=[
            pltpu.VMEM((D_MODEL, tn), BF16),
            pltpu.VMEM((D_MODEL, tn), BF16),
            pltpu.VMEM((tm + 8, tn), F32),
            pltpu.VMEM((tm + 8, tn), F32),
        ],
        compiler_params=_params(
            ("arbitrary", "arbitrary"),
            pipelined=[((tm, D_MODEL), BF16), ((D_MODEL, 2 * tn), F32), ((tm, tn), BF16),
                       ((wd_rows, D_MODEL), F32), ((wd_rows, D_MODEL), BF16)],
            resident=[((D_MODEL, 2 * tn), BF16), ((2 * (tm + 8), tn), F32)],
            temps=[((sub, tn), F32), ((sub, tn), F32)]),
        name="upconv",
    )(c, c_meta, w_up, w_up, conv_w, conv_w, conv_b, conv_b, w_down)


def _down_kernel(a_ref, w_ref, h_ref, g_ref, o_ref, *, tm, sub):
    for rs in _row_tiles(tm, sub):
        hcur = h_ref[rs, :] + jnp.dot(a_ref[rs, :], w_ref[...], preferred_element_type=F32)
        o_ref[rs, :] = _rms(hcur, g_ref[...])


def _down(act, w_down_b, h1, gain, *, tm, sub):
    m = act.shape[0]
    return pl.pallas_call(
        functools.partial(_down_kernel, tm=tm, sub=sub),
        grid=(m // tm,),
        in_specs=[
            pl.BlockSpec((tm, D_FF), lambda i: (i, 0)),
            pl.BlockSpec((D_FF, D_MODEL), lambda i: (0, 0), pipeline_mode=pl.Buffered(1)),
            pl.BlockSpec((tm, D_MODEL), lambda i: (i, 0)),
            pl.BlockSpec((1, D_MODEL), lambda i: (0, 0)),
        ],
        out_specs=pl.BlockSpec((tm, D_MODEL), lambda i: (i, 0)),
        out_shape=jax.ShapeDtypeStruct((m, D_MODEL), F32),
        compiler_params=_params(
            ("arbitrary",),
            pipelined=[((tm, D_FF), BF16), ((tm, D_MODEL), F32), ((tm, D_MODEL), F32)],
            resident=[((D_FF, D_MODEL), BF16)],
            temps=[((sub, D_MODEL), F32), ((sub, D_MODEL), F32)]),
        name="downproj",
    )(act, w_down_b, h1, gain)


def _rotary_tables(pos):
    inv_freq = 1.0 / (ROPE_BASE ** (jnp.arange(0, HEAD_DIM, 2, dtype=F32) / HEAD_DIM))
    ang = pos[:, None] * inv_freq[None, :]
    cos, sin = jnp.cos(ang), jnp.sin(ang)
    return jnp.concatenate([cos, cos], axis=1), jnp.concatenate([-sin, sin], axis=1)


def kernel(x, meta_tokens, norm1_gain, w_in, b_forget, ret_norm_gain, w_out, norm2_gain,
           w_up, conv_w, conv_b, w_down, final_norm_gain):
    n_batch, seq, d_model = x.shape
    assert d_model == D_MODEL and seq % CHUNK == 0 and w_in.shape[0] == 1
    assert meta_tokens.shape == (N_META, D_MODEL)
    t = _Tiles()
    x2d = x.reshape(n_batch * seq, D_MODEL)
    xm = jnp.concatenate([jnp.zeros((META_PAD, D_MODEL), F32), meta_tokens.astype(F32)], axis=0)

    w_in_t = w_in[0].T
    b_row = jnp.pad(b_forget[0], (0, LANES - N_HEADS)).reshape(1, LANES)
    log_g = jnp.log1p(-jnp.exp2(-5.0 - jnp.arange(N_HEADS, dtype=F32)))
    g1 = norm1_gain[0].reshape(1, D_MODEL)
    g2 = norm2_gain[0].reshape(1, D_MODEL)
    gf = final_norm_gain.reshape(1, D_MODEL)
    gr = ret_norm_gain[0].reshape(1, RET_DIM)

    cos_r, sin_r = _rotary_tables(N_META + jnp.arange(seq, dtype=F32))
    cos_m, sin_m = _rotary_tables(jnp.maximum(jnp.arange(CHUNK, dtype=F32) - META_PAD, 0.0))

    proj_m, ff_m, w_main_b, w_f = _inproj_meta(xm, g1, w_in_t, cos_m, sin_m,
                                               tn=t.inproj_first_cols)
    cum_m = _cum(ff_m, b_row, rows_per_batch=CHUNK, valid_from=META_PAD, rel_last=True)
    ckm_rows = cum_m[:, :N_HEADS].T.reshape(N_HEADS, 1, CHUNK)
    ret_m = _retention(log_g, proj_m, None, gr, n_batch=1, rows_per_batch=CHUNK, meta_prefix=False,
                       heads=t.mixer_heads)
    fox_m = _fox(proj_m, proj_m, cum_m, None, ckm_rows, n_batch=1, rows_per_batch=CHUNK,
                 tq=CHUNK, meta_only=True, heads=t.mixer_heads)
    c_m, w_out_b = _outproj_meta(jnp.concatenate([ret_m, fox_m], axis=1), w_out[0], xm, g2)

    proj, ff = _inproj_main(x2d, g1, w_main_b, w_f, cos_r, sin_r,
                            tm=t.inproj_rows, tn=t.inproj_cols, sub=t.proj_sub_rows)
    cum = _cum(ff, b_row, rows_per_batch=seq)
    ck_rows = (cum[:, :N_HEADS].reshape(n_batch, seq, N_HEADS).transpose(0, 2, 1)
               .reshape(n_batch * N_HEADS, 1, seq))
    ret = _retention(log_g, proj, proj_m, gr, n_batch=n_batch, rows_per_batch=seq, meta_prefix=True,
                     heads=t.mixer_heads)
    fox = _fox(proj, proj_m, cum, ck_rows, ckm_rows, n_batch=n_batch, rows_per_batch=seq,
               tq=t.fox_q_rows, meta_only=False, heads=t.mixer_heads)
    h1, c = _outproj(ret, fox, w_out_b, x2d, g2, tm=t.outproj_rows, sub=t.proj_sub_rows)
    act, w_down_b = _upconv(c, c_m, w_up[0], conv_w[0], conv_b, w_down[0],
                            rows_per_batch=seq, tm=t.up_rows, tn=t.up_cols, sub=t.up_sub_rows)
    out = _down(act, w_down_b, h1, gf, tm=t.down_rows, sub=t.proj_sub_rows)
    return out.reshape(n_batch, seq, D_MODEL)
```

```python
import functools
from typing import NamedTuple

import jax
import jax.numpy as jnp
from jax import lax
from jax.experimental import pallas as pl
from jax.experimental.pallas import tpu as pltpu

F32 = jnp.float32
BF16 = jnp.bfloat16

D_MODEL = 2048
N_META = 16
CHUNK = 128
N_HEADS = 8
HEAD_DIM = 128
RET_DIM = N_HEADS * HEAD_DIM
FOX_DIM = N_HEADS * HEAD_DIM
MAIN_COLS = 4 * RET_DIM + 3 * FOX_DIM
D_FF = 5632
ROPE_BASE = 10000.0
NORM_EPS = 1e-6
META_PAD = CHUNK - N_META
MASKED = -1e30
LANES = 128
MIB = 1024 * 1024
LOG2E = 1.4426950408889634
FOXQ_SCALE = HEAD_DIM ** -0.5 * LOG2E

(SEC_RQ, SEC_RK, SEC_RV, SEC_RG, SEC_FQ, SEC_FK, SEC_FV) = range(7)
BLK = {s: s * N_HEADS for s in range(7)}


V7X_VMEM_BYTES = 64 * MIB
V7X_VMEM_UNSCOPED = 6 * MIB


class _Tiles(NamedTuple):
    inproj_rows: int = 1024
    inproj_first_cols: int = 512
    inproj_cols: int = 1024
    proj_sub_rows: int = 256
    fox_q_rows: int = 256
    mixer_heads: int = 2
    outproj_rows: int = 512
    up_rows: int = 2048
    up_sub_rows: int = 1024
    up_cols: int = 512
    down_rows: int = 512


def _nbytes(shape, dtype):
    n = jnp.dtype(dtype).itemsize
    for d in shape:
        n *= d
    return n


def _params(sem, pipelined, resident=(), temps=()):
    need = (2 * sum(_nbytes(*b) for b in pipelined) + sum(_nbytes(*b) for b in resident)
            + sum(_nbytes(*b) for b in temps))
    limit = -(-need // MIB) * MIB
    assert limit <= V7X_VMEM_BYTES - V7X_VMEM_UNSCOPED, (limit, sem)
    return pltpu.CompilerParams(dimension_semantics=sem, vmem_limit_bytes=limit)


def _rms(x, gain):
    ms = jnp.mean(x * x, axis=-1, keepdims=True)
    return (x * lax.rsqrt(ms + NORM_EPS)) * gain


def _silu(x):
    h = 0.5 * x
    return h + h * jnp.tanh(h)


def _row_tiles(tm, sub):
    return [slice(r * sub, (r + 1) * sub) for r in range(tm // sub)]


class _Rows(NamedTuple):
    x_ref: object
    cos_ref: object
    sin_ref: object
    o_ref: object
    ff_ref: object
    a_scr: object
    tiles: list


def _inproj_body(j, tn, groups, g_ref, w_bf, wf_ref):
    sec = j // (RET_DIM // tn)

    def rotary_store(grp, rs, acc, scale):
        cos = grp.cos_ref[rs, :]
        sin = grp.sin_ref[rs, :]
        for hh in range(tn // LANES):
            cols = slice(hh * LANES, (hh + 1) * LANES)
            t = acc[:, cols]
            y = t * cos + pltpu.roll(t, HEAD_DIM // 2, 1) * sin
            if scale is not None:
                y = y * scale
            grp.o_ref[rs, cols] = y.astype(BF16)

    @pl.when(j == 0)
    def _():
        for grp in groups:
            for rs in grp.tiles:
                a = _rms(grp.x_ref[rs, :], g_ref[...]).astype(BF16)
                grp.a_scr[rs, :] = a
                grp.ff_ref[rs, :] = lax.dot_general(a, wf_ref[...], (((1,), (1,)), ((), ())),
                                                    preferred_element_type=F32)
                rotary_store(grp, rs, jnp.dot(a, w_bf[...], preferred_element_type=F32), None)

    @pl.when((j > 0) & (sec <= SEC_RK))
    def _():
        scale = jnp.where(sec == SEC_RK, HEAD_DIM ** -0.5, 1.0).astype(F32)
        for grp in groups:
            for rs in grp.tiles:
                acc = jnp.dot(grp.a_scr[rs, :], w_bf[...], preferred_element_type=F32)
                rotary_store(grp, rs, acc, scale)

    @pl.when(sec > SEC_RK)
    def _():
        scale = jnp.where(sec == SEC_FQ, FOXQ_SCALE, 1.0).astype(F32)
        for grp in groups:
            for rs in grp.tiles:
                acc = jnp.dot(grp.a_scr[rs, :], w_bf[...], preferred_element_type=F32)
                grp.o_ref[rs, :] = (acc * scale).astype(BF16)


def _inproj_meta_kernel(xm_ref, g_ref, w_ref, wf8_ref, cosm_ref, sinm_ref,
                        om_ref, ffm_ref, wq_ref, wf_ref, am_scr, *, tn):
    j = pl.program_id(0)
    wq_ref[...] = w_ref[...].T.astype(BF16)

    @pl.when(j == 0)
    def _():
        pad = jnp.zeros((LANES - N_HEADS, D_MODEL), F32)
        wf_ref[...] = jnp.concatenate([wf8_ref[...], pad], axis=0).astype(BF16)

    groups = [_Rows(xm_ref, cosm_ref, sinm_ref, om_ref, ffm_ref, am_scr, _row_tiles(CHUNK, CHUNK))]
    _inproj_body(j, tn, groups, g_ref, wq_ref, wf_ref)


def _inproj_main_kernel(x_ref, g_ref, w_ref, wf_ref, cos_ref, sin_ref, o_ref, ff_ref, a_scr,
                        *, tm, tn, sub):
    groups = [_Rows(x_ref, cos_ref, sin_ref, o_ref, ff_ref, a_scr, _row_tiles(tm, sub))]
    _inproj_body(pl.program_id(1), tn, groups, g_ref, w_ref, wf_ref)


def _inproj_meta(xm, gain, w_in_t, cos_m, sin_m, *, tn):
    const = lambda shape: pl.BlockSpec(shape, lambda j: (0, 0))
    return pl.pallas_call(
        functools.partial(_inproj_meta_kernel, tn=tn),
        grid=(MAIN_COLS // tn,),
        in_specs=[
            const((CHUNK, D_MODEL)),
            const((1, D_MODEL)),
            pl.BlockSpec((tn, D_MODEL), lambda j: (j, 0)),
            pl.BlockSpec((N_HEADS, D_MODEL), lambda j: (MAIN_COLS // N_HEADS, 0)),
            const((CHUNK, LANES)),
            const((CHUNK, LANES)),
        ],
        out_specs=[
            pl.BlockSpec((CHUNK, tn), lambda j: (0, j)),
            const((CHUNK, LANES)),
            pl.BlockSpec((D_MODEL, tn), lambda j: (0, j)),
            const((LANES, D_MODEL)),
        ],
        out_shape=[
            jax.ShapeDtypeStruct((CHUNK, MAIN_COLS), BF16),
            jax.ShapeDtypeStruct((CHUNK, LANES), F32),
            jax.ShapeDtypeStruct((D_MODEL, MAIN_COLS), BF16),
            jax.ShapeDtypeStruct((LANES, D_MODEL), BF16),
        ],
        scratch_shapes=[pltpu.VMEM((CHUNK, D_MODEL), BF16)],
        compiler_params=_params(
            ("arbitrary",),
            pipelined=[((tn, D_MODEL), F32), ((D_MODEL, tn), BF16), ((CHUNK, D_MODEL), F32),
                       ((CHUNK, tn), BF16), ((3 * CHUNK, LANES), F32), ((LANES, D_MODEL), BF16)],
            resident=[((CHUNK, D_MODEL), BF16)],
            temps=[((tn, D_MODEL), F32), ((CHUNK, D_MODEL), F32), ((CHUNK, tn), F32)]),
        name="inproj_meta",
    )(xm, gain, w_in_t, w_in_t, cos_m, sin_m)


def _inproj_main(x2d, gain, w_main_b, w_f, cos_t, sin_t, *, tm, tn, sub):
    m = x2d.shape[0]
    nb = cos_t.shape[0] // tm
    return pl.pallas_call(
        functools.partial(_inproj_main_kernel, tm=tm, tn=tn, sub=sub),
        grid=(m // tm, MAIN_COLS // tn),
        in_specs=[
            pl.BlockSpec((tm, D_MODEL), lambda i, j: (i, 0)),
            pl.BlockSpec((1, D_MODEL), lambda i, j: (0, 0)),
            pl.BlockSpec((D_MODEL, tn), lambda i, j: (0, j)),
            pl.BlockSpec((LANES, D_MODEL), lambda i, j: (0, 0)),
            pl.BlockSpec((tm, LANES), lambda i, j: (i % nb, 0)),
            pl.BlockSpec((tm, LANES), lambda i, j: (i % nb, 0)),
        ],
        out_specs=[
            pl.BlockSpec((tm, tn), lambda i, j: (i, j)),
            pl.BlockSpec((tm, LANES), lambda i, j: (i, 0)),
        ],
        out_shape=[
            jax.ShapeDtypeStruct((m, MAIN_COLS), BF16),
            jax.ShapeDtypeStruct((m, LANES), F32),
        ],
        scratch_shapes=[pltpu.VMEM((tm, D_MODEL), BF16)],
        compiler_params=_params(
            ("arbitrary", "arbitrary"),
            pipelined=[((tm, D_MODEL), F32), ((D_MODEL, tn), BF16), ((tm, tn), BF16),
                       ((3 * tm, LANES), F32), ((LANES, D_MODEL), BF16)],
            resident=[((tm, D_MODEL), BF16)],
            temps=[((sub, D_MODEL), F32), ((sub, tn), F32), ((sub, tn), F32)]),
        name="inproj_main",
    )(x2d, gain, w_main_b, w_f, cos_t, sin_t)


def _cum_kernel(ff_ref, b_ref, o_ref, *, n_blk, valid_from, rel_last):
    row = lax.broadcasted_iota(jnp.int32, (CHUNK, CHUNK), 0)
    col = lax.broadcasted_iota(jnp.int32, (CHUNK, CHUNK), 1)
    tri = (row >= col).astype(BF16)
    rows = lax.broadcasted_iota(jnp.int32, (CHUNK, LANES), 0)
    carry = jnp.zeros((1, LANES), F32)
    for blk in range(n_blk):
        z = ff_ref[blk * CHUNK:(blk + 1) * CHUNK, :] + b_ref[...]
        lf = jnp.minimum(z, 0.0) - jnp.log1p(jnp.exp(-jnp.abs(z)))
        if valid_from:
            lf = jnp.where(rows >= valid_from, lf, 0.0)
        hi = lf.astype(BF16)
        r1 = lf - hi.astype(F32)
        mid = r1.astype(BF16)
        lo = (r1 - mid.astype(F32)).astype(BF16)
        cum = (jnp.dot(tri, hi, preferred_element_type=F32)
               + jnp.dot(tri, mid, preferred_element_type=F32)
               + jnp.dot(tri, lo, preferred_element_type=F32)) + carry
        o_ref[blk * CHUNK:(blk + 1) * CHUNK, :] = cum
        carry = cum[CHUNK - 1:CHUNK, :]
    if rel_last:
        o_ref[...] = o_ref[...] - carry


def _cum(ff, b_row, *, rows_per_batch, valid_from=0, rel_last=False):
    m = ff.shape[0]
    return pl.pallas_call(
        functools.partial(_cum_kernel, n_blk=rows_per_batch // CHUNK,
                          valid_from=valid_from, rel_last=rel_last),
        grid=(m // rows_per_batch,),
        in_specs=[
            pl.BlockSpec((rows_per_batch, LANES), lambda b: (b, 0)),
            pl.BlockSpec((1, LANES), lambda b: (0, 0)),
        ],
        out_specs=pl.BlockSpec((rows_per_batch, LANES), lambda b: (b, 0)),
        out_shape=jax.ShapeDtypeStruct((m, LANES), F32),
        compiler_params=_params(("arbitrary",), pipelined=[((2 * rows_per_batch, LANES), F32)],
                                temps=[((rows_per_batch, LANES), F32)]),
        name="cumgate",
    )(ff, b_row)


def _head_view(ref, hh):
    return ref.at[:, pl.ds(hh * HEAD_DIM, HEAD_DIM)]


def _ret_kernel(logg_ref, *refs, n_chunks, meta_prefix, heads):
    *io_refs, state_scr = refs
    for hh in range(heads):
        _ret_head(logg_ref[pl.program_id(1) * heads + hh], *[_head_view(r, hh) for r in io_refs],
                  state_scr.at[hh], n_chunks=n_chunks, meta_prefix=meta_prefix)


def _ret_head(lg, *refs, n_chunks, meta_prefix):
    if meta_prefix:
        q_ref, k_ref, v_ref, g_ref, km_ref, vm_ref, gain_ref, o_ref, state_scr = refs
    else:
        q_ref, k_ref, v_ref, g_ref, gain_ref, o_ref, state_scr = refs
    ri = lax.broadcasted_iota(jnp.int32, (CHUNK, CHUNK), 0)
    ci = lax.broadcasted_iota(jnp.int32, (CHUNK, CHUNK), 1)
    diff = (ri - ci).astype(F32)
    dmat = jnp.where(diff >= 0, jnp.exp(jnp.maximum(diff, 0.0) * lg), 0.0)
    pos = lax.broadcasted_iota(jnp.int32, (CHUNK, 1), 0).astype(F32)
    xi = jnp.exp((pos + 1.0) * lg)
    zeta = jnp.exp((CHUNK - 1.0 - pos) * lg)
    g_chunk = jnp.exp(jnp.full((1, 1), float(CHUNK), F32) * lg)

    def advance(state, kc, vc):
        kz = (kc.astype(F32) * zeta).astype(BF16)
        return g_chunk * state + lax.dot_general(
            kz, vc, (((0,), (0,)), ((), ())), preferred_element_type=F32)

    state = jnp.zeros((HEAD_DIM, HEAD_DIM), F32)
    if meta_prefix:
        state = advance(state, km_ref[...], vm_ref[...])
    for c in range(n_chunks):
        state_scr[c] = state.astype(BF16)
        if c + 1 < n_chunks:
            rows = slice(c * CHUNK, (c + 1) * CHUNK)
            state = advance(state, k_ref[rows, :], v_ref[rows, :])

    gain = gain_ref[...]

    for c in range(n_chunks):
        rows = slice(c * CHUNK, (c + 1) * CHUNK)
        qc = q_ref[rows, :]
        kc = k_ref[rows, :]
        vc = v_ref[rows, :]
        s = lax.dot_general(qc, kc, (((1,), (1,)), ((), ())), preferred_element_type=F32) * dmat
        o = (jnp.dot(s.astype(BF16), vc, preferred_element_type=F32)
             + xi * jnp.dot(qc, state_scr[c], preferred_element_type=F32))
        mu = jnp.mean(o, axis=-1, keepdims=True)
        d = o - mu
        var = jnp.mean(d * d, axis=-1, keepdims=True)
        y = (d * lax.rsqrt(var + NORM_EPS)) * gain
        gt = g_ref[rows, :].astype(F32)
        o_ref[rows, :] = (_silu(gt) * y).astype(BF16)


def _retention(log_g, proj, proj_meta, gain, *, n_batch, rows_per_batch, meta_prefix, heads):
    width = heads * HEAD_DIM
    blk = lambda sec: pl.BlockSpec((rows_per_batch, width),
                                   lambda b, h, s=BLK[sec] // heads: (b, s + h))
    in_specs = [pl.BlockSpec(memory_space=pltpu.SMEM),
                blk(SEC_RQ), blk(SEC_RK), blk(SEC_RV), blk(SEC_RG)]
    args = [log_g, proj, proj, proj, proj]
    if meta_prefix:
        mblk = lambda sec: pl.BlockSpec((CHUNK, width), lambda b, h, s=BLK[sec] // heads: (0, s + h))
        in_specs += [mblk(SEC_RK), mblk(SEC_RV)]
        args += [proj_meta, proj_meta]
    in_specs.append(pl.BlockSpec((1, width), lambda b, h: (0, h)))
    args.append(gain)
    return pl.pallas_call(
        functools.partial(_ret_kernel, n_chunks=rows_per_batch // CHUNK, meta_prefix=meta_prefix,
                          heads=heads),
        grid=(n_batch, N_HEADS // heads),
        in_specs=in_specs,
        out_specs=pl.BlockSpec((rows_per_batch, width), lambda b, h: (b, h)),
        out_shape=jax.ShapeDtypeStruct((n_batch * rows_per_batch, RET_DIM), BF16),
        scratch_shapes=[pltpu.VMEM((heads, rows_per_batch // CHUNK, HEAD_DIM, HEAD_DIM), BF16)],
        compiler_params=_params(
            ("arbitrary", "arbitrary"),
            pipelined=[((5 * rows_per_batch + 2 * CHUNK, width), BF16)],
            resident=[((rows_per_batch, width), BF16)],
            temps=[((rows_per_batch, HEAD_DIM), F32)]),
        name="retention",
    )(*args)


def _fox_kernel(*refs, n_q, tq, meta_only, heads):
    for hh in range(heads):
        _fox_head(pl.program_id(1) * heads + hh, hh, *refs, n_q=n_q, tq=tq, meta_only=meta_only)


def _fox_head(h, hh, *refs, n_q, tq, meta_only):
    if meta_only:
        q_ref, km_ref, vm_ref, cq_ref, ckm_ref, o_ref, k_all, v_all = refs
        ck_all = ckm_ref[hh] * LOG2E
    else:
        q_ref, k_ref, v_ref, km_ref, vm_ref, cq_ref, ck_ref, ckm_ref, o_ref, k_all, v_all = refs
        k_ref, v_ref = _head_view(k_ref, hh), _head_view(v_ref, hh)
        k_all[CHUNK:, :] = k_ref[...]
        v_all[CHUNK:, :] = v_ref[...]
        ck_all = jnp.concatenate([ckm_ref[hh], ck_ref[hh]], axis=1) * LOG2E
    q_ref, o_ref = _head_view(q_ref, hh), _head_view(o_ref, hh)
    km_ref, vm_ref = _head_view(km_ref, hh), _head_view(vm_ref, hh)
    k_all[0:CHUNK, :] = km_ref[...]
    v_all[0:CHUNK, :] = vm_ref[...]
    lane = lax.broadcasted_iota(jnp.int32, (1, LANES), 1)
    col_m = lax.broadcasted_iota(jnp.int32, (tq, CHUNK), 1)
    row_m = lax.broadcasted_iota(jnp.int32, (tq, CHUNK), 0)
    mask_m = col_m >= META_PAD
    if meta_only:
        mask_m = mask_m & (row_m >= col_m)
    tri = (lax.broadcasted_iota(jnp.int32, (tq, tq), 0)
           >= lax.broadcasted_iota(jnp.int32, (tq, tq), 1))

    def n_keys(qi):
        return CHUNK if meta_only else CHUNK + (qi + 1) * tq

    def logit_pass(qi):
        n = n_keys(qi)
        q = q_ref[qi * tq:(qi + 1) * tq, :]
        t = lax.dot_general(q, k_all[0:n, :], (((1,), (1,)), ((), ())),
                            preferred_element_type=F32) - ck_all[:, 0:n]
        parts = [jnp.where(mask_m, t[:, 0:CHUNK], MASKED)]
        if not meta_only:
            if qi > 0:
                parts.append(t[:, CHUNK:n - tq])
            parts.append(jnp.where(tri, t[:, n - tq:n], MASKED))
        t = parts[0] if len(parts) == 1 else jnp.concatenate(parts, axis=1)
        return t, jnp.max(t, axis=1, keepdims=True)

    def exp_pass(qi, t, mx):
        rows = slice(qi * tq, (qi + 1) * tq)
        cq2 = LOG2E * jnp.sum(jnp.where(lane == h, cq_ref[rows, :], 0.0), axis=1, keepdims=True)
        m_row = mx + cq2
        p = jnp.exp2(t - (m_row - cq2))
        l = jnp.sum(p, axis=1, keepdims=True)
        acc = jnp.dot(p.astype(BF16), v_all[0:n_keys(qi), :], preferred_element_type=F32)
        o_ref[rows, :] = (acc / l).astype(BF16)

    cur = logit_pass(0)
    for qi in range(n_q):
        nxt = logit_pass(qi + 1) if qi + 1 < n_q else None
        exp_pass(qi, *cur)
        cur = nxt


def _fox(proj, proj_meta, cum_col, ck_rows, ckm_rows, *, n_batch, rows_per_batch, tq, meta_only,
         heads):
    n_q = rows_per_batch // tq
    n_all = CHUNK if meta_only else CHUNK + rows_per_batch
    width = heads * HEAD_DIM
    groups = N_HEADS // heads
    qblk = lambda sec: pl.BlockSpec((rows_per_batch, width),
                                    lambda b, h, s=BLK[sec] // heads: (b, s + h))
    mblk = lambda sec: pl.BlockSpec((CHUNK, width), lambda b, h, s=BLK[sec] // heads: (0, s + h))
    cq_spec = pl.BlockSpec((rows_per_batch, LANES), lambda b, h: (b, 0))
    ckm_spec = pl.BlockSpec((heads, 1, CHUNK), lambda b, h: (h, 0, 0))
    if meta_only:
        in_specs = [qblk(SEC_FQ), mblk(SEC_FK), mblk(SEC_FV), cq_spec, ckm_spec]
        args = [proj, proj_meta, proj_meta, cum_col, ckm_rows]
    else:
        ck_spec = pl.BlockSpec((heads, 1, rows_per_batch), lambda b, h: (b * groups + h, 0, 0))
        in_specs = [qblk(SEC_FQ), qblk(SEC_FK), qblk(SEC_FV), mblk(SEC_FK), mblk(SEC_FV),
                    cq_spec, ck_spec, ckm_spec]
        args = [proj, proj, proj, proj_meta, proj_meta, cum_col, ck_rows, ckm_rows]
    return pl.pallas_call(
        functools.partial(_fox_kernel, n_q=n_q, tq=tq, meta_only=meta_only, heads=heads),
        grid=(n_batch, groups),
        in_specs=in_specs,
        out_specs=pl.BlockSpec((rows_per_batch, width), lambda b, h: (b, h)),
        out_shape=jax.ShapeDtypeStruct((n_batch * rows_per_batch, FOX_DIM), BF16),
        scratch_shapes=[pltpu.VMEM((n_all, HEAD_DIM), BF16)] * 2,
        compiler_params=_params(
            ("arbitrary", "arbitrary"),
            pipelined=[((4 * rows_per_batch + 2 * CHUNK, width), BF16),
                       ((rows_per_batch, LANES), F32), ((8 * heads, n_all), F32)],
            resident=[((2 * n_all, HEAD_DIM), BF16)],
            temps=[((4 * tq * heads, n_all), F32)]),
        name="foxattn",
    )(*args)


def _outproj_kernel(ret_ref, fox_ref, w1_ref, w2_ref, x_ref, g_ref, h_ref, c_ref, *, tm, sub):
    for rs in _row_tiles(tm, sub):
        hcur = (x_ref[rs, :]
                + jnp.dot(ret_ref[rs, :], w1_ref[...], preferred_element_type=F32)
                + jnp.dot(fox_ref[rs, :], w2_ref[...], preferred_element_type=F32))
        h_ref[rs, :] = hcur
        c_ref[rs, :] = _rms(hcur, g_ref[...]).astype(BF16)


def _outproj(ret, fox, w_out_b, x2d, gain, *, tm, sub):
    m = x2d.shape[0]
    return pl.pallas_call(
        functools.partial(_outproj_kernel, tm=tm, sub=sub),
        grid=(m // tm,),
        in_specs=[
            pl.BlockSpec((tm, RET_DIM), lambda i: (i, 0)),
            pl.BlockSpec((tm, FOX_DIM), lambda i: (i, 0)),
            pl.BlockSpec((RET_DIM, D_MODEL), lambda i: (0, 0)),
            pl.BlockSpec((FOX_DIM, D_MODEL), lambda i: (1, 0)),
            pl.BlockSpec((tm, D_MODEL), lambda i: (i, 0)),
            pl.BlockSpec((1, D_MODEL), lambda i: (0, 0)),
        ],
        out_specs=[
            pl.BlockSpec((tm, D_MODEL), lambda i: (i, 0)),
            pl.BlockSpec((tm, D_MODEL), lambda i: (i, 0)),
        ],
        out_shape=[
            jax.ShapeDtypeStruct((m, D_MODEL), F32),
            jax.ShapeDtypeStruct((m, D_MODEL), BF16),
        ],
        compiler_params=_params(
            ("arbitrary",),
            pipelined=[((tm, RET_DIM + FOX_DIM), BF16), ((tm, D_MODEL), F32), ((tm, D_MODEL), F32),
                       ((tm, D_MODEL), BF16)],
            resident=[((RET_DIM + FOX_DIM, D_MODEL), BF16)],
            temps=[((sub, D_MODEL), F32), ((sub, D_MODEL), F32)]),
        name="outproj",
    )(ret, fox, w_out_b, w_out_b, x2d, gain)


def _outproj_meta_kernel(mix_ref, w_ref, x_ref, g_ref, c_ref, wq_ref, acc_scr):
    k = pl.program_id(0)
    wq_ref[...] = w_ref[...].astype(BF16)
    part = jnp.dot(mix_ref[...], wq_ref[...], preferred_element_type=F32)

    @pl.when(k == 0)
    def _():
        acc_scr[...] = x_ref[...] + part

    @pl.when(k == 1)
    def _():
        c_ref[...] = _rms(acc_scr[...] + part, g_ref[...]).astype(BF16)


def _outproj_meta(mix_m, w_out, xm, gain):
    half = D_MODEL // 2
    return pl.pallas_call(
        _outproj_meta_kernel,
        grid=(2,),
        in_specs=[
            pl.BlockSpec((CHUNK, half), lambda k: (0, k)),
            pl.BlockSpec((half, D_MODEL), lambda k: (k, 0)),
            pl.BlockSpec((CHUNK, D_MODEL), lambda k: (0, 0)),
            pl.BlockSpec((1, D_MODEL), lambda k: (0, 0)),
        ],
        out_specs=[
            pl.BlockSpec((CHUNK, D_MODEL), lambda k: (0, 0)),
            pl.BlockSpec((half, D_MODEL), lambda k: (k, 0)),
        ],
        out_shape=[
            jax.ShapeDtypeStruct((CHUNK, D_MODEL), BF16),
            jax.ShapeDtypeStruct((D_MODEL, D_MODEL), BF16),
        ],
        scratch_shapes=[pltpu.VMEM((CHUNK, D_MODEL), F32)],
        compiler_params=_params(
            ("arbitrary",),
            pipelined=[((half, D_MODEL), F32), ((half, D_MODEL), BF16), ((CHUNK, 2 * D_MODEL), F32)],
            resident=[((CHUNK, D_MODEL), F32)],
            temps=[((CHUNK, D_MODEL), F32), ((CHUNK, D_MODEL), F32)]),
        name="outproj_meta",
    )(mix_m, w_out, xm, gain)


def _up_kernel(c_ref, cm_ref, wg_ref, wv_ref, cwg_ref, cwv_ref, cbg_ref, cbv_ref, wd_ref,
               o_ref, wdq_ref, wg_s, wv_s, ug_s, uv_s, umg_s, umv_s, *, tm, sub, tiles_per_batch):
    i = pl.program_id(1)
    wdq_ref[...] = wd_ref[...].astype(BF16)

    @pl.when(i == 0)
    def _():
        wg_s[...] = wg_ref[...].astype(BF16)
        wv_s[...] = wv_ref[...].astype(BF16)
        cm = cm_ref[...]
        umg_s[...] = jnp.dot(cm, wg_s[...], preferred_element_type=F32)[8:16, :]
        umv_s[...] = jnp.dot(cm, wv_s[...], preferred_element_type=F32)[8:16, :]

    @pl.when(i % tiles_per_batch == 0)
    def _():
        ug_s[0:8, :] = umg_s[...]
        uv_s[0:8, :] = umv_s[...]

    def conv(c, rs, w_s, u_s, cw_ref, cb_ref):
        u_s[8 + rs.start:8 + rs.stop, :] = jnp.dot(c, w_s[...], preferred_element_type=F32)
        ext = u_s[rs.start:8 + rs.stop, :]
        n = rs.stop - rs.start
        return (cb_ref[...]
                + cw_ref[0:1, :] * pltpu.roll(ext, 2, 0)[8:8 + n, :]
                + cw_ref[1:2, :] * pltpu.roll(ext, 1, 0)[8:8 + n, :]
                + cw_ref[2:3, :] * ext[8:8 + n, :])

    for rs in _row_tiles(tm, sub):
        c = c_ref[rs, :]
        gate = conv(c, rs, wg_s, ug_s, cwg_ref, cbg_ref)
        val = conv(c, rs, wv_s, uv_s, cwv_ref, cbv_ref)
        o_ref[rs, :] = ((gate / (1.0 + jnp.exp(-gate))) * val).astype(BF16)
    ug_s[0:8, :] = ug_s[tm:tm + 8, :]
    uv_s[0:8, :] = uv_s[tm:tm + 8, :]


def _upconv(c, c_meta, w_up, conv_w, conv_b, w_down, *, rows_per_batch, tm, tn, sub):
    m = c.shape[0]
    nj = D_FF // tn
    ni = m // tm
    wd_rows = D_FF // (nj * ni)
    assert wd_rows * nj * ni == D_FF and wd_rows % 16 == 0
    tiles_per_batch = rows_per_batch // tm
    halo_blk = c_meta.shape[0] // 16 - 1
    return pl.pallas_call(
        functools.partial(_up_kernel, tm=tm, sub=sub, tiles_per_batch=tiles_per_batch),
        grid=(nj, m // tm),
        in_specs=[
            pl.BlockSpec((tm, D_MODEL), lambda j, i: (i, 0)),
            pl.BlockSpec((16, D_MODEL), lambda j, i: (halo_blk, 0)),
            pl.BlockSpec((D_MODEL, tn), lambda j, i: (0, j)),
            pl.BlockSpec((D_MODEL, tn), lambda j, i: (0, j + nj)),
            pl.BlockSpec((3, tn), lambda j, i: (0, j)),
            pl.BlockSpec((3, tn), lambda j, i: (0, j + nj)),
            pl.BlockSpec((1, tn), lambda j, i: (0, j)),
            pl.BlockSpec((1, tn), lambda j, i: (0, j + nj)),
            pl.BlockSpec((wd_rows, D_MODEL), lambda j, i: (j * ni + i, 0)),
        ],
        out_specs=[
            pl.BlockSpec((tm, tn), lambda j, i: (i, j)),
            pl.BlockSpec((wd_rows, D_MODEL), lambda j, i: (j * ni + i, 0)),
        ],
        out_shape=[
            jax.ShapeDtypeStruct((m, D_FF), BF16),
            jax.ShapeDtypeStruct((D_FF, D_MODEL), BF16),
        ],
        scratch_shapes=[
            pltpu.VMEM((D_MODEL, tn), BF16),
            pltpu.VMEM((D_MODEL, tn), BF16),
            pltpu.VMEM((tm + 8, tn), F32),
            pltpu.VMEM((tm + 8, tn), F32),
            pltpu.VMEM((8, tn), F32),
            pltpu.VMEM((8, tn), F32),
        ],
        compiler_params=_params(
            ("arbitrary", "arbitrary"),
            pipelined=[((tm, D_MODEL), BF16), ((D_MODEL, 2 * tn), F32), ((tm, tn), BF16),
                       ((wd_rows, D_MODEL), F32), ((wd_rows, D_MODEL), BF16)],
            resident=[((D_MODEL, 2 * tn), BF16), ((2 * (tm + 8), tn), F32)],
            temps=[((sub, tn), F32), ((sub, tn), F32)]),
        name="upconv",
    )(c, c_meta, w_up, w_up, conv_w, conv_w, conv_b, conv_b, w_down)


def _down_kernel(a_ref, w_ref, h_ref, g_ref, o_ref, *, tm, sub):
    for rs in _row_tiles(tm, sub):
        hcur = h_ref[rs, :] + jnp.dot(a_ref[rs, :], w_ref[...], preferred_element_type=F32)
        o_ref[rs, :] = _rms(hcur, g_ref[...])


def _down(act, w_down_b, h1, gain, *, tm, sub):
    m = act.shape[0]
    return pl.pallas_call(
        functools.partial(_down_kernel, tm=tm, sub=sub),
        grid=(m // tm,),
        in_specs=[
            pl.BlockSpec((tm, D_FF), lambda i: (i, 0)),
            pl.BlockSpec((D_FF, D_MODEL), lambda i: (0, 0), pipeline_mode=pl.Buffered(1)),
            pl.BlockSpec((tm, D_MODEL), lambda i: (i, 0)),
            pl.BlockSpec((1, D_MODEL), lambda i: (0, 0)),
        ],
        out_specs=pl.BlockSpec((tm, D_MODEL), lambda i: (i, 0)),
        out_shape=jax.ShapeDtypeStruct((m, D_MODEL), F32),
        compiler_params=_params(
            ("arbitrary",),
            pipelined=[((tm, D_FF), BF16), ((tm, D_MODEL), F32), ((tm, D_MODEL), F32)],
            resident=[((D_FF, D_MODEL), BF16)],
            temps=[((sub, D_MODEL), F32), ((sub, D_MODEL), F32)]),
        name="downproj",
    )(act, w_down_b, h1, gain)


def _rotary_tables(pos):
    inv_freq = 1.0 / (ROPE_BASE ** (jnp.arange(0, HEAD_DIM, 2, dtype=F32) / HEAD_DIM))
    ang = pos[:, None] * inv_freq[None, :]
    cos, sin = jnp.cos(ang), jnp.sin(ang)
    return jnp.concatenate([cos, cos], axis=1), jnp.concatenate([-sin, sin], axis=1)


def kernel(x, meta_tokens, norm1_gain, w_in, b_forget, ret_norm_gain, w_out, norm2_gain,
           w_up, conv_w, conv_b, w_down, final_norm_gain):
    n_batch, seq, d_model = x.shape
    assert d_model == D_MODEL and seq % CHUNK == 0 and w_in.shape[0] == 1
    assert meta_tokens.shape == (N_META, D_MODEL)
    t = _Tiles()
    x2d = x.reshape(n_batch * seq, D_MODEL)
    xm = jnp.concatenate([jnp.zeros((META_PAD, D_MODEL), F32), meta_tokens.astype(F32)], axis=0)

    w_in_t = w_in[0].T
    b_row = jnp.pad(b_forget[0], (0, LANES - N_HEADS)).reshape(1, LANES)
    log_g = jnp.log1p(-jnp.exp2(-5.0 - jnp.arange(N_HEADS, dtype=F32)))
    g1 = norm1_gain[0].reshape(1, D_MODEL)
    g2 = norm2_gain[0].reshape(1, D_MODEL)
    gf = final_norm_gain.reshape(1, D_MODEL)
    gr = ret_norm_gain[0].reshape(1, RET_DIM)

    cos_r, sin_r = _rotary_tables(N_META + jnp.arange(seq, dtype=F32))
    cos_m, sin_m = _rotary_tables(jnp.maximum(jnp.arange(CHUNK, dtype=F32) - META_PAD, 0.0))

    proj_m, ff_m, w_main_b, w_f = _inproj_meta(xm, g1, w_in_t, cos_m, sin_m,
                                               tn=t.inproj_first_cols)
    cum_m = _cum(ff_m, b_row, rows_per_batch=CHUNK, valid_from=META_PAD, rel_last=True)
    ckm_rows = cum_m[:, :N_HEADS].T.reshape(N_HEADS, 1, CHUNK)
    ret_m = _retention(log_g, proj_m, None, gr, n_batch=1, rows_per_batch=CHUNK, meta_prefix=False,
                       heads=t.mixer_heads)
    fox_m = _fox(proj_m, proj_m, cum_m, None, ckm_rows, n_batch=1, rows_per_batch=CHUNK,
                 tq=CHUNK, meta_only=True, heads=t.mixer_heads)
    c_m, w_out_b = _outproj_meta(jnp.concatenate([ret_m, fox_m], axis=1), w_out[0], xm, g2)

    proj, ff = _inproj_main(x2d, g1, w_main_b, w_f, cos_r, sin_r,
                            tm=t.inproj_rows, tn=t.inproj_cols, sub=t.proj_sub_rows)
    cum = _cum(ff, b_row, rows_per_batch=seq)
    ck_rows = (cum[:, :N_HEADS].reshape(n_batch, seq, N_HEADS).transpose(0, 2, 1)
               .reshape(n_batch * N_HEADS, 1, seq))
    ret = _retention(log_g, proj, proj_m, gr, n_batch=n_batch, rows_per_batch=seq, meta_prefix=True,
                     heads=t.mixer_heads)
    fox = _fox(proj, proj_m, cum, ck_rows, ckm_rows, n_batch=n_batch, rows_per_batch=seq,
               tq=t.fox_q_rows, meta_only=False, heads=t.mixer_heads)
    h1, c = _outproj(ret, fox, w_out_b, x2d, g2, tm=t.outproj_rows, sub=t.proj_sub_rows)
    act, w_down_b = _upconv(c, c_m, w_up[0], conv_w[0], conv_b, w_down[0],
                            rows_per_batch=seq, tm=t.up_rows, tn=t.up_cols, sub=t.up_sub_rows)
    out = _down(act, w_down_b, h1, gf, tm=t.down_rows, sub=t.proj_sub_rows)
    return out.reshape(n_batch, seq, D_MODEL)
```

```python
import functools
from typing import NamedTuple

import jax
import jax.numpy as jnp
from jax import lax
from jax.experimental import pallas as pl
from jax.experimental.pallas import tpu as pltpu

F32 = jnp.float32
BF16 = jnp.bfloat16

D_MODEL = 2048
N_META = 16
CHUNK = 128
N_HEADS = 8
HEAD_DIM = 128
RET_DIM = N_HEADS * HEAD_DIM
FOX_DIM = N_HEADS * HEAD_DIM
MAIN_COLS = 4 * RET_DIM + 3 * FOX_DIM
D_FF = 5632
ROPE_BASE = 10000.0
NORM_EPS = 1e-6
META_PAD = CHUNK - N_META
MASKED = -1e30
LANES = 128
MIB = 1024 * 1024
LOG2E = 1.4426950408889634
FOXQ_SCALE = HEAD_DIM ** -0.5 * LOG2E

(SEC_RQ, SEC_RK, SEC_RV, SEC_RG, SEC_FQ, SEC_FK, SEC_FV) = range(7)
BLK = {s: s * N_HEADS for s in range(7)}


V7X_VMEM_BYTES = 64 * MIB
V7X_VMEM_UNSCOPED = 6 * MIB


class _Tiles(NamedTuple):
    inproj_rows: int = 1024
    inproj_first_cols: int = 512
    inproj_cols: int = 1024
    proj_sub_rows: int = 256
    fox_q_rows: int = 256
    mixer_heads: int = 2
    fox_heads: int = 4
    outproj_rows: int = 512
    up_rows: int = 2048
    up_sub_rows: int = 1024
    up_cols: int = 512
    down_rows: int = 512


def _nbytes(shape, dtype):
    n = jnp.dtype(dtype).itemsize
    for d in shape:
        n *= d
    return n


def _params(sem, pipelined, resident=(), temps=()):
    need = (2 * sum(_nbytes(*b) for b in pipelined) + sum(_nbytes(*b) for b in resident)
            + sum(_nbytes(*b) for b in temps))
    limit = -(-need // MIB) * MIB
    assert limit <= V7X_VMEM_BYTES - V7X_VMEM_UNSCOPED, (limit, sem)
    return pltpu.CompilerParams(dimension_semantics=sem, vmem_limit_bytes=limit)


def _rms(x, gain):
    ms = jnp.mean(x * x, axis=-1, keepdims=True)
    return (x * lax.rsqrt(ms + NORM_EPS)) * gain


def _silu(x):
    h = 0.5 * x
    return h + h * jnp.tanh(h)


def _row_tiles(tm, sub):
    return [slice(r * sub, (r + 1) * sub) for r in range(tm // sub)]


class _Rows(NamedTuple):
    x_ref: object
    cos_ref: object
    sin_ref: object
    o_ref: object
    ff_ref: object
    a_scr: object
    tiles: list


def _inproj_body(j, tn, groups, g_ref, w_bf, wf_ref):
    sec = j // (RET_DIM // tn)

    def rotary_store(grp, rs, acc, scale):
        cos = grp.cos_ref[rs, :]
        sin = grp.sin_ref[rs, :]
        for hh in range(tn // LANES):
            cols = slice(hh * LANES, (hh + 1) * LANES)
            t = acc[:, cols]
            y = t * cos + pltpu.roll(t, HEAD_DIM // 2, 1) * sin
            if scale is not None:
                y = y * scale
            grp.o_ref[rs, cols] = y.astype(BF16)

    @pl.when(j == 0)
    def _():
        for grp in groups:
            for rs in grp.tiles:
                a = _rms(grp.x_ref[rs, :], g_ref[...]).astype(BF16)
                grp.a_scr[rs, :] = a
                grp.ff_ref[rs, :] = lax.dot_general(a, wf_ref[...], (((1,), (1,)), ((), ())),
                                                    preferred_element_type=F32)
                rotary_store(grp, rs, jnp.dot(a, w_bf[...], preferred_element_type=F32), None)

    @pl.when((j > 0) & (sec <= SEC_RK))
    def _():
        scale = jnp.where(sec == SEC_RK, HEAD_DIM ** -0.5, 1.0).astype(F32)
        for grp in groups:
            for rs in grp.tiles:
                acc = jnp.dot(grp.a_scr[rs, :], w_bf[...], preferred_element_type=F32)
                rotary_store(grp, rs, acc, scale)

    @pl.when(sec > SEC_RK)
    def _():
        scale = jnp.where(sec == SEC_FQ, FOXQ_SCALE, 1.0).astype(F32)
        for grp in groups:
            for rs in grp.tiles:
                acc = jnp.dot(grp.a_scr[rs, :], w_bf[...], preferred_element_type=F32)
                grp.o_ref[rs, :] = (acc * scale).astype(BF16)


def _inproj_meta_kernel(xm_ref, g_ref, w_ref, wf8_ref, cosm_ref, sinm_ref,
                        om_ref, ffm_ref, wq_ref, wf_ref, am_scr, *, tn):
    j = pl.program_id(0)
    wq_ref[...] = w_ref[...].T.astype(BF16)

    @pl.when(j == 0)
    def _():
        pad = jnp.zeros((LANES - N_HEADS, D_MODEL), F32)
        wf_ref[...] = jnp.concatenate([wf8_ref[...], pad], axis=0).astype(BF16)

    groups = [_Rows(xm_ref, cosm_ref, sinm_ref, om_ref, ffm_ref, am_scr, _row_tiles(CHUNK, CHUNK))]
    _inproj_body(j, tn, groups, g_ref, wq_ref, wf_ref)


def _inproj_main_kernel(x_ref, g_ref, w_ref, wf_ref, cos_ref, sin_ref, o_ref, ff_ref, a_scr,
                        *, tm, tn, sub):
    groups = [_Rows(x_ref, cos_ref, sin_ref, o_ref, ff_ref, a_scr, _row_tiles(tm, sub))]
    _inproj_body(pl.program_id(1), tn, groups, g_ref, w_ref, wf_ref)


def _inproj_meta(xm, gain, w_in_t, cos_m, sin_m, *, tn):
    const = lambda shape: pl.BlockSpec(shape, lambda j: (0, 0))
    return pl.pallas_call(
        functools.partial(_inproj_meta_kernel, tn=tn),
        grid=(MAIN_COLS // tn,),
        in_specs=[
            const((CHUNK, D_MODEL)),
            const((1, D_MODEL)),
            pl.BlockSpec((tn, D_MODEL), lambda j: (j, 0)),
            pl.BlockSpec((N_HEADS, D_MODEL), lambda j: (MAIN_COLS // N_HEADS, 0)),
            const((CHUNK, LANES)),
            const((CHUNK, LANES)),
        ],
        out_specs=[
            pl.BlockSpec((CHUNK, tn), lambda j: (0, j)),
            const((CHUNK, LANES)),
            pl.BlockSpec((D_MODEL, tn), lambda j: (0, j)),
            const((LANES, D_MODEL)),
        ],
        out_shape=[
            jax.ShapeDtypeStruct((CHUNK, MAIN_COLS), BF16),
            jax.ShapeDtypeStruct((CHUNK, LANES), F32),
            jax.ShapeDtypeStruct((D_MODEL, MAIN_COLS), BF16),
            jax.ShapeDtypeStruct((LANES, D_MODEL), BF16),
        ],
        scratch_shapes=[pltpu.VMEM((CHUNK, D_MODEL), BF16)],
        compiler_params=_params(
            ("arbitrary",),
            pipelined=[((tn, D_MODEL), F32), ((D_MODEL, tn), BF16), ((CHUNK, D_MODEL), F32),
                       ((CHUNK, tn), BF16), ((3 * CHUNK, LANES), F32), ((LANES, D_MODEL), BF16)],
            resident=[((CHUNK, D_MODEL), BF16)],
            temps=[((tn, D_MODEL), F32), ((CHUNK, D_MODEL), F32), ((CHUNK, tn), F32)]),
        name="inproj_meta",
    )(xm, gain, w_in_t, w_in_t, cos_m, sin_m)


def _inproj_main(x2d, gain, w_main_b, w_f, cos_t, sin_t, *, tm, tn, sub):
    m = x2d.shape[0]
    nb = cos_t.shape[0] // tm
    return pl.pallas_call(
        functools.partial(_inproj_main_kernel, tm=tm, tn=tn, sub=sub),
        grid=(m // tm, MAIN_COLS // tn),
        in_specs=[
            pl.BlockSpec((tm, D_MODEL), lambda i, j: (i, 0)),
            pl.BlockSpec((1, D_MODEL), lambda i, j: (0, 0)),
            pl.BlockSpec((D_MODEL, tn), lambda i, j: (0, j)),
            pl.BlockSpec((LANES, D_MODEL), lambda i, j: (0, 0)),
            pl.BlockSpec((tm, LANES), lambda i, j: (i % nb, 0)),
            pl.BlockSpec((tm, LANES), lambda i, j: (i % nb, 0)),
        ],
        out_specs=[
            pl.BlockSpec((tm, tn), lambda i, j: (i, j)),
            pl.BlockSpec((tm, LANES), lambda i, j: (i, 0)),
        ],
        out_shape=[
            jax.ShapeDtypeStruct((m, MAIN_COLS), BF16),
            jax.ShapeDtypeStruct((m, LANES), F32),
        ],
        scratch_shapes=[pltpu.VMEM((tm, D_MODEL), BF16)],
        compiler_params=_params(
            ("arbitrary", "arbitrary"),
            pipelined=[((tm, D_MODEL), F32), ((D_MODEL, tn), BF16), ((tm, tn), BF16),
                       ((3 * tm, LANES), F32), ((LANES, D_MODEL), BF16)],
            resident=[((tm, D_MODEL), BF16)],
            temps=[((sub, D_MODEL), F32), ((sub, tn), F32), ((sub, tn), F32)]),
        name="inproj_main",
    )(x2d, gain, w_main_b, w_f, cos_t, sin_t)


def _cum_kernel(ff_ref, b_ref, o_ref, *, n_blk, valid_from, rel_last):
    row = lax.broadcasted_iota(jnp.int32, (CHUNK, CHUNK), 0)
    col = lax.broadcasted_iota(jnp.int32, (CHUNK, CHUNK), 1)
    tri = (row >= col).astype(BF16)
    rows = lax.broadcasted_iota(jnp.int32, (CHUNK, LANES), 0)
    carry = jnp.zeros((1, LANES), F32)
    for blk in range(n_blk):
        z = ff_ref[blk * CHUNK:(blk + 1) * CHUNK, :] + b_ref[...]
        lf = jnp.minimum(z, 0.0) - jnp.log1p(jnp.exp(-jnp.abs(z)))
        if valid_from:
            lf = jnp.where(rows >= valid_from, lf, 0.0)
        hi = lf.astype(BF16)
        r1 = lf - hi.astype(F32)
        mid = r1.astype(BF16)
        lo = (r1 - mid.astype(F32)).astype(BF16)
        cum = (jnp.dot(tri, hi, preferred_element_type=F32)
               + jnp.dot(tri, mid, preferred_element_type=F32)
               + jnp.dot(tri, lo, preferred_element_type=F32)) + carry
        o_ref[blk * CHUNK:(blk + 1) * CHUNK, :] = cum
        carry = cum[CHUNK - 1:CHUNK, :]
    if rel_last:
        o_ref[...] = o_ref[...] - carry


def _cum(ff, b_row, *, rows_per_batch, valid_from=0, rel_last=False):
    m = ff.shape[0]
    return pl.pallas_call(
        functools.partial(_cum_kernel, n_blk=rows_per_batch // CHUNK,
                          valid_from=valid_from, rel_last=rel_last),
        grid=(m // rows_per_batch,),
        in_specs=[
            pl.BlockSpec((rows_per_batch, LANES), lambda b: (b, 0)),
            pl.BlockSpec((1, LANES), lambda b: (0, 0)),
        ],
        out_specs=pl.BlockSpec((rows_per_batch, LANES), lambda b: (b, 0)),
        out_shape=jax.ShapeDtypeStruct((m, LANES), F32),
        compiler_params=_params(("arbitrary",), pipelined=[((2 * rows_per_batch, LANES), F32)],
                                temps=[((rows_per_batch, LANES), F32)]),
        name="cumgate",
    )(ff, b_row)


def _head_view(ref, hh):
    return ref.at[:, pl.ds(hh * HEAD_DIM, HEAD_DIM)]


def _ret_kernel(logg_ref, *refs, n_chunks, meta_prefix, heads):
    *io_refs, state_scr = refs
    for hh in range(heads):
        _ret_head(logg_ref[pl.program_id(1) * heads + hh], *[_head_view(r, hh) for r in io_refs],
                  state_scr.at[hh], n_chunks=n_chunks, meta_prefix=meta_prefix)


def _ret_head(lg, *refs, n_chunks, meta_prefix):
    if meta_prefix:
        q_ref, k_ref, v_ref, g_ref, km_ref, vm_ref, gain_ref, o_ref, state_scr = refs
    else:
        q_ref, k_ref, v_ref, g_ref, gain_ref, o_ref, state_scr = refs
    ri = lax.broadcasted_iota(jnp.int32, (CHUNK, CHUNK), 0)
    ci = lax.broadcasted_iota(jnp.int32, (CHUNK, CHUNK), 1)
    diff = (ri - ci).astype(F32)
    dmat = jnp.where(diff >= 0, jnp.exp(jnp.maximum(diff, 0.0) * lg), 0.0)
    pos = lax.broadcasted_iota(jnp.int32, (CHUNK, 1), 0).astype(F32)
    xi = jnp.exp((pos + 1.0) * lg)
    zeta = jnp.exp((CHUNK - 1.0 - pos) * lg)
    g_chunk = jnp.exp(jnp.full((1, 1), float(CHUNK), F32) * lg)

    def advance(state, kc, vc):
        kz = (kc.astype(F32) * zeta).astype(BF16)
        return g_chunk * state + lax.dot_general(
            kz, vc, (((0,), (0,)), ((), ())), preferred_element_type=F32)

    state = jnp.zeros((HEAD_DIM, HEAD_DIM), F32)
    if meta_prefix:
        state = advance(state, km_ref[...], vm_ref[...])
    for c in range(n_chunks):
        state_scr[c] = state.astype(BF16)
        if c + 1 < n_chunks:
            rows = slice(c * CHUNK, (c + 1) * CHUNK)
            state = advance(state, k_ref[rows, :], v_ref[rows, :])

    gain = gain_ref[...]

    for c in range(n_chunks):
        rows = slice(c * CHUNK, (c + 1) * CHUNK)
        qc = q_ref[rows, :]
        kc = k_ref[rows, :]
        vc = v_ref[rows, :]
        s = lax.dot_general(qc, kc, (((1,), (1,)), ((), ())), preferred_element_type=F32) * dmat
        o = (jnp.dot(s.astype(BF16), vc, preferred_element_type=F32)
             + xi * jnp.dot(qc, state_scr[c], preferred_element_type=F32))
        mu = jnp.mean(o, axis=-1, keepdims=True)
        d = o - mu
        var = jnp.mean(d * d, axis=-1, keepdims=True)
        y = (d * lax.rsqrt(var + NORM_EPS)) * gain
        gt = g_ref[rows, :].astype(F32)
        o_ref[rows, :] = (_silu(gt) * y).astype(BF16)


def _retention(log_g, proj, proj_meta, gain, *, n_batch, rows_per_batch, meta_prefix, heads):
    width = heads * HEAD_DIM
    blk = lambda sec: pl.BlockSpec((rows_per_batch, width),
                                   lambda b, h, s=BLK[sec] // heads: (b, s + h))
    in_specs = [pl.BlockSpec(memory_space=pltpu.SMEM),
                blk(SEC_RQ), blk(SEC_RK), blk(SEC_RV), blk(SEC_RG)]
    args = [log_g, proj, proj, proj, proj]
    if meta_prefix:
        mblk = lambda sec: pl.BlockSpec((CHUNK, width), lambda b, h, s=BLK[sec] // heads: (0, s + h))
        in_specs += [mblk(SEC_RK), mblk(SEC_RV)]
        args += [proj_meta, proj_meta]
    in_specs.append(pl.BlockSpec((1, width), lambda b, h: (0, h)))
    args.append(gain)
    return pl.pallas_call(
        functools.partial(_ret_kernel, n_chunks=rows_per_batch // CHUNK, meta_prefix=meta_prefix,
                          heads=heads),
        grid=(n_batch, N_HEADS // heads),
        in_specs=in_specs,
        out_specs=pl.BlockSpec((rows_per_batch, width), lambda b, h: (b, h)),
        out_shape=jax.ShapeDtypeStruct((n_batch * rows_per_batch, RET_DIM), BF16),
        scratch_shapes=[pltpu.VMEM((heads, rows_per_batch // CHUNK, HEAD_DIM, HEAD_DIM), BF16)],
        compiler_params=_params(
            ("arbitrary", "arbitrary"),
            pipelined=[((5 * rows_per_batch + 2 * CHUNK, width), BF16)],
            resident=[((rows_per_batch, width), BF16)],
            temps=[((rows_per_batch, HEAD_DIM), F32)]),
        name="retention",
    )(*args)


def _fox_kernel(*refs, n_q, tq, meta_only, heads):
    for hh in range(heads):
        _fox_head(pl.program_id(1) * heads + hh, hh, *refs, n_q=n_q, tq=tq, meta_only=meta_only)


def _fox_head(h, hh, *refs, n_q, tq, meta_only):
    if meta_only:
        q_ref, km_ref, vm_ref, cq_ref, ckm_ref, o_ref, k_all, v_all = refs
        ck_all = ckm_ref[hh] * LOG2E
    else:
        q_ref, k_ref, v_ref, km_ref, vm_ref, cq_ref, ck_ref, ckm_ref, o_ref, k_all, v_all = refs
        k_ref, v_ref = _head_view(k_ref, hh), _head_view(v_ref, hh)
        k_all[CHUNK:, :] = k_ref[...]
        v_all[CHUNK:, :] = v_ref[...]
        ck_all = jnp.concatenate([ckm_ref[hh], ck_ref[hh]], axis=1) * LOG2E
    q_ref, o_ref = _head_view(q_ref, hh), _head_view(o_ref, hh)
    km_ref, vm_ref = _head_view(km_ref, hh), _head_view(vm_ref, hh)
    k_all[0:CHUNK, :] = km_ref[...]
    v_all[0:CHUNK, :] = vm_ref[...]
    lane = lax.broadcasted_iota(jnp.int32, (1, LANES), 1)
    col_m = lax.broadcasted_iota(jnp.int32, (tq, CHUNK), 1)
    row_m = lax.broadcasted_iota(jnp.int32, (tq, CHUNK), 0)
    mask_m = col_m >= META_PAD
    if meta_only:
        mask_m = mask_m & (row_m >= col_m)
    tri = (lax.broadcasted_iota(jnp.int32, (tq, tq), 0)
           >= lax.broadcasted_iota(jnp.int32, (tq, tq), 1))

    def n_keys(qi):
        return CHUNK if meta_only else CHUNK + (qi + 1) * tq

    def logit_pass(qi):
        n = n_keys(qi)
        q = q_ref[qi * tq:(qi + 1) * tq, :]
        t = lax.dot_general(q, k_all[0:n, :], (((1,), (1,)), ((), ())),
                            preferred_element_type=F32) - ck_all[:, 0:n]
        parts = [jnp.where(mask_m, t[:, 0:CHUNK], MASKED)]
        if not meta_only:
            if qi > 0:
                parts.append(t[:, CHUNK:n - tq])
            parts.append(jnp.where(tri, t[:, n - tq:n], MASKED))
        t = parts[0] if len(parts) == 1 else jnp.concatenate(parts, axis=1)
        return t, jnp.max(t, axis=1, keepdims=True)

    def exp_pass(qi, t, mx):
        rows = slice(qi * tq, (qi + 1) * tq)
        cq2 = LOG2E * jnp.sum(jnp.where(lane == h, cq_ref[rows, :], 0.0), axis=1, keepdims=True)
        m_row = mx + cq2
        p = jnp.exp2(t - (m_row - cq2))
        l = jnp.sum(p, axis=1, keepdims=True)
        acc = jnp.dot(p.astype(BF16), v_all[0:n_keys(qi), :], preferred_element_type=F32)
        o_ref[rows, :] = (acc / l).astype(BF16)

    cur = logit_pass(0)
    for qi in range(n_q):
        nxt = logit_pass(qi + 1) if qi + 1 < n_q else None
        exp_pass(qi, *cur)
        cur = nxt


def _fox(proj, proj_meta, cum_col, ck_rows, ckm_rows, *, n_batch, rows_per_batch, tq, meta_only,
         heads):
    n_q = rows_per_batch // tq
    n_all = CHUNK if meta_only else CHUNK + rows_per_batch
    width = heads * HEAD_DIM
    groups = N_HEADS // heads
    qblk = lambda sec: pl.BlockSpec((rows_per_batch, width),
                                    lambda b, h, s=BLK[sec] // heads: (b, s + h))
    mblk = lambda sec: pl.BlockSpec((CHUNK, width), lambda b, h, s=BLK[sec] // heads: (0, s + h))
    cq_spec = pl.BlockSpec((rows_per_batch, LANES), lambda b, h: (b, 0))
    ckm_spec = pl.BlockSpec((heads, 1, CHUNK), lambda b, h: (h, 0, 0))
    if meta_only:
        in_specs = [qblk(SEC_FQ), mblk(SEC_FK), mblk(SEC_FV), cq_spec, ckm_spec]
        args = [proj, proj_meta, proj_meta, cum_col, ckm_rows]
    else:
        ck_spec = pl.BlockSpec((heads, 1, rows_per_batch), lambda b, h: (b * groups + h, 0, 0))
        in_specs = [qblk(SEC_FQ), qblk(SEC_FK), qblk(SEC_FV), mblk(SEC_FK), mblk(SEC_FV),
                    cq_spec, ck_spec, ckm_spec]
        args = [proj, proj, proj, proj_meta, proj_meta, cum_col, ck_rows, ckm_rows]
    return pl.pallas_call(
        functools.partial(_fox_kernel, n_q=n_q, tq=tq, meta_only=meta_only, heads=heads),
        grid=(n_batch, groups),
        in_specs=in_specs,
        out_specs=pl.BlockSpec((rows_per_batch, width), lambda b, h: (b, h)),
        out_shape=jax.ShapeDtypeStruct((n_batch * rows_per_batch, FOX_DIM), BF16),
        scratch_shapes=[pltpu.VMEM((n_all, HEAD_DIM), BF16)] * 2,
        compiler_params=_params(
            ("arbitrary", "arbitrary"),
            pipelined=[((4 * rows_per_batch + 2 * CHUNK, width), BF16),
                       ((rows_per_batch, LANES), F32), ((8 * heads, n_all), F32)],
            resident=[((2 * n_all, HEAD_DIM), BF16)],
            temps=[((4 * tq * heads, n_all), F32)]),
        name="foxattn",
    )(*args)


def _outproj_kernel(ret_ref, fox_ref, w1_ref, w2_ref, x_ref, g_ref, h_ref, c_ref, *, tm, sub):
    for rs in _row_tiles(tm, sub):
        hcur = (x_ref[rs, :]
                + jnp.dot(ret_ref[rs, :], w1_ref[...], preferred_element_type=F32)
                + jnp.dot(fox_ref[rs, :], w2_ref[...], preferred_element_type=F32))
        h_ref[rs, :] = hcur
        c_ref[rs, :] = _rms(hcur, g_ref[...]).astype(BF16)


def _outproj(ret, fox, w_out_b, x2d, gain, *, tm, sub):
    m = x2d.shape[0]
    return pl.pallas_call(
        functools.partial(_outproj_kernel, tm=tm, sub=sub),
        grid=(m // tm,),
        in_specs=[
            pl.BlockSpec((tm, RET_DIM), lambda i: (i, 0)),
            pl.BlockSpec((tm, FOX_DIM), lambda i: (i, 0)),
            pl.BlockSpec((RET_DIM, D_MODEL), lambda i: (0, 0)),
            pl.BlockSpec((FOX_DIM, D_MODEL), lambda i: (1, 0)),
            pl.BlockSpec((tm, D_MODEL), lambda i: (i, 0)),
            pl.BlockSpec((1, D_MODEL), lambda i: (0, 0)),
        ],
        out_specs=[
            pl.BlockSpec((tm, D_MODEL), lambda i: (i, 0)),
            pl.BlockSpec((tm, D_MODEL), lambda i: (i, 0)),
        ],
        out_shape=[
            jax.ShapeDtypeStruct((m, D_MODEL), F32),
            jax.ShapeDtypeStruct((m, D_MODEL), BF16),
        ],
        compiler_params=_params(
            ("arbitrary",),
            pipelined=[((tm, RET_DIM + FOX_DIM), BF16), ((tm, D_MODEL), F32), ((tm, D_MODEL), F32),
                       ((tm, D_MODEL), BF16)],
            resident=[((RET_DIM + FOX_DIM, D_MODEL), BF16)],
            temps=[((sub, D_MODEL), F32), ((sub, D_MODEL), F32)]),
        name="outproj",
    )(ret, fox, w_out_b, w_out_b, x2d, gain)


def _outproj_meta_kernel(mix_ref, w_ref, x_ref, g_ref, c_ref, wq_ref, acc_scr):
    k = pl.program_id(0)
    wq_ref[...] = w_ref[...].astype(BF16)
    part = jnp.dot(mix_ref[...], wq_ref[...], preferred_element_type=F32)

    @pl.when(k == 0)
    def _():
        acc_scr[...] = x_ref[...] + part

    @pl.when(k == 1)
    def _():
        c_ref[...] = _rms(acc_scr[...] + part, g_ref[...]).astype(BF16)


def _outproj_meta(mix_m, w_out, xm, gain):
    half = D_MODEL // 2
    return pl.pallas_call(
        _outproj_meta_kernel,
        grid=(2,),
        in_specs=[
            pl.BlockSpec((CHUNK, half), lambda k: (0, k)),
            pl.BlockSpec((half, D_MODEL), lambda k: (k, 0)),
            pl.BlockSpec((CHUNK, D_MODEL), lambda k: (0, 0)),
            pl.BlockSpec((1, D_MODEL), lambda k: (0, 0)),
        ],
        out_specs=[
            pl.BlockSpec((CHUNK, D_MODEL), lambda k: (0, 0)),
            pl.BlockSpec((half, D_MODEL), lambda k: (k, 0)),
        ],
        out_shape=[
            jax.ShapeDtypeStruct((CHUNK, D_MODEL), BF16),
            jax.ShapeDtypeStruct((D_MODEL, D_MODEL), BF16),
        ],
        scratch_shapes=[pltpu.VMEM((CHUNK, D_MODEL), F32)],
        compiler_params=_params(
            ("arbitrary",),
            pipelined=[((half, D_MODEL), F32), ((half, D_MODEL), BF16), ((CHUNK, 2 * D_MODEL), F32)],
            resident=[((CHUNK, D_MODEL), F32)],
            temps=[((CHUNK, D_MODEL), F32), ((CHUNK, D_MODEL), F32)]),
        name="outproj_meta",
    )(mix_m, w_out, xm, gain)


def _up_kernel(c_ref, cm_ref, wg_ref, wv_ref, cwg_ref, cwv_ref, cbg_ref, cbv_ref, wd_ref,
               o_ref, wdq_ref, wg_s, wv_s, ug_s, uv_s, umg_s, umv_s, *, tm, sub, tiles_per_batch):
    i = pl.program_id(1)
    wdq_ref[...] = wd_ref[...].astype(BF16)

    @pl.when(i == 0)
    def _():
        wg_s[...] = wg_ref[...].astype(BF16)
        wv_s[...] = wv_ref[...].astype(BF16)
        cm = cm_ref[...]
        umg_s[...] = jnp.dot(cm, wg_s[...], preferred_element_type=F32)[8:16, :]
        umv_s[...] = jnp.dot(cm, wv_s[...], preferred_element_type=F32)[8:16, :]

    @pl.when(i % tiles_per_batch == 0)
    def _():
        ug_s[0:8, :] = umg_s[...]
        uv_s[0:8, :] = umv_s[...]

    def conv(c, rs, w_s, u_s, cw_ref, cb_ref):
        u_s[8 + rs.start:8 + rs.stop, :] = jnp.dot(c, w_s[...], preferred_element_type=F32)
        ext = u_s[rs.start:8 + rs.stop, :]
        n = rs.stop - rs.start
        return (cb_ref[...]
                + cw_ref[0:1, :] * pltpu.roll(ext, 2, 0)[8:8 + n, :]
                + cw_ref[1:2, :] * pltpu.roll(ext, 1, 0)[8:8 + n, :]
                + cw_ref[2:3, :] * ext[8:8 + n, :])

    for rs in _row_tiles(tm, sub):
        c = c_ref[rs, :]
        gate = conv(c, rs, wg_s, ug_s, cwg_ref, cbg_ref)
        val = conv(c, rs, wv_s, uv_s, cwv_ref, cbv_ref)
        o_ref[rs, :] = ((gate / (1.0 + jnp.exp(-gate))) * val).astype(BF16)
    ug_s[0:8, :] = ug_s[tm:tm + 8, :]
    uv_s[0:8, :] = uv_s[tm:tm + 8, :]


def _upconv(c, c_meta, w_up, conv_w, conv_b, w_down, *, rows_per_batch, tm, tn, sub):
    m = c.shape[0]
    nj = D_FF // tn
    ni = m // tm
    wd_rows = D_FF // (nj * ni)
    assert wd_rows * nj * ni == D_FF and wd_rows % 16 == 0
    tiles_per_batch = rows_per_batch // tm
    halo_blk = c_meta.shape[0] // 16 - 1
    return pl.pallas_call(
        functools.partial(_up_kernel, tm=tm, sub=sub, tiles_per_batch=tiles_per_batch),
        grid=(nj, m // tm),
        in_specs=[
            pl.BlockSpec((tm, D_MODEL), lambda j, i: (i, 0)),
            pl.BlockSpec((16, D_MODEL), lambda j, i: (halo_blk, 0)),
            pl.BlockSpec((D_MODEL, tn), lambda j, i: (0, j)),
            pl.BlockSpec((D_MODEL, tn), lambda j, i: (0, j + nj)),
            pl.BlockSpec((3, tn), lambda j, i: (0, j)),
            pl.BlockSpec((3, tn), lambda j, i: (0, j + nj)),
            pl.BlockSpec((1, tn), lambda j, i: (0, j)),
            pl.BlockSpec((1, tn), lambda j, i: (0, j + nj)),
            pl.BlockSpec((wd_rows, D_MODEL), lambda j, i: (j * ni + i, 0)),
        ],
        out_specs=[
            pl.BlockSpec((tm, tn), lambda j, i: (i, j)),
            pl.BlockSpec((wd_rows, D_MODEL), lambda j, i: (j * ni + i, 0)),
        ],
        out_shape=[
            jax.ShapeDtypeStruct((m, D_FF), BF16),
            jax.ShapeDtypeStruct((D_FF, D_MODEL), BF16),
        ],
        scratch_shapes=[
            pltpu.VMEM((D_MODEL, tn), BF16),
            pltpu.VMEM((D_MODEL, tn), BF16),
            pltpu.VMEM((tm + 8, tn), F32),
            pltpu.VMEM((tm + 8, tn), F32),
            pltpu.VMEM((8, tn), F32),
            pltpu.VMEM((8, tn), F32),
        ],
        compiler_params=_params(
            ("arbitrary", "arbitrary"),
            pipelined=[((tm, D_MODEL), BF16), ((D_MODEL, 2 * tn), F32), ((tm, tn), BF16),
                       ((wd_rows, D_MODEL), F32), ((wd_rows, D_MODEL), BF16)],
            resident=[((D_MODEL, 2 * tn), BF16), ((2 * (tm + 8), tn), F32)],
            temps=[((sub, tn), F32), ((sub, tn), F32)]),
        name="upconv",
    )(c, c_meta, w_up, w_up, conv_w, conv_w, conv_b, conv_b, w_down)


def _down_kernel(a_ref, w_ref, h_ref, g_ref, o_ref, *, tm, sub):
    for rs in _row_tiles(tm, sub):
        hcur = h_ref[rs, :] + jnp.dot(a_ref[rs, :], w_ref[...], preferred_element_type=F32)
        o_ref[rs, :] = _rms(hcur, g_ref[...])


def _down(act, w_down_b, h1, gain, *, tm, sub):
    m = act.shape[0]
    return pl.pallas_call(
        functools.partial(_down_kernel, tm=tm, sub=sub),
        grid=(m // tm,),
        in_specs=[
            pl.BlockSpec((tm, D_FF), lambda i: (i, 0)),
            pl.BlockSpec((D_FF, D_MODEL), lambda i: (0, 0), pipeline_mode=pl.Buffered(1)),
            pl.BlockSpec((tm, D_MODEL), lambda i: (i, 0)),
            pl.BlockSpec((1, D_MODEL), lambda i: (0, 0)),
        ],
        out_specs=pl.BlockSpec((tm, D_MODEL), lambda i: (i, 0)),
        out_shape=jax.ShapeDtypeStruct((m, D_MODEL), F32),
        compiler_params=_params(
            ("arbitrary",),
            pipelined=[((tm, D_FF), BF16), ((tm, D_MODEL), F32), ((tm, D_MODEL), F32)],
            resident=[((D_FF, D_MODEL), BF16)],
            temps=[((sub, D_MODEL), F32), ((sub, D_MODEL), F32)]),
        name="downproj",
    )(act, w_down_b, h1, gain)


def _rotary_tables(pos):
    inv_freq = 1.0 / (ROPE_BASE ** (jnp.arange(0, HEAD_DIM, 2, dtype=F32) / HEAD_DIM))
    ang = pos[:, None] * inv_freq[None, :]
    cos, sin = jnp.cos(ang), jnp.sin(ang)
    return jnp.concatenate([cos, cos], axis=1), jnp.concatenate([-sin, sin], axis=1)


def kernel(x, meta_tokens, norm1_gain, w_in, b_forget, ret_norm_gain, w_out, norm2_gain,
           w_up, conv_w, conv_b, w_down, final_norm_gain):
    n_batch, seq, d_model = x.shape
    assert d_model == D_MODEL and seq % CHUNK == 0 and w_in.shape[0] == 1
    assert meta_tokens.shape == (N_META, D_MODEL)
    t = _Tiles()
    x2d = x.reshape(n_batch * seq, D_MODEL)
    xm = jnp.concatenate([jnp.zeros((META_PAD, D_MODEL), F32), meta_tokens.astype(F32)], axis=0)

    w_in_t = w_in[0].T
    b_row = jnp.pad(b_forget[0], (0, LANES - N_HEADS)).reshape(1, LANES)
    log_g = jnp.log1p(-jnp.exp2(-5.0 - jnp.arange(N_HEADS, dtype=F32)))
    g1 = norm1_gain[0].reshape(1, D_MODEL)
    g2 = norm2_gain[0].reshape(1, D_MODEL)
    gf = final_norm_gain.reshape(1, D_MODEL)
    gr = ret_norm_gain[0].reshape(1, RET_DIM)

    cos_r, sin_r = _rotary_tables(N_META + jnp.arange(seq, dtype=F32))
    cos_m, sin_m = _rotary_tables(jnp.maximum(jnp.arange(CHUNK, dtype=F32) - META_PAD, 0.0))

    proj_m, ff_m, w_main_b, w_f = _inproj_meta(xm, g1, w_in_t, cos_m, sin_m,
                                               tn=t.inproj_first_cols)
    cum_m = _cum(ff_m, b_row, rows_per_batch=CHUNK, valid_from=META_PAD, rel_last=True)
    ckm_rows = cum_m[:, :N_HEADS].T.reshape(N_HEADS, 1, CHUNK)
    ret_m = _retention(log_g, proj_m, None, gr, n_batch=1, rows_per_batch=CHUNK, meta_prefix=False,
                       heads=t.mixer_heads)
    fox_m = _fox(proj_m, proj_m, cum_m, None, ckm_rows, n_batch=1, rows_per_batch=CHUNK,
                 tq=CHUNK, meta_only=True, heads=t.mixer_heads)
    c_m, w_out_b = _outproj_meta(jnp.concatenate([ret_m, fox_m], axis=1), w_out[0], xm, g2)

    proj, ff = _inproj_main(x2d, g1, w_main_b, w_f, cos_r, sin_r,
                            tm=t.inproj_rows, tn=t.inproj_cols, sub=t.proj_sub_rows)
    cum = _cum(ff, b_row, rows_per_batch=seq)
    ck_rows = (cum[:, :N_HEADS].reshape(n_batch, seq, N_HEADS).transpose(0, 2, 1)
               .reshape(n_batch * N_HEADS, 1, seq))
    ret = _retention(log_g, proj, proj_m, gr, n_batch=n_batch, rows_per_batch=seq, meta_prefix=True,
                     heads=t.mixer_heads)
    fox = _fox(proj, proj_m, cum, ck_rows, ckm_rows, n_batch=n_batch, rows_per_batch=seq,
               tq=t.fox_q_rows, meta_only=False, heads=t.fox_heads)
    h1, c = _outproj(ret, fox, w_out_b, x2d, g2, tm=t.outproj_rows, sub=t.proj_sub_rows)
    act, w_down_b = _upconv(c, c_m, w_up[0], conv_w[0], conv_b, w_down[0],
                            rows_per_batch=seq, tm=t.up_rows, tn=t.up_cols, sub=t.up_sub_rows)
    out = _down(act, w_down_b, h1, gf, tm=t.down_rows, sub=t.proj_sub_rows)
    return out.reshape(n_batch, seq, D_MODEL)
```

```python
import functools
from typing import NamedTuple

import jax
import jax.numpy as jnp
from jax import lax
from jax.experimental import pallas as pl
from jax.experimental.pallas import tpu as pltpu

F32 = jnp.float32
BF16 = jnp.bfloat16

D_MODEL = 2048
N_META = 16
CHUNK = 128
N_HEADS = 8
HEAD_DIM = 128
RET_DIM = N_HEADS * HEAD_DIM
FOX_DIM = N_HEADS * HEAD_DIM
MAIN_COLS = 4 * RET_DIM + 3 * FOX_DIM
D_FF = 5632
ROPE_BASE = 10000.0
NORM_EPS = 1e-6
META_PAD = CHUNK - N_META
MASKED = -1e30
LANES = 128
MIB = 1024 * 1024
LOG2E = 1.4426950408889634
FOXQ_SCALE = HEAD_DIM ** -0.5 * LOG2E

(SEC_RQ, SEC_RK, SEC_RV, SEC_RG, SEC_FQ, SEC_FK, SEC_FV) = range(7)
BLK = {s: s * N_HEADS for s in range(7)}


V7X_VMEM_BYTES = 64 * MIB
V7X_VMEM_UNSCOPED = 6 * MIB


class _Tiles(NamedTuple):
    inproj_rows: int = 1024
    inproj_first_cols: int = 1024
    inproj_cols: int = 1024
    proj_sub_rows: int = 256
    fox_q_rows: int = 256
    mixer_heads: int = 2
    fox_heads: int = 4
    outproj_rows: int = 512
    up_rows: int = 2048
    up_sub_rows: int = 1024
    up_cols: int = 512
    down_rows: int = 512


def _nbytes(shape, dtype):
    n = jnp.dtype(dtype).itemsize
    for d in shape:
        n *= d
    return n


def _params(sem, pipelined, resident=(), temps=()):
    need = (2 * sum(_nbytes(*b) for b in pipelined) + sum(_nbytes(*b) for b in resident)
            + sum(_nbytes(*b) for b in temps))
    limit = -(-need // MIB) * MIB
    assert limit <= V7X_VMEM_BYTES - V7X_VMEM_UNSCOPED, (limit, sem)
    return pltpu.CompilerParams(dimension_semantics=sem, vmem_limit_bytes=limit)


def _rms(x, gain):
    ms = jnp.mean(x * x, axis=-1, keepdims=True)
    return (x * lax.rsqrt(ms + NORM_EPS)) * gain


def _silu(x):
    h = 0.5 * x
    return h + h * jnp.tanh(h)


def _row_tiles(tm, sub):
    return [slice(r * sub, (r + 1) * sub) for r in range(tm // sub)]


class _Rows(NamedTuple):
    x_ref: object
    cos_ref: object
    sin_ref: object
    o_ref: object
    ff_ref: object
    a_scr: object
    tiles: list


def _inproj_body(j, tn, groups, g_ref, w_bf, wf_ref):
    sec = j // (RET_DIM // tn)

    def rotary_store(grp, rs, acc, scale):
        cos = grp.cos_ref[rs, :]
        sin = grp.sin_ref[rs, :]
        for hh in range(tn // LANES):
            cols = slice(hh * LANES, (hh + 1) * LANES)
            t = acc[:, cols]
            y = t * cos + pltpu.roll(t, HEAD_DIM // 2, 1) * sin
            if scale is not None:
                y = y * scale
            grp.o_ref[rs, cols] = y.astype(BF16)

    @pl.when(j == 0)
    def _():
        for grp in groups:
            for rs in grp.tiles:
                a = _rms(grp.x_ref[rs, :], g_ref[...]).astype(BF16)
                grp.a_scr[rs, :] = a
                grp.ff_ref[rs, :] = lax.dot_general(a, wf_ref[...], (((1,), (1,)), ((), ())),
                                                    preferred_element_type=F32)
                rotary_store(grp, rs, jnp.dot(a, w_bf[...], preferred_element_type=F32), None)

    @pl.when((j > 0) & (sec <= SEC_RK))
    def _():
        scale = jnp.where(sec == SEC_RK, HEAD_DIM ** -0.5, 1.0).astype(F32)
        for grp in groups:
            for rs in grp.tiles:
                acc = jnp.dot(grp.a_scr[rs, :], w_bf[...], preferred_element_type=F32)
                rotary_store(grp, rs, acc, scale)

    @pl.when(sec > SEC_RK)
    def _():
        scale = jnp.where(sec == SEC_FQ, FOXQ_SCALE, 1.0).astype(F32)
        for grp in groups:
            for rs in grp.tiles:
                acc = jnp.dot(grp.a_scr[rs, :], w_bf[...], preferred_element_type=F32)
                grp.o_ref[rs, :] = (acc * scale).astype(BF16)


def _inproj_meta_kernel(xm_ref, g_ref, w_ref, wf8_ref, cosm_ref, sinm_ref,
                        om_ref, ffm_ref, wq_ref, wf_ref, am_scr, *, tn):
    j = pl.program_id(0)
    wq_ref[...] = w_ref[...].T.astype(BF16)

    @pl.when(j == 0)
    def _():
        pad = jnp.zeros((LANES - N_HEADS, D_MODEL), F32)
        wf_ref[...] = jnp.concatenate([wf8_ref[...], pad], axis=0).astype(BF16)

    groups = [_Rows(xm_ref, cosm_ref, sinm_ref, om_ref, ffm_ref, am_scr, _row_tiles(CHUNK, CHUNK))]
    _inproj_body(j, tn, groups, g_ref, wq_ref, wf_ref)


def _inproj_main_kernel(x_ref, g_ref, w_ref, wf_ref, cos_ref, sin_ref, o_ref, ff_ref, a_scr,
                        *, tm, tn, sub):
    groups = [_Rows(x_ref, cos_ref, sin_ref, o_ref, ff_ref, a_scr, _row_tiles(tm, sub))]
    _inproj_body(pl.program_id(1), tn, groups, g_ref, w_ref, wf_ref)


def _inproj_meta(xm, gain, w_in_t, cos_m, sin_m, *, tn):
    const = lambda shape: pl.BlockSpec(shape, lambda j: (0, 0))
    return pl.pallas_call(
        functools.partial(_inproj_meta_kernel, tn=tn),
        grid=(MAIN_COLS // tn,),
        in_specs=[
            const((CHUNK, D_MODEL)),
            const((1, D_MODEL)),
            pl.BlockSpec((tn, D_MODEL), lambda j: (j, 0)),
            pl.BlockSpec((N_HEADS, D_MODEL), lambda j: (MAIN_COLS // N_HEADS, 0)),
            const((CHUNK, LANES)),
            const((CHUNK, LANES)),
        ],
        out_specs=[
            pl.BlockSpec((CHUNK, tn), lambda j: (0, j)),
            const((CHUNK, LANES)),
            pl.BlockSpec((D_MODEL, tn), lambda j: (0, j)),
            const((LANES, D_MODEL)),
        ],
        out_shape=[
            jax.ShapeDtypeStruct((CHUNK, MAIN_COLS), BF16),
            jax.ShapeDtypeStruct((CHUNK, LANES), F32),
            jax.ShapeDtypeStruct((D_MODEL, MAIN_COLS), BF16),
            jax.ShapeDtypeStruct((LANES, D_MODEL), BF16),
        ],
        scratch_shapes=[pltpu.VMEM((CHUNK, D_MODEL), BF16)],
        compiler_params=_params(
            ("arbitrary",),
            pipelined=[((tn, D_MODEL), F32), ((D_MODEL, tn), BF16), ((CHUNK, D_MODEL), F32),
                       ((CHUNK, tn), BF16), ((3 * CHUNK, LANES), F32), ((LANES, D_MODEL), BF16)],
            resident=[((CHUNK, D_MODEL), BF16)],
            temps=[((tn, D_MODEL), F32), ((CHUNK, D_MODEL), F32), ((CHUNK, tn), F32)]),
        name="inproj_meta",
    )(xm, gain, w_in_t, w_in_t, cos_m, sin_m)


def _inproj_main(x2d, gain, w_main_b, w_f, cos_t, sin_t, *, tm, tn, sub):
    m = x2d.shape[0]
    nb = cos_t.shape[0] // tm
    return pl.pallas_call(
        functools.partial(_inproj_main_kernel, tm=tm, tn=tn, sub=sub),
        grid=(m // tm, MAIN_COLS // tn),
        in_specs=[
            pl.BlockSpec((tm, D_MODEL), lambda i, j: (i, 0)),
            pl.BlockSpec((1, D_MODEL), lambda i, j: (0, 0)),
            pl.BlockSpec((D_MODEL, tn), lambda i, j: (0, j)),
            pl.BlockSpec((LANES, D_MODEL), lambda i, j: (0, 0)),
            pl.BlockSpec((tm, LANES), lambda i, j: (i % nb, 0)),
            pl.BlockSpec((tm, LANES), lambda i, j: (i % nb, 0)),
        ],
        out_specs=[
            pl.BlockSpec((tm, tn), lambda i, j: (i, j)),
            pl.BlockSpec((tm, LANES), lambda i, j: (i, 0)),
        ],
        out_shape=[
            jax.ShapeDtypeStruct((m, MAIN_COLS), BF16),
            jax.ShapeDtypeStruct((m, LANES), F32),
        ],
        scratch_shapes=[pltpu.VMEM((tm, D_MODEL), BF16)],
        compiler_params=_params(
            ("arbitrary", "arbitrary"),
            pipelined=[((tm, D_MODEL), F32), ((D_MODEL, tn), BF16), ((tm, tn), BF16),
                       ((3 * tm, LANES), F32), ((LANES, D_MODEL), BF16)],
            resident=[((tm, D_MODEL), BF16)],
            temps=[((sub, D_MODEL), F32), ((sub, tn), F32), ((sub, tn), F32)]),
        name="inproj_main",
    )(x2d, gain, w_main_b, w_f, cos_t, sin_t)


def _cum_kernel(ff_ref, b_ref, o_ref, *, n_blk, valid_from, rel_last):
    row = lax.broadcasted_iota(jnp.int32, (CHUNK, CHUNK), 0)
    col = lax.broadcasted_iota(jnp.int32, (CHUNK, CHUNK), 1)
    tri = (row >= col).astype(BF16)
    rows = lax.broadcasted_iota(jnp.int32, (CHUNK, LANES), 0)
    carry = jnp.zeros((1, LANES), F32)
    for blk in range(n_blk):
        z = ff_ref[blk * CHUNK:(blk + 1) * CHUNK, :] + b_ref[...]
        lf = jnp.minimum(z, 0.0) - jnp.log1p(jnp.exp(-jnp.abs(z)))
        if valid_from:
            lf = jnp.where(rows >= valid_from, lf, 0.0)
        hi = lf.astype(BF16)
        r1 = lf - hi.astype(F32)
        mid = r1.astype(BF16)
        lo = (r1 - mid.astype(F32)).astype(BF16)
        cum = (jnp.dot(tri, hi, preferred_element_type=F32)
               + jnp.dot(tri, mid, preferred_element_type=F32)
               + jnp.dot(tri, lo, preferred_element_type=F32)) + carry
        o_ref[blk * CHUNK:(blk + 1) * CHUNK, :] = cum
        carry = cum[CHUNK - 1:CHUNK, :]
    if rel_last:
        o_ref[...] = o_ref[...] - carry


def _cum(ff, b_row, *, rows_per_batch, valid_from=0, rel_last=False):
    m = ff.shape[0]
    return pl.pallas_call(
        functools.partial(_cum_kernel, n_blk=rows_per_batch // CHUNK,
                          valid_from=valid_from, rel_last=rel_last),
        grid=(m // rows_per_batch,),
        in_specs=[
            pl.BlockSpec((rows_per_batch, LANES), lambda b: (b, 0)),
            pl.BlockSpec((1, LANES), lambda b: (0, 0)),
        ],
        out_specs=pl.BlockSpec((rows_per_batch, LANES), lambda b: (b, 0)),
        out_shape=jax.ShapeDtypeStruct((m, LANES), F32),
        compiler_params=_params(("arbitrary",), pipelined=[((2 * rows_per_batch, LANES), F32)],
                                temps=[((rows_per_batch, LANES), F32)]),
        name="cumgate",
    )(ff, b_row)


def _head_view(ref, hh):
    return ref.at[:, pl.ds(hh * HEAD_DIM, HEAD_DIM)]


def _ret_kernel(logg_ref, *refs, n_chunks, meta_prefix, heads):
    *io_refs, state_scr = refs
    for hh in range(heads):
        _ret_head(logg_ref[pl.program_id(1) * heads + hh], *[_head_view(r, hh) for r in io_refs],
                  state_scr.at[hh], n_chunks=n_chunks, meta_prefix=meta_prefix)


def _ret_head(lg, *refs, n_chunks, meta_prefix):
    if meta_prefix:
        q_ref, k_ref, v_ref, g_ref, km_ref, vm_ref, gain_ref, o_ref, state_scr = refs
    else:
        q_ref, k_ref, v_ref, g_ref, gain_ref, o_ref, state_scr = refs
    ri = lax.broadcasted_iota(jnp.int32, (CHUNK, CHUNK), 0)
    ci = lax.broadcasted_iota(jnp.int32, (CHUNK, CHUNK), 1)
    diff = (ri - ci).astype(F32)
    dmat = jnp.where(diff >= 0, jnp.exp(jnp.maximum(diff, 0.0) * lg), 0.0)
    pos = lax.broadcasted_iota(jnp.int32, (CHUNK, 1), 0).astype(F32)
    xi = jnp.exp((pos + 1.0) * lg)
    zeta = jnp.exp((CHUNK - 1.0 - pos) * lg)
    g_chunk = jnp.exp(jnp.full((1, 1), float(CHUNK), F32) * lg)

    def advance(state, kc, vc):
        kz = (kc.astype(F32) * zeta).astype(BF16)
        return g_chunk * state + lax.dot_general(
            kz, vc, (((0,), (0,)), ((), ())), preferred_element_type=F32)

    state = jnp.zeros((HEAD_DIM, HEAD_DIM), F32)
    if meta_prefix:
        state = advance(state, km_ref[...], vm_ref[...])
    for c in range(n_chunks):
        state_scr[c] = state.astype(BF16)
        if c + 1 < n_chunks:
            rows = slice(c * CHUNK, (c + 1) * CHUNK)
            state = advance(state, k_ref[rows, :], v_ref[rows, :])

    gain = gain_ref[...]

    for c in range(n_chunks):
        rows = slice(c * CHUNK, (c + 1) * CHUNK)
        qc = q_ref[rows, :]
        kc = k_ref[rows, :]
        vc = v_ref[rows, :]
        s = lax.dot_general(qc, kc, (((1,), (1,)), ((), ())), preferred_element_type=F32) * dmat
        o = (jnp.dot(s.astype(BF16), vc, preferred_element_type=F32)
             + xi * jnp.dot(qc, state_scr[c], preferred_element_type=F32))
        mu = jnp.mean(o, axis=-1, keepdims=True)
        d = o - mu
        var = jnp.mean(d * d, axis=-1, keepdims=True)
        y = (d * lax.rsqrt(var + NORM_EPS)) * gain
        gt = g_ref[rows, :].astype(F32)
        o_ref[rows, :] = (_silu(gt) * y).astype(BF16)


def _retention(log_g, proj, proj_meta, gain, *, n_batch, rows_per_batch, meta_prefix, heads):
    width = heads * HEAD_DIM
    blk = lambda sec: pl.BlockSpec((rows_per_batch, width),
                                   lambda b, h, s=BLK[sec] // heads: (b, s + h))
    in_specs = [pl.BlockSpec(memory_space=pltpu.SMEM),
                blk(SEC_RQ), blk(SEC_RK), blk(SEC_RV), blk(SEC_RG)]
    args = [log_g, proj, proj, proj, proj]
    if meta_prefix:
        mblk = lambda sec: pl.BlockSpec((CHUNK, width), lambda b, h, s=BLK[sec] // heads: (0, s + h))
        in_specs += [mblk(SEC_RK), mblk(SEC_RV)]
        args += [proj_meta, proj_meta]
    in_specs.append(pl.BlockSpec((1, width), lambda b, h: (0, h)))
    args.append(gain)
    return pl.pallas_call(
        functools.partial(_ret_kernel, n_chunks=rows_per_batch // CHUNK, meta_prefix=meta_prefix,
                          heads=heads),
        grid=(n_batch, N_HEADS // heads),
        in_specs=in_specs,
        out_specs=pl.BlockSpec((rows_per_batch, width), lambda b, h: (b, h)),
        out_shape=jax.ShapeDtypeStruct((n_batch * rows_per_batch, RET_DIM), BF16),
        scratch_shapes=[pltpu.VMEM((heads, rows_per_batch // CHUNK, HEAD_DIM, HEAD_DIM), BF16)],
        compiler_params=_params(
            ("arbitrary", "arbitrary"),
            pipelined=[((5 * rows_per_batch + 2 * CHUNK, width), BF16)],
            resident=[((rows_per_batch, width), BF16)],
            temps=[((rows_per_batch, HEAD_DIM), F32)]),
        name="retention",
    )(*args)


def _fox_kernel(*refs, n_q, tq, meta_only, heads):
    for hh in range(heads):
        _fox_head(pl.program_id(1) * heads + hh, hh, *refs, n_q=n_q, tq=tq, meta_only=meta_only)


def _fox_head(h, hh, *refs, n_q, tq, meta_only):
    if meta_only:
        q_ref, km_ref, vm_ref, cq_ref, ckm_ref, o_ref, k_all, v_all = refs
        ck_all = ckm_ref[hh] * LOG2E
    else:
        q_ref, k_ref, v_ref, km_ref, vm_ref, cq_ref, ck_ref, ckm_ref, o_ref, k_all, v_all = refs
        k_ref, v_ref = _head_view(k_ref, hh), _head_view(v_ref, hh)
        k_all[CHUNK:, :] = k_ref[...]
        v_all[CHUNK:, :] = v_ref[...]
        ck_all = jnp.concatenate([ckm_ref[hh], ck_ref[hh]], axis=1) * LOG2E
    q_ref, o_ref = _head_view(q_ref, hh), _head_view(o_ref, hh)
    km_ref, vm_ref = _head_view(km_ref, hh), _head_view(vm_ref, hh)
    k_all[0:CHUNK, :] = km_ref[...]
    v_all[0:CHUNK, :] = vm_ref[...]
    lane = lax.broadcasted_iota(jnp.int32, (1, LANES), 1)
    col_m = lax.broadcasted_iota(jnp.int32, (tq, CHUNK), 1)
    row_m = lax.broadcasted_iota(jnp.int32, (tq, CHUNK), 0)
    mask_m = col_m >= META_PAD
    if meta_only:
        mask_m = mask_m & (row_m >= col_m)
    tri = (lax.broadcasted_iota(jnp.int32, (tq, tq), 0)
           >= lax.broadcasted_iota(jnp.int32, (tq, tq), 1))

    def n_keys(qi):
        return CHUNK if meta_only else CHUNK + (qi + 1) * tq

    def logit_pass(qi):
        n = n_keys(qi)
        q = q_ref[qi * tq:(qi + 1) * tq, :]
        t = lax.dot_general(q, k_all[0:n, :], (((1,), (1,)), ((), ())),
                            preferred_element_type=F32) - ck_all[:, 0:n]
        parts = [jnp.where(mask_m, t[:, 0:CHUNK], MASKED)]
        if not meta_only:
            if qi > 0:
                parts.append(t[:, CHUNK:n - tq])
            parts.append(jnp.where(tri, t[:, n - tq:n], MASKED))
        t = parts[0] if len(parts) == 1 else jnp.concatenate(parts, axis=1)
        return t, jnp.max(t, axis=1, keepdims=True)

    def exp_pass(qi, t, mx):
        rows = slice(qi * tq, (qi + 1) * tq)
        cq2 = LOG2E * jnp.sum(jnp.where(lane == h, cq_ref[rows, :], 0.0), axis=1, keepdims=True)
        m_row = mx + cq2
        p = jnp.exp2(t - (m_row - cq2))
        l = jnp.sum(p, axis=1, keepdims=True)
        acc = jnp.dot(p.astype(BF16), v_all[0:n_keys(qi), :], preferred_element_type=F32)
        o_ref[rows, :] = (acc / l).astype(BF16)

    cur = logit_pass(0)
    for qi in range(n_q):
        nxt = logit_pass(qi + 1) if qi + 1 < n_q else None
        exp_pass(qi, *cur)
        cur = nxt


def _fox(proj, proj_meta, cum_col, ck_rows, ckm_rows, *, n_batch, rows_per_batch, tq, meta_only,
         heads):
    n_q = rows_per_batch // tq
    n_all = CHUNK if meta_only else CHUNK + rows_per_batch
    width = heads * HEAD_DIM
    groups = N_HEADS // heads
    qblk = lambda sec: pl.BlockSpec((rows_per_batch, width),
                                    lambda b, h, s=BLK[sec] // heads: (b, s + h))
    mblk = lambda sec: pl.BlockSpec((CHUNK, width), lambda b, h, s=BLK[sec] // heads: (0, s + h))
    cq_spec = pl.BlockSpec((rows_per_batch, LANES), lambda b, h: (b, 0))
    ckm_spec = pl.BlockSpec((heads, 1, CHUNK), lambda b, h: (h, 0, 0))
    if meta_only:
        in_specs = [qblk(SEC_FQ), mblk(SEC_FK), mblk(SEC_FV), cq_spec, ckm_spec]
        args = [proj, proj_meta, proj_meta, cum_col, ckm_rows]
    else:
        ck_spec = pl.BlockSpec((heads, 1, rows_per_batch), lambda b, h: (b * groups + h, 0, 0))
        in_specs = [qblk(SEC_FQ), qblk(SEC_FK), qblk(SEC_FV), mblk(SEC_FK), mblk(SEC_FV),
                    cq_spec, ck_spec, ckm_spec]
        args = [proj, proj, proj, proj_meta, proj_meta, cum_col, ck_rows, ckm_rows]
    return pl.pallas_call(
        functools.partial(_fox_kernel, n_q=n_q, tq=tq, meta_only=meta_only, heads=heads),
        grid=(n_batch, groups),
        in_specs=in_specs,
        out_specs=pl.BlockSpec((rows_per_batch, width), lambda b, h: (b, h)),
        out_shape=jax.ShapeDtypeStruct((n_batch * rows_per_batch, FOX_DIM), BF16),
        scratch_shapes=[pltpu.VMEM((n_all, HEAD_DIM), BF16)] * 2,
        compiler_params=_params(
            ("arbitrary", "arbitrary"),
            pipelined=[((4 * rows_per_batch + 2 * CHUNK, width), BF16),
                       ((rows_per_batch, LANES), F32), ((8 * heads, n_all), F32)],
            resident=[((2 * n_all, HEAD_DIM), BF16)],
            temps=[((4 * tq * heads, n_all), F32)]),
        name="foxattn",
    )(*args)


def _outproj_kernel(ret_ref, fox_ref, w1_ref, w2_ref, x_ref, g_ref, h_ref, c_ref, *, tm, sub):
    for rs in _row_tiles(tm, sub):
        hcur = (x_ref[rs, :]
                + jnp.dot(ret_ref[rs, :], w1_ref[...], preferred_element_type=F32)
                + jnp.dot(fox_ref[rs, :], w2_ref[...], preferred_element_type=F32))
        h_ref[rs, :] = hcur
        c_ref[rs, :] = _rms(hcur, g_ref[...]).astype(BF16)


def _outproj(ret, fox, w_out_b, x2d, gain, *, tm, sub):
    m = x2d.shape[0]
    return pl.pallas_call(
        functools.partial(_outproj_kernel, tm=tm, sub=sub),
        grid=(m // tm,),
        in_specs=[
            pl.BlockSpec((tm, RET_DIM), lambda i: (i, 0)),
            pl.BlockSpec((tm, FOX_DIM), lambda i: (i, 0)),
            pl.BlockSpec((RET_DIM, D_MODEL), lambda i: (0, 0)),
            pl.BlockSpec((FOX_DIM, D_MODEL), lambda i: (1, 0)),
            pl.BlockSpec((tm, D_MODEL), lambda i: (i, 0)),
            pl.BlockSpec((1, D_MODEL), lambda i: (0, 0)),
        ],
        out_specs=[
            pl.BlockSpec((tm, D_MODEL), lambda i: (i, 0)),
            pl.BlockSpec((tm, D_MODEL), lambda i: (i, 0)),
        ],
        out_shape=[
            jax.ShapeDtypeStruct((m, D_MODEL), F32),
            jax.ShapeDtypeStruct((m, D_MODEL), BF16),
        ],
        compiler_params=_params(
            ("arbitrary",),
            pipelined=[((tm, RET_DIM + FOX_DIM), BF16), ((tm, D_MODEL), F32), ((tm, D_MODEL), F32),
                       ((tm, D_MODEL), BF16)],
            resident=[((RET_DIM + FOX_DIM, D_MODEL), BF16)],
            temps=[((sub, D_MODEL), F32), ((sub, D_MODEL), F32)]),
        name="outproj",
    )(ret, fox, w_out_b, w_out_b, x2d, gain)


def _outproj_meta_kernel(mix_ref, w_ref, x_ref, g_ref, c_ref, wq_ref, acc_scr):
    k = pl.program_id(0)
    wq_ref[...] = w_ref[...].astype(BF16)
    part = jnp.dot(mix_ref[...], wq_ref[...], preferred_element_type=F32)

    @pl.when(k == 0)
    def _():
        acc_scr[...] = x_ref[...] + part

    @pl.when(k == 1)
    def _():
        c_ref[...] = _rms(acc_scr[...] + part, g_ref[...]).astype(BF16)


def _outproj_meta(mix_m, w_out, xm, gain):
    half = D_MODEL // 2
    return pl.pallas_call(
        _outproj_meta_kernel,
        grid=(2,),
        in_specs=[
            pl.BlockSpec((CHUNK, half), lambda k: (0, k)),
            pl.BlockSpec((half, D_MODEL), lambda k: (k, 0)),
            pl.BlockSpec((CHUNK, D_MODEL), lambda k: (0, 0)),
            pl.BlockSpec((1, D_MODEL), lambda k: (0, 0)),
        ],
        out_specs=[
            pl.BlockSpec((CHUNK, D_MODEL), lambda k: (0, 0)),
            pl.BlockSpec((half, D_MODEL), lambda k: (k, 0)),
        ],
        out_shape=[
            jax.ShapeDtypeStruct((CHUNK, D_MODEL), BF16),
            jax.ShapeDtypeStruct((D_MODEL, D_MODEL), BF16),
        ],
        scratch_shapes=[pltpu.VMEM((CHUNK, D_MODEL), F32)],
        compiler_params=_params(
            ("arbitrary",),
            pipelined=[((half, D_MODEL), F32), ((half, D_MODEL), BF16), ((CHUNK, 2 * D_MODEL), F32)],
            resident=[((CHUNK, D_MODEL), F32)],
            temps=[((CHUNK, D_MODEL), F32), ((CHUNK, D_MODEL), F32)]),
        name="outproj_meta",
    )(mix_m, w_out, xm, gain)


def _up_kernel(c_ref, cm_ref, wg_ref, wv_ref, cwg_ref, cwv_ref, cbg_ref, cbv_ref, wd_ref,
               o_ref, wdq_ref, wg_s, wv_s, ug_s, uv_s, umg_s, umv_s, *, tm, sub, tiles_per_batch):
    i = pl.program_id(1)
    wdq_ref[...] = wd_ref[...].astype(BF16)

    @pl.when(i == 0)
    def _():
        wg_s[...] = wg_ref[...].astype(BF16)
        wv_s[...] = wv_ref[...].astype(BF16)
        cm = cm_ref[...]
        umg_s[...] = jnp.dot(cm, wg_s[...], preferred_element_type=F32)[8:16, :]
        umv_s[...] = jnp.dot(cm, wv_s[...], preferred_element_type=F32)[8:16, :]

    @pl.when(i % tiles_per_batch == 0)
    def _():
        ug_s[0:8, :] = umg_s[...]
        uv_s[0:8, :] = umv_s[...]

    def conv(c, rs, w_s, u_s, cw_ref, cb_ref):
        u_s[8 + rs.start:8 + rs.stop, :] = jnp.dot(c, w_s[...], preferred_element_type=F32)
        ext = u_s[rs.start:8 + rs.stop, :]
        n = rs.stop - rs.start
        return (cb_ref[...]
                + cw_ref[0:1, :] * pltpu.roll(ext, 2, 0)[8:8 + n, :]
                + cw_ref[1:2, :] * pltpu.roll(ext, 1, 0)[8:8 + n, :]
                + cw_ref[2:3, :] * ext[8:8 + n, :])

    for rs in _row_tiles(tm, sub):
        c = c_ref[rs, :]
        gate = conv(c, rs, wg_s, ug_s, cwg_ref, cbg_ref)
        val = conv(c, rs, wv_s, uv_s, cwv_ref, cbv_ref)
        o_ref[rs, :] = ((gate / (1.0 + jnp.exp(-gate))) * val).astype(BF16)
    ug_s[0:8, :] = ug_s[tm:tm + 8, :]
    uv_s[0:8, :] = uv_s[tm:tm + 8, :]


def _upconv(c, c_meta, w_up, conv_w, conv_b, w_down, *, rows_per_batch, tm, tn, sub):
    m = c.shape[0]
    nj = D_FF // tn
    ni = m // tm
    wd_rows = D_FF // (nj * ni)
    assert wd_rows * nj * ni == D_FF and wd_rows % 16 == 0
    tiles_per_batch = rows_per_batch // tm
    halo_blk = c_meta.shape[0] // 16 - 1
    return pl.pallas_call(
        functools.partial(_up_kernel, tm=tm, sub=sub, tiles_per_batch=tiles_per_batch),
        grid=(nj, m // tm),
        in_specs=[
            pl.BlockSpec((tm, D_MODEL), lambda j, i: (i, 0)),
            pl.BlockSpec((16, D_MODEL), lambda j, i: (halo_blk, 0)),
            pl.BlockSpec((D_MODEL, tn), lambda j, i: (0, j)),
            pl.BlockSpec((D_MODEL, tn), lambda j, i: (0, j + nj)),
            pl.BlockSpec((3, tn), lambda j, i: (0, j)),
            pl.BlockSpec((3, tn), lambda j, i: (0, j + nj)),
            pl.BlockSpec((1, tn), lambda j, i: (0, j)),
            pl.BlockSpec((1, tn), lambda j, i: (0, j + nj)),
            pl.BlockSpec((wd_rows, D_MODEL), lambda j, i: (j * ni + i, 0)),
        ],
        out_specs=[
            pl.BlockSpec((tm, tn), lambda j, i: (i, j)),
            pl.BlockSpec((wd_rows, D_MODEL), lambda j, i: (j * ni + i, 0)),
        ],
        out_shape=[
            jax.ShapeDtypeStruct((m, D_FF), BF16),
            jax.ShapeDtypeStruct((D_FF, D_MODEL), BF16),
        ],
        scratch_shapes=[
            pltpu.VMEM((D_MODEL, tn), BF16),
            pltpu.VMEM((D_MODEL, tn), BF16),
            pltpu.VMEM((tm + 8, tn), F32),
            pltpu.VMEM((tm + 8, tn), F32),
            pltpu.VMEM((8, tn), F32),
            pltpu.VMEM((8, tn), F32),
        ],
        compiler_params=_params(
            ("arbitrary", "arbitrary"),
            pipelined=[((tm, D_MODEL), BF16), ((D_MODEL, 2 * tn), F32), ((tm, tn), BF16),
                       ((wd_rows, D_MODEL), F32), ((wd_rows, D_MODEL), BF16)],
            resident=[((D_MODEL, 2 * tn), BF16), ((2 * (tm + 8), tn), F32)],
            temps=[((sub, tn), F32), ((sub, tn), F32)]),
        name="upconv",
    )(c, c_meta, w_up, w_up, conv_w, conv_w, conv_b, conv_b, w_down)


def _down_kernel(a_ref, w_ref, h_ref, g_ref, o_ref, *, tm, sub):
    for rs in _row_tiles(tm, sub):
        hcur = h_ref[rs, :] + jnp.dot(a_ref[rs, :], w_ref[...], preferred_element_type=F32)
        o_ref[rs, :] = _rms(hcur, g_ref[...])


def _down(act, w_down_b, h1, gain, *, tm, sub):
    m = act.shape[0]
    return pl.pallas_call(
        functools.partial(_down_kernel, tm=tm, sub=sub),
        grid=(m // tm,),
        in_specs=[
            pl.BlockSpec((tm, D_FF), lambda i: (i, 0)),
            pl.BlockSpec((D_FF, D_MODEL), lambda i: (0, 0), pipeline_mode=pl.Buffered(1)),
            pl.BlockSpec((tm, D_MODEL), lambda i: (i, 0)),
            pl.BlockSpec((1, D_MODEL), lambda i: (0, 0)),
        ],
        out_specs=pl.BlockSpec((tm, D_MODEL), lambda i: (i, 0)),
        out_shape=jax.ShapeDtypeStruct((m, D_MODEL), F32),
        compiler_params=_params(
            ("arbitrary",),
            pipelined=[((tm, D_FF), BF16), ((tm, D_MODEL), F32), ((tm, D_MODEL), F32)],
            resident=[((D_FF, D_MODEL), BF16)],
            temps=[((sub, D_MODEL), F32), ((sub, D_MODEL), F32)]),
        name="downproj",
    )(act, w_down_b, h1, gain)


def _rotary_tables(pos):
    inv_freq = 1.0 / (ROPE_BASE ** (jnp.arange(0, HEAD_DIM, 2, dtype=F32) / HEAD_DIM))
    ang = pos[:, None] * inv_freq[None, :]
    cos, sin = jnp.cos(ang), jnp.sin(ang)
    return jnp.concatenate([cos, cos], axis=1), jnp.concatenate([-sin, sin], axis=1)


def kernel(x, meta_tokens, norm1_gain, w_in, b_forget, ret_norm_gain, w_out, norm2_gain,
           w_up, conv_w, conv_b, w_down, final_norm_gain):
    n_batch, seq, d_model = x.shape
    assert d_model == D_MODEL and seq % CHUNK == 0 and w_in.shape[0] == 1
    assert meta_tokens.shape == (N_META, D_MODEL)
    t = _Tiles()
    x2d = x.reshape(n_batch * seq, D_MODEL)
    xm = jnp.concatenate([jnp.zeros((META_PAD, D_MODEL), F32), meta_tokens.astype(F32)], axis=0)

    w_in_t = w_in[0].T
    b_row = jnp.pad(b_forget[0], (0, LANES - N_HEADS)).reshape(1, LANES)
    log_g = jnp.log1p(-jnp.exp2(-5.0 - jnp.arange(N_HEADS, dtype=F32)))
    g1 = norm1_gain[0].reshape(1, D_MODEL)
    g2 = norm2_gain[0].reshape(1, D_MODEL)
    gf = final_norm_gain.reshape(1, D_MODEL)
    gr = ret_norm_gain[0].reshape(1, RET_DIM)

    cos_r, sin_r = _rotary_tables(N_META + jnp.arange(seq, dtype=F32))
    cos_m, sin_m = _rotary_tables(jnp.maximum(jnp.arange(CHUNK, dtype=F32) - META_PAD, 0.0))

    proj_m, ff_m, w_main_b, w_f = _inproj_meta(xm, g1, w_in_t, cos_m, sin_m,
                                               tn=t.inproj_first_cols)
    cum_m = _cum(ff_m, b_row, rows_per_batch=CHUNK, valid_from=META_PAD, rel_last=True)
    ckm_rows = cum_m[:, :N_HEADS].T.reshape(N_HEADS, 1, CHUNK)
    ret_m = _retention(log_g, proj_m, None, gr, n_batch=1, rows_per_batch=CHUNK, meta_prefix=False,
                       heads=t.mixer_heads)
    fox_m = _fox(proj_m, proj_m, cum_m, None, ckm_rows, n_batch=1, rows_per_batch=CHUNK,
                 tq=CHUNK, meta_only=True, heads=t.mixer_heads)
    c_m, w_out_b = _outproj_meta(jnp.concatenate([ret_m, fox_m], axis=1), w_out[0], xm, g2)

    proj, ff = _inproj_main(x2d, g1, w_main_b, w_f, cos_r, sin_r,
                            tm=t.inproj_rows, tn=t.inproj_cols, sub=t.proj_sub_rows)
    cum = _cum(ff, b_row, rows_per_batch=seq)
    ck_rows = (cum[:, :N_HEADS].reshape(n_batch, seq, N_HEADS).transpose(0, 2, 1)
               .reshape(n_batch * N_HEADS, 1, seq))
    ret = _retention(log_g, proj, proj_m, gr, n_batch=n_batch, rows_per_batch=seq, meta_prefix=True,
                     heads=t.mixer_heads)
    fox = _fox(proj, proj_m, cum, ck_rows, ckm_rows, n_batch=n_batch, rows_per_batch=seq,
               tq=t.fox_q_rows, meta_only=False, heads=t.fox_heads)
    h1, c = _outproj(ret, fox, w_out_b, x2d, g2, tm=t.outproj_rows, sub=t.proj_sub_rows)
    act, w_down_b = _upconv(c, c_m, w_up[0], conv_w[0], conv_b, w_down[0],
                            rows_per_batch=seq, tm=t.up_rows, tn=t.up_cols, sub=t.up_sub_rows)
    out = _down(act, w_down_b, h1, gf, tm=t.down_rows, sub=t.proj_sub_rows)
    return out.reshape(n_batch, seq, D_MODEL)
```

```python
import functools
from typing import NamedTuple

import jax
import jax.numpy as jnp
from jax import lax
from jax.experimental import pallas as pl
from jax.experimental.pallas import tpu as pltpu

F32 = jnp.float32
BF16 = jnp.bfloat16

D_MODEL = 2048
N_META = 16
CHUNK = 128
N_HEADS = 8
HEAD_DIM = 128
RET_DIM = N_HEADS * HEAD_DIM
FOX_DIM = N_HEADS * HEAD_DIM
MAIN_COLS = 4 * RET_DIM + 3 * FOX_DIM
D_FF = 5632
ROPE_BASE = 10000.0
NORM_EPS = 1e-6
META_PAD = CHUNK - N_META
MASKED = -1e30
LANES = 128
MIB = 1024 * 1024
LOG2E = 1.4426950408889634
FOXQ_SCALE = HEAD_DIM ** -0.5 * LOG2E

(SEC_RQ, SEC_RK, SEC_RV, SEC_RG, SEC_FQ, SEC_FK, SEC_FV) = range(7)
BLK = {s: s * N_HEADS for s in range(7)}


V7X_VMEM_BYTES = 64 * MIB
V7X_VMEM_UNSCOPED = 6 * MIB


class _Tiles(NamedTuple):
    inproj_rows: int = 1024
    inproj_first_cols: int = 512
    inproj_cols: int = 1024
    proj_sub_rows: int = 256
    fox_q_rows: int = 256
    mixer_heads: int = 2
    fox_heads: int = 4
    outproj_rows: int = 512
    up_rows: int = 2048
    up_sub_rows: int = 1024
    up_cols: int = 512
    down_rows: int = 512


def _nbytes(shape, dtype):
    n = jnp.dtype(dtype).itemsize
    for d in shape:
        n *= d
    return n


def _params(sem, pipelined, resident=(), temps=()):
    need = (2 * sum(_nbytes(*b) for b in pipelined) + sum(_nbytes(*b) for b in resident)
            + sum(_nbytes(*b) for b in temps))
    limit = -(-need // MIB) * MIB
    assert limit <= V7X_VMEM_BYTES - V7X_VMEM_UNSCOPED, (limit, sem)
    return pltpu.CompilerParams(dimension_semantics=sem, vmem_limit_bytes=limit)


def _rms(x, gain):
    ms = jnp.mean(x * x, axis=-1, keepdims=True)
    return (x * lax.rsqrt(ms + NORM_EPS)) * gain


def _silu(x):
    h = 0.5 * x
    return h + h * jnp.tanh(h)


def _row_tiles(tm, sub):
    return [slice(r * sub, (r + 1) * sub) for r in range(tm // sub)]


class _Rows(NamedTuple):
    x_ref: object
    cos_ref: object
    sin_ref: object
    o_ref: object
    ff_ref: object
    a_scr: object
    tiles: list


def _inproj_body(j, tn, groups, g_ref, w_bf, wf_ref):
    sec = j // (RET_DIM // tn)

    def rotary_store(grp, rs, acc, scale):
        cos = grp.cos_ref[rs, :]
        sin = grp.sin_ref[rs, :]
        for hh in range(tn // LANES):
            cols = slice(hh * LANES, (hh + 1) * LANES)
            t = acc[:, cols]
            y = t * cos + pltpu.roll(t, HEAD_DIM // 2, 1) * sin
            if scale is not None:
                y = y * scale
            grp.o_ref[rs, cols] = y.astype(BF16)

    @pl.when(j == 0)
    def _():
        for grp in groups:
            for rs in grp.tiles:
                a = _rms(grp.x_ref[rs, :], g_ref[...]).astype(BF16)
                grp.a_scr[rs, :] = a
                grp.ff_ref[rs, :] = lax.dot_general(a, wf_ref[...], (((1,), (1,)), ((), ())),
                                                    preferred_element_type=F32)
                rotary_store(grp, rs, jnp.dot(a, w_bf[...], preferred_element_type=F32), None)

    @pl.when((j > 0) & (sec <= SEC_RK))
    def _():
        scale = jnp.where(sec == SEC_RK, HEAD_DIM ** -0.5, 1.0).astype(F32)
        for grp in groups:
            for rs in grp.tiles:
                acc = jnp.dot(grp.a_scr[rs, :], w_bf[...], preferred_element_type=F32)
                rotary_store(grp, rs, acc, scale)

    @pl.when(sec > SEC_RK)
    def _():
        scale = jnp.where(sec == SEC_FQ, FOXQ_SCALE, 1.0).astype(F32)
        for grp in groups:
            for rs in grp.tiles:
                acc = jnp.dot(grp.a_scr[rs, :], w_bf[...], preferred_element_type=F32)
                grp.o_ref[rs, :] = (acc * scale).astype(BF16)


def _inproj_meta_kernel(xm_ref, g_ref, w_ref, wf8_ref, cosm_ref, sinm_ref,
                        om_ref, ffm_ref, wq_ref, wf_ref, am_scr, *, tn):
    j = pl.program_id(0)
    wq_ref[...] = w_ref[...].T.astype(BF16)

    @pl.when(j == 0)
    def _():
        pad = jnp.zeros((LANES - N_HEADS, D_MODEL), F32)
        wf_ref[...] = jnp.concatenate([wf8_ref[...], pad], axis=0).astype(BF16)

    groups = [_Rows(xm_ref, cosm_ref, sinm_ref, om_ref, ffm_ref, am_scr, _row_tiles(CHUNK, CHUNK))]
    _inproj_body(j, tn, groups, g_ref, wq_ref, wf_ref)


def _inproj_main_kernel(x_ref, g_ref, w_ref, wf_ref, cos_ref, sin_ref, o_ref, ff_ref, a_scr,
                        *, tm, tn, sub):
    groups = [_Rows(x_ref, cos_ref, sin_ref, o_ref, ff_ref, a_scr, _row_tiles(tm, sub))]
    _inproj_body(pl.program_id(1), tn, groups, g_ref, w_ref, wf_ref)


def _inproj_meta(xm, gain, w_in_t, cos_m, sin_m, *, tn):
    const = lambda shape: pl.BlockSpec(shape, lambda j: (0, 0))
    return pl.pallas_call(
        functools.partial(_inproj_meta_kernel, tn=tn),
        grid=(MAIN_COLS // tn,),
        in_specs=[
            const((CHUNK, D_MODEL)),
            const((1, D_MODEL)),
            pl.BlockSpec((tn, D_MODEL), lambda j: (j, 0)),
            pl.BlockSpec((N_HEADS, D_MODEL), lambda j: (MAIN_COLS // N_HEADS, 0)),
            const((CHUNK, LANES)),
            const((CHUNK, LANES)),
        ],
        out_specs=[
            pl.BlockSpec((CHUNK, tn), lambda j: (0, j)),
            const((CHUNK, LANES)),
            pl.BlockSpec((D_MODEL, tn), lambda j: (0, j)),
            const((LANES, D_MODEL)),
        ],
        out_shape=[
            jax.ShapeDtypeStruct((CHUNK, MAIN_COLS), BF16),
            jax.ShapeDtypeStruct((CHUNK, LANES), F32),
            jax.ShapeDtypeStruct((D_MODEL, MAIN_COLS), BF16),
            jax.ShapeDtypeStruct((LANES, D_MODEL), BF16),
        ],
        scratch_shapes=[pltpu.VMEM((CHUNK, D_MODEL), BF16)],
        compiler_params=_params(
            ("arbitrary",),
            pipelined=[((tn, D_MODEL), F32), ((D_MODEL, tn), BF16), ((CHUNK, D_MODEL), F32),
                       ((CHUNK, tn), BF16), ((3 * CHUNK, LANES), F32), ((LANES, D_MODEL), BF16)],
            resident=[((CHUNK, D_MODEL), BF16)],
            temps=[((tn, D_MODEL), F32), ((CHUNK, D_MODEL), F32), ((CHUNK, tn), F32)]),
        name="inproj_meta",
    )(xm, gain, w_in_t, w_in_t, cos_m, sin_m)


def _inproj_main(x2d, gain, w_main_b, w_f, cos_t, sin_t, *, tm, tn, sub):
    m = x2d.shape[0]
    nb = cos_t.shape[0] // tm
    return pl.pallas_call(
        functools.partial(_inproj_main_kernel, tm=tm, tn=tn, sub=sub),
        grid=(m // tm, MAIN_COLS // tn),
        in_specs=[
            pl.BlockSpec((tm, D_MODEL), lambda i, j: (i, 0)),
            pl.BlockSpec((1, D_MODEL), lambda i, j: (0, 0)),
            pl.BlockSpec((D_MODEL, tn), lambda i, j: (0, j)),
            pl.BlockSpec((LANES, D_MODEL), lambda i, j: (0, 0)),
            pl.BlockSpec((tm, LANES), lambda i, j: (i % nb, 0)),
            pl.BlockSpec((tm, LANES), lambda i, j: (i % nb, 0)),
        ],
        out_specs=[
            pl.BlockSpec((tm, tn), lambda i, j: (i, j)),
            pl.BlockSpec((tm, LANES), lambda i, j: (i, 0)),
        ],
        out_shape=[
            jax.ShapeDtypeStruct((m, MAIN_COLS), BF16),
            jax.ShapeDtypeStruct((m, LANES), F32),
        ],
        scratch_shapes=[pltpu.VMEM((tm, D_MODEL), BF16)],
        compiler_params=_params(
            ("arbitrary", "arbitrary"),
            pipelined=[((tm, D_MODEL), F32), ((D_MODEL, tn), BF16), ((tm, tn), BF16),
                       ((3 * tm, LANES), F32), ((LANES, D_MODEL), BF16)],
            resident=[((tm, D_MODEL), BF16)],
            temps=[((sub, D_MODEL), F32), ((sub, tn), F32), ((sub, tn), F32)]),
        name="inproj_main",
    )(x2d, gain, w_main_b, w_f, cos_t, sin_t)


def _cum_kernel(ff_ref, b_ref, o_ref, *, n_blk, valid_from, rel_last):
    row = lax.broadcasted_iota(jnp.int32, (CHUNK, CHUNK), 0)
    col = lax.broadcasted_iota(jnp.int32, (CHUNK, CHUNK), 1)
    tri = (row >= col).astype(BF16)
    rows = lax.broadcasted_iota(jnp.int32, (CHUNK, LANES), 0)
    carry = jnp.zeros((1, LANES), F32)
    for blk in range(n_blk):
        z = ff_ref[blk * CHUNK:(blk + 1) * CHUNK, :] + b_ref[...]
        lf = jnp.minimum(z, 0.0) - jnp.log1p(jnp.exp(-jnp.abs(z)))
        if valid_from:
            lf = jnp.where(rows >= valid_from, lf, 0.0)
        hi = lf.astype(BF16)
        r1 = lf - hi.astype(F32)
        mid = r1.astype(BF16)
        lo = (r1 - mid.astype(F32)).astype(BF16)
        cum = (jnp.dot(tri, hi, preferred_element_type=F32)
               + jnp.dot(tri, mid, preferred_element_type=F32)
               + jnp.dot(tri, lo, preferred_element_type=F32)) + carry
        o_ref[blk * CHUNK:(blk + 1) * CHUNK, :] = cum
        carry = cum[CHUNK - 1:CHUNK, :]
    if rel_last:
        o_ref[...] = o_ref[...] - carry


def _cum(ff, b_row, *, rows_per_batch, valid_from=0, rel_last=False):
    m = ff.shape[0]
    return pl.pallas_call(
        functools.partial(_cum_kernel, n_blk=rows_per_batch // CHUNK,
                          valid_from=valid_from, rel_last=rel_last),
        grid=(m // rows_per_batch,),
        in_specs=[
            pl.BlockSpec((rows_per_batch, LANES), lambda b: (b, 0)),
            pl.BlockSpec((1, LANES), lambda b: (0, 0)),
        ],
        out_specs=pl.BlockSpec((rows_per_batch, LANES), lambda b: (b, 0)),
        out_shape=jax.ShapeDtypeStruct((m, LANES), F32),
        compiler_params=_params(("arbitrary",), pipelined=[((2 * rows_per_batch, LANES), F32)],
                                temps=[((rows_per_batch, LANES), F32)]),
        name="cumgate",
    )(ff, b_row)


def _head_view(ref, hh):
    return ref.at[:, pl.ds(hh * HEAD_DIM, HEAD_DIM)]


def _ret_kernel(logg_ref, *refs, n_chunks, meta_prefix, heads):
    *io_refs, state_scr = refs
    for hh in range(heads):
        _ret_head(logg_ref[pl.program_id(1) * heads + hh], *[_head_view(r, hh) for r in io_refs],
                  state_scr.at[hh], n_chunks=n_chunks, meta_prefix=meta_prefix)


def _ret_head(lg, *refs, n_chunks, meta_prefix):
    if meta_prefix:
        q_ref, k_ref, v_ref, g_ref, km_ref, vm_ref, gain_ref, o_ref, state_scr = refs
    else:
        q_ref, k_ref, v_ref, g_ref, gain_ref, o_ref, state_scr = refs
    ri = lax.broadcasted_iota(jnp.int32, (CHUNK, CHUNK), 0)
    ci = lax.broadcasted_iota(jnp.int32, (CHUNK, CHUNK), 1)
    diff = (ri - ci).astype(F32)
    dmat = jnp.where(diff >= 0, jnp.exp(jnp.maximum(diff, 0.0) * lg), 0.0)
    pos = lax.broadcasted_iota(jnp.int32, (CHUNK, 1), 0).astype(F32)
    xi = jnp.exp((pos + 1.0) * lg)
    zeta = jnp.exp((CHUNK - 1.0 - pos) * lg)
    g_chunk = jnp.exp(jnp.full((1, 1), float(CHUNK), F32) * lg)

    def advance(state, kc, vc):
        kz = (kc.astype(F32) * zeta).astype(BF16)
        return g_chunk * state + lax.dot_general(
            kz, vc, (((0,), (0,)), ((), ())), preferred_element_type=F32)

    state = jnp.zeros((HEAD_DIM, HEAD_DIM), F32)
    if meta_prefix:
        state = advance(state, km_ref[...], vm_ref[...])
    for c in range(n_chunks):
        state_scr[c] = state.astype(BF16)
        if c + 1 < n_chunks:
            rows = slice(c * CHUNK, (c + 1) * CHUNK)
            state = advance(state, k_ref[rows, :], v_ref[rows, :])

    gain = gain_ref[...]

    for c in range(n_chunks):
        rows = slice(c * CHUNK, (c + 1) * CHUNK)
        qc = q_ref[rows, :]
        kc = k_ref[rows, :]
        vc = v_ref[rows, :]
        s = lax.dot_general(qc, kc, (((1,), (1,)), ((), ())), preferred_element_type=F32) * dmat
        o = (jnp.dot(s.astype(BF16), vc, preferred_element_type=F32)
             + xi * jnp.dot(qc, state_scr[c], preferred_element_type=F32))
        mu = jnp.mean(o, axis=-1, keepdims=True)
        d = o - mu
        var = jnp.mean(d * d, axis=-1, keepdims=True)
        y = (d * lax.rsqrt(var + NORM_EPS)) * gain
        gt = g_ref[rows, :].astype(F32)
        o_ref[rows, :] = (_silu(gt) * y).astype(BF16)


def _retention(log_g, proj, proj_meta, gain, *, n_batch, rows_per_batch, meta_prefix, heads):
    width = heads * HEAD_DIM
    blk = lambda sec: pl.BlockSpec((rows_per_batch, width),
                                   lambda b, h, s=BLK[sec] // heads: (b, s + h))
    in_specs = [pl.BlockSpec(memory_space=pltpu.SMEM),
                blk(SEC_RQ), blk(SEC_RK), blk(SEC_RV), blk(SEC_RG)]
    args = [log_g, proj, proj, proj, proj]
    if meta_prefix:
        mblk = lambda sec: pl.BlockSpec((CHUNK, width), lambda b, h, s=BLK[sec] // heads: (0, s + h))
        in_specs += [mblk(SEC_RK), mblk(SEC_RV)]
        args += [proj_meta, proj_meta]
    in_specs.append(pl.BlockSpec((1, width), lambda b, h: (0, h)))
    args.append(gain)
    return pl.pallas_call(
        functools.partial(_ret_kernel, n_chunks=rows_per_batch // CHUNK, meta_prefix=meta_prefix,
                          heads=heads),
        grid=(n_batch, N_HEADS // heads),
        in_specs=in_specs,
        out_specs=pl.BlockSpec((rows_per_batch, width), lambda b, h: (b, h)),
        out_shape=jax.ShapeDtypeStruct((n_batch * rows_per_batch, RET_DIM), BF16),
        scratch_shapes=[pltpu.VMEM((heads, rows_per_batch // CHUNK, HEAD_DIM, HEAD_DIM), BF16)],
        compiler_params=_params(
            ("arbitrary", "arbitrary"),
            pipelined=[((5 * rows_per_batch + 2 * CHUNK, width), BF16)],
            resident=[((rows_per_batch, width), BF16)],
            temps=[((rows_per_batch, HEAD_DIM), F32)]),
        name="retention",
    )(*args)


def _fox_kernel(*refs, n_q, tq, meta_only, heads):
    for hh in range(heads):
        _fox_head(pl.program_id(1) * heads + hh, hh, *refs, n_q=n_q, tq=tq, meta_only=meta_only)


def _fox_head(h, hh, *refs, n_q, tq, meta_only):
    if meta_only:
        q_ref, km_ref, vm_ref, cq_ref, ckm_ref, o_ref, k_all, v_all = refs
        ck_all = ckm_ref[hh] * LOG2E
    else:
        q_ref, k_ref, v_ref, km_ref, vm_ref, cq_ref, ck_ref, ckm_ref, o_ref, k_all, v_all = refs
        k_ref, v_ref = _head_view(k_ref, hh), _head_view(v_ref, hh)
        k_all[CHUNK:, :] = k_ref[...]
        v_all[CHUNK:, :] = v_ref[...]
        ck_all = jnp.concatenate([ckm_ref[hh], ck_ref[hh]], axis=1) * LOG2E
    q_ref, o_ref = _head_view(q_ref, hh), _head_view(o_ref, hh)
    km_ref, vm_ref = _head_view(km_ref, hh), _head_view(vm_ref, hh)
    k_all[0:CHUNK, :] = km_ref[...]
    v_all[0:CHUNK, :] = vm_ref[...]
    lane = lax.broadcasted_iota(jnp.int32, (1, LANES), 1)

    def meta_mask(w):
        col_m = lax.broadcasted_iota(jnp.int32, (w, CHUNK), 1)
        mask = col_m >= META_PAD
        if meta_only:
            mask = mask & (lax.broadcasted_iota(jnp.int32, (w, CHUNK), 0) >= col_m)
        return mask

    def causal_mask(w):
        return (lax.broadcasted_iota(jnp.int32, (w, w), 0)
                >= lax.broadcasted_iota(jnp.int32, (w, w), 1))

    if meta_only or tq != 2 * CHUNK:
        blocks = [(qi * tq, (qi + 1) * tq) for qi in range(n_q)]
    else:
        edges = [0] + list(range(CHUNK, n_q * tq, tq)) + [n_q * tq]
        blocks = list(zip(edges[:-1], edges[1:]))

    def n_keys(blk):
        return CHUNK if meta_only else CHUNK + blk[1]

    def logit_pass(blk):
        n, w = n_keys(blk), blk[1] - blk[0]
        q = q_ref[blk[0]:blk[1], :]
        t = lax.dot_general(q, k_all[0:n, :], (((1,), (1,)), ((), ())),
                            preferred_element_type=F32) - ck_all[:, 0:n]
        parts = [jnp.where(meta_mask(w), t[:, 0:CHUNK], MASKED)]
        if not meta_only:
            if n - w > CHUNK:
                parts.append(t[:, CHUNK:n - w])
            parts.append(jnp.where(causal_mask(w), t[:, n - w:n], MASKED))
        t = parts[0] if len(parts) == 1 else jnp.concatenate(parts, axis=1)
        return t, jnp.max(t, axis=1, keepdims=True)

    def exp_pass(blk, t, mx):
        rows = slice(blk[0], blk[1])
        cq2 = LOG2E * jnp.sum(jnp.where(lane == h, cq_ref[rows, :], 0.0), axis=1, keepdims=True)
        m_row = mx + cq2
        p = jnp.exp2(t - (m_row - cq2))
        l = jnp.sum(p, axis=1, keepdims=True)
        acc = jnp.dot(p.astype(BF16), v_all[0:n_keys(blk), :], preferred_element_type=F32)
        o_ref[rows, :] = (acc / l).astype(BF16)

    cur = logit_pass(blocks[0])
    for i, blk in enumerate(blocks):
        nxt = logit_pass(blocks[i + 1]) if i + 1 < len(blocks) else None
        exp_pass(blk, *cur)
        cur = nxt


def _fox(proj, proj_meta, cum_col, ck_rows, ckm_rows, *, n_batch, rows_per_batch, tq, meta_only,
         heads):
    n_q = rows_per_batch // tq
    n_all = CHUNK if meta_only else CHUNK + rows_per_batch
    width = heads * HEAD_DIM
    groups = N_HEADS // heads
    qblk = lambda sec: pl.BlockSpec((rows_per_batch, width),
                                    lambda b, h, s=BLK[sec] // heads: (b, s + h))
    mblk = lambda sec: pl.BlockSpec((CHUNK, width), lambda b, h, s=BLK[sec] // heads: (0, s + h))
    cq_spec = pl.BlockSpec((rows_per_batch, LANES), lambda b, h: (b, 0))
    ckm_spec = pl.BlockSpec((heads, 1, CHUNK), lambda b, h: (h, 0, 0))
    if meta_only:
        in_specs = [qblk(SEC_FQ), mblk(SEC_FK), mblk(SEC_FV), cq_spec, ckm_spec]
        args = [proj, proj_meta, proj_meta, cum_col, ckm_rows]
    else:
        ck_spec = pl.BlockSpec((heads, 1, rows_per_batch), lambda b, h: (b * groups + h, 0, 0))
        in_specs = [qblk(SEC_FQ), qblk(SEC_FK), qblk(SEC_FV), mblk(SEC_FK), mblk(SEC_FV),
                    cq_spec, ck_spec, ckm_spec]
        args = [proj, proj, proj, proj_meta, proj_meta, cum_col, ck_rows, ckm_rows]
    return pl.pallas_call(
        functools.partial(_fox_kernel, n_q=n_q, tq=tq, meta_only=meta_only, heads=heads),
        grid=(n_batch, groups),
        in_specs=in_specs,
        out_specs=pl.BlockSpec((rows_per_batch, width), lambda b, h: (b, h)),
        out_shape=jax.ShapeDtypeStruct((n_batch * rows_per_batch, FOX_DIM), BF16),
        scratch_shapes=[pltpu.VMEM((n_all, HEAD_DIM), BF16)] * 2,
        compiler_params=_params(
            ("arbitrary", "arbitrary"),
            pipelined=[((4 * rows_per_batch + 2 * CHUNK, width), BF16),
                       ((rows_per_batch, LANES), F32), ((8 * heads, n_all), F32)],
            resident=[((2 * n_all, HEAD_DIM), BF16)],
            temps=[((4 * tq * heads, n_all), F32)]),
        name="foxattn",
    )(*args)


def _outproj_kernel(ret_ref, fox_ref, w1_ref, w2_ref, x_ref, g_ref, h_ref, c_ref, *, tm, sub):
    for rs in _row_tiles(tm, sub):
        hcur = (x_ref[rs, :]
                + jnp.dot(ret_ref[rs, :], w1_ref[...], preferred_element_type=F32)
                + jnp.dot(fox_ref[rs, :], w2_ref[...], preferred_element_type=F32))
        h_ref[rs, :] = hcur
        c_ref[rs, :] = _rms(hcur, g_ref[...]).astype(BF16)


def _outproj(ret, fox, w_out_b, x2d, gain, *, tm, sub):
    m = x2d.shape[0]
    return pl.pallas_call(
        functools.partial(_outproj_kernel, tm=tm, sub=sub),
        grid=(m // tm,),
        in_specs=[
            pl.BlockSpec((tm, RET_DIM), lambda i: (i, 0)),
            pl.BlockSpec((tm, FOX_DIM), lambda i: (i, 0)),
            pl.BlockSpec((RET_DIM, D_MODEL), lambda i: (0, 0)),
            pl.BlockSpec((FOX_DIM, D_MODEL), lambda i: (1, 0)),
            pl.BlockSpec((tm, D_MODEL), lambda i: (i, 0)),
            pl.BlockSpec((1, D_MODEL), lambda i: (0, 0)),
        ],
        out_specs=[
            pl.BlockSpec((tm, D_MODEL), lambda i: (i, 0)),
            pl.BlockSpec((tm, D_MODEL), lambda i: (i, 0)),
        ],
        out_shape=[
            jax.ShapeDtypeStruct((m, D_MODEL), F32),
            jax.ShapeDtypeStruct((m, D_MODEL), BF16),
        ],
        compiler_params=_params(
            ("arbitrary",),
            pipelined=[((tm, RET_DIM + FOX_DIM), BF16), ((tm, D_MODEL), F32), ((tm, D_MODEL), F32),
                       ((tm, D_MODEL), BF16)],
            resident=[((RET_DIM + FOX_DIM, D_MODEL), BF16)],
            temps=[((sub, D_MODEL), F32), ((sub, D_MODEL), F32)]),
        name="outproj",
    )(ret, fox, w_out_b, w_out_b, x2d, gain)


def _outproj_meta_kernel(mix_ref, w_ref, x_ref, g_ref, c_ref, wq_ref, acc_scr):
    k = pl.program_id(0)
    wq_ref[...] = w_ref[...].astype(BF16)
    part = jnp.dot(mix_ref[...], wq_ref[...], preferred_element_type=F32)

    @pl.when(k == 0)
    def _():
        acc_scr[...] = x_ref[...] + part

    @pl.when(k == 1)
    def _():
        c_ref[...] = _rms(acc_scr[...] + part, g_ref[...]).astype(BF16)


def _outproj_meta(mix_m, w_out, xm, gain):
    half = D_MODEL // 2
    return pl.pallas_call(
        _outproj_meta_kernel,
        grid=(2,),
        in_specs=[
            pl.BlockSpec((CHUNK, half), lambda k: (0, k)),
            pl.BlockSpec((half, D_MODEL), lambda k: (k, 0)),
            pl.BlockSpec((CHUNK, D_MODEL), lambda k: (0, 0)),
            pl.BlockSpec((1, D_MODEL), lambda k: (0, 0)),
        ],
        out_specs=[
            pl.BlockSpec((CHUNK, D_MODEL), lambda k: (0, 0)),
            pl.BlockSpec((half, D_MODEL), lambda k: (k, 0)),
        ],
        out_shape=[
            jax.ShapeDtypeStruct((CHUNK, D_MODEL), BF16),
            jax.ShapeDtypeStruct((D_MODEL, D_MODEL), BF16),
        ],
        scratch_shapes=[pltpu.VMEM((CHUNK, D_MODEL), F32)],
        compiler_params=_params(
            ("arbitrary",),
            pipelined=[((half, D_MODEL), F32), ((half, D_MODEL), BF16), ((CHUNK, 2 * D_MODEL), F32)],
            resident=[((CHUNK, D_MODEL), F32)],
            temps=[((CHUNK, D_MODEL), F32), ((CHUNK, D_MODEL), F32)]),
        name="outproj_meta",
    )(mix_m, w_out, xm, gain)


def _up_kernel(c_ref, cm_ref, wg_ref, wv_ref, cwg_ref, cwv_ref, cbg_ref, cbv_ref, wd_ref,
               o_ref, wdq_ref, wg_s, wv_s, ug_s, uv_s, umg_s, umv_s, *, tm, sub, tiles_per_batch):
    i = pl.program_id(1)
    wdq_ref[...] = wd_ref[...].astype(BF16)

    @pl.when(i == 0)
    def _():
        wg_s[...] = wg_ref[...].astype(BF16)
        wv_s[...] = wv_ref[...].astype(BF16)
        cm = cm_ref[...]
        umg_s[...] = jnp.dot(cm, wg_s[...], preferred_element_type=F32)[8:16, :]
        umv_s[...] = jnp.dot(cm, wv_s[...], preferred_element_type=F32)[8:16, :]

    @pl.when(i % tiles_per_batch == 0)
    def _():
        ug_s[0:8, :] = umg_s[...]
        uv_s[0:8, :] = umv_s[...]

    def conv(c, rs, w_s, u_s, cw_ref, cb_ref):
        u_s[8 + rs.start:8 + rs.stop, :] = jnp.dot(c, w_s[...], preferred_element_type=F32)
        ext = u_s[rs.start:8 + rs.stop, :]
        n = rs.stop - rs.start
        return (cb_ref[...]
                + cw_ref[0:1, :] * pltpu.roll(ext, 2, 0)[8:8 + n, :]
                + cw_ref[1:2, :] * pltpu.roll(ext, 1, 0)[8:8 + n, :]
                + cw_ref[2:3, :] * ext[8:8 + n, :])

    for rs in _row_tiles(tm, sub):
        c = c_ref[rs, :]
        gate = conv(c, rs, wg_s, ug_s, cwg_ref, cbg_ref)
        val = conv(c, rs, wv_s, uv_s, cwv_ref, cbv_ref)
        o_ref[rs, :] = ((gate / (1.0 + jnp.exp(-gate))) * val).astype(BF16)
    ug_s[0:8, :] = ug_s[tm:tm + 8, :]
    uv_s[0:8, :] = uv_s[tm:tm + 8, :]


def _upconv(c, c_meta, w_up, conv_w, conv_b, w_down, *, rows_per_batch, tm, tn, sub):
    m = c.shape[0]
    nj = D_FF // tn
    ni = m // tm
    wd_rows = D_FF // (nj * ni)
    assert wd_rows * nj * ni == D_FF and wd_rows % 16 == 0
    tiles_per_batch = rows_per_batch // tm
    halo_blk = c_meta.shape[0] // 16 - 1
    return pl.pallas_call(
        functools.partial(_up_kernel, tm=tm, sub=sub, tiles_per_batch=tiles_per_batch),
        grid=(nj, m // tm),
        in_specs=[
            pl.BlockSpec((tm, D_MODEL), lambda j, i: (i, 0)),
            pl.BlockSpec((16, D_MODEL), lambda j, i: (halo_blk, 0)),
            pl.BlockSpec((D_MODEL, tn), lambda j, i: (0, j)),
            pl.BlockSpec((D_MODEL, tn), lambda j, i: (0, j + nj)),
            pl.BlockSpec((3, tn), lambda j, i: (0, j)),
            pl.BlockSpec((3, tn), lambda j, i: (0, j + nj)),
            pl.BlockSpec((1, tn), lambda j, i: (0, j)),
            pl.BlockSpec((1, tn), lambda j, i: (0, j + nj)),
            pl.BlockSpec((wd_rows, D_MODEL), lambda j, i: (j * ni + i, 0)),
        ],
        out_specs=[
            pl.BlockSpec((tm, tn), lambda j, i: (i, j)),
            pl.BlockSpec((wd_rows, D_MODEL), lambda j, i: (j * ni + i, 0)),
        ],
        out_shape=[
            jax.ShapeDtypeStruct((m, D_FF), BF16),
            jax.ShapeDtypeStruct((D_FF, D_MODEL), BF16),
        ],
        scratch_shapes=[
            pltpu.VMEM((D_MODEL, tn), BF16),
            pltpu.VMEM((D_MODEL, tn), BF16),
            pltpu.VMEM((tm + 8, tn), F32),
            pltpu.VMEM((tm + 8, tn), F32),
            pltpu.VMEM((8, tn), F32),
            pltpu.VMEM((8, tn), F32),
        ],
        compiler_params=_params(
            ("arbitrary", "arbitrary"),
            pipelined=[((tm, D_MODEL), BF16), ((D_MODEL, 2 * tn), F32), ((tm, tn), BF16),
                       ((wd_rows, D_MODEL), F32), ((wd_rows, D_MODEL), BF16)],
            resident=[((D_MODEL, 2 * tn), BF16), ((2 * (tm + 8), tn), F32)],
            temps=[((sub, tn), F32), ((sub, tn), F32)]),
        name="upconv",
    )(c, c_meta, w_up, w_up, conv_w, conv_w, conv_b, conv_b, w_down)


def _down_kernel(a_ref, w_ref, h_ref, g_ref, o_ref, *, tm, sub):
    for rs in _row_tiles(tm, sub):
        hcur = h_ref[rs, :] + jnp.dot(a_ref[rs, :], w_ref[...], preferred_element_type=F32)
        o_ref[rs, :] = _rms(hcur, g_ref[...])


def _down(act, w_down_b, h1, gain, *, tm, sub):
    m = act.shape[0]
    return pl.pallas_call(
        functools.partial(_down_kernel, tm=tm, sub=sub),
        grid=(m // tm,),
        in_specs=[
            pl.BlockSpec((tm, D_FF), lambda i: (i, 0)),
            pl.BlockSpec((D_FF, D_MODEL), lambda i: (0, 0), pipeline_mode=pl.Buffered(1)),
            pl.BlockSpec((tm, D_MODEL), lambda i: (i, 0)),
            pl.BlockSpec((1, D_MODEL), lambda i: (0, 0)),
        ],
        out_specs=pl.BlockSpec((tm, D_MODEL), lambda i: (i, 0)),
        out_shape=jax.ShapeDtypeStruct((m, D_MODEL), F32),
        compiler_params=_params(
            ("arbitrary",),
            pipelined=[((tm, D_FF), BF16), ((tm, D_MODEL), F32), ((tm, D_MODEL), F32)],
            resident=[((D_FF, D_MODEL), BF16)],
            temps=[((sub, D_MODEL), F32), ((sub, D_MODEL), F32)]),
        name="downproj",
    )(act, w_down_b, h1, gain)


def _rotary_tables(pos):
    inv_freq = 1.0 / (ROPE_BASE ** (jnp.arange(0, HEAD_DIM, 2, dtype=F32) / HEAD_DIM))
    ang = pos[:, None] * inv_freq[None, :]
    cos, sin = jnp.cos(ang), jnp.sin(ang)
    return jnp.concatenate([cos, cos], axis=1), jnp.concatenate([-sin, sin], axis=1)


def kernel(x, meta_tokens, norm1_gain, w_in, b_forget, ret_norm_gain, w_out, norm2_gain,
           w_up, conv_w, conv_b, w_down, final_norm_gain):
    n_batch, seq, d_model = x.shape
    assert d_model == D_MODEL and seq % CHUNK == 0 and w_in.shape[0] == 1
    assert meta_tokens.shape == (N_META, D_MODEL)
    t = _Tiles()
    x2d = x.reshape(n_batch * seq, D_MODEL)
    xm = jnp.concatenate([jnp.zeros((META_PAD, D_MODEL), F32), meta_tokens.astype(F32)], axis=0)

    w_in_t = w_in[0].T
    b_row = jnp.pad(b_forget[0], (0, LANES - N_HEADS)).reshape(1, LANES)
    log_g = jnp.log1p(-jnp.exp2(-5.0 - jnp.arange(N_HEADS, dtype=F32)))
    g1 = norm1_gain[0].reshape(1, D_MODEL)
    g2 = norm2_gain[0].reshape(1, D_MODEL)
    gf = final_norm_gain.reshape(1, D_MODEL)
    gr = ret_norm_gain[0].reshape(1, RET_DIM)

    cos_r, sin_r = _rotary_tables(N_META + jnp.arange(seq, dtype=F32))
    cos_m, sin_m = _rotary_tables(jnp.maximum(jnp.arange(CHUNK, dtype=F32) - META_PAD, 0.0))

    proj_m, ff_m, w_main_b, w_f = _inproj_meta(xm, g1, w_in_t, cos_m, sin_m,
                                               tn=t.inproj_first_cols)
    cum_m = _cum(ff_m, b_row, rows_per_batch=CHUNK, valid_from=META_PAD, rel_last=True)
    ckm_rows = cum_m[:, :N_HEADS].T.reshape(N_HEADS, 1, CHUNK)
    ret_m = _retention(log_g, proj_m, None, gr, n_batch=1, rows_per_batch=CHUNK, meta_prefix=False,
                       heads=t.mixer_heads)
    fox_m = _fox(proj_m, proj_m, cum_m, None, ckm_rows, n_batch=1, rows_per_batch=CHUNK,
                 tq=CHUNK, meta_only=True, heads=t.mixer_heads)
    c_m, w_out_b = _outproj_meta(jnp.concatenate([ret_m, fox_m], axis=1), w_out[0], xm, g2)

    proj, ff = _inproj_main(x2d, g1, w_main_b, w_f, cos_r, sin_r,
                            tm=t.inproj_rows, tn=t.inproj_cols, sub=t.proj_sub_rows)
    cum = _cum(ff, b_row, rows_per_batch=seq)
    ck_rows = (cum[:, :N_HEADS].reshape(n_batch, seq, N_HEADS).transpose(0, 2, 1)
               .reshape(n_batch * N_HEADS, 1, seq))
    ret = _retention(log_g, proj, proj_m, gr, n_batch=n_batch, rows_per_batch=seq, meta_prefix=True,
                     heads=t.mixer_heads)
    fox = _fox(proj, proj_m, cum, ck_rows, ckm_rows, n_batch=n_batch, rows_per_batch=seq,
               tq=t.fox_q_rows, meta_only=False, heads=t.fox_heads)
    h1, c = _outproj(ret, fox, w_out_b, x2d, g2, tm=t.outproj_rows, sub=t.proj_sub_rows)
    act, w_down_b = _upconv(c, c_m, w_up[0], conv_w[0], conv_b, w_down[0],
                            rows_per_batch=seq, tm=t.up_rows, tn=t.up_cols, sub=t.up_sub_rows)
    out = _down(act, w_down_b, h1, gf, tm=t.down_rows, sub=t.proj_sub_rows)
    return out.reshape(n_batch, seq, D_MODEL)
```

```python
import functools
from typing import NamedTuple

import jax
import jax.numpy as jnp
from jax import lax
from jax.experimental import pallas as pl
from jax.experimental.pallas import tpu as pltpu

F32 = jnp.float32
BF16 = jnp.bfloat16

D_MODEL = 2048
N_META = 16
CHUNK = 128
N_HEADS = 8
HEAD_DIM = 128
RET_DIM = N_HEADS * HEAD_DIM
FOX_DIM = N_HEADS * HEAD_DIM
MAIN_COLS = 4 * RET_DIM + 3 * FOX_DIM
D_FF = 5632
ROPE_BASE = 10000.0
NORM_EPS = 1e-6
META_PAD = CHUNK - N_META
MASKED = -1e30
LANES = 128
MIB = 1024 * 1024
LOG2E = 1.4426950408889634
FOXQ_SCALE = HEAD_DIM ** -0.5 * LOG2E

(SEC_RQ, SEC_RK, SEC_RV, SEC_RG, SEC_FQ, SEC_FK, SEC_FV) = range(7)
BLK = {s: s * N_HEADS for s in range(7)}


V7X_VMEM_BYTES = 64 * MIB
V7X_VMEM_UNSCOPED = 6 * MIB


class _Tiles(NamedTuple):
    inproj_rows: int = 1024
    inproj_first_cols: int = 512
    inproj_cols: int = 1024
    proj_sub_rows: int = 256
    fox_q_rows: int = 256
    mixer_heads: int = 2
    fox_heads: int = 4
    outproj_rows: int = 512
    up_rows: int = 2048
    up_sub_rows: int = 1024
    up_cols: int = 512
    down_rows: int = 512


def _nbytes(shape, dtype):
    n = jnp.dtype(dtype).itemsize
    for d in shape:
        n *= d
    return n


def _params(sem, pipelined, resident=(), temps=()):
    need = (2 * sum(_nbytes(*b) for b in pipelined) + sum(_nbytes(*b) for b in resident)
            + sum(_nbytes(*b) for b in temps))
    limit = -(-need // MIB) * MIB
    assert limit <= V7X_VMEM_BYTES - V7X_VMEM_UNSCOPED, (limit, sem)
    return pltpu.CompilerParams(dimension_semantics=sem, vmem_limit_bytes=limit)


def _rms(x, gain):
    ms = jnp.mean(x * x, axis=-1, keepdims=True)
    return (x * lax.rsqrt(ms + NORM_EPS)) * gain


def _silu(x):
    h = 0.5 * x
    return h + h * jnp.tanh(h)


def _row_tiles(tm, sub):
    return [slice(r * sub, (r + 1) * sub) for r in range(tm // sub)]


class _Rows(NamedTuple):
    x_ref: object
    cos_ref: object
    sin_ref: object
    o_ref: object
    ff_ref: object
    a_scr: object
    tiles: list


def _inproj_body(j, tn, groups, g_ref, w_bf, wf_ref):
    sec = j // (RET_DIM // tn)

    def rotary_store(grp, rs, acc, scale):
        cos = grp.cos_ref[rs, :]
        sin = grp.sin_ref[rs, :]
        for hh in range(tn // LANES):
            cols = slice(hh * LANES, (hh + 1) * LANES)
            t = acc[:, cols]
            y = t * cos + pltpu.roll(t, HEAD_DIM // 2, 1) * sin
            if scale is not None:
                y = y * scale
            grp.o_ref[rs, cols] = y.astype(BF16)

    @pl.when(j == 0)
    def _():
        for grp in groups:
            for rs in grp.tiles:
                a = _rms(grp.x_ref[rs, :], g_ref[...]).astype(BF16)
                grp.a_scr[rs, :] = a
                grp.ff_ref[rs, :] = lax.dot_general(a, wf_ref[...], (((1,), (1,)), ((), ())),
                                                    preferred_element_type=F32)
                rotary_store(grp, rs, jnp.dot(a, w_bf[...], preferred_element_type=F32), None)

    @pl.when((j > 0) & (sec <= SEC_RK))
    def _():
        scale = jnp.where(sec == SEC_RK, HEAD_DIM ** -0.5, 1.0).astype(F32)
        for grp in groups:
            for rs in grp.tiles:
                acc = jnp.dot(grp.a_scr[rs, :], w_bf[...], preferred_element_type=F32)
                rotary_store(grp, rs, acc, scale)

    @pl.when(sec > SEC_RK)
    def _():
        scale = jnp.where(sec == SEC_FQ, FOXQ_SCALE, 1.0).astype(F32)
        for grp in groups:
            for rs in grp.tiles:
                acc = jnp.dot(grp.a_scr[rs, :], w_bf[...], preferred_element_type=F32)
                grp.o_ref[rs, :] = (acc * scale).astype(BF16)


def _inproj_meta_kernel(xm_ref, g_ref, w_ref, wf8_ref, cosm_ref, sinm_ref,
                        om_ref, ffm_ref, wq_ref, wf_ref, am_scr, *, tn):
    j = pl.program_id(0)
    wq_ref[...] = w_ref[...].T.astype(BF16)

    @pl.when(j == 0)
    def _():
        pad = jnp.zeros((LANES - N_HEADS, D_MODEL), F32)
        wf_ref[...] = jnp.concatenate([wf8_ref[...], pad], axis=0).astype(BF16)

    groups = [_Rows(xm_ref, cosm_ref, sinm_ref, om_ref, ffm_ref, am_scr, _row_tiles(CHUNK, CHUNK))]
    _inproj_body(j, tn, groups, g_ref, wq_ref, wf_ref)


def _inproj_main_kernel(x_ref, g_ref, w_ref, wf_ref, cos_ref, sin_ref, o_ref, ff_ref, a_scr,
                        *, tm, tn, sub):
    groups = [_Rows(x_ref, cos_ref, sin_ref, o_ref, ff_ref, a_scr, _row_tiles(tm, sub))]
    _inproj_body(pl.program_id(1), tn, groups, g_ref, w_ref, wf_ref)


def _inproj_meta(xm, gain, w_in_t, cos_m, sin_m, *, tn):
    const = lambda shape: pl.BlockSpec(shape, lambda j: (0, 0))
    return pl.pallas_call(
        functools.partial(_inproj_meta_kernel, tn=tn),
        grid=(MAIN_COLS // tn,),
        in_specs=[
            const((CHUNK, D_MODEL)),
            const((1, D_MODEL)),
            pl.BlockSpec((tn, D_MODEL), lambda j: (j, 0)),
            pl.BlockSpec((N_HEADS, D_MODEL), lambda j: (MAIN_COLS // N_HEADS, 0)),
            const((CHUNK, LANES)),
            const((CHUNK, LANES)),
        ],
        out_specs=[
            pl.BlockSpec((CHUNK, tn), lambda j: (0, j)),
            const((CHUNK, LANES)),
            pl.BlockSpec((D_MODEL, tn), lambda j: (0, j)),
            const((LANES, D_MODEL)),
        ],
        out_shape=[
            jax.ShapeDtypeStruct((CHUNK, MAIN_COLS), BF16),
            jax.ShapeDtypeStruct((CHUNK, LANES), F32),
            jax.ShapeDtypeStruct((D_MODEL, MAIN_COLS), BF16),
            jax.ShapeDtypeStruct((LANES, D_MODEL), BF16),
        ],
        scratch_shapes=[pltpu.VMEM((CHUNK, D_MODEL), BF16)],
        compiler_params=_params(
            ("arbitrary",),
            pipelined=[((tn, D_MODEL), F32), ((D_MODEL, tn), BF16), ((CHUNK, D_MODEL), F32),
                       ((CHUNK, tn), BF16), ((3 * CHUNK, LANES), F32), ((LANES, D_MODEL), BF16)],
            resident=[((CHUNK, D_MODEL), BF16)],
            temps=[((tn, D_MODEL), F32), ((CHUNK, D_MODEL), F32), ((CHUNK, tn), F32)]),
        name="inproj_meta",
    )(xm, gain, w_in_t, w_in_t, cos_m, sin_m)


def _inproj_main(x2d, gain, w_main_b, w_f, cos_t, sin_t, *, tm, tn, sub):
    m = x2d.shape[0]
    nb = cos_t.shape[0] // tm
    return pl.pallas_call(
        functools.partial(_inproj_main_kernel, tm=tm, tn=tn, sub=sub),
        grid=(m // tm, MAIN_COLS // tn),
        in_specs=[
            pl.BlockSpec((tm, D_MODEL), lambda i, j: (i, 0)),
            pl.BlockSpec((1, D_MODEL), lambda i, j: (0, 0)),
            pl.BlockSpec((D_MODEL, tn), lambda i, j: (0, j)),
            pl.BlockSpec((LANES, D_MODEL), lambda i, j: (0, 0)),
            pl.BlockSpec((tm, LANES), lambda i, j: (i % nb, 0)),
            pl.BlockSpec((tm, LANES), lambda i, j: (i % nb, 0)),
        ],
        out_specs=[
            pl.BlockSpec((tm, tn), lambda i, j: (i, j)),
            pl.BlockSpec((tm, LANES), lambda i, j: (i, 0)),
        ],
        out_shape=[
            jax.ShapeDtypeStruct((m, MAIN_COLS), BF16),
            jax.ShapeDtypeStruct((m, LANES), F32),
        ],
        scratch_shapes=[pltpu.VMEM((tm, D_MODEL), BF16)],
        compiler_params=_params(
            ("arbitrary", "arbitrary"),
            pipelined=[((tm, D_MODEL), F32), ((D_MODEL, tn), BF16), ((tm, tn), BF16),
                       ((3 * tm, LANES), F32), ((LANES, D_MODEL), BF16)],
            resident=[((tm, D_MODEL), BF16)],
            temps=[((sub, D_MODEL), F32), ((sub, tn), F32), ((sub, tn), F32)]),
        name="inproj_main",
    )(x2d, gain, w_main_b, w_f, cos_t, sin_t)


def _cum_kernel(ff_ref, b_ref, o_ref, *, n_blk, valid_from, rel_last):
    row = lax.broadcasted_iota(jnp.int32, (CHUNK, CHUNK), 0)
    col = lax.broadcasted_iota(jnp.int32, (CHUNK, CHUNK), 1)
    tri = (row >= col).astype(BF16)
    rows = lax.broadcasted_iota(jnp.int32, (CHUNK, LANES), 0)
    carry = jnp.zeros((1, LANES), F32)
    for blk in range(n_blk):
        z = ff_ref[blk * CHUNK:(blk + 1) * CHUNK, :] + b_ref[...]
        lf = jnp.minimum(z, 0.0) - jnp.log1p(jnp.exp(-jnp.abs(z)))
        if valid_from:
            lf = jnp.where(rows >= valid_from, lf, 0.0)
        hi = lf.astype(BF16)
        r1 = lf - hi.astype(F32)
        mid = r1.astype(BF16)
        lo = (r1 - mid.astype(F32)).astype(BF16)
        cum = (jnp.dot(tri, hi, preferred_element_type=F32)
               + jnp.dot(tri, mid, preferred_element_type=F32)
               + jnp.dot(tri, lo, preferred_element_type=F32)) + carry
        o_ref[blk * CHUNK:(blk + 1) * CHUNK, :] = cum
        carry = cum[CHUNK - 1:CHUNK, :]
    if rel_last:
        o_ref[...] = o_ref[...] - carry


def _cum(ff, b_row, *, rows_per_batch, valid_from=0, rel_last=False):
    m = ff.shape[0]
    return pl.pallas_call(
        functools.partial(_cum_kernel, n_blk=rows_per_batch // CHUNK,
                          valid_from=valid_from, rel_last=rel_last),
        grid=(m // rows_per_batch,),
        in_specs=[
            pl.BlockSpec((rows_per_batch, LANES), lambda b: (b, 0)),
            pl.BlockSpec((1, LANES), lambda b: (0, 0)),
        ],
        out_specs=pl.BlockSpec((rows_per_batch, LANES), lambda b: (b, 0)),
        out_shape=jax.ShapeDtypeStruct((m, LANES), F32),
        compiler_params=_params(("arbitrary",), pipelined=[((2 * rows_per_batch, LANES), F32)],
                                temps=[((rows_per_batch, LANES), F32)]),
        name="cumgate",
    )(ff, b_row)


def _head_view(ref, hh):
    return ref.at[:, pl.ds(hh * HEAD_DIM, HEAD_DIM)]


def _ret_kernel(logg_ref, *refs, n_chunks, meta_prefix, heads):
    *io_refs, state_scr = refs
    for hh in range(heads):
        _ret_head(logg_ref[pl.program_id(1) * heads + hh], *[_head_view(r, hh) for r in io_refs],
                  state_scr.at[hh], n_chunks=n_chunks, meta_prefix=meta_prefix)


def _ret_head(lg, *refs, n_chunks, meta_prefix):
    if meta_prefix:
        q_ref, k_ref, v_ref, g_ref, km_ref, vm_ref, gain_ref, o_ref, state_scr = refs
    else:
        q_ref, k_ref, v_ref, g_ref, gain_ref, o_ref, state_scr = refs
    ri = lax.broadcasted_iota(jnp.int32, (CHUNK, CHUNK), 0)
    ci = lax.broadcasted_iota(jnp.int32, (CHUNK, CHUNK), 1)
    diff = (ri - ci).astype(F32)
    dmat = jnp.where(diff >= 0, jnp.exp(jnp.maximum(diff, 0.0) * lg), 0.0)
    pos = lax.broadcasted_iota(jnp.int32, (CHUNK, 1), 0).astype(F32)
    xi = jnp.exp((pos + 1.0) * lg)
    zeta = jnp.exp((CHUNK - 1.0 - pos) * lg)
    g_chunk = jnp.exp(jnp.full((1, 1), float(CHUNK), F32) * lg)

    def advance(state, kc, vc):
        kz = (kc.astype(F32) * zeta).astype(BF16)
        return g_chunk * state + lax.dot_general(
            kz, vc, (((0,), (0,)), ((), ())), preferred_element_type=F32)

    state = jnp.zeros((HEAD_DIM, HEAD_DIM), F32)
    if meta_prefix:
        state = advance(state, km_ref[...], vm_ref[...])
    for c in range(n_chunks):
        state_scr[c] = state.astype(BF16)
        if c + 1 < n_chunks:
            rows = slice(c * CHUNK, (c + 1) * CHUNK)
            state = advance(state, k_ref[rows, :], v_ref[rows, :])

    gain = gain_ref[...]

    for c in range(n_chunks):
        rows = slice(c * CHUNK, (c + 1) * CHUNK)
        qc = q_ref[rows, :]
        kc = k_ref[rows, :]
        vc = v_ref[rows, :]
        s = lax.dot_general(qc, kc, (((1,), (1,)), ((), ())), preferred_element_type=F32) * dmat
        o = (jnp.dot(s.astype(BF16), vc, preferred_element_type=F32)
             + xi * jnp.dot(qc, state_scr[c], preferred_element_type=F32))
        mu = jnp.mean(o, axis=-1, keepdims=True)
        d = o - mu
        var = jnp.mean(d * d, axis=-1, keepdims=True)
        y = (d * lax.rsqrt(var + NORM_EPS)) * gain
        gt = g_ref[rows, :].astype(F32)
        o_ref[rows, :] = (_silu(gt) * y).astype(BF16)


def _retention(log_g, proj, proj_meta, gain, *, n_batch, rows_per_batch, meta_prefix, heads):
    width = heads * HEAD_DIM
    blk = lambda sec: pl.BlockSpec((rows_per_batch, width),
                                   lambda b, h, s=BLK[sec] // heads: (b, s + h))
    in_specs = [pl.BlockSpec(memory_space=pltpu.SMEM),
                blk(SEC_RQ), blk(SEC_RK), blk(SEC_RV), blk(SEC_RG)]
    args = [log_g, proj, proj, proj, proj]
    if meta_prefix:
        mblk = lambda sec: pl.BlockSpec((CHUNK, width), lambda b, h, s=BLK[sec] // heads: (0, s + h))
        in_specs += [mblk(SEC_RK), mblk(SEC_RV)]
        args += [proj_meta, proj_meta]
    in_specs.append(pl.BlockSpec((1, width), lambda b, h: (0, h)))
    args.append(gain)
    return pl.pallas_call(
        functools.partial(_ret_kernel, n_chunks=rows_per_batch // CHUNK, meta_prefix=meta_prefix,
                          heads=heads),
        grid=(n_batch, N_HEADS // heads),
        in_specs=in_specs,
        out_specs=pl.BlockSpec((rows_per_batch, width), lambda b, h: (b, h)),
        out_shape=jax.ShapeDtypeStruct((n_batch * rows_per_batch, RET_DIM), BF16),
        scratch_shapes=[pltpu.VMEM((heads, rows_per_batch // CHUNK, HEAD_DIM, HEAD_DIM), BF16)],
        compiler_params=_params(
            ("arbitrary", "arbitrary"),
            pipelined=[((5 * rows_per_batch + 2 * CHUNK, width), BF16)],
            resident=[((rows_per_batch, width), BF16)],
            temps=[((rows_per_batch, HEAD_DIM), F32)]),
        name="retention",
    )(*args)


def _fox_kernel(*refs, n_q, tq, meta_only, heads):
    for hh in range(heads):
        _fox_head(pl.program_id(1) * heads + hh, hh, *refs, n_q=n_q, tq=tq, meta_only=meta_only)


def _fox_head(h, hh, *refs, n_q, tq, meta_only):
    if meta_only:
        q_ref, km_ref, vm_ref, cq_ref, ckm_ref, o_ref, k_all, v_all = refs
        ck_all = ckm_ref[hh] * LOG2E
    else:
        q_ref, k_ref, v_ref, km_ref, vm_ref, cq_ref, ck_ref, ckm_ref, o_ref, k_all, v_all = refs
        k_ref, v_ref = _head_view(k_ref, hh), _head_view(v_ref, hh)
        k_all[CHUNK:, :] = k_ref[...]
        v_all[CHUNK:, :] = v_ref[...]
        ck_all = jnp.concatenate([ckm_ref[hh], ck_ref[hh]], axis=1) * LOG2E
    q_ref, o_ref = _head_view(q_ref, hh), _head_view(o_ref, hh)
    km_ref, vm_ref = _head_view(km_ref, hh), _head_view(vm_ref, hh)
    k_all[0:CHUNK, :] = km_ref[...]
    v_all[0:CHUNK, :] = vm_ref[...]
    lane = lax.broadcasted_iota(jnp.int32, (1, LANES), 1)
    col_m = lax.broadcasted_iota(jnp.int32, (tq, CHUNK), 1)
    row_m = lax.broadcasted_iota(jnp.int32, (tq, CHUNK), 0)
    mask_m = col_m >= META_PAD
    if meta_only:
        mask_m = mask_m & (row_m >= col_m)
    tri = (lax.broadcasted_iota(jnp.int32, (tq, tq), 0)
           >= lax.broadcasted_iota(jnp.int32, (tq, tq), 1))

    def n_keys(qi):
        return CHUNK if meta_only else CHUNK + (qi + 1) * tq

    def logit_pass(qi):
        n = n_keys(qi)
        q = q_ref[qi * tq:(qi + 1) * tq, :]
        t = lax.dot_general(q, k_all[0:n, :], (((1,), (1,)), ((), ())),
                            preferred_element_type=F32) - ck_all[:, 0:n]
        parts = [jnp.where(mask_m, t[:, 0:CHUNK], MASKED)]
        if not meta_only:
            if qi > 0:
                parts.append(t[:, CHUNK:n - tq])
            parts.append(jnp.where(tri, t[:, n - tq:n], MASKED))
        t = parts[0] if len(parts) == 1 else jnp.concatenate(parts, axis=1)
        return t, jnp.max(t, axis=1, keepdims=True)

    def exp_pass(qi, t, mx):
        rows = slice(qi * tq, (qi + 1) * tq)
        cq2 = LOG2E * jnp.sum(jnp.where(lane == h, cq_ref[rows, :], 0.0), axis=1, keepdims=True)
        m_row = mx + cq2
        p = jnp.exp2(t - (m_row - cq2))
        l = jnp.sum(p, axis=1, keepdims=True)
        acc = jnp.dot(p.astype(BF16), v_all[0:n_keys(qi), :], preferred_element_type=F32)
        o_ref[rows, :] = (acc / l).astype(BF16)

    cur = logit_pass(0)
    for qi in range(n_q):
        nxt = logit_pass(qi + 1) if qi + 1 < n_q else None
        exp_pass(qi, *cur)
        cur = nxt


def _fox(proj, proj_meta, cum_col, ck_rows, ckm_rows, *, n_batch, rows_per_batch, tq, meta_only,
         heads):
    n_q = rows_per_batch // tq
    n_all = CHUNK if meta_only else CHUNK + rows_per_batch
    width = heads * HEAD_DIM
    groups = N_HEADS // heads
    qblk = lambda sec: pl.BlockSpec((rows_per_batch, width),
                                    lambda b, h, s=BLK[sec] // heads: (b, s + h))
    mblk = lambda sec: pl.BlockSpec((CHUNK, width), lambda b, h, s=BLK[sec] // heads: (0, s + h))
    cq_spec = pl.BlockSpec((rows_per_batch, LANES), lambda b, h: (b, 0))
    ckm_spec = pl.BlockSpec((heads, 1, CHUNK), lambda b, h: (h, 0, 0))
    if meta_only:
        in_specs = [qblk(SEC_FQ), mblk(SEC_FK), mblk(SEC_FV), cq_spec, ckm_spec]
        args = [proj, proj_meta, proj_meta, cum_col, ckm_rows]
    else:
        ck_spec = pl.BlockSpec((heads, 1, rows_per_batch), lambda b, h: (b * groups + h, 0, 0))
        in_specs = [qblk(SEC_FQ), qblk(SEC_FK), qblk(SEC_FV), mblk(SEC_FK), mblk(SEC_FV),
                    cq_spec, ck_spec, ckm_spec]
        args = [proj, proj, proj, proj_meta, proj_meta, cum_col, ck_rows, ckm_rows]
    return pl.pallas_call(
        functools.partial(_fox_kernel, n_q=n_q, tq=tq, meta_only=meta_only, heads=heads),
        grid=(n_batch, groups),
        in_specs=in_specs,
        out_specs=pl.BlockSpec((rows_per_batch, width), lambda b, h: (b, h)),
        out_shape=jax.ShapeDtypeStruct((n_batch * rows_per_batch, FOX_DIM), BF16),
        scratch_shapes=[pltpu.VMEM((n_all, HEAD_DIM), BF16)] * 2,
        compiler_params=_params(
            ("arbitrary", "arbitrary"),
            pipelined=[((4 * rows_per_batch + 2 * CHUNK, width), BF16),
                       ((rows_per_batch, LANES), F32), ((8 * heads, n_all), F32)],
            resident=[((2 * n_all, HEAD_DIM), BF16)],
            temps=[((4 * tq * heads, n_all), F32)]),
        name="foxattn",
    )(*args)


def _outproj_kernel(ret_ref, fox_ref, w1_ref, w2_ref, x_ref, g_ref, h_ref, c_ref, *, tm, sub):
    for rs in _row_tiles(tm, sub):
        hcur = (x_ref[rs, :]
                + jnp.dot(ret_ref[rs, :], w1_ref[...], preferred_element_type=F32)
                + jnp.dot(fox_ref[rs, :], w2_ref[...], preferred_element_type=F32))
        h_ref[rs, :] = hcur
        c_ref[rs, :] = _rms(hcur, g_ref[...]).astype(BF16)


def _outproj(ret, fox, w_out_b, x2d, gain, *, tm, sub):
    m = x2d.shape[0]
    return pl.pallas_call(
        functools.partial(_outproj_kernel, tm=tm, sub=sub),
        grid=(m // tm,),
        in_specs=[
            pl.BlockSpec((tm, RET_DIM), lambda i: (i, 0)),
            pl.BlockSpec((tm, FOX_DIM), lambda i: (i, 0)),
            pl.BlockSpec((RET_DIM, D_MODEL), lambda i: (0, 0)),
            pl.BlockSpec((FOX_DIM, D_MODEL), lambda i: (1, 0)),
            pl.BlockSpec((tm, D_MODEL), lambda i: (i, 0)),
            pl.BlockSpec((1, D_MODEL), lambda i: (0, 0)),
        ],
        out_specs=[
            pl.BlockSpec((tm, D_MODEL), lambda i: (i, 0)),
            pl.BlockSpec((tm, D_MODEL), lambda i: (i, 0)),
        ],
        out_shape=[
            jax.ShapeDtypeStruct((m, D_MODEL), F32),
            jax.ShapeDtypeStruct((m, D_MODEL), BF16),
        ],
        compiler_params=_params(
            ("arbitrary",),
            pipelined=[((tm, RET_DIM + FOX_DIM), BF16), ((tm, D_MODEL), F32), ((tm, D_MODEL), F32),
                       ((tm, D_MODEL), BF16)],
            resident=[((RET_DIM + FOX_DIM, D_MODEL), BF16)],
            temps=[((sub, D_MODEL), F32), ((sub, D_MODEL), F32)]),
        name="outproj",
    )(ret, fox, w_out_b, w_out_b, x2d, gain)


def _outproj_meta_kernel(mix_ref, w_ref, x_ref, g_ref, c_ref, wq_ref, acc_scr):
    k = pl.program_id(0)
    wq_ref[...] = w_ref[...].astype(BF16)
    part = jnp.dot(mix_ref[...], wq_ref[...], preferred_element_type=F32)

    @pl.when(k == 0)
    def _():
        acc_scr[...] = x_ref[...] + part

    @pl.when(k == 1)
    def _():
        c_ref[...] = _rms(acc_scr[...] + part, g_ref[...]).astype(BF16)


def _outproj_meta(mix_m, w_out, xm, gain):
    half = D_MODEL // 2
    return pl.pallas_call(
        _outproj_meta_kernel,
        grid=(2,),
        in_specs=[
            pl.BlockSpec((CHUNK, half), lambda k: (0, k)),
            pl.BlockSpec((half, D_MODEL), lambda k: (k, 0)),
            pl.BlockSpec((CHUNK, D_MODEL), lambda k: (0, 0)),
            pl.BlockSpec((1, D_MODEL), lambda k: (0, 0)),
        ],
        out_specs=[
            pl.BlockSpec((CHUNK, D_MODEL), lambda k: (0, 0)),
            pl.BlockSpec((half, D_MODEL), lambda k: (k, 0)),
        ],
        out_shape=[
            jax.ShapeDtypeStruct((CHUNK, D_MODEL), BF16),
            jax.ShapeDtypeStruct((D_MODEL, D_MODEL), BF16),
        ],
        scratch_shapes=[pltpu.VMEM((CHUNK, D_MODEL), F32)],
        compiler_params=_params(
            ("arbitrary",),
            pipelined=[((half, D_MODEL), F32), ((half, D_MODEL), BF16), ((CHUNK, 2 * D_MODEL), F32)],
            resident=[((CHUNK, D_MODEL), F32)],
            temps=[((CHUNK, D_MODEL), F32), ((CHUNK, D_MODEL), F32)]),
        name="outproj_meta",
    )(mix_m, w_out, xm, gain)


def _up_kernel(c_ref, cm_ref, wg_ref, wv_ref, cwg_ref, cwv_ref, cbg_ref, cbv_ref, wd_ref,
               o_ref, wdq_ref, wg_s, wv_s, ug_s, uv_s, umg_s, umv_s, *, tm, sub, tiles_per_batch):
    i = pl.program_id(1)
    wdq_ref[...] = wd_ref[...].astype(BF16)

    @pl.when(i == 0)
    def _():
        wg_s[...] = wg_ref[...].astype(BF16)
        wv_s[...] = wv_ref[...].astype(BF16)
        cm = cm_ref[...]
        umg_s[...] = jnp.dot(cm, wg_s[...], preferred_element_type=F32)[8:16, :]
        umv_s[...] = jnp.dot(cm, wv_s[...], preferred_element_type=F32)[8:16, :]

    @pl.when(i % tiles_per_batch == 0)
    def _():
        ug_s[0:8, :] = umg_s[...]
        uv_s[0:8, :] = umv_s[...]

    def conv(c, rs, w_s, u_s, cw_ref, cb_ref):
        u_s[8 + rs.start:8 + rs.stop, :] = jnp.dot(c, w_s[...], preferred_element_type=F32)
        ext = u_s[rs.start:8 + rs.stop, :]
        n = rs.stop - rs.start
        return (cb_ref[...]
                + cw_ref[0:1, :] * pltpu.roll(ext, 2, 0)[8:8 + n, :]
                + cw_ref[1:2, :] * pltpu.roll(ext, 1, 0)[8:8 + n, :]
                + cw_ref[2:3, :] * ext[8:8 + n, :])

    for rs in _row_tiles(tm, sub):
        c = c_ref[rs, :]
        gate = conv(c, rs, wg_s, ug_s, cwg_ref, cbg_ref)
        val = conv(c, rs, wv_s, uv_s, cwv_ref, cbv_ref)
        o_ref[rs, :] = ((gate / (1.0 + jnp.exp(-gate))) * val).astype(BF16)
    ug_s[0:8, :] = ug_s[tm:tm + 8, :]
    uv_s[0:8, :] = uv_s[tm:tm + 8, :]


def _upconv(c, c_meta, w_up, conv_w, conv_b, w_down, *, rows_per_batch, tm, tn, sub):
    m = c.shape[0]
    nj = D_FF // tn
    ni = m // tm
    wd_rows = D_FF // (nj * ni)
    assert wd_rows * nj * ni == D_FF and wd_rows % 16 == 0
    tiles_per_batch = rows_per_batch // tm
    halo_blk = c_meta.shape[0] // 16 - 1
    return pl.pallas_call(
        functools.partial(_up_kernel, tm=tm, sub=sub, tiles_per_batch=tiles_per_batch),
        grid=(nj, m // tm),
        in_specs=[
            pl.BlockSpec((tm, D_MODEL), lambda j, i: (i, 0)),
            pl.BlockSpec((16, D_MODEL), lambda j, i: (halo_blk, 0)),
            pl.BlockSpec((D_MODEL, tn), lambda j, i: (0, j)),
            pl.BlockSpec((D_MODEL, tn), lambda j, i: (0, j + nj)),
            pl.BlockSpec((3, tn), lambda j, i: (0, j)),
            pl.BlockSpec((3, tn), lambda j, i: (0, j + nj)),
            pl.BlockSpec((1, tn), lambda j, i: (0, j)),
            pl.BlockSpec((1, tn), lambda j, i: (0, j + nj)),
            pl.BlockSpec((wd_rows, D_MODEL), lambda j, i: (j * ni + i, 0)),
        ],
        out_specs=[
            pl.BlockSpec((tm, tn), lambda j, i: (i, j)),
            pl.BlockSpec((wd_rows, D_MODEL), lambda j, i: (j * ni + i, 0)),
        ],
        out_shape=[
            jax.ShapeDtypeStruct((m, D_FF), BF16),
            jax.ShapeDtypeStruct((D_FF, D_MODEL), BF16),
        ],
        scratch_shapes=[
            pltpu.VMEM((D_MODEL, tn), BF16),
            pltpu.VMEM((D_MODEL, tn), BF16),
            pltpu.VMEM((tm + 8, tn), F32),
            pltpu.VMEM((tm + 8, tn), F32),
            pltpu.VMEM((8, tn), F32),
            pltpu.VMEM((8, tn), F32),
        ],
        compiler_params=_params(
            ("arbitrary", "arbitrary"),
            pipelined=[((tm, D_MODEL), BF16), ((D_MODEL, 2 * tn), F32), ((tm, tn), BF16),
                       ((wd_rows, D_MODEL), F32), ((wd_rows, D_MODEL), BF16)],
            resident=[((D_MODEL, 2 * tn), BF16), ((2 * (tm + 8), tn), F32)],
            temps=[((sub, tn), F32), ((sub, tn), F32)]),
        name="upconv",
    )(c, c_meta, w_up, w_up, conv_w, conv_w, conv_b, conv_b, w_down)


DOWN_W_CHUNKS = 4


def _down_kernel(a_ref, w_hbm, h_ref, g_ref, o_ref, w_vmem, sems, *, tm, sub):
    i = pl.program_id(0)
    ck = D_FF // DOWN_W_CHUNKS
    tiles = _row_tiles(tm, sub)

    def chunk_copy(k):
        rows = pl.ds(k * ck, ck)
        return pltpu.make_async_copy(w_hbm.at[rows, :], w_vmem.at[rows, :], sems.at[k])

    @pl.when(i == 0)
    def _():
        for k in range(DOWN_W_CHUNKS):
            chunk_copy(k).start()
        o_ref[...] = h_ref[...]
        for k in range(DOWN_W_CHUNKS):
            chunk_copy(k).wait()
            for rs in tiles:
                o_ref[rs, :] += jnp.dot(a_ref[rs, k * ck:(k + 1) * ck], w_vmem[k * ck:(k + 1) * ck, :],
                                        preferred_element_type=F32)
        o_ref[...] = _rms(o_ref[...], g_ref[...])

    @pl.when(i > 0)
    def _():
        for rs in tiles:
            hcur = h_ref[rs, :] + jnp.dot(a_ref[rs, :], w_vmem[...], preferred_element_type=F32)
            o_ref[rs, :] = _rms(hcur, g_ref[...])


def _down(act, w_down_b, h1, gain, *, tm, sub):
    m = act.shape[0]
    assert D_FF % (DOWN_W_CHUNKS * LANES) == 0
    return pl.pallas_call(
        functools.partial(_down_kernel, tm=tm, sub=sub),
        grid=(m // tm,),
        in_specs=[
            pl.BlockSpec((tm, D_FF), lambda i: (i, 0)),
            pl.BlockSpec(memory_space=pl.ANY),
            pl.BlockSpec((tm, D_MODEL), lambda i: (i, 0)),
            pl.BlockSpec((1, D_MODEL), lambda i: (0, 0)),
        ],
        out_specs=pl.BlockSpec((tm, D_MODEL), lambda i: (i, 0)),
        out_shape=jax.ShapeDtypeStruct((m, D_MODEL), F32),
        scratch_shapes=[pltpu.VMEM((D_FF, D_MODEL), BF16),
                        pltpu.SemaphoreType.DMA((DOWN_W_CHUNKS,))],
        compiler_params=_params(
            ("arbitrary",),
            pipelined=[((tm, D_FF), BF16), ((tm, D_MODEL), F32), ((tm, D_MODEL), F32)],
            resident=[((D_FF, D_MODEL), BF16)],
            temps=[((sub, D_MODEL), F32), ((sub, D_MODEL), F32)]),
        name="downproj",
    )(act, w_down_b, h1, gain)


def _rotary_tables(pos):
    inv_freq = 1.0 / (ROPE_BASE ** (jnp.arange(0, HEAD_DIM, 2, dtype=F32) / HEAD_DIM))
    ang = pos[:, None] * inv_freq[None, :]
    cos, sin = jnp.cos(ang), jnp.sin(ang)
    return jnp.concatenate([cos, cos], axis=1), jnp.concatenate([-sin, sin], axis=1)


def kernel(x, meta_tokens, norm1_gain, w_in, b_forget, ret_norm_gain, w_out, norm2_gain,
           w_up, conv_w, conv_b, w_down, final_norm_gain):
    n_batch, seq, d_model = x.shape
    assert d_model == D_MODEL and seq % CHUNK == 0 and w_in.shape[0] == 1
    assert meta_tokens.shape == (N_META, D_MODEL)
    t = _Tiles()
    x2d = x.reshape(n_batch * seq, D_MODEL)
    xm = jnp.concatenate([jnp.zeros((META_PAD, D_MODEL), F32), meta_tokens.astype(F32)], axis=0)

    w_in_t = w_in[0].T
    b_row = jnp.pad(b_forget[0], (0, LANES - N_HEADS)).reshape(1, LANES)
    log_g = jnp.log1p(-jnp.exp2(-5.0 - jnp.arange(N_HEADS, dtype=F32)))
    g1 = norm1_gain[0].reshape(1, D_MODEL)
    g2 = norm2_gain[0].reshape(1, D_MODEL)
    gf = final_norm_gain.reshape(1, D_MODEL)
    gr = ret_norm_gain[0].reshape(1, RET_DIM)

    cos_r, sin_r = _rotary_tables(N_META + jnp.arange(seq, dtype=F32))
    cos_m, sin_m = _rotary_tables(jnp.maximum(jnp.arange(CHUNK, dtype=F32) - META_PAD, 0.0))

    proj_m, ff_m, w_main_b, w_f = _inproj_meta(xm, g1, w_in_t, cos_m, sin_m,
                                               tn=t.inproj_first_cols)
    cum_m = _cum(ff_m, b_row, rows_per_batch=CHUNK, valid_from=META_PAD, rel_last=True)
    ckm_rows = cum_m[:, :N_HEADS].T.reshape(N_HEADS, 1, CHUNK)
    ret_m = _retention(log_g, proj_m, None, gr, n_batch=1, rows_per_batch=CHUNK, meta_prefix=False,
                       heads=t.mixer_heads)
    fox_m = _fox(proj_m, proj_m, cum_m, None, ckm_rows, n_batch=1, rows_per_batch=CHUNK,
                 tq=CHUNK, meta_only=True, heads=t.mixer_heads)
    c_m, w_out_b = _outproj_meta(jnp.concatenate([ret_m, fox_m], axis=1), w_out[0], xm, g2)

    proj, ff = _inproj_main(x2d, g1, w_main_b, w_f, cos_r, sin_r,
                            tm=t.inproj_rows, tn=t.inproj_cols, sub=t.proj_sub_rows)
    cum = _cum(ff, b_row, rows_per_batch=seq)
    ck_rows = (cum[:, :N_HEADS].reshape(n_batch, seq, N_HEADS).transpose(0, 2, 1)
               .reshape(n_batch * N_HEADS, 1, seq))
    ret = _retention(log_g, proj, proj_m, gr, n_batch=n_batch, rows_per_batch=seq, meta_prefix=True,
                     heads=t.mixer_heads)
    fox = _fox(proj, proj_m, cum, ck_rows, ckm_rows, n_batch=n_batch, rows_per_batch=seq,
               tq=t.fox_q_rows, meta_only=False, heads=t.fox_heads)
    h1, c = _outproj(ret, fox, w_out_b, x2d, g2, tm=t.outproj_rows, sub=t.proj_sub_rows)
    act, w_down_b = _upconv(c, c_m, w_up[0], conv_w[0], conv_b, w_down[0],
                            rows_per_batch=seq, tm=t.up_rows, tn=t.up_cols, sub=t.up_sub_rows)
    out = _down(act, w_down_b, h1, gf, tm=t.down_rows, sub=t.proj_sub_rows)
    return out.reshape(n_batch, seq, D_MODEL)
```

```python
import functools
from typing import NamedTuple

import jax
import jax.numpy as jnp
from jax import lax
from jax.experimental import pallas as pl
from jax.experimental.pallas import tpu as pltpu

F32 = jnp.float32
BF16 = jnp.bfloat16

D_MODEL = 2048
N_META = 16
CHUNK = 128
N_HEADS = 8
HEAD_DIM = 128
RET_DIM = N_HEADS * HEAD_DIM
FOX_DIM = N_HEADS * HEAD_DIM
MAIN_COLS = 4 * RET_DIM + 3 * FOX_DIM
D_FF = 5632
ROPE_BASE = 10000.0
NORM_EPS = 1e-6
META_PAD = CHUNK - N_META
MASKED = -1e30
LANES = 128
MIB = 1024 * 1024
LOG2E = 1.4426950408889634
FOXQ_SCALE = HEAD_DIM ** -0.5 * LOG2E

(SEC_RQ, SEC_RK, SEC_RV, SEC_RG, SEC_FQ, SEC_FK, SEC_FV) = range(7)
BLK = {s: s * N_HEADS for s in range(7)}


V7X_VMEM_BYTES = 64 * MIB
V7X_VMEM_UNSCOPED = 6 * MIB


class _Tiles(NamedTuple):
    inproj_rows: int = 1024
    inproj_first_cols: int = 512
    inproj_cols: int = 1024
    proj_sub_rows: int = 256
    fox_q_rows: int = 256
    mixer_heads: int = 2
    fox_heads: int = 4
    outproj_rows: int = 512
    up_rows: int = 2048
    up_sub_rows: int = 1024
    up_cols: int = 512
    down_rows: int = 512


def _nbytes(shape, dtype):
    n = jnp.dtype(dtype).itemsize
    for d in shape:
        n *= d
    return n


def _params(sem, pipelined, resident=(), temps=()):
    need = (2 * sum(_nbytes(*b) for b in pipelined) + sum(_nbytes(*b) for b in resident)
            + sum(_nbytes(*b) for b in temps))
    limit = -(-need // MIB) * MIB
    assert limit <= V7X_VMEM_BYTES - V7X_VMEM_UNSCOPED, (limit, sem)
    return pltpu.CompilerParams(dimension_semantics=sem, vmem_limit_bytes=limit)


def _rms(x, gain):
    ms = jnp.mean(x * x, axis=-1, keepdims=True)
    return (x * lax.rsqrt(ms + NORM_EPS)) * gain


def _silu(x):
    h = 0.5 * x
    return h + h * jnp.tanh(h)


def _row_tiles(tm, sub):
    return [slice(r * sub, (r + 1) * sub) for r in range(tm // sub)]


class _Rows(NamedTuple):
    x_ref: object
    cos_ref: object
    sin_ref: object
    o_ref: object
    ff_ref: object
    a_scr: object
    tiles: list


def _inproj_body(j, tn, groups, g_ref, w_bf, wf_ref):
    sec = j // (RET_DIM // tn)

    def rotary_store(grp, rs, acc, scale):
        cos = grp.cos_ref[rs, :]
        sin = grp.sin_ref[rs, :]
        for hh in range(tn // LANES):
            cols = slice(hh * LANES, (hh + 1) * LANES)
            t = acc[:, cols]
            y = t * cos + pltpu.roll(t, HEAD_DIM // 2, 1) * sin
            if scale is not None:
                y = y * scale
            grp.o_ref[rs, cols] = y.astype(BF16)

    @pl.when(j == 0)
    def _():
        for grp in groups:
            for rs in grp.tiles:
                a = _rms(grp.x_ref[rs, :], g_ref[...]).astype(BF16)
                grp.a_scr[rs, :] = a
                grp.ff_ref[rs, :] = lax.dot_general(a, wf_ref[...], (((1,), (1,)), ((), ())),
                                                    preferred_element_type=F32)
                rotary_store(grp, rs, jnp.dot(a, w_bf[...], preferred_element_type=F32), None)

    @pl.when((j > 0) & (sec <= SEC_RK))
    def _():
        scale = jnp.where(sec == SEC_RK, HEAD_DIM ** -0.5, 1.0).astype(F32)
        for grp in groups:
            for rs in grp.tiles:
                acc = jnp.dot(grp.a_scr[rs, :], w_bf[...], preferred_element_type=F32)
                rotary_store(grp, rs, acc, scale)

    @pl.when(sec > SEC_RK)
    def _():
        scale = jnp.where(sec == SEC_FQ, FOXQ_SCALE, 1.0).astype(F32)
        for grp in groups:
            for rs in grp.tiles:
                acc = jnp.dot(grp.a_scr[rs, :], w_bf[...], preferred_element_type=F32)
                grp.o_ref[rs, :] = (acc * scale).astype(BF16)


def _inproj_meta_kernel(xm_ref, g_ref, w_ref, wf8_ref, cosm_ref, sinm_ref,
                        om_ref, ffm_ref, wq_ref, wf_ref, am_scr, *, tn):
    j = pl.program_id(0)
    wq_ref[...] = w_ref[...].T.astype(BF16)

    @pl.when(j == 0)
    def _():
        pad = jnp.zeros((LANES - N_HEADS, D_MODEL), F32)
        wf_ref[...] = jnp.concatenate([wf8_ref[...], pad], axis=0).astype(BF16)

    groups = [_Rows(xm_ref, cosm_ref, sinm_ref, om_ref, ffm_ref, am_scr, _row_tiles(CHUNK, CHUNK))]
    _inproj_body(j, tn, groups, g_ref, wq_ref, wf_ref)


def _inproj_main_kernel(x_ref, g_ref, w_ref, wf_ref, cos_ref, sin_ref, o_ref, ff_ref, a_scr,
                        *, tm, tn, sub):
    groups = [_Rows(x_ref, cos_ref, sin_ref, o_ref, ff_ref, a_scr, _row_tiles(tm, sub))]
    _inproj_body(pl.program_id(1), tn, groups, g_ref, w_ref, wf_ref)


def _inproj_meta(xm, gain, w_in_t, cos_m, sin_m, *, tn):
    const = lambda shape: pl.BlockSpec(shape, lambda j: (0, 0))
    return pl.pallas_call(
        functools.partial(_inproj_meta_kernel, tn=tn),
        grid=(MAIN_COLS // tn,),
        in_specs=[
            const((CHUNK, D_MODEL)),
            const((1, D_MODEL)),
            pl.BlockSpec((tn, D_MODEL), lambda j: (j, 0)),
            pl.BlockSpec((N_HEADS, D_MODEL), lambda j: (MAIN_COLS // N_HEADS, 0)),
            const((CHUNK, LANES)),
            const((CHUNK, LANES)),
        ],
        out_specs=[
            pl.BlockSpec((CHUNK, tn), lambda j: (0, j)),
            const((CHUNK, LANES)),
            pl.BlockSpec((D_MODEL, tn), lambda j: (0, j)),
            const((LANES, D_MODEL)),
        ],
        out_shape=[
            jax.ShapeDtypeStruct((CHUNK, MAIN_COLS), BF16),
            jax.ShapeDtypeStruct((CHUNK, LANES), F32),
            jax.ShapeDtypeStruct((D_MODEL, MAIN_COLS), BF16),
            jax.ShapeDtypeStruct((LANES, D_MODEL), BF16),
        ],
        scratch_shapes=[pltpu.VMEM((CHUNK, D_MODEL), BF16)],
        compiler_params=_params(
            ("arbitrary",),
            pipelined=[((tn, D_MODEL), F32), ((D_MODEL, tn), BF16), ((CHUNK, D_MODEL), F32),
                       ((CHUNK, tn), BF16), ((3 * CHUNK, LANES), F32), ((LANES, D_MODEL), BF16)],
            resident=[((CHUNK, D_MODEL), BF16)],
            temps=[((tn, D_MODEL), F32), ((CHUNK, D_MODEL), F32), ((CHUNK, tn), F32)]),
        name="inproj_meta",
    )(xm, gain, w_in_t, w_in_t, cos_m, sin_m)


def _inproj_main(x2d, gain, w_main_b, w_f, cos_t, sin_t, *, tm, tn, sub):
    m = x2d.shape[0]
    nb = cos_t.shape[0] // tm
    return pl.pallas_call(
        functools.partial(_inproj_main_kernel, tm=tm, tn=tn, sub=sub),
        grid=(m // tm, MAIN_COLS // tn),
        in_specs=[
            pl.BlockSpec((tm, D_MODEL), lambda i, j: (i, 0)),
            pl.BlockSpec((1, D_MODEL), lambda i, j: (0, 0)),
            pl.BlockSpec((D_MODEL, tn), lambda i, j: (0, j)),
            pl.BlockSpec((LANES, D_MODEL), lambda i, j: (0, 0)),
            pl.BlockSpec((tm, LANES), lambda i, j: (i % nb, 0)),
            pl.BlockSpec((tm, LANES), lambda i, j: (i % nb, 0)),
        ],
        out_specs=[
            pl.BlockSpec((tm, tn), lambda i, j: (i, j)),
            pl.BlockSpec((tm, LANES), lambda i, j: (i, 0)),
        ],
        out_shape=[
            jax.ShapeDtypeStruct((m, MAIN_COLS), BF16),
            jax.ShapeDtypeStruct((m, LANES), F32),
        ],
        scratch_shapes=[pltpu.VMEM((tm, D_MODEL), BF16)],
        compiler_params=_params(
            ("arbitrary", "arbitrary"),
            pipelined=[((tm, D_MODEL), F32), ((D_MODEL, tn), BF16), ((tm, tn), BF16),
                       ((3 * tm, LANES), F32), ((LANES, D_MODEL), BF16)],
            resident=[((tm, D_MODEL), BF16)],
            temps=[((sub, D_MODEL), F32), ((sub, tn), F32), ((sub, tn), F32)]),
        name="inproj_main",
    )(x2d, gain, w_main_b, w_f, cos_t, sin_t)


def _cum_kernel(ff_ref, b_ref, o_ref, *, n_blk, valid_from, rel_last):
    row = lax.broadcasted_iota(jnp.int32, (CHUNK, CHUNK), 0)
    col = lax.broadcasted_iota(jnp.int32, (CHUNK, CHUNK), 1)
    tri = (row >= col).astype(BF16)
    rows = lax.broadcasted_iota(jnp.int32, (CHUNK, LANES), 0)
    carry = jnp.zeros((1, LANES), F32)
    for blk in range(n_blk):
        z = ff_ref[blk * CHUNK:(blk + 1) * CHUNK, :] + b_ref[...]
        lf = jnp.minimum(z, 0.0) - jnp.log1p(jnp.exp(-jnp.abs(z)))
        if valid_from:
            lf = jnp.where(rows >= valid_from, lf, 0.0)
        hi = lf.astype(BF16)
        r1 = lf - hi.astype(F32)
        mid = r1.astype(BF16)
        lo = (r1 - mid.astype(F32)).astype(BF16)
        cum = (jnp.dot(tri, hi, preferred_element_type=F32)
               + jnp.dot(tri, mid, preferred_element_type=F32)
               + jnp.dot(tri, lo, preferred_element_type=F32)) + carry
        o_ref[blk * CHUNK:(blk + 1) * CHUNK, :] = cum
        carry = cum[CHUNK - 1:CHUNK, :]
    if rel_last:
        o_ref[...] = o_ref[...] - carry


def _cum(ff, b_row, *, rows_per_batch, valid_from=0, rel_last=False):
    m = ff.shape[0]
    return pl.pallas_call(
        functools.partial(_cum_kernel, n_blk=rows_per_batch // CHUNK,
                          valid_from=valid_from, rel_last=rel_last),
        grid=(m // rows_per_batch,),
        in_specs=[
            pl.BlockSpec((rows_per_batch, LANES), lambda b: (b, 0)),
            pl.BlockSpec((1, LANES), lambda b: (0, 0)),
        ],
        out_specs=pl.BlockSpec((rows_per_batch, LANES), lambda b: (b, 0)),
        out_shape=jax.ShapeDtypeStruct((m, LANES), F32),
        compiler_params=_params(("arbitrary",), pipelined=[((2 * rows_per_batch, LANES), F32)],
                                temps=[((rows_per_batch, LANES), F32)]),
        name="cumgate",
    )(ff, b_row)


def _head_view(ref, hh):
    return ref.at[:, pl.ds(hh * HEAD_DIM, HEAD_DIM)]


def _ret_kernel(logg_ref, *refs, n_chunks, meta_prefix, heads):
    *io_refs, state_scr = refs
    for hh in range(heads):
        _ret_head(logg_ref[pl.program_id(1) * heads + hh], *[_head_view(r, hh) for r in io_refs],
                  state_scr.at[hh], n_chunks=n_chunks, meta_prefix=meta_prefix)


def _ret_head(lg, *refs, n_chunks, meta_prefix):
    if meta_prefix:
        q_ref, k_ref, v_ref, g_ref, km_ref, vm_ref, gain_ref, o_ref, state_scr = refs
    else:
        q_ref, k_ref, v_ref, g_ref, gain_ref, o_ref, state_scr = refs
    ri = lax.broadcasted_iota(jnp.int32, (CHUNK, CHUNK), 0)
    ci = lax.broadcasted_iota(jnp.int32, (CHUNK, CHUNK), 1)
    diff = (ri - ci).astype(F32)
    dmat = jnp.where(diff >= 0, jnp.exp(jnp.maximum(diff, 0.0) * lg), 0.0)
    pos = lax.broadcasted_iota(jnp.int32, (CHUNK, 1), 0).astype(F32)
    xi = jnp.exp((pos + 1.0) * lg)
    zeta = jnp.exp((CHUNK - 1.0 - pos) * lg)
    g_chunk = jnp.exp(jnp.full((1, 1), float(CHUNK), F32) * lg)

    def advance(state, kc, vc):
        kz = (kc.astype(F32) * zeta).astype(BF16)
        return g_chunk * state + lax.dot_general(
            kz, vc, (((0,), (0,)), ((), ())), preferred_element_type=F32)

    state = jnp.zeros((HEAD_DIM, HEAD_DIM), F32)
    if meta_prefix:
        state = advance(state, km_ref[...], vm_ref[...])
    for c in range(n_chunks):
        state_scr[c] = state.astype(BF16)
        if c + 1 < n_chunks:
            rows = slice(c * CHUNK, (c + 1) * CHUNK)
            state = advance(state, k_ref[rows, :], v_ref[rows, :])

    gain = gain_ref[...]

    for c in range(n_chunks):
        rows = slice(c * CHUNK, (c + 1) * CHUNK)
        qc = q_ref[rows, :]
        kc = k_ref[rows, :]
        vc = v_ref[rows, :]
        s = lax.dot_general(qc, kc, (((1,), (1,)), ((), ())), preferred_element_type=F32) * dmat
        o = (jnp.dot(s.astype(BF16), vc, preferred_element_type=F32)
             + xi * jnp.dot(qc, state_scr[c], preferred_element_type=F32))
        mu = jnp.mean(o, axis=-1, keepdims=True)
        d = o - mu
        var = jnp.mean(d * d, axis=-1, keepdims=True)
        y = (d * lax.rsqrt(var + NORM_EPS)) * gain
        gt = g_ref[rows, :].astype(F32)
        o_ref[rows, :] = (_silu(gt) * y).astype(BF16)


def _retention(log_g, proj, proj_meta, gain, *, n_batch, rows_per_batch, meta_prefix, heads):
    width = heads * HEAD_DIM
    blk = lambda sec: pl.BlockSpec((rows_per_batch, width),
                                   lambda b, h, s=BLK[sec] // heads: (b, s + h))
    in_specs = [pl.BlockSpec(memory_space=pltpu.SMEM),
                blk(SEC_RQ), blk(SEC_RK), blk(SEC_RV), blk(SEC_RG)]
    args = [log_g, proj, proj, proj, proj]
    if meta_prefix:
        mblk = lambda sec: pl.BlockSpec((CHUNK, width), lambda b, h, s=BLK[sec] // heads: (0, s + h))
        in_specs += [mblk(SEC_RK), mblk(SEC_RV)]
        args += [proj_meta, proj_meta]
    in_specs.append(pl.BlockSpec((1, width), lambda b, h: (0, h)))
    args.append(gain)
    return pl.pallas_call(
        functools.partial(_ret_kernel, n_chunks=rows_per_batch // CHUNK, meta_prefix=meta_prefix,
                          heads=heads),
        grid=(n_batch, N_HEADS // heads),
        in_specs=in_specs,
        out_specs=pl.BlockSpec((rows_per_batch, width), lambda b, h: (b, h)),
        out_shape=jax.ShapeDtypeStruct((n_batch * rows_per_batch, RET_DIM), BF16),
        scratch_shapes=[pltpu.VMEM((heads, rows_per_batch // CHUNK, HEAD_DIM, HEAD_DIM), BF16)],
        compiler_params=_params(
            ("arbitrary", "arbitrary"),
            pipelined=[((5 * rows_per_batch + 2 * CHUNK, width), BF16)],
            resident=[((rows_per_batch, width), BF16)],
            temps=[((rows_per_batch, HEAD_DIM), F32)]),
        name="retention",
    )(*args)


def _fox_kernel(*refs, n_q, tq, meta_only, heads):
    for hh in range(heads):
        _fox_head(pl.program_id(1) * heads + hh, hh, *refs, n_q=n_q, tq=tq, meta_only=meta_only)


def _fox_head(h, hh, *refs, n_q, tq, meta_only):
    if meta_only:
        q_ref, km_ref, vm_ref, cq_ref, ckm_ref, o_ref, k_all, v_all = refs
        ck_all = ckm_ref[hh] * LOG2E
    else:
        q_ref, k_ref, v_ref, km_ref, vm_ref, cq_ref, ck_ref, ckm_ref, o_ref, k_all, v_all = refs
        k_ref, v_ref = _head_view(k_ref, hh), _head_view(v_ref, hh)
        k_all[CHUNK:, :] = k_ref[...]
        v_all[CHUNK:, :] = v_ref[...]
        ck_all = jnp.concatenate([ckm_ref[hh], ck_ref[hh]], axis=1) * LOG2E
    q_ref, o_ref = _head_view(q_ref, hh), _head_view(o_ref, hh)
    km_ref, vm_ref = _head_view(km_ref, hh), _head_view(vm_ref, hh)
    k_all[0:CHUNK, :] = km_ref[...]
    v_all[0:CHUNK, :] = vm_ref[...]
    lane = lax.broadcasted_iota(jnp.int32, (1, LANES), 1)
    col_m = lax.broadcasted_iota(jnp.int32, (tq, CHUNK), 1)
    row_m = lax.broadcasted_iota(jnp.int32, (tq, CHUNK), 0)
    mask_m = col_m >= META_PAD
    if meta_only:
        mask_m = mask_m & (row_m >= col_m)
    tri = (lax.broadcasted_iota(jnp.int32, (tq, tq), 0)
           >= lax.broadcasted_iota(jnp.int32, (tq, tq), 1))

    def n_keys(qi):
        return CHUNK if meta_only else CHUNK + (qi + 1) * tq

    def logit_pass(qi):
        n = n_keys(qi)
        q = q_ref[qi * tq:(qi + 1) * tq, :]
        t = lax.dot_general(q, k_all[0:n, :], (((1,), (1,)), ((), ())),
                            preferred_element_type=F32) - ck_all[:, 0:n]
        parts = [jnp.where(mask_m, t[:, 0:CHUNK], MASKED)]
        if not meta_only:
            if qi > 0:
                parts.append(t[:, CHUNK:n - tq])
            parts.append(jnp.where(tri, t[:, n - tq:n], MASKED))
        t = parts[0] if len(parts) == 1 else jnp.concatenate(parts, axis=1)
        return t, jnp.max(t, axis=1, keepdims=True)

    def exp_pass(qi, t, mx):
        rows = slice(qi * tq, (qi + 1) * tq)
        cq2 = LOG2E * jnp.sum(jnp.where(lane == h, cq_ref[rows, :], 0.0), axis=1, keepdims=True)
        m_row = mx + cq2
        p = jnp.exp2(t - (m_row - cq2))
        l = jnp.sum(p, axis=1, keepdims=True)
        acc = jnp.dot(p.astype(BF16), v_all[0:n_keys(qi), :], preferred_element_type=F32)
        o_ref[rows, :] = (acc / l).astype(BF16)

    cur = logit_pass(0)
    for qi in range(n_q):
        nxt = logit_pass(qi + 1) if qi + 1 < n_q else None
        exp_pass(qi, *cur)
        cur = nxt


def _fox(proj, proj_meta, cum_col, ck_rows, ckm_rows, *, n_batch, rows_per_batch, tq, meta_only,
         heads):
    n_q = rows_per_batch // tq
    n_all = CHUNK if meta_only else CHUNK + rows_per_batch
    width = heads * HEAD_DIM
    groups = N_HEADS // heads
    qblk = lambda sec: pl.BlockSpec((rows_per_batch, width),
                                    lambda b, h, s=BLK[sec] // heads: (b, s + h))
    mblk = lambda sec: pl.BlockSpec((CHUNK, width), lambda b, h, s=BLK[sec] // heads: (0, s + h))
    cq_spec = pl.BlockSpec((rows_per_batch, LANES), lambda b, h: (b, 0))
    ckm_spec = pl.BlockSpec((heads, 1, CHUNK), lambda b, h: (h, 0, 0))
    if meta_only:
        in_specs = [qblk(SEC_FQ), mblk(SEC_FK), mblk(SEC_FV), cq_spec, ckm_spec]
        args = [proj, proj_meta, proj_meta, cum_col, ckm_rows]
    else:
        ck_spec = pl.BlockSpec((heads, 1, rows_per_batch), lambda b, h: (b * groups + h, 0, 0))
        in_specs = [qblk(SEC_FQ), qblk(SEC_FK), qblk(SEC_FV), mblk(SEC_FK), mblk(SEC_FV),
                    cq_spec, ck_spec, ckm_spec]
        args = [proj, proj, proj, proj_meta, proj_meta, cum_col, ck_rows, ckm_rows]
    return pl.pallas_call(
        functools.partial(_fox_kernel, n_q=n_q, tq=tq, meta_only=meta_only, heads=heads),
        grid=(n_batch, groups),
        in_specs=in_specs,
        out_specs=pl.BlockSpec((rows_per_batch, width), lambda b, h: (b, h)),
        out_shape=jax.ShapeDtypeStruct((n_batch * rows_per_batch, FOX_DIM), BF16),
        scratch_shapes=[pltpu.VMEM((n_all, HEAD_DIM), BF16)] * 2,
        compiler_params=_params(
            ("arbitrary", "arbitrary"),
            pipelined=[((4 * rows_per_batch + 2 * CHUNK, width), BF16),
                       ((rows_per_batch, LANES), F32), ((8 * heads, n_all), F32)],
            resident=[((2 * n_all, HEAD_DIM), BF16)],
            temps=[((4 * tq * heads, n_all), F32)]),
        name="foxattn",
    )(*args)


def _outproj_kernel(ret_hbm, fox_hbm, w_ref, x_hbm, g_ref, h_hbm, c_hbm, *, tm, sub, n_steps):
    def body(ret_ref, fox_ref, x_ref, h_ref, c_ref):
        for rs in _row_tiles(tm, sub):
            hcur = (x_ref[rs, :]
                    + jnp.dot(ret_ref[rs, :], w_ref[0:RET_DIM, :], preferred_element_type=F32)
                    + jnp.dot(fox_ref[rs, :], w_ref[RET_DIM:, :], preferred_element_type=F32))
            h_ref[rs, :] = hcur
            c_ref[rs, :] = _rms(hcur, g_ref[...]).astype(BF16)

    row = lambda width, **kw: pl.BlockSpec((tm, width), lambda i: (i, 0), **kw)
    pltpu.emit_pipeline(
        body,
        grid=(n_steps,),
        in_specs=[row(RET_DIM), row(FOX_DIM), row(D_MODEL, pipeline_mode=pl.Buffered(3))],
        out_specs=[row(D_MODEL), row(D_MODEL)],
    )(ret_hbm, fox_hbm, x_hbm, h_hbm, c_hbm)


def _outproj(ret, fox, w_out_b, x2d, gain, *, tm, sub):
    m = x2d.shape[0]
    hbm = pl.BlockSpec(memory_space=pl.ANY)
    vmem = pl.BlockSpec(memory_space=pltpu.VMEM)
    return pl.pallas_call(
        functools.partial(_outproj_kernel, tm=tm, sub=sub, n_steps=m // tm),
        in_specs=[hbm, hbm, vmem, hbm, vmem],
        out_specs=[hbm, hbm],
        out_shape=[
            jax.ShapeDtypeStruct((m, D_MODEL), F32),
            jax.ShapeDtypeStruct((m, D_MODEL), BF16),
        ],
        compiler_params=_params(
            None,
            pipelined=[((tm, RET_DIM + FOX_DIM), BF16), ((tm, D_MODEL), F32), ((tm, D_MODEL), F32),
                       ((tm, D_MODEL), BF16)],
            resident=[((RET_DIM + FOX_DIM, D_MODEL), BF16), ((tm, D_MODEL), F32)],
            temps=[((sub, D_MODEL), F32), ((sub, D_MODEL), F32)]),
        name="outproj",
    )(ret, fox, w_out_b, x2d, gain)


def _outproj_meta_kernel(mix_ref, w_ref, x_ref, g_ref, c_ref, wq_ref, acc_scr):
    k = pl.program_id(0)
    wq_ref[...] = w_ref[...].astype(BF16)
    part = jnp.dot(mix_ref[...], wq_ref[...], preferred_element_type=F32)

    @pl.when(k == 0)
    def _():
        acc_scr[...] = x_ref[...] + part

    @pl.when(k == 1)
    def _():
        c_ref[...] = _rms(acc_scr[...] + part, g_ref[...]).astype(BF16)


def _outproj_meta(mix_m, w_out, xm, gain):
    half = D_MODEL // 2
    return pl.pallas_call(
        _outproj_meta_kernel,
        grid=(2,),
        in_specs=[
            pl.BlockSpec((CHUNK, half), lambda k: (0, k)),
            pl.BlockSpec((half, D_MODEL), lambda k: (k, 0)),
            pl.BlockSpec((CHUNK, D_MODEL), lambda k: (0, 0)),
            pl.BlockSpec((1, D_MODEL), lambda k: (0, 0)),
        ],
        out_specs=[
            pl.BlockSpec((CHUNK, D_MODEL), lambda k: (0, 0)),
            pl.BlockSpec((half, D_MODEL), lambda k: (k, 0)),
        ],
        out_shape=[
            jax.ShapeDtypeStruct((CHUNK, D_MODEL), BF16),
            jax.ShapeDtypeStruct((D_MODEL, D_MODEL), BF16),
        ],
        scratch_shapes=[pltpu.VMEM((CHUNK, D_MODEL), F32)],
        compiler_params=_params(
            ("arbitrary",),
            pipelined=[((half, D_MODEL), F32), ((half, D_MODEL), BF16), ((CHUNK, 2 * D_MODEL), F32)],
            resident=[((CHUNK, D_MODEL), F32)],
            temps=[((CHUNK, D_MODEL), F32), ((CHUNK, D_MODEL), F32)]),
        name="outproj_meta",
    )(mix_m, w_out, xm, gain)


def _up_kernel(c_ref, cm_ref, wg_ref, wv_ref, cwg_ref, cwv_ref, cbg_ref, cbv_ref, wd_ref,
               o_ref, wdq_ref, wg_s, wv_s, ug_s, uv_s, umg_s, umv_s, *, tm, sub, tiles_per_batch):
    i = pl.program_id(1)
    wdq_ref[...] = wd_ref[...].astype(BF16)

    @pl.when(i == 0)
    def _():
        wg_s[...] = wg_ref[...].astype(BF16)
        wv_s[...] = wv_ref[...].astype(BF16)
        cm = cm_ref[...]
        umg_s[...] = jnp.dot(cm, wg_s[...], preferred_element_type=F32)[8:16, :]
        umv_s[...] = jnp.dot(cm, wv_s[...], preferred_element_type=F32)[8:16, :]

    @pl.when(i % tiles_per_batch == 0)
    def _():
        ug_s[0:8, :] = umg_s[...]
        uv_s[0:8, :] = umv_s[...]

    def conv(c, rs, w_s, u_s, cw_ref, cb_ref):
        u_s[8 + rs.start:8 + rs.stop, :] = jnp.dot(c, w_s[...], preferred_element_type=F32)
        ext = u_s[rs.start:8 + rs.stop, :]
        n = rs.stop - rs.start
        return (cb_ref[...]
                + cw_ref[0:1, :] * pltpu.roll(ext, 2, 0)[8:8 + n, :]
                + cw_ref[1:2, :] * pltpu.roll(ext, 1, 0)[8:8 + n, :]
                + cw_ref[2:3, :] * ext[8:8 + n, :])

    for rs in _row_tiles(tm, sub):
        c = c_ref[rs, :]
        gate = conv(c, rs, wg_s, ug_s, cwg_ref, cbg_ref)
        val = conv(c, rs, wv_s, uv_s, cwv_ref, cbv_ref)
        o_ref[rs, :] = ((gate / (1.0 + jnp.exp(-gate))) * val).astype(BF16)
    ug_s[0:8, :] = ug_s[tm:tm + 8, :]
    uv_s[0:8, :] = uv_s[tm:tm + 8, :]


def _upconv(c, c_meta, w_up, conv_w, conv_b, w_down, *, rows_per_batch, tm, tn, sub):
    m = c.shape[0]
    nj = D_FF // tn
    ni = m // tm
    wd_rows = D_FF // (nj * ni)
    assert wd_rows * nj * ni == D_FF and wd_rows % 16 == 0
    tiles_per_batch = rows_per_batch // tm
    halo_blk = c_meta.shape[0] // 16 - 1
    return pl.pallas_call(
        functools.partial(_up_kernel, tm=tm, sub=sub, tiles_per_batch=tiles_per_batch),
        grid=(nj, m // tm),
        in_specs=[
            pl.BlockSpec((tm, D_MODEL), lambda j, i: (i, 0)),
            pl.BlockSpec((16, D_MODEL), lambda j, i: (halo_blk, 0)),
            pl.BlockSpec((D_MODEL, tn), lambda j, i: (0, j)),
            pl.BlockSpec((D_MODEL, tn), lambda j, i: (0, j + nj)),
            pl.BlockSpec((3, tn), lambda j, i: (0, j)),
            pl.BlockSpec((3, tn), lambda j, i: (0, j + nj)),
            pl.BlockSpec((1, tn), lambda j, i: (0, j)),
            pl.BlockSpec((1, tn), lambda j, i: (0, j + nj)),
            pl.BlockSpec((wd_rows, D_MODEL), lambda j, i: (j * ni + i, 0)),
        ],
        out_specs=[
            pl.BlockSpec((tm, tn), lambda j, i: (i, j)),
            pl.BlockSpec((wd_rows, D_MODEL), lambda j, i: (j * ni + i, 0)),
        ],
        out_shape=[
            jax.ShapeDtypeStruct((m, D_FF), BF16),
            jax.ShapeDtypeStruct((D_FF, D_MODEL), BF16),
        ],
        scratch_shapes=[
            pltpu.VMEM((D_MODEL, tn), BF16),
            pltpu.VMEM((D_MODEL, tn), BF16),
            pltpu.VMEM((tm + 8, tn), F32),
            pltpu.VMEM((tm + 8, tn), F32),
            pltpu.VMEM((8, tn), F32),
            pltpu.VMEM((8, tn), F32),
        ],
        compiler_params=_params(
            ("arbitrary", "arbitrary"),
            pipelined=[((tm, D_MODEL), BF16), ((D_MODEL, 2 * tn), F32), ((tm, tn), BF16),
                       ((wd_rows, D_MODEL), F32), ((wd_rows, D_MODEL), BF16)],
            resident=[((D_MODEL, 2 * tn), BF16), ((2 * (tm + 8), tn), F32)],
            temps=[((sub, tn), F32), ((sub, tn), F32)]),
        name="upconv",
    )(c, c_meta, w_up, w_up, conv_w, conv_w, conv_b, conv_b, w_down)


def _down_kernel(a_ref, w_ref, h_ref, g_ref, o_ref, *, tm, sub):
    for rs in _row_tiles(tm, sub):
        hcur = h_ref[rs, :] + jnp.dot(a_ref[rs, :], w_ref[...], preferred_element_type=F32)
        o_ref[rs, :] = _rms(hcur, g_ref[...])


def _down(act, w_down_b, h1, gain, *, tm, sub):
    m = act.shape[0]
    return pl.pallas_call(
        functools.partial(_down_kernel, tm=tm, sub=sub),
        grid=(m // tm,),
        in_specs=[
            pl.BlockSpec((tm, D_FF), lambda i: (i, 0)),
            pl.BlockSpec((D_FF, D_MODEL), lambda i: (0, 0), pipeline_mode=pl.Buffered(1)),
            pl.BlockSpec((tm, D_MODEL), lambda i: (i, 0)),
            pl.BlockSpec((1, D_MODEL), lambda i: (0, 0)),
        ],
        out_specs=pl.BlockSpec((tm, D_MODEL), lambda i: (i, 0)),
        out_shape=jax.ShapeDtypeStruct((m, D_MODEL), F32),
        compiler_params=_params(
            ("arbitrary",),
            pipelined=[((tm, D_FF), BF16), ((tm, D_MODEL), F32), ((tm, D_MODEL), F32)],
            resident=[((D_FF, D_MODEL), BF16)],
            temps=[((sub, D_MODEL), F32), ((sub, D_MODEL), F32)]),
        name="downproj",
    )(act, w_down_b, h1, gain)


def _rotary_tables(pos):
    inv_freq = 1.0 / (ROPE_BASE ** (jnp.arange(0, HEAD_DIM, 2, dtype=F32) / HEAD_DIM))
    ang = pos[:, None] * inv_freq[None, :]
    cos, sin = jnp.cos(ang), jnp.sin(ang)
    return jnp.concatenate([cos, cos], axis=1), jnp.concatenate([-sin, sin], axis=1)


def kernel(x, meta_tokens, norm1_gain, w_in, b_forget, ret_norm_gain, w_out, norm2_gain,
           w_up, conv_w, conv_b, w_down, final_norm_gain):
    n_batch, seq, d_model = x.shape
    assert d_model == D_MODEL and seq % CHUNK == 0 and w_in.shape[0] == 1
    assert meta_tokens.shape == (N_META, D_MODEL)
    t = _Tiles()
    x2d = x.reshape(n_batch * seq, D_MODEL)
    xm = jnp.concatenate([jnp.zeros((META_PAD, D_MODEL), F32), meta_tokens.astype(F32)], axis=0)

    w_in_t = w_in[0].T
    b_row = jnp.pad(b_forget[0], (0, LANES - N_HEADS)).reshape(1, LANES)
    log_g = jnp.log1p(-jnp.exp2(-5.0 - jnp.arange(N_HEADS, dtype=F32)))
    g1 = norm1_gain[0].reshape(1, D_MODEL)
    g2 = norm2_gain[0].reshape(1, D_MODEL)
    gf = final_norm_gain.reshape(1, D_MODEL)
    gr = ret_norm_gain[0].reshape(1, RET_DIM)

    cos_r, sin_r = _rotary_tables(N_META + jnp.arange(seq, dtype=F32))
    cos_m, sin_m = _rotary_tables(jnp.maximum(jnp.arange(CHUNK, dtype=F32) - META_PAD, 0.0))

    proj_m, ff_m, w_main_b, w_f = _inproj_meta(xm, g1, w_in_t, cos_m, sin_m,
                                               tn=t.inproj_first_cols)
    cum_m = _cum(ff_m, b_row, rows_per_batch=CHUNK, valid_from=META_PAD, rel_last=True)
    ckm_rows = cum_m[:, :N_HEADS].T.reshape(N_HEADS, 1, CHUNK)
    ret_m = _retention(log_g, proj_m, None, gr, n_batch=1, rows_per_batch=CHUNK, meta_prefix=False,
                       heads=t.mixer_heads)
    fox_m = _fox(proj_m, proj_m, cum_m, None, ckm_rows, n_batch=1, rows_per_batch=CHUNK,
                 tq=CHUNK, meta_only=True, heads=t.mixer_heads)
    c_m, w_out_b = _outproj_meta(jnp.concatenate([ret_m, fox_m], axis=1), w_out[0], xm, g2)

    proj, ff = _inproj_main(x2d, g1, w_main_b, w_f, cos_r, sin_r,
                            tm=t.inproj_rows, tn=t.inproj_cols, sub=t.proj_sub_rows)
    cum = _cum(ff, b_row, rows_per_batch=seq)
    ck_rows = (cum[:, :N_HEADS].reshape(n_batch, seq, N_HEADS).transpose(0, 2, 1)
               .reshape(n_batch * N_HEADS, 1, seq))
    ret = _retention(log_g, proj, proj_m, gr, n_batch=n_batch, rows_per_batch=seq, meta_prefix=True,
                     heads=t.mixer_heads)
    fox = _fox(proj, proj_m, cum, ck_rows, ckm_rows, n_batch=n_batch, rows_per_batch=seq,
               tq=t.fox_q_rows, meta_only=False, heads=t.fox_heads)
    h1, c = _outproj(ret, fox, w_out_b, x2d, g2, tm=t.outproj_rows, sub=t.proj_sub_rows)
    act, w_down_b = _upconv(c, c_m, w_up[0], conv_w[0], conv_b, w_down[0],
                            rows_per_batch=seq, tm=t.up_rows, tn=t.up_cols, sub=t.up_sub_rows)
    out = _down(act, w_down_b, h1, gf, tm=t.down_rows, sub=t.proj_sub_rows)
    return out.reshape(n_batch, seq, D_MODEL)
```

```python
import functools
from typing import NamedTuple

import jax
import jax.numpy as jnp
from jax import lax
from jax.experimental import pallas as pl
from jax.experimental.pallas import tpu as pltpu

F32 = jnp.float32
BF16 = jnp.bfloat16

D_MODEL = 2048
N_META = 16
CHUNK = 128
N_HEADS = 8
HEAD_DIM = 128
RET_DIM = N_HEADS * HEAD_DIM
FOX_DIM = N_HEADS * HEAD_DIM
MAIN_COLS = 4 * RET_DIM + 3 * FOX_DIM
D_FF = 5632
ROPE_BASE = 10000.0
NORM_EPS = 1e-6
META_PAD = CHUNK - N_META
MASKED = -1e30
LANES = 128
MIB = 1024 * 1024
LOG2E = 1.4426950408889634
FOXQ_SCALE = HEAD_DIM ** -0.5 * LOG2E

(SEC_RQ, SEC_RK, SEC_RV, SEC_RG, SEC_FQ, SEC_FK, SEC_FV) = range(7)
BLK = {s: s * N_HEADS for s in range(7)}


V7X_VMEM_BYTES = 64 * MIB
V7X_VMEM_UNSCOPED = 6 * MIB


class _Tiles(NamedTuple):
    inproj_rows: int = 1024
    inproj_first_cols: int = 512
    inproj_cols: int = 1024
    proj_sub_rows: int = 256
    fox_q_rows: int = 256
    mixer_heads: int = 2
    fox_heads: int = 4
    outproj_rows: int = 512
    up_rows: int = 2048
    up_sub_rows: int = 1024
    up_cols: int = 512
    down_rows: int = 512


def _nbytes(shape, dtype):
    n = jnp.dtype(dtype).itemsize
    for d in shape:
        n *= d
    return n


def _params(sem, pipelined, resident=(), temps=()):
    need = (2 * sum(_nbytes(*b) for b in pipelined) + sum(_nbytes(*b) for b in resident)
            + sum(_nbytes(*b) for b in temps))
    limit = -(-need // MIB) * MIB
    assert limit <= V7X_VMEM_BYTES - V7X_VMEM_UNSCOPED, (limit, sem)
    return pltpu.CompilerParams(dimension_semantics=sem, vmem_limit_bytes=limit)


def _rms(x, gain):
    ms = jnp.mean(x * x, axis=-1, keepdims=True)
    return (x * lax.rsqrt(ms + NORM_EPS)) * gain


def _silu(x):
    h = 0.5 * x
    return h + h * jnp.tanh(h)


def _row_tiles(tm, sub):
    return [slice(r * sub, (r + 1) * sub) for r in range(tm // sub)]


class _Rows(NamedTuple):
    x_ref: object
    cos_ref: object
    sin_ref: object
    o_ref: object
    ff_ref: object
    a_scr: object
    tiles: list


def _inproj_body(j, tn, groups, g_ref, w_bf, wf_ref):
    sec = j // (RET_DIM // tn)

    def rotary_store(grp, rs, acc, scale):
        cos = grp.cos_ref[rs, :]
        sin = grp.sin_ref[rs, :]
        for hh in range(tn // LANES):
            cols = slice(hh * LANES, (hh + 1) * LANES)
            t = acc[:, cols]
            y = t * cos + pltpu.roll(t, HEAD_DIM // 2, 1) * sin
            if scale is not None:
                y = y * scale
            grp.o_ref[rs, cols] = y.astype(BF16)

    @pl.when(j == 0)
    def _():
        for grp in groups:
            for rs in grp.tiles:
                a = _rms(grp.x_ref[rs, :], g_ref[...]).astype(BF16)
                grp.a_scr[rs, :] = a
                grp.ff_ref[rs, :] = lax.dot_general(a, wf_ref[...], (((1,), (1,)), ((), ())),
                                                    preferred_element_type=F32)
                rotary_store(grp, rs, jnp.dot(a, w_bf[...], preferred_element_type=F32), None)

    @pl.when((j > 0) & (sec <= SEC_RK))
    def _():
        scale = jnp.where(sec == SEC_RK, HEAD_DIM ** -0.5, 1.0).astype(F32)
        for grp in groups:
            for rs in grp.tiles:
                acc = jnp.dot(grp.a_scr[rs, :], w_bf[...], preferred_element_type=F32)
                rotary_store(grp, rs, acc, scale)

    @pl.when(sec > SEC_RK)
    def _():
        scale = jnp.where(sec == SEC_FQ, FOXQ_SCALE, 1.0).astype(F32)
        for grp in groups:
            for rs in grp.tiles:
                acc = jnp.dot(grp.a_scr[rs, :], w_bf[...], preferred_element_type=F32)
                grp.o_ref[rs, :] = (acc * scale).astype(BF16)


def _inproj_meta_kernel(xm_ref, g_ref, w_ref, wf8_ref, cosm_ref, sinm_ref,
                        om_ref, ffm_ref, wq_ref, wf_ref, am_scr, *, tn):
    j = pl.program_id(0)
    wq_ref[...] = w_ref[...].T.astype(BF16)

    @pl.when(j == 0)
    def _():
        pad = jnp.zeros((LANES - N_HEADS, D_MODEL), F32)
        wf_ref[...] = jnp.concatenate([wf8_ref[...], pad], axis=0).astype(BF16)

    groups = [_Rows(xm_ref, cosm_ref, sinm_ref, om_ref, ffm_ref, am_scr, _row_tiles(CHUNK, CHUNK))]
    _inproj_body(j, tn, groups, g_ref, wq_ref, wf_ref)


def _inproj_main_kernel(x_ref, g_ref, w_ref, wf_ref, cos_ref, sin_ref, o_ref, ff_ref, a_scr,
                        *, tm, tn, sub):
    groups = [_Rows(x_ref, cos_ref, sin_ref, o_ref, ff_ref, a_scr, _row_tiles(tm, sub))]
    _inproj_body(pl.program_id(1), tn, groups, g_ref, w_ref, wf_ref)


def _inproj_meta(xm, gain, w_in_t, cos_m, sin_m, *, tn):
    const = lambda shape: pl.BlockSpec(shape, lambda j: (0, 0))
    return pl.pallas_call(
        functools.partial(_inproj_meta_kernel, tn=tn),
        grid=(MAIN_COLS // tn,),
        in_specs=[
            const((CHUNK, D_MODEL)),
            const((1, D_MODEL)),
            pl.BlockSpec((tn, D_MODEL), lambda j: (j, 0)),
            pl.BlockSpec((N_HEADS, D_MODEL), lambda j: (MAIN_COLS // N_HEADS, 0)),
            const((CHUNK, LANES)),
            const((CHUNK, LANES)),
        ],
        out_specs=[
            pl.BlockSpec((CHUNK, tn), lambda j: (0, j)),
            const((CHUNK, LANES)),
            pl.BlockSpec((D_MODEL, tn), lambda j: (0, j)),
            const((LANES, D_MODEL)),
        ],
        out_shape=[
            jax.ShapeDtypeStruct((CHUNK, MAIN_COLS), BF16),
            jax.ShapeDtypeStruct((CHUNK, LANES), F32),
            jax.ShapeDtypeStruct((D_MODEL, MAIN_COLS), BF16),
            jax.ShapeDtypeStruct((LANES, D_MODEL), BF16),
        ],
        scratch_shapes=[pltpu.VMEM((CHUNK, D_MODEL), BF16)],
        compiler_params=_params(
            ("arbitrary",),
            pipelined=[((tn, D_MODEL), F32), ((D_MODEL, tn), BF16), ((CHUNK, D_MODEL), F32),
                       ((CHUNK, tn), BF16), ((3 * CHUNK, LANES), F32), ((LANES, D_MODEL), BF16)],
            resident=[((CHUNK, D_MODEL), BF16)],
            temps=[((tn, D_MODEL), F32), ((CHUNK, D_MODEL), F32), ((CHUNK, tn), F32)]),
        name="inproj_meta",
    )(xm, gain, w_in_t, w_in_t, cos_m, sin_m)


def _inproj_main(x2d, gain, w_main_b, w_f, cos_t, sin_t, *, tm, tn, sub):
    m = x2d.shape[0]
    nb = cos_t.shape[0] // tm
    return pl.pallas_call(
        functools.partial(_inproj_main_kernel, tm=tm, tn=tn, sub=sub),
        grid=(m // tm, MAIN_COLS // tn),
        in_specs=[
            pl.BlockSpec((tm, D_MODEL), lambda i, j: (i, 0)),
            pl.BlockSpec((1, D_MODEL), lambda i, j: (0, 0)),
            pl.BlockSpec((D_MODEL, tn), lambda i, j: (0, j)),
            pl.BlockSpec((LANES, D_MODEL), lambda i, j: (0, 0)),
            pl.BlockSpec((tm, LANES), lambda i, j: (i % nb, 0)),
            pl.BlockSpec((tm, LANES), lambda i, j: (i % nb, 0)),
        ],
        out_specs=[
            pl.BlockSpec((tm, tn), lambda i, j: (i, j)),
            pl.BlockSpec((tm, LANES), lambda i, j: (i, 0)),
        ],
        out_shape=[
            jax.ShapeDtypeStruct((m, MAIN_COLS), BF16),
            jax.ShapeDtypeStruct((m, LANES), F32),
        ],
        scratch_shapes=[pltpu.VMEM((tm, D_MODEL), BF16)],
        compiler_params=_params(
            ("arbitrary", "arbitrary"),
            pipelined=[((tm, D_MODEL), F32), ((D_MODEL, tn), BF16), ((tm, tn), BF16),
                       ((3 * tm, LANES), F32), ((LANES, D_MODEL), BF16)],
            resident=[((tm, D_MODEL), BF16)],
            temps=[((sub, D_MODEL), F32), ((sub, tn), F32), ((sub, tn), F32)]),
        name="inproj_main",
    )(x2d, gain, w_main_b, w_f, cos_t, sin_t)


def _cum_kernel(ff_ref, b_ref, o_ref, *, n_batch, n_blk, valid_from, rel_last):
    for b in range(n_batch):
        rows = pl.ds(b * n_blk * CHUNK, n_blk * CHUNK)
        _cum_batch(ff_ref.at[rows, :], b_ref, o_ref.at[rows, :], n_blk=n_blk,
                   valid_from=valid_from, rel_last=rel_last)


def _cum_batch(ff_ref, b_ref, o_ref, *, n_blk, valid_from, rel_last):
    row = lax.broadcasted_iota(jnp.int32, (CHUNK, CHUNK), 0)
    col = lax.broadcasted_iota(jnp.int32, (CHUNK, CHUNK), 1)
    tri = (row >= col).astype(BF16)
    rows = lax.broadcasted_iota(jnp.int32, (CHUNK, LANES), 0)
    carry = jnp.zeros((1, LANES), F32)
    for blk in range(n_blk):
        z = ff_ref[blk * CHUNK:(blk + 1) * CHUNK, :] + b_ref[...]
        lf = jnp.minimum(z, 0.0) - jnp.log1p(jnp.exp(-jnp.abs(z)))
        if valid_from:
            lf = jnp.where(rows >= valid_from, lf, 0.0)
        hi = lf.astype(BF16)
        r1 = lf - hi.astype(F32)
        mid = r1.astype(BF16)
        lo = (r1 - mid.astype(F32)).astype(BF16)
        cum = (jnp.dot(tri, hi, preferred_element_type=F32)
               + jnp.dot(tri, mid, preferred_element_type=F32)
               + jnp.dot(tri, lo, preferred_element_type=F32)) + carry
        o_ref[blk * CHUNK:(blk + 1) * CHUNK, :] = cum
        carry = cum[CHUNK - 1:CHUNK, :]
    if rel_last:
        o_ref[...] = o_ref[...] - carry


def _cum(ff, b_row, *, rows_per_batch, valid_from=0, rel_last=False):
    m = ff.shape[0]
    return pl.pallas_call(
        functools.partial(_cum_kernel, n_batch=m // rows_per_batch, n_blk=rows_per_batch // CHUNK,
                          valid_from=valid_from, rel_last=rel_last),
        grid=(1,),
        in_specs=[
            pl.BlockSpec((m, LANES), lambda b: (0, 0)),
            pl.BlockSpec((1, LANES), lambda b: (0, 0)),
        ],
        out_specs=pl.BlockSpec((m, LANES), lambda b: (0, 0)),
        out_shape=jax.ShapeDtypeStruct((m, LANES), F32),
        compiler_params=_params(("arbitrary",), pipelined=[((2 * m, LANES), F32)],
                                temps=[((m, LANES), F32)]),
        name="cumgate",
    )(ff, b_row)


def _head_view(ref, hh):
    return ref.at[:, pl.ds(hh * HEAD_DIM, HEAD_DIM)]


def _ret_kernel(logg_ref, *refs, n_chunks, meta_prefix, heads):
    *io_refs, state_scr = refs
    for hh in range(heads):
        _ret_head(logg_ref[pl.program_id(1) * heads + hh], *[_head_view(r, hh) for r in io_refs],
                  state_scr.at[hh], n_chunks=n_chunks, meta_prefix=meta_prefix)


def _ret_head(lg, *refs, n_chunks, meta_prefix):
    if meta_prefix:
        q_ref, k_ref, v_ref, g_ref, km_ref, vm_ref, gain_ref, o_ref, state_scr = refs
    else:
        q_ref, k_ref, v_ref, g_ref, gain_ref, o_ref, state_scr = refs
    ri = lax.broadcasted_iota(jnp.int32, (CHUNK, CHUNK), 0)
    ci = lax.broadcasted_iota(jnp.int32, (CHUNK, CHUNK), 1)
    diff = (ri - ci).astype(F32)
    dmat = jnp.where(diff >= 0, jnp.exp(jnp.maximum(diff, 0.0) * lg), 0.0)
    pos = lax.broadcasted_iota(jnp.int32, (CHUNK, 1), 0).astype(F32)
    xi = jnp.exp((pos + 1.0) * lg)
    zeta = jnp.exp((CHUNK - 1.0 - pos) * lg)
    g_chunk = jnp.exp(jnp.full((1, 1), float(CHUNK), F32) * lg)

    def advance(state, kc, vc):
        kz = (kc.astype(F32) * zeta).astype(BF16)
        return g_chunk * state + lax.dot_general(
            kz, vc, (((0,), (0,)), ((), ())), preferred_element_type=F32)

    state = jnp.zeros((HEAD_DIM, HEAD_DIM), F32)
    if meta_prefix:
        state = advance(state, km_ref[...], vm_ref[...])
    for c in range(n_chunks):
        state_scr[c] = state.astype(BF16)
        if c + 1 < n_chunks:
            rows = slice(c * CHUNK, (c + 1) * CHUNK)
            state = advance(state, k_ref[rows, :], v_ref[rows, :])

    gain = gain_ref[...]

    for c in range(n_chunks):
        rows = slice(c * CHUNK, (c + 1) * CHUNK)
        qc = q_ref[rows, :]
        kc = k_ref[rows, :]
        vc = v_ref[rows, :]
        s = lax.dot_general(qc, kc, (((1,), (1,)), ((), ())), preferred_element_type=F32) * dmat
        o = (jnp.dot(s.astype(BF16), vc, preferred_element_type=F32)
             + xi * jnp.dot(qc, state_scr[c], preferred_element_type=F32))
        mu = jnp.mean(o, axis=-1, keepdims=True)
        d = o - mu
        var = jnp.mean(d * d, axis=-1, keepdims=True)
        y = (d * lax.rsqrt(var + NORM_EPS)) * gain
        gt = g_ref[rows, :].astype(F32)
        o_ref[rows, :] = (_silu(gt) * y).astype(BF16)


def _retention(log_g, proj, proj_meta, gain, *, n_batch, rows_per_batch, meta_prefix, heads):
    width = heads * HEAD_DIM
    blk = lambda sec: pl.BlockSpec((rows_per_batch, width),
                                   lambda b, h, s=BLK[sec] // heads: (b, s + h))
    in_specs = [pl.BlockSpec(memory_space=pltpu.SMEM),
                blk(SEC_RQ), blk(SEC_RK), blk(SEC_RV), blk(SEC_RG)]
    args = [log_g, proj, proj, proj, proj]
    if meta_prefix:
        mblk = lambda sec: pl.BlockSpec((CHUNK, width), lambda b, h, s=BLK[sec] // heads: (0, s + h))
        in_specs += [mblk(SEC_RK), mblk(SEC_RV)]
        args += [proj_meta, proj_meta]
    in_specs.append(pl.BlockSpec((1, width), lambda b, h: (0, h)))
    args.append(gain)
    return pl.pallas_call(
        functools.partial(_ret_kernel, n_chunks=rows_per_batch // CHUNK, meta_prefix=meta_prefix,
                          heads=heads),
        grid=(n_batch, N_HEADS // heads),
        in_specs=in_specs,
        out_specs=pl.BlockSpec((rows_per_batch, width), lambda b, h: (b, h)),
        out_shape=jax.ShapeDtypeStruct((n_batch * rows_per_batch, RET_DIM), BF16),
        scratch_shapes=[pltpu.VMEM((heads, rows_per_batch // CHUNK, HEAD_DIM, HEAD_DIM), BF16)],
        compiler_params=_params(
            ("arbitrary", "arbitrary"),
            pipelined=[((5 * rows_per_batch + 2 * CHUNK, width), BF16)],
            resident=[((rows_per_batch, width), BF16)],
            temps=[((rows_per_batch, HEAD_DIM), F32)]),
        name="retention",
    )(*args)


def _fox_kernel(*refs, n_q, tq, meta_only, heads):
    for hh in range(heads):
        _fox_head(pl.program_id(1) * heads + hh, hh, *refs, n_q=n_q, tq=tq, meta_only=meta_only)


def _fox_head(h, hh, *refs, n_q, tq, meta_only):
    if meta_only:
        q_ref, km_ref, vm_ref, cq_ref, ckm_ref, o_ref, k_all, v_all = refs
        ck_all = ckm_ref[hh] * LOG2E
    else:
        q_ref, k_ref, v_ref, km_ref, vm_ref, cq_ref, ck_ref, ckm_ref, o_ref, k_all, v_all = refs
        k_ref, v_ref = _head_view(k_ref, hh), _head_view(v_ref, hh)
        k_all[CHUNK:, :] = k_ref[...]
        v_all[CHUNK:, :] = v_ref[...]
        ck_all = jnp.concatenate([ckm_ref[hh], ck_ref[hh]], axis=1) * LOG2E
    q_ref, o_ref = _head_view(q_ref, hh), _head_view(o_ref, hh)
    km_ref, vm_ref = _head_view(km_ref, hh), _head_view(vm_ref, hh)
    k_all[0:CHUNK, :] = km_ref[...]
    v_all[0:CHUNK, :] = vm_ref[...]
    lane = lax.broadcasted_iota(jnp.int32, (1, LANES), 1)
    col_m = lax.broadcasted_iota(jnp.int32, (tq, CHUNK), 1)
    row_m = lax.broadcasted_iota(jnp.int32, (tq, CHUNK), 0)
    mask_m = col_m >= META_PAD
    if meta_only:
        mask_m = mask_m & (row_m >= col_m)
    tri = (lax.broadcasted_iota(jnp.int32, (tq, tq), 0)
           >= lax.broadcasted_iota(jnp.int32, (tq, tq), 1))

    def n_keys(qi):
        return CHUNK if meta_only else CHUNK + (qi + 1) * tq

    def logit_pass(qi):
        n = n_keys(qi)
        q = q_ref[qi * tq:(qi + 1) * tq, :]
        t = lax.dot_general(q, k_all[0:n, :], (((1,), (1,)), ((), ())),
                            preferred_element_type=F32) - ck_all[:, 0:n]
        parts = [jnp.where(mask_m, t[:, 0:CHUNK], MASKED)]
        if not meta_only:
            if qi > 0:
                parts.append(t[:, CHUNK:n - tq])
            parts.append(jnp.where(tri, t[:, n - tq:n], MASKED))
        t = parts[0] if len(parts) == 1 else jnp.concatenate(parts, axis=1)
        return t, jnp.max(t, axis=1, keepdims=True)

    def exp_pass(qi, t, mx):
        rows = slice(qi * tq, (qi + 1) * tq)
        cq2 = LOG2E * jnp.sum(jnp.where(lane == h, cq_ref[rows, :], 0.0), axis=1, keepdims=True)
        m_row = mx + cq2
        p = jnp.exp2(t - (m_row - cq2))
        l = jnp.sum(p, axis=1, keepdims=True)
        acc = jnp.dot(p.astype(BF16), v_all[0:n_keys(qi), :], preferred_element_type=F32)
        o_ref[rows, :] = (acc / l).astype(BF16)

    cur = logit_pass(0)
    for qi in range(n_q):
        nxt = logit_pass(qi + 1) if qi + 1 < n_q else None
        exp_pass(qi, *cur)
        cur = nxt


def _fox(proj, proj_meta, cum_col, ck_rows, ckm_rows, *, n_batch, rows_per_batch, tq, meta_only,
         heads):
    n_q = rows_per_batch // tq
    n_all = CHUNK if meta_only else CHUNK + rows_per_batch
    width = heads * HEAD_DIM
    groups = N_HEADS // heads
    qblk = lambda sec: pl.BlockSpec((rows_per_batch, width),
                                    lambda b, h, s=BLK[sec] // heads: (b, s + h))
    mblk = lambda sec: pl.BlockSpec((CHUNK, width), lambda b, h, s=BLK[sec] // heads: (0, s + h))
    cq_spec = pl.BlockSpec((rows_per_batch, LANES), lambda b, h: (b, 0))
    ckm_spec = pl.BlockSpec((heads, 1, CHUNK), lambda b, h: (h, 0, 0))
    if meta_only:
        in_specs = [qblk(SEC_FQ), mblk(SEC_FK), mblk(SEC_FV), cq_spec, ckm_spec]
        args = [proj, proj_meta, proj_meta, cum_col, ckm_rows]
    else:
        ck_spec = pl.BlockSpec((heads, 1, rows_per_batch), lambda b, h: (b * groups + h, 0, 0))
        in_specs = [qblk(SEC_FQ), qblk(SEC_FK), qblk(SEC_FV), mblk(SEC_FK), mblk(SEC_FV),
                    cq_spec, ck_spec, ckm_spec]
        args = [proj, proj, proj, proj_meta, proj_meta, cum_col, ck_rows, ckm_rows]
    return pl.pallas_call(
        functools.partial(_fox_kernel, n_q=n_q, tq=tq, meta_only=meta_only, heads=heads),
        grid=(n_batch, groups),
        in_specs=in_specs,
        out_specs=pl.BlockSpec((rows_per_batch, width), lambda b, h: (b, h)),
        out_shape=jax.ShapeDtypeStruct((n_batch * rows_per_batch, FOX_DIM), BF16),
        scratch_shapes=[pltpu.VMEM((n_all, HEAD_DIM), BF16)] * 2,
        compiler_params=_params(
            ("arbitrary", "arbitrary"),
            pipelined=[((4 * rows_per_batch + 2 * CHUNK, width), BF16),
                       ((rows_per_batch, LANES), F32), ((8 * heads, n_all), F32)],
            resident=[((2 * n_all, HEAD_DIM), BF16)],
            temps=[((4 * tq * heads, n_all), F32)]),
        name="foxattn",
    )(*args)


def _outproj_kernel(ret_ref, fox_ref, w1_ref, w2_ref, x_ref, g_ref, h_ref, c_ref, *, tm, sub):
    for rs in _row_tiles(tm, sub):
        hcur = (x_ref[rs, :]
                + jnp.dot(ret_ref[rs, :], w1_ref[...], preferred_element_type=F32)
                + jnp.dot(fox_ref[rs, :], w2_ref[...], preferred_element_type=F32))
        h_ref[rs, :] = hcur
        c_ref[rs, :] = _rms(hcur, g_ref[...]).astype(BF16)


def _outproj(ret, fox, w_out_b, x2d, gain, *, tm, sub):
    m = x2d.shape[0]
    return pl.pallas_call(
        functools.partial(_outproj_kernel, tm=tm, sub=sub),
        grid=(m // tm,),
        in_specs=[
            pl.BlockSpec((tm, RET_DIM), lambda i: (i, 0)),
            pl.BlockSpec((tm, FOX_DIM), lambda i: (i, 0)),
            pl.BlockSpec((RET_DIM, D_MODEL), lambda i: (0, 0)),
            pl.BlockSpec((FOX_DIM, D_MODEL), lambda i: (1, 0)),
            pl.BlockSpec((tm, D_MODEL), lambda i: (i, 0)),
            pl.BlockSpec((1, D_MODEL), lambda i: (0, 0)),
        ],
        out_specs=[
            pl.BlockSpec((tm, D_MODEL), lambda i: (i, 0)),
            pl.BlockSpec((tm, D_MODEL), lambda i: (i, 0)),
        ],
        out_shape=[
            jax.ShapeDtypeStruct((m, D_MODEL), F32),
            jax.ShapeDtypeStruct((m, D_MODEL), BF16),
        ],
        compiler_params=_params(
            ("arbitrary",),
            pipelined=[((tm, RET_DIM + FOX_DIM), BF16), ((tm, D_MODEL), F32), ((tm, D_MODEL), F32),
                       ((tm, D_MODEL), BF16)],
            resident=[((RET_DIM + FOX_DIM, D_MODEL), BF16)],
            temps=[((sub, D_MODEL), F32), ((sub, D_MODEL), F32)]),
        name="outproj",
    )(ret, fox, w_out_b, w_out_b, x2d, gain)


def _outproj_meta_kernel(mix_ref, w_ref, x_ref, g_ref, c_ref, wq_ref, acc_scr):
    k = pl.program_id(0)
    wq_ref[...] = w_ref[...].astype(BF16)
    part = jnp.dot(mix_ref[...], wq_ref[...], preferred_element_type=F32)

    @pl.when(k == 0)
    def _():
        acc_scr[...] = x_ref[...] + part

    @pl.when(k == 1)
    def _():
        c_ref[...] = _rms(acc_scr[...] + part, g_ref[...]).astype(BF16)


def _outproj_meta(mix_m, w_out, xm, gain):
    half = D_MODEL // 2
    return pl.pallas_call(
        _outproj_meta_kernel,
        grid=(2,),
        in_specs=[
            pl.BlockSpec((CHUNK, half), lambda k: (0, k)),
            pl.BlockSpec((half, D_MODEL), lambda k: (k, 0)),
            pl.BlockSpec((CHUNK, D_MODEL), lambda k: (0, 0)),
            pl.BlockSpec((1, D_MODEL), lambda k: (0, 0)),
        ],
        out_specs=[
            pl.BlockSpec((CHUNK, D_MODEL), lambda k: (0, 0)),
            pl.BlockSpec((half, D_MODEL), lambda k: (k, 0)),
        ],
        out_shape=[
            jax.ShapeDtypeStruct((CHUNK, D_MODEL), BF16),
            jax.ShapeDtypeStruct((D_MODEL, D_MODEL), BF16),
        ],
        scratch_shapes=[pltpu.VMEM((CHUNK, D_MODEL), F32)],
        compiler_params=_params(
            ("arbitrary",),
            pipelined=[((half, D_MODEL), F32), ((half, D_MODEL), BF16), ((CHUNK, 2 * D_MODEL), F32)],
            resident=[((CHUNK, D_MODEL), F32)],
            temps=[((CHUNK, D_MODEL), F32), ((CHUNK, D_MODEL), F32)]),
        name="outproj_meta",
    )(mix_m, w_out, xm, gain)


def _up_kernel(c_ref, cm_ref, wg_ref, wv_ref, cwg_ref, cwv_ref, cbg_ref, cbv_ref, wd_ref,
               o_ref, wdq_ref, wg_s, wv_s, ug_s, uv_s, umg_s, umv_s, *, tm, sub, tiles_per_batch):
    i = pl.program_id(1)
    wdq_ref[...] = wd_ref[...].astype(BF16)

    @pl.when(i == 0)
    def _():
        wg_s[...] = wg_ref[...].astype(BF16)
        wv_s[...] = wv_ref[...].astype(BF16)
        cm = cm_ref[...]
        umg_s[...] = jnp.dot(cm, wg_s[...], preferred_element_type=F32)[8:16, :]
        umv_s[...] = jnp.dot(cm, wv_s[...], preferred_element_type=F32)[8:16, :]

    @pl.when(i % tiles_per_batch == 0)
    def _():
        ug_s[0:8, :] = umg_s[...]
        uv_s[0:8, :] = umv_s[...]

    def conv(c, rs, w_s, u_s, cw_ref, cb_ref):
        u_s[8 + rs.start:8 + rs.stop, :] = jnp.dot(c, w_s[...], preferred_element_type=F32)
        ext = u_s[rs.start:8 + rs.stop, :]
        n = rs.stop - rs.start
        return (cb_ref[...]
                + cw_ref[0:1, :] * pltpu.roll(ext, 2, 0)[8:8 + n, :]
                + cw_ref[1:2, :] * pltpu.roll(ext, 1, 0)[8:8 + n, :]
                + cw_ref[2:3, :] * ext[8:8 + n, :])

    for rs in _row_tiles(tm, sub):
        c = c_ref[rs, :]
        gate = conv(c, rs, wg_s, ug_s, cwg_ref, cbg_ref)
        val = conv(c, rs, wv_s, uv_s, cwv_ref, cbv_ref)
        o_ref[rs, :] = ((gate / (1.0 + jnp.exp(-gate))) * val).astype(BF16)
    ug_s[0:8, :] = ug_s[tm:tm + 8, :]
    uv_s[0:8, :] = uv_s[tm:tm + 8, :]


def _upconv(c, c_meta, w_up, conv_w, conv_b, w_down, *, rows_per_batch, tm, tn, sub):
    m = c.shape[0]
    nj = D_FF // tn
    ni = m // tm
    wd_rows = D_FF // (nj * ni)
    assert wd_rows * nj * ni == D_FF and wd_rows % 16 == 0
    tiles_per_batch = rows_per_batch // tm
    halo_blk = c_meta.shape[0] // 16 - 1
    return pl.pallas_call(
        functools.partial(_up_kernel, tm=tm, sub=sub, tiles_per_batch=tiles_per_batch),
        grid=(nj, m // tm),
        in_specs=[
            pl.BlockSpec((tm, D_MODEL), lambda j, i: (i, 0)),
            pl.BlockSpec((16, D_MODEL), lambda j, i: (halo_blk, 0)),
            pl.BlockSpec((D_MODEL, tn), lambda j, i: (0, j)),
            pl.BlockSpec((D_MODEL, tn), lambda j, i: (0, j + nj)),
            pl.BlockSpec((3, tn), lambda j, i: (0, j)),
            pl.BlockSpec((3, tn), lambda j, i: (0, j + nj)),
            pl.BlockSpec((1, tn), lambda j, i: (0, j)),
            pl.BlockSpec((1, tn), lambda j, i: (0, j + nj)),
            pl.BlockSpec((wd_rows, D_MODEL), lambda j, i: (j * ni + i, 0)),
        ],
        out_specs=[
            pl.BlockSpec((tm, tn), lambda j, i: (i, j)),
            pl.BlockSpec((wd_rows, D_MODEL), lambda j, i: (j * ni + i, 0)),
        ],
        out_shape=[
            jax.ShapeDtypeStruct((m, D_FF), BF16),
            jax.ShapeDtypeStruct((D_FF, D_MODEL), BF16),
        ],
        scratch_shapes=[
            pltpu.VMEM((D_MODEL, tn), BF16),
            pltpu.VMEM((D_MODEL, tn), BF16),
            pltpu.VMEM((tm + 8, tn), F32),
            pltpu.VMEM((tm + 8, tn), F32),
            pltpu.VMEM((8, tn), F32),
            pltpu.VMEM((8, tn), F32),
        ],
        compiler_params=_params(
            ("arbitrary", "arbitrary"),
            pipelined=[((tm, D_MODEL), BF16), ((D_MODEL, 2 * tn), F32), ((tm, tn), BF16),
                       ((wd_rows, D_MODEL), F32), ((wd_rows, D_MODEL), BF16)],
            resident=[((D_MODEL, 2 * tn), BF16), ((2 * (tm + 8), tn), F32)],
            temps=[((sub, tn), F32), ((sub, tn), F32)]),
        name="upconv",
    )(c, c_meta, w_up, w_up, conv_w, conv_w, conv_b, conv_b, w_down)


def _down_kernel(a_ref, w_ref, h_ref, g_ref, o_ref, *, tm, sub):
    for rs in _row_tiles(tm, sub):
        hcur = h_ref[rs, :] + jnp.dot(a_ref[rs, :], w_ref[...], preferred_element_type=F32)
        o_ref[rs, :] = _rms(hcur, g_ref[...])


def _down(act, w_down_b, h1, gain, *, tm, sub):
    m = act.shape[0]
    return pl.pallas_call(
        functools.partial(_down_kernel, tm=tm, sub=sub),
        grid=(m // tm,),
        in_specs=[
            pl.BlockSpec((tm, D_FF), lambda i: (i, 0)),
            pl.BlockSpec((D_FF, D_MODEL), lambda i: (0, 0), pipeline_mode=pl.Buffered(1)),
            pl.BlockSpec((tm, D_MODEL), lambda i: (i, 0)),
            pl.BlockSpec((1, D_MODEL), lambda i: (0, 0)),
        ],
        out_specs=pl.BlockSpec((tm, D_MODEL), lambda i: (i, 0)),
        out_shape=jax.ShapeDtypeStruct((m, D_MODEL), F32),
        compiler_params=_params(
            ("arbitrary",),
            pipelined=[((tm, D_FF), BF16), ((tm, D_MODEL), F32), ((tm, D_MODEL), F32)],
            resident=[((D_FF, D_MODEL), BF16)],
            temps=[((sub, D_MODEL), F32), ((sub, D_MODEL), F32)]),
        name="downproj",
    )(act, w_down_b, h1, gain)


def _rotary_tables(pos):
    inv_freq = 1.0 / (ROPE_BASE ** (jnp.arange(0, HEAD_DIM, 2, dtype=F32) / HEAD_DIM))
    ang = pos[:, None] * inv_freq[None, :]
    cos, sin = jnp.cos(ang), jnp.sin(ang)
    return jnp.concatenate([cos, cos], axis=1), jnp.concatenate([-sin, sin], axis=1)


def kernel(x, meta_tokens, norm1_gain, w_in, b_forget, ret_norm_gain, w_out, norm2_gain,
           w_up, conv_w, conv_b, w_down, final_norm_gain):
    n_batch, seq, d_model = x.shape
    assert d_model == D_MODEL and seq % CHUNK == 0 and w_in.shape[0] == 1
    assert meta_tokens.shape == (N_META, D_MODEL)
    t = _Tiles()
    x2d = x.reshape(n_batch * seq, D_MODEL)
    xm = jnp.concatenate([jnp.zeros((META_PAD, D_MODEL), F32), meta_tokens.astype(F32)], axis=0)

    w_in_t = w_in[0].T
    b_row = jnp.pad(b_forget[0], (0, LANES - N_HEADS)).reshape(1, LANES)
    log_g = jnp.log1p(-jnp.exp2(-5.0 - jnp.arange(N_HEADS, dtype=F32)))
    g1 = norm1_gain[0].reshape(1, D_MODEL)
    g2 = norm2_gain[0].reshape(1, D_MODEL)
    gf = final_norm_gain.reshape(1, D_MODEL)
    gr = ret_norm_gain[0].reshape(1, RET_DIM)

    cos_r, sin_r = _rotary_tables(N_META + jnp.arange(seq, dtype=F32))
    cos_m, sin_m = _rotary_tables(jnp.maximum(jnp.arange(CHUNK, dtype=F32) - META_PAD, 0.0))

    proj_m, ff_m, w_main_b, w_f = _inproj_meta(xm, g1, w_in_t, cos_m, sin_m,
                                               tn=t.inproj_first_cols)
    cum_m = _cum(ff_m, b_row, rows_per_batch=CHUNK, valid_from=META_PAD, rel_last=True)
    ckm_rows = cum_m[:, :N_HEADS].T.reshape(N_HEADS, 1, CHUNK)
    ret_m = _retention(log_g, proj_m, None, gr, n_batch=1, rows_per_batch=CHUNK, meta_prefix=False,
                       heads=t.mixer_heads)
    fox_m = _fox(proj_m, proj_m, cum_m, None, ckm_rows, n_batch=1, rows_per_batch=CHUNK,
                 tq=CHUNK, meta_only=True, heads=t.mixer_heads)
    c_m, w_out_b = _outproj_meta(jnp.concatenate([ret_m, fox_m], axis=1), w_out[0], xm, g2)

    proj, ff = _inproj_main(x2d, g1, w_main_b, w_f, cos_r, sin_r,
                            tm=t.inproj_rows, tn=t.inproj_cols, sub=t.proj_sub_rows)
    cum = _cum(ff, b_row, rows_per_batch=seq)
    ck_rows = (cum[:, :N_HEADS].reshape(n_batch, seq, N_HEADS).transpose(0, 2, 1)
               .reshape(n_batch * N_HEADS, 1, seq))
    ret = _retention(log_g, proj, proj_m, gr, n_batch=n_batch, rows_per_batch=seq, meta_prefix=True,
                     heads=t.mixer_heads)
    fox = _fox(proj, proj_m, cum, ck_rows, ckm_rows, n_batch=n_batch, rows_per_batch=seq,
               tq=t.fox_q_rows, meta_only=False, heads=t.fox_heads)
    h1, c = _outproj(ret, fox, w_out_b, x2d, g2, tm=t.outproj_rows, sub=t.proj_sub_rows)
    act, w_down_b = _upconv(c, c_m, w_up[0], conv_w[0], conv_b, w_down[0],
                            rows_per_batch=seq, tm=t.up_rows, tn=t.up_cols, sub=t.up_sub_rows)
    out = _down(act, w_down_b, h1, gf, tm=t.down_rows, sub=t.proj_sub_rows)
    return out.reshape(n_batch, seq, D_MODEL)
```
